```python
import math
import jax, jax.numpy as jnp
from jax import lax
import numpy as np

D_MODEL = 1024
BATCH = 8
SEQ = 2048
DEPTH = 2

PLE_DIM = 256
BLOCK_Q = 128
DIFF_HEADS = 4
DIFF_DK = 64
DIFF_DV = 2 * DIFF_DK
RET_HEADS = 4
RET_DK = 64
RET_DV = 128
RET_CHUNK = 128
FOX_HEADS = 16
FOX_DH = D_MODEL // FOX_HEADS
REL_BUCKETS = 32
REL_MAX_DIST = 128
N_GROUPS = 4
EXPERTS_PER_GROUP = 4
N_EXPERTS = N_GROUPS * EXPERTS_PER_GROUP
TOP_K = 2
D_FF_EXPERT = D_MODEL // 2
DEEPNORM_ALPHA = (2 * DEPTH) ** 0.25
DEEPNORM_BETA = (8 * DEPTH) ** -0.25
LN_EPS = 1e-5
ROPE_BASE = 10000.0

A_QK = DIFF_HEADS * 2 * DIFF_DK
A_V = DIFF_HEADS * DIFF_DV
B_QK = RET_HEADS * RET_DK
B_V = RET_HEADS * RET_DV
EVEN_IN = 2 * A_QK + A_V + 2 * B_QK + 2 * B_V
EVEN_OUT = A_V + B_V
ODD_IN = 3 * FOX_HEADS * FOX_DH + FOX_HEADS

kernel_name = "hybrid_diffattn_retnet_fox_groupmoe"


def layer_norm(x, g, b):
    xf = x.astype(jnp.float32)
    mu = jnp.mean(xf, -1, keepdims=True)
    var = jnp.mean(jnp.square(xf - mu), -1, keepdims=True)
    return ((xf - mu) * lax.rsqrt(var + LN_EPS) * g.astype(jnp.float32) + b.astype(jnp.float32)).astype(x.dtype)


def head_rms_norm(y, g):
    yf = y.astype(jnp.float32)
    yf = yf * lax.rsqrt(jnp.mean(jnp.square(yf), -1, keepdims=True) + LN_EPS)
    return (yf * g.astype(jnp.float32)).astype(y.dtype)


def head_layer_norm(y, g):
    yf = y.astype(jnp.float32)
    mu = jnp.mean(yf, -1, keepdims=True)
    var = jnp.mean(jnp.square(yf - mu), -1, keepdims=True)
    return ((yf - mu) * lax.rsqrt(var + LN_EPS) * g.astype(jnp.float32)).astype(y.dtype)


def to_heads(t, n_heads):
    b, s, _ = t.shape
    return t.reshape(b, s, n_heads, -1).transpose(0, 2, 1, 3)


def merge_heads(t):
    b, h, s, d = t.shape
    return t.transpose(0, 2, 1, 3).reshape(b, s, h * d)


def rotary(x, pos):
    half = x.shape[-1] // 2
    inv = ROPE_BASE ** (-jnp.arange(half, dtype=jnp.float32) / half)
    ang = pos.astype(jnp.float32)[:, None] * inv[None, :]
    cos, sin = jnp.cos(ang), jnp.sin(ang)
    x1, x2 = x[..., :half], x[..., half:]
    return jnp.concatenate([x1 * cos - x2 * sin, x2 * cos + x1 * sin], -1).astype(x.dtype)


def t5_bucket(dist):
    max_exact = REL_BUCKETS // 2
    d = jnp.maximum(dist, 1).astype(jnp.float32)
    large = max_exact + (jnp.log(d / max_exact) / math.log(REL_MAX_DIST / max_exact)
                         * (REL_BUCKETS - max_exact)).astype(jnp.int32)
    large = jnp.minimum(large, REL_BUCKETS - 1)
    return jnp.where(dist < max_exact, dist, large)


def sweep_query_blocks(fn, seq):
    out = lax.map(fn, jnp.arange(seq // BLOCK_Q) * BLOCK_Q)
    n, b, h, q, d = out.shape
    return out.transpose(1, 2, 0, 3, 4).reshape(b, h, n * q, d)


def diff_attention(q1, q2, k1, k2, v, rel_bias, lam):
    seq = q1.shape[2]
    key_pos = jnp.arange(seq)
    scale = DIFF_DK ** -0.5

    def block(start):
        q_pos = start + jnp.arange(BLOCK_Q)
        dist = q_pos[:, None] - key_pos[None, :]
        causal = dist >= 0
        bias = jnp.moveaxis(rel_bias[t5_bucket(jnp.maximum(dist, 0))], -1, 0).astype(jnp.float32)

        def probs(q, k):
            qb = lax.dynamic_slice_in_dim(q, start, BLOCK_Q, axis=2)
            s = jnp.einsum('bhqd,bhkd->bhqk', qb, k).astype(jnp.float32) * scale + bias
            return jax.nn.softmax(jnp.where(causal, s, -jnp.inf), axis=-1)

        a = probs(q1, k1) - lam * probs(q2, k2)
        return jnp.einsum('bhqk,bhkd->bhqd', a.astype(v.dtype), v)

    return sweep_query_blocks(block, seq)


def retention(q, k, v):
    b, h, s, dk = q.shape
    dv = v.shape[-1]
    c = RET_CHUNK
    n = s // c
    log_g = jnp.log1p(-jnp.exp2(-5.0 - jnp.arange(h, dtype=jnp.float32)))
    j = jnp.arange(c, dtype=jnp.float32)
    rel = j[:, None] - j[None, :]
    decay_in = jnp.where(rel >= 0, jnp.exp(jnp.maximum(rel, 0.0)[None] * log_g[:, None, None]), 0.0)
    q_decay = jnp.exp((j + 1.0)[None] * log_g[:, None])
    k_decay = jnp.exp((c - 1.0 - j)[None] * log_g[:, None])
    chunk_decay = jnp.exp(c * log_g)

    qc = q.reshape(b, h, n, c, dk)
    kc = k.reshape(b, h, n, c, dk)
    vc = v.reshape(b, h, n, c, dv)
    scores = jnp.einsum('bhncd,bhnmd->bhncm', qc, kc) * decay_in[None, :, None]
    inner = jnp.einsum('bhncm,bhnme->bhnce', scores, vc)
    kv = jnp.einsum('bhnmd,bhnme->bhnde', kc * k_decay[None, :, None, :, None], vc)

    def step(state, kv_i):
        return chunk_decay[None, :, None, None] * state + kv_i, state

    init = jnp.zeros((b, h, dk, dv), kv.dtype)
    _, state_prev = lax.scan(step, init, jnp.moveaxis(kv, 2, 0))
    state_prev = jnp.moveaxis(state_prev, 0, 2)
    cross = jnp.einsum('bhncd,bhnde->bhnce', qc * q_decay[None, :, None, :, None], state_prev)
    return (inner + cross).reshape(b, h, s, dv).astype(v.dtype)


def forgetting_attention(q, k, v, log_f):
    seq = q.shape[2]
    key_pos = jnp.arange(seq)
    scale = FOX_DH ** -0.5
    cum = jnp.cumsum(log_f.astype(jnp.float32), axis=-1)

    def block(start):
        q_pos = start + jnp.arange(BLOCK_Q)
        causal = q_pos[:, None] >= key_pos[None, :]
        qb = lax.dynamic_slice_in_dim(q, start, BLOCK_Q, axis=2)
        cq = lax.dynamic_slice_in_dim(cum, start, BLOCK_Q, axis=2)
        s = (jnp.einsum('bhqd,bhkd->bhqk', qb, k).astype(jnp.float32) * scale
             + cq[..., :, None] - cum[..., None, :])
        pr = jax.nn.softmax(jnp.where(causal, s, -jnp.inf), axis=-1)
        return jnp.einsum('bhqk,bhkd->bhqd', pr.astype(v.dtype), v)

    return sweep_query_blocks(block, seq)


def even_mixer(x, w_in, w_out, lam_params, diff_g, ret_g, rel_bias, layer_idx):
    b, s, _ = x.shape
    u = x @ w_in
    cuts = np.cumsum([A_QK, A_QK, A_V, B_QK, B_QK, B_V])
    qa, ka, va, qb, kb, vb, gb = jnp.split(u, cuts, axis=-1)
    qa = qa.reshape(b, s, DIFF_HEADS, 2, DIFF_DK).transpose(0, 2, 3, 1, 4)
    ka = ka.reshape(b, s, DIFF_HEADS, 2, DIFF_DK).transpose(0, 2, 3, 1, 4)
    va = to_heads(va, DIFF_HEADS)
    lam_init = 0.8 - 0.6 * math.exp(-0.3 * layer_idx)
    lp = lam_params.astype(jnp.float32)
    lam = jnp.exp(jnp.sum(lp[0] * lp[1])) - jnp.exp(jnp.sum(lp[2] * lp[3])) + lam_init
    ya = diff_attention(qa[:, :, 0], qa[:, :, 1], ka[:, :, 0], ka[:, :, 1], va, rel_bias, lam)
    ya = head_rms_norm(ya, diff_g) * (1.0 - lam_init)
    pos = jnp.arange(s)
    qr = rotary(to_heads(qb, RET_HEADS), pos)
    kr = rotary(to_heads(kb, RET_HEADS), pos) * (RET_DK ** -0.5)
    yb = head_layer_norm(retention(qr, kr, to_heads(vb, RET_HEADS)), ret_g)
    yb = jax.nn.silu(gb) * merge_heads(yb)
    y = jnp.concatenate([merge_heads(ya).astype(x.dtype), yb.astype(x.dtype)], axis=-1)
    return y @ w_out


def odd_mixer(x, w_in, b_forget, w_out):
    u = x @ w_in
    width = FOX_HEADS * FOX_DH
    q, k, v, f_logit = jnp.split(u, [width, 2 * width, 3 * width], axis=-1)
    log_f = jax.nn.log_sigmoid((f_logit + b_forget).astype(jnp.float32)).transpose(0, 2, 1)
    y = forgetting_attention(to_heads(q, FOX_HEADS), to_heads(k, FOX_HEADS), to_heads(v, FOX_HEADS), log_f)
    return merge_heads(y) @ w_out


def grouped_moe(h, router_w, w_gate, w_up, w_down):
    b, s, _ = h.shape
    probs = jax.nn.softmax(jnp.einsum('bsd,de->bse', h, router_w).astype(jnp.float32), axis=-1)
    grp = probs.reshape(b, s, N_GROUPS, EXPERTS_PER_GROUP)
    grp_score = jnp.sum(lax.top_k(grp, TOP_K)[0], axis=-1)
    best = jnp.argmax(grp_score, axis=-1)
    in_group = jnp.repeat(jax.nn.one_hot(best, N_GROUPS, dtype=jnp.bool_), EXPERTS_PER_GROUP, axis=-1)
    vals, idx = lax.top_k(jnp.where(in_group, probs, -1.0), TOP_K)
    gates = vals / jnp.sum(vals, -1, keepdims=True)
    dense_gate = jnp.sum(jax.nn.one_hot(idx, N_EXPERTS, dtype=jnp.float32) * gates[..., None], axis=-2)
    dense_gate = dense_gate.astype(h.dtype)
    y = jnp.zeros_like(h)
    for e in range(N_EXPERTS):
        a = jax.nn.silu(h @ w_gate[e]) * (h @ w_up[e])
        y = y + dense_gate[..., e:e + 1] * (a @ w_down[e])
    return y


def setup_inputs(seed: int = 0) -> dict:
    key = jax.random.key(seed)
    ks = jax.random.split(key, 24)
    f32 = jnp.float32
    n_even = (DEPTH + 1) // 2
    n_odd = DEPTH // 2
    beta = DEEPNORM_BETA

    def nrm(k, shape, scale):
        return jax.random.normal(k, shape, f32) * scale

    even_cols = np.concatenate([np.ones(2 * A_QK), np.full(A_V, beta), np.ones(2 * B_QK),
                                np.full(B_V, beta), np.ones(B_V)]).astype(np.float32)
    odd_cols = np.concatenate([np.ones(2 * FOX_HEADS * FOX_DH), np.full(FOX_HEADS * FOX_DH, beta),
                               np.full(FOX_HEADS, 0.5)]).astype(np.float32)
    return {
        "x": nrm(ks[0], (BATCH, SEQ, D_MODEL), 1.0),
        "p": nrm(ks[1], (DEPTH, BATCH, SEQ, PLE_DIM), 1.0),
        "rel_bias": nrm(ks[2], (REL_BUCKETS, DIFF_HEADS), 0.5),
        "router_w": nrm(ks[3], (D_MODEL, N_EXPERTS), D_MODEL ** -0.5),
        "even_w_in": nrm(ks[4], (n_even, D_MODEL, EVEN_IN), D_MODEL ** -0.5) * jnp.asarray(even_cols),
        "even_w_out": nrm(ks[5], (n_even, EVEN_OUT, D_MODEL), beta * EVEN_OUT ** -0.5),
        "even_lambda": nrm(ks[6], (n_even, 4, DIFF_DK), 0.1),
        "even_diff_norm": 1.0 + nrm(ks[7], (n_even, DIFF_DV), 0.02),
        "even_ret_norm": 1.0 + nrm(ks[8], (n_even, RET_DV), 0.02),
        "odd_w_in": nrm(ks[9], (n_odd, D_MODEL, ODD_IN), D_MODEL ** -0.5) * jnp.asarray(odd_cols),
        "odd_b_forget": jax.random.uniform(ks[10], (n_odd, FOX_HEADS), f32, 1.0, 4.0),
        "odd_w_out": nrm(ks[11], (n_odd, FOX_HEADS * FOX_DH, D_MODEL), beta * D_MODEL ** -0.5),
        "ln_mix_g": 1.0 + nrm(ks[12], (DEPTH, D_MODEL), 0.02),
        "ln_mix_b": nrm(ks[13], (DEPTH, D_MODEL), 0.02),
        "ln_ffn_g": 1.0 + nrm(ks[14], (DEPTH, D_MODEL), 0.02),
        "ln_ffn_b": nrm(ks[15], (DEPTH, D_MODEL), 0.02),
        "moe_w_gate": nrm(ks[16], (DEPTH, N_EXPERTS, D_MODEL, D_FF_EXPERT), beta * D_MODEL ** -0.5),
        "moe_w_up": nrm(ks[17], (DEPTH, N_EXPERTS, D_MODEL, D_FF_EXPERT), beta * D_MODEL ** -0.5),
        "moe_w_down": nrm(ks[18], (DEPTH, N_EXPERTS, D_FF_EXPERT, D_MODEL), beta * D_FF_EXPERT ** -0.5),
        "ple_proj": nrm(ks[19], (DEPTH, PLE_DIM, D_MODEL), 0.5 * PLE_DIM ** -0.5),
        "ple_gate": nrm(ks[20], (DEPTH, D_MODEL, D_MODEL), D_MODEL ** -0.5),
    }


def reference(x, p, rel_bias, router_w, even_w_in, even_w_out, even_lambda, even_diff_norm,
              even_ret_norm, odd_w_in, odd_b_forget, odd_w_out, ln_mix_g, ln_mix_b,
              ln_ffn_g, ln_ffn_b, moe_w_gate, moe_w_up, moe_w_down, ple_proj, ple_gate):
    for i in range(DEPTH):
        j = i // 2
        if i % 2 == 0:
            mix = even_mixer(x, even_w_in[j], even_w_out[j], even_lambda[j], even_diff_norm[j],
                             even_ret_norm[j], rel_bias, i)
        else:
            mix = odd_mixer(x, odd_w_in[j], odd_b_forget[j], odd_w_out[j])
        h = layer_norm(DEEPNORM_ALPHA * x + mix.astype(x.dtype), ln_mix_g[i], ln_mix_b[i])
        ffn = grouped_moe(h, router_w, moe_w_gate[i], moe_w_up[i], moe_w_down[i])
        h = layer_norm(DEEPNORM_ALPHA * h + ffn, ln_ffn_g[i], ln_ffn_b[i])
        x = h + jax.nn.sigmoid(h @ ple_gate[i]) * (p[i] @ ple_proj[i])
    return x
```

```python
import functools
import math

import numpy as np
import jax
import jax.numpy as jnp
from jax import lax
from jax.experimental import pallas as pl
from jax.experimental.pallas import tpu as pltpu

F32 = jnp.float32
BF16 = jnp.bfloat16
I32 = jnp.int32

DIFF_HEADS = 4
DIFF_DK = 64
RET_HEADS = 4
RET_DK = 64
RET_DV = 128
RET_CHUNK = 128
FOX_HEADS = 16
FOX_DH = 64
REL_BUCKETS = 32
REL_MAX_DIST = 128
N_GROUPS = 4
EXPERTS_PER_GROUP = 4
N_EXPERTS = 16
DEPTH = 2
DEEPNORM_ALPHA = (2 * DEPTH) ** 0.25
LN_EPS = 1e-5
ROPE_BASE = 10000.0
NEG_BIG = -1e30

VMEM_LIMIT_BYTES = 48 * 1024 * 1024
LANES = 128

PROJ_TM = 512
ATT_TQ = 256
MOE_TR = 256
COMB_TM = 256


def _cparams(*sem):
    return pltpu.CompilerParams(dimension_semantics=sem, vmem_limit_bytes=VMEM_LIMIT_BYTES)


def _dot(a, b):
    return jnp.dot(a, b, preferred_element_type=F32)


def _dot_nt(a, b):
    return lax.dot_general(a, b, (((1,), (1,)), ((), ())), preferred_element_type=F32)


def _layer_norm_rows(z, g, b):
    mu = jnp.mean(z, axis=-1, keepdims=True)
    zc = z - mu
    var = jnp.mean(zc * zc, axis=-1, keepdims=True)
    return zc * lax.rsqrt(var + LN_EPS) * g + b


def _silu(x):
    return x * (1.0 / (1.0 + jnp.exp(-x)))


def _sigmoid(x):
    return 1.0 / (1.0 + jnp.exp(-x))


def _even_inproj_kernel(x_ref, w_ref, cos_ref, sin_ref,
                        qa_ref, ka_ref, va_ref, qb_ref, kb_ref, vb_ref, gb_ref):
    x = x_ref[...].astype(BF16)

    def mm(c0, c1):
        return _dot(x, w_ref[:, c0:c1])

    qa_ref[...] = (mm(0, 512) * (DIFF_DK ** -0.5)).astype(BF16)
    ka_ref[...] = mm(512, 1024).astype(BF16)
    va_ref[...] = mm(1024, 1536).astype(BF16)
    qk = mm(1536, 2048)
    cos = cos_ref[...]
    sin = sin_ref[...]
    lane = lax.broadcasted_iota(I32, cos.shape, 1)
    first_half = (lane % RET_DK) < (RET_DK // 2)

    def rot(t):
        sw = jnp.where(first_half, pltpu.roll(t, t.shape[1] - RET_DK // 2, 1),
                       pltpu.roll(t, RET_DK // 2, 1))
        return t * cos + sw * sin

    qb_ref[...] = rot(qk[:, :256]).astype(BF16)
    kb_ref[...] = (rot(qk[:, 256:]) * (RET_DK ** -0.5)).astype(BF16)
    vb_ref[...] = mm(2048, 2560).astype(BF16)
    gb_ref[...] = mm(2560, 3072).astype(BF16)


def _even_inproj(x2, w_bf, cos_t, sin_t, seq):
    t, d = x2.shape
    tm = min(PROJ_TM, seq)
    nblk_s = seq // tm
    widths = (512, 512, 512, 256, 256, 512, 512)
    row = lambda i: (i, 0)
    return pl.pallas_call(
        _even_inproj_kernel,
        grid=(t // tm,),
        in_specs=[
            pl.BlockSpec((tm, d), row),
            pl.BlockSpec(w_bf.shape, lambda i: (0, 0)),
            pl.BlockSpec((tm, 256), lambda i: (i % nblk_s, 0)),
            pl.BlockSpec((tm, 256), lambda i: (i % nblk_s, 0)),
        ],
        out_specs=[pl.BlockSpec((tm, w), row) for w in widths],
        out_shape=[jax.ShapeDtypeStruct((t, w), BF16) for w in widths],
        compiler_params=_cparams("parallel"),
        name="even_inproj",
    )(x2, w_bf, cos_t, sin_t)


def _diff_attn_kernel(lam_ref, q_ref, k_ref, v_ref, bias_ref, g_ref, o_ref, *, lam_init):
    tq = q_ref.shape[0]
    tk = tq
    i = pl.program_id(2)
    q = q_ref[...]
    lane = lax.broadcasted_iota(I32, q.shape, 1)
    zero = jnp.zeros_like(q)
    q1 = jnp.where(lane < DIFF_DK, q, zero)
    q2 = jnp.where(lane >= DIFF_DK, q, zero)

    def body(j, carry):
        m1, l1, a1, m2, l2, a2 = carry
        start = pl.multiple_of(j * tk, tk)
        k = k_ref[pl.ds(start, tk), :]
        v = v_ref[pl.ds(start, tk), :]
        bias = bias_ref[jnp.minimum(i - j, 2)]

        def upd(qm, m, l, a):
            s = _dot_nt(qm, k) + bias
            m_new = jnp.maximum(m, jnp.max(s, axis=-1, keepdims=True))
            alpha = jnp.exp(m - m_new)
            p = jnp.exp(s - m_new)
            l_new = alpha * l + jnp.sum(p, axis=-1, keepdims=True)
            a_new = alpha * a + _dot(p.astype(BF16), v)
            return m_new, l_new, a_new

        m1, l1, a1 = upd(q1, m1, l1, a1)
        m2, l2, a2 = upd(q2, m2, l2, a2)
        return m1, l1, a1, m2, l2, a2

    m0 = jnp.full((tq, 1), NEG_BIG, F32)
    l0 = jnp.zeros((tq, 1), F32)
    a0 = jnp.zeros((tq, LANES), F32)
    m1, l1, a1, m2, l2, a2 = lax.fori_loop(0, i + 1, body, (m0, l0, a0, m0, l0, a0))

    lp = lam_ref[...]
    lam = (jnp.exp(jnp.sum(lp[0:1, :] * lp[1:2, :], axis=-1, keepdims=True))
           - jnp.exp(jnp.sum(lp[2:3, :] * lp[3:4, :], axis=-1, keepdims=True)) + lam_init)
    o = a1 / l1 - lam * (a2 / l2)
    o = o * lax.rsqrt(jnp.mean(o * o, axis=-1, keepdims=True) + LN_EPS)
    o_ref[...] = (o * g_ref[...] * (1.0 - lam_init)).astype(BF16)


def _diff_attention(qa, ka, va, bias_tab, lam_params, diff_g, lam_init):
    b, s, _ = qa.shape
    tq = min(ATT_TQ, s)
    kern = functools.partial(_diff_attn_kernel, lam_init=lam_init)
    return pl.pallas_call(
        kern,
        grid=(b, DIFF_HEADS, s // tq),
        in_specs=[
            pl.BlockSpec(lam_params.shape, lambda bi, h, i: (0, 0)),
            pl.BlockSpec((None, tq, LANES), lambda bi, h, i: (bi, i, h)),
            pl.BlockSpec((None, s, LANES), lambda bi, h, i: (bi, 0, h)),
            pl.BlockSpec((None, s, LANES), lambda bi, h, i: (bi, 0, h)),
            pl.BlockSpec((None, 3, tq, tq), lambda bi, h, i: (h, 0, 0, 0)),
            pl.BlockSpec((1, LANES), lambda bi, h, i: (0, 0)),
        ],
        out_specs=pl.BlockSpec((None, tq, LANES), lambda bi, h, i: (bi, i, h)),
        out_shape=jax.ShapeDtypeStruct((b, s, DIFF_HEADS * LANES), BF16),
        compiler_params=_cparams("parallel", "parallel", "parallel"),
        name="diff_attention",
    )(lam_params, qa, ka, va, bias_tab, diff_g)


def _retention_kernel(q_ref, k_ref, v_ref, gate_ref, din_ref, qd_ref, kd_ref, cd_ref, g_ref, o_ref):
    s = q_ref.shape[0]
    c = RET_CHUNK
    par = pl.program_id(1) % 2
    lane = lax.broadcasted_iota(I32, (c, LANES), 1)
    own = (lane // RET_DK) == par
    din = din_ref[...]
    qd = qd_ref[...]
    kd = kd_ref[...]
    cd = cd_ref[...]
    g = g_ref[...]

    def chunk(n, state):
        r = pl.ds(pl.multiple_of(n * c, c), c)
        q = jnp.where(own, q_ref[r, :].astype(F32), 0.0)
        k = jnp.where(own, k_ref[r, :].astype(F32), 0.0)
        v = v_ref[r, :]
        scores = _dot_nt(q.astype(BF16), k.astype(BF16)) * din
        inner = _dot(scores.astype(BF16), v)
        cross = _dot((q * qd).astype(BF16), state.astype(BF16))
        kv = _dot((k * kd).T.astype(BF16), v)
        y = inner + cross
        mu = jnp.mean(y, axis=-1, keepdims=True)
        yc = y - mu
        var = jnp.mean(yc * yc, axis=-1, keepdims=True)
        yn = yc * lax.rsqrt(var + LN_EPS) * g
        gate = gate_ref[r, :].astype(F32)
        o_ref[r, :] = (_silu(gate) * yn).astype(BF16)
        return cd * state + kv

    lax.fori_loop(0, s // c, chunk, jnp.zeros((LANES, RET_DV), F32))


def _retention(qb, kb, vb, gb, tabs, ret_g):
    b, s, _ = qb.shape
    din, qd, kd, cd = tabs
    pair = lambda bi, h: (bi, 0, h // 2)
    head = lambda bi, h: (bi, 0, h)
    tab = lambda bi, h: (h, 0, 0)
    return pl.pallas_call(
        _retention_kernel,
        grid=(b, RET_HEADS),
        in_specs=[
            pl.BlockSpec((None, s, LANES), pair),
            pl.BlockSpec((None, s, LANES), pair),
            pl.BlockSpec((None, s, RET_DV), head),
            pl.BlockSpec((None, s, RET_DV), head),
            pl.BlockSpec((None, RET_CHUNK, RET_CHUNK), tab),
            pl.BlockSpec((None, RET_CHUNK, LANES), tab),
            pl.BlockSpec((None, RET_CHUNK, LANES), tab),
            pl.BlockSpec((None, 1, LANES), tab),
            pl.BlockSpec((1, RET_DV), lambda bi, h: (0, 0)),
        ],
        out_specs=pl.BlockSpec((None, s, RET_DV), head),
        out_shape=jax.ShapeDtypeStruct((b, s, RET_HEADS * RET_DV), BF16),
        compiler_params=_cparams("parallel", "parallel"),
        name="retention",
    )(qb, kb, vb, gb, din, qd, kd, cd, ret_g)


def _odd_inproj_kernel(x_ref, w_ref, wf_ref, bf_ref, q_ref, k_ref, v_ref, cum_ref, carry_ref, *, nblk_s):
    i = pl.program_id(0)
    x = x_ref[...].astype(BF16)
    d = q_ref.shape[1]
    q_ref[...] = (_dot(x, w_ref[:, 0:d]) * (FOX_DH ** -0.5)).astype(BF16)
    k_ref[...] = _dot(x, w_ref[:, d:2 * d]).astype(BF16)
    v_ref[...] = _dot(x, w_ref[:, 2 * d:3 * d]).astype(BF16)
    z = _dot(x, wf_ref[...]) + bf_ref[...]
    c = jnp.minimum(z, 0.0) - jnp.log1p(jnp.exp(-jnp.abs(z)))
    tm = c.shape[0]
    row = lax.broadcasted_iota(I32, c.shape, 0)
    step = 1
    while step < tm:
        c = c + jnp.where(row >= step, pltpu.roll(c, step, 0), 0.0)
        step *= 2

    @pl.when(i % nblk_s == 0)
    def _():
        carry_ref[...] = jnp.zeros_like(carry_ref)

    c = c + carry_ref[...]
    cum_ref[...] = c
    carry_ref[...] = c[tm - 1:tm, :]


def _odd_inproj(x2, w_bf, wf_bf, bfg, seq):
    t, d = x2.shape
    tm = min(PROJ_TM, seq)
    nblk_s = seq // tm
    row = lambda i: (i, 0)
    kern = functools.partial(_odd_inproj_kernel, nblk_s=nblk_s)
    return pl.pallas_call(
        kern,
        grid=(t // tm,),
        in_specs=[
            pl.BlockSpec((tm, d), row),
            pl.BlockSpec(w_bf.shape, lambda i: (0, 0)),
            pl.BlockSpec(wf_bf.shape, lambda i: (0, 0)),
            pl.BlockSpec(bfg.shape, lambda i: (0, 0)),
        ],
        out_specs=[pl.BlockSpec((tm, d), row)] * 3 + [pl.BlockSpec((tm, LANES), row)],
        out_shape=[jax.ShapeDtypeStruct((t, d), BF16)] * 3 + [jax.ShapeDtypeStruct((t, LANES), F32)],
        scratch_shapes=[pltpu.VMEM((1, LANES), F32)],
        compiler_params=_cparams("arbitrary"),
        name="odd_inproj",
    )(x2, w_bf, wf_bf, bfg)


def _fox_attn_kernel(q_ref, k_ref, v_ref, cq_ref, ck_ref, o_ref):
    tq = q_ref.shape[0]
    tk = tq
    hp = pl.program_id(1)
    i = pl.program_id(2)
    q = q_ref[...]
    lane = lax.broadcasted_iota(I32, (tq, LANES), 1)
    rowi = lax.broadcasted_iota(I32, (tq, tk), 0)
    coli = lax.broadcasted_iota(I32, (tq, tk), 1)
    cq_all = cq_ref[...]
    outs = []
    for par in range(2):
        own = (lane // FOX_DH) == par
        qm = jnp.where(own, q, jnp.zeros_like(q))
        cq = jnp.sum(jnp.where(lane == 2 * hp + par, cq_all, 0.0), axis=-1, keepdims=True)

        def body(j, carry, masked, own=own, qm=qm, cq=cq, par=par):
            m, acc = carry
            start = pl.multiple_of(j * tk, tk)
            k = k_ref[pl.ds(start, tk), :]
            v = v_ref[pl.ds(start, tk), :]
            v_aug = jnp.where(own[:tk], v, jnp.ones_like(v))
            ck = ck_ref[pl.ds(2 * hp + par, 1), pl.ds(start, tk)]
            s = _dot_nt(qm, k) + cq - ck
            if masked:
                s = jnp.where(coli > rowi, NEG_BIG, s)
            m_new = jnp.maximum(m, jnp.max(s, axis=-1, keepdims=True))
            alpha = jnp.exp(m - m_new)
            p = jnp.exp(s - m_new)
            acc = alpha * acc + _dot(p.astype(BF16), v_aug)
            return m_new, acc

        carry = (jnp.full((tq, 1), NEG_BIG, F32), jnp.zeros((tq, LANES), F32))
        carry = lax.fori_loop(0, i, functools.partial(body, masked=False), carry)
        _, acc = body(i, carry, True)
        den = acc[:, FOX_DH:FOX_DH + 1] if par == 0 else acc[:, 0:1]
        outs.append(acc / den)
    o_ref[...] = jnp.where(lane < FOX_DH, outs[0], outs[1]).astype(BF16)


def _fox_attention(q, k, v, cum, cum_t):
    b, s, d = q.shape
    tq = min(ATT_TQ, s)
    npair = d // LANES
    return pl.pallas_call(
        _fox_attn_kernel,
        grid=(b, npair, s // tq),
        in_specs=[
            pl.BlockSpec((None, tq, LANES), lambda bi, h, i: (bi, i, h)),
            pl.BlockSpec((None, s, LANES), lambda bi, h, i: (bi, 0, h)),
            pl.BlockSpec((None, s, LANES), lambda bi, h, i: (bi, 0, h)),
            pl.BlockSpec((None, tq, LANES), lambda bi, h, i: (bi, i, 0)),
            pl.BlockSpec((None, FOX_HEADS, s), lambda bi, h, i: (bi, 0, 0)),
        ],
        out_specs=pl.BlockSpec((None, tq, LANES), lambda bi, h, i: (bi, i, h)),
        out_shape=jax.ShapeDtypeStruct((b, s, d), BF16),
        compiler_params=_cparams("parallel", "parallel", "parallel"),
        name="fox_attention",
    )(q, k, v, cum, cum_t)


def _outproj_router_kernel(*refs, n_y):
    y_refs = refs[:n_y]
    w_refs = refs[n_y:2 * n_y]
    x_ref, g_ref, b_ref, rw_ref, h_ref, route_ref = refs[2 * n_y:]
    mix = _dot(y_refs[0][...], w_refs[0][...])
    for yr, wr in zip(y_refs[1:], w_refs[1:]):
        mix = mix + _dot(yr[...], wr[...])
    h = _layer_norm_rows(DEEPNORM_ALPHA * x_ref[...] + mix, g_ref[...], b_ref[...])
    h_ref[...] = h

    rw = rw_ref[...]
    rw_hi = rw.astype(BF16)
    rw_lo = (rw - rw_hi.astype(F32)).astype(BF16)
    h_hi = h.astype(BF16)
    h_lo = (h - h_hi.astype(F32)).astype(BF16)
    logits = _dot(h_hi, rw_hi) + (_dot(h_lo, rw_hi) + _dot(h_hi, rw_lo))
    tm = logits.shape[0]
    lane = lax.broadcasted_iota(I32, (tm, LANES), 1)
    valid = lane < N_EXPERTS
    logits = jnp.where(valid, logits, NEG_BIG)
    mx = jnp.max(logits, axis=-1, keepdims=True)
    ex = jnp.exp(logits - mx)
    probs = ex / jnp.sum(ex, axis=-1, keepdims=True)
    grp = lane // EXPERTS_PER_GROUP

    def top2(vals):
        v1 = jnp.max(vals, axis=-1, keepdims=True)
        i1 = jnp.min(jnp.where(vals == v1, lane, LANES), axis=-1, keepdims=True)
        rest = jnp.where(lane == i1, -2.0, vals)
        v2 = jnp.max(rest, axis=-1, keepdims=True)
        i2 = jnp.min(jnp.where(rest == v2, lane, LANES), axis=-1, keepdims=True)
        return v1, i1, v2, i2

    best_score = None
    best = None
    for gi in range(N_GROUPS):
        v1, _, v2, _ = top2(jnp.where(grp == gi, probs, -1.0))
        score = v1 + v2
        if gi == 0:
            best_score, best = score, jnp.zeros_like(score, dtype=I32)
        else:
            better = score > best_score
            best = jnp.where(better, gi, best)
            best_score = jnp.where(better, score, best_score)
    v1, i1, v2, i2 = top2(jnp.where(grp == best, probs, -1.0))
    tot = v1 + v2
    g1 = v1 / tot
    g2 = v2 / tot
    route = jnp.where(lane == i1, g1, jnp.where(lane == i2, g2, 0.0))
    route = jnp.where(lane == N_EXPERTS, i1.astype(F32), route)
    route = jnp.where(lane == N_EXPERTS + 1, i2.astype(F32), route)
    route = jnp.where(lane == N_EXPERTS + 2, g1, route)
    route = jnp.where(lane == N_EXPERTS + 3, g2, route)
    route_ref[...] = route


def _outproj_router(ys, ws, x2, ln_g, ln_b, rw_pad):
    t, d = x2.shape
    tm = min(PROJ_TM, t)
    row = lambda i: (i, 0)
    full = lambda i: (0, 0)
    n_y = len(ys)
    kern = functools.partial(_outproj_router_kernel, n_y=n_y)
    return pl.pallas_call(
        kern,
        grid=(t // tm,),
        in_specs=([pl.BlockSpec((tm, y.shape[1]), row) for y in ys]
                  + [pl.BlockSpec(w.shape, full) for w in ws]
                  + [pl.BlockSpec((tm, d), row), pl.BlockSpec((1, d), full), pl.BlockSpec((1, d), full),
                     pl.BlockSpec(rw_pad.shape, full)]),
        out_specs=[pl.BlockSpec((tm, d), row), pl.BlockSpec((tm, LANES), row)],
        out_shape=[jax.ShapeDtypeStruct((t, d), F32), jax.ShapeDtypeStruct((t, LANES), F32)],
        compiler_params=_cparams("parallel"),
        name="outproj_router",
    )(*ys, *ws, x2, ln_g, ln_b, rw_pad)


def _row_gather_copy(src_hbm, row, dst_vmem, slot, sem):
    return pltpu.make_async_copy(src_hbm.at[pl.ds(row, 1)], dst_vmem.at[pl.ds(slot, 1)], sem)


def _expert_kernel(te_ref, nt_ref, src_ref, h_hbm, wg_ref, wu_ref, wd_ref, o_ref, xbuf, sem):
    r = pl.program_id(0)
    tr = xbuf.shape[0]

    @pl.when(r < nt_ref[0])
    def _():
        base = r * tr

        def issue(i, c):
            _row_gather_copy(h_hbm, src_ref[base + i], xbuf, i, sem).start()
            return c

        lax.fori_loop(0, tr, issue, 0)
        pltpu.make_async_copy(h_hbm.at[pl.ds(0, tr)], xbuf, sem).wait()
        x = xbuf[...].astype(BF16)
        a = _silu(_dot(x, wg_ref[...])) * _dot(x, wu_ref[...])
        o_ref[...] = _dot(a.astype(BF16), wd_ref[...])

    @pl.when(r >= nt_ref[0])
    def _():
        o_ref[...] = jnp.zeros_like(o_ref)


def _expert_mlps(tile_expert, n_tiles, src_rows, h, wg, wu, wd):
    t, d = h.shape
    n_slots = src_rows.shape[0]
    tr = MOE_TR
    dff = wg.shape[2]
    grid_spec = pltpu.PrefetchScalarGridSpec(
        num_scalar_prefetch=3,
        grid=(n_slots // tr,),
        in_specs=[
            pl.BlockSpec(memory_space=pl.ANY),
            pl.BlockSpec((None, d, dff), lambda r, te, nt, src: (te[r], 0, 0)),
            pl.BlockSpec((None, d, dff), lambda r, te, nt, src: (te[r], 0, 0)),
            pl.BlockSpec((None, dff, d), lambda r, te, nt, src: (te[r], 0, 0)),
        ],
        out_specs=pl.BlockSpec((tr, d), lambda r, te, nt, src: (r, 0)),
        scratch_shapes=[pltpu.VMEM((tr, d), F32), pltpu.SemaphoreType.DMA],
    )
    return pl.pallas_call(
        _expert_kernel,
        grid_spec=grid_spec,
        out_shape=jax.ShapeDtypeStruct((n_slots, d), F32),
        compiler_params=_cparams("arbitrary"),
        name="expert_mlps",
    )(tile_expert, n_tiles, src_rows, h, wg, wu, wd)


def _combine_kernel(pos_ref, ys_hbm, h_ref, route_ref, p_ref, g_ref, b_ref, pg_ref, pp_ref,
                    o_ref, y1buf, y2buf, sem):
    i = pl.program_id(0)
    tm = h_ref.shape[0]
    base = i * tm

    def issue(r, c):
        _row_gather_copy(ys_hbm, pos_ref[2 * (base + r)], y1buf, r, sem).start()
        _row_gather_copy(ys_hbm, pos_ref[2 * (base + r) + 1], y2buf, r, sem).start()
        return c

    lax.fori_loop(0, tm, issue, 0)
    pltpu.make_async_copy(ys_hbm.at[pl.ds(0, tm)], y1buf, sem).wait()
    pltpu.make_async_copy(ys_hbm.at[pl.ds(0, tm)], y2buf, sem).wait()
    route = route_ref[...]
    g1 = route[:, N_EXPERTS + 2:N_EXPERTS + 3]
    g2 = route[:, N_EXPERTS + 3:N_EXPERTS + 4]
    ffn = g1 * y1buf[...] + g2 * y2buf[...]
    h2 = _layer_norm_rows(DEEPNORM_ALPHA * h_ref[...] + ffn, g_ref[...], b_ref[...])
    gate = _sigmoid(_dot(h2.astype(BF16), pg_ref[...]))
    pe = _dot(p_ref[...].astype(BF16), pp_ref[...])
    o_ref[...] = h2 + gate * pe


def _combine(pos, ys, h, route, p2, ln_g, ln_b, pg_bf, pp_bf):
    t, d = h.shape
    tm = min(COMB_TM, t)
    pdim = p2.shape[1]
    row = lambda i, pos: (i, 0)
    full = lambda i, pos: (0, 0)
    grid_spec = pltpu.PrefetchScalarGridSpec(
        num_scalar_prefetch=1,
        grid=(t // tm,),
        in_specs=[
            pl.BlockSpec(memory_space=pl.ANY),
            pl.BlockSpec((tm, d), row),
            pl.BlockSpec((tm, LANES), row),
            pl.BlockSpec((tm, pdim), row),
            pl.BlockSpec((1, d), full),
            pl.BlockSpec((1, d), full),
            pl.BlockSpec(pg_bf.shape, full),
            pl.BlockSpec(pp_bf.shape, full),
        ],
        out_specs=pl.BlockSpec((tm, d), row),
        scratch_shapes=[pltpu.VMEM((tm, d), F32), pltpu.VMEM((tm, d), F32), pltpu.SemaphoreType.DMA],
    )
    return pl.pallas_call(
        _combine_kernel,
        grid_spec=grid_spec,
        out_shape=jax.ShapeDtypeStruct((t, d), F32),
        compiler_params=_cparams("arbitrary"),
        name="moe_combine",
    )(pos, ys, h, route, p2, ln_g, ln_b, pg_bf, pp_bf)


def _rotary_tables(seq):
    half = RET_DK // 2
    inv = (np.float32(ROPE_BASE) ** (-np.arange(half, dtype=np.float32) / np.float32(half))).astype(np.float32)
    ang = (np.arange(seq, dtype=np.float32)[:, None] * inv[None, :]).astype(np.float32)
    cos = np.cos(ang.astype(np.float64))
    sin = np.sin(ang.astype(np.float64))
    cos_h = np.concatenate([cos, cos], axis=1)
    sin_h = np.concatenate([-sin, sin], axis=1)
    return (jnp.asarray(np.tile(cos_h, (1, RET_HEADS)), F32), jnp.asarray(np.tile(sin_h, (1, RET_HEADS)), F32))


def _retention_tables():
    c = RET_CHUNK
    h = np.arange(RET_HEADS, dtype=np.float64)
    log_g = np.log1p(-np.exp2(-5.0 - h))
    j = np.arange(c, dtype=np.float64)
    rel = j[:, None] - j[None, :]
    din = np.where(rel >= 0, np.exp(np.maximum(rel, 0.0)[None] * log_g[:, None, None]), 0.0)
    qd = np.exp((j + 1.0)[None] * log_g[:, None])
    kd = np.exp((c - 1.0 - j)[None] * log_g[:, None])
    cd = np.exp(c * log_g)
    qd = np.broadcast_to(qd[:, :, None], (RET_HEADS, c, LANES))
    kd = np.broadcast_to(kd[:, :, None], (RET_HEADS, c, LANES))
    cd = np.broadcast_to(cd[:, None, None], (RET_HEADS, 1, LANES))
    return tuple(jnp.asarray(a, F32) for a in (din, qd, kd, cd))


def _t5_bucket_np(dist):
    max_exact = REL_BUCKETS // 2
    d = np.maximum(dist, 1).astype(np.float32)
    large = max_exact + (np.log(d / np.float32(max_exact)) / np.float32(math.log(REL_MAX_DIST / max_exact))
                         * np.float32(REL_BUCKETS - max_exact)).astype(np.int32)
    large = np.minimum(large, REL_BUCKETS - 1)
    return np.where(dist < max_exact, dist, large)


def _diff_bias_tables(rel_bias, tq):
    r = np.arange(tq)
    dist0 = r[:, None] - r[None, :]
    b0 = _t5_bucket_np(np.maximum(dist0, 0))
    b1 = _t5_bucket_np(dist0 + tq)
    b2 = _t5_bucket_np(dist0 + 2 * tq)
    rb = rel_bias.astype(F32).T
    t0 = jnp.where(jnp.asarray(dist0 >= 0)[None], rb[:, b0], NEG_BIG)
    return jnp.stack([t0, rb[:, b1], rb[:, b2]], axis=1)


def _route_tables(route, tr):
    t = route.shape[0]
    e = route[:, N_EXPERTS:N_EXPERTS + 2].astype(I32).reshape(-1)
    onehot = (e[:, None] == jnp.arange(N_EXPERTS, dtype=I32)[None, :]).astype(I32)
    incl = jnp.cumsum(onehot, axis=0)
    rank = jnp.sum((incl - onehot) * onehot, axis=1)
    counts = incl[-1]
    padded = ((counts + tr - 1) // tr) * tr
    ends = jnp.cumsum(padded)
    offs = ends - padded
    pos = (offs[e] + rank).astype(I32)
    n_slots = 2 * t + N_EXPERTS * tr
    src = jnp.zeros((n_slots,), I32).at[pos].set(jnp.arange(2 * t, dtype=I32) // 2)
    n_tiles = (ends[-1] // tr).astype(I32)
    tile_start = jnp.arange(n_slots // tr, dtype=I32) * tr
    tile_expert = jnp.sum((tile_start[:, None] >= ends[None, :]).astype(I32), axis=1)
    last = jnp.sum((((n_tiles - 1) * tr) >= ends).astype(I32))
    tile_expert = jnp.minimum(tile_expert, last).astype(I32)
    return pos, src, tile_expert, n_tiles.reshape(1)


def kernel(x, p, rel_bias, router_w, even_w_in, even_w_out, even_lambda, even_diff_norm, even_ret_norm,
           odd_w_in, odd_b_forget, odd_w_out, ln_mix_g, ln_mix_b, ln_ffn_g, ln_ffn_b,
           moe_w_gate, moe_w_up, moe_w_down, ple_proj, ple_gate):
    b, s, d = x.shape
    t = b * s
    assert d == 1024 and p.shape[0] == DEPTH and even_w_in.shape[2] == 3072
    assert odd_w_in.shape[2] == 3 * d + FOX_HEADS and moe_w_gate.shape[1] == N_EXPERTS
    assert s % RET_CHUNK == 0 and s % min(ATT_TQ, s) == 0 and t % min(PROJ_TM, s) == 0

    cos_t, sin_t = _rotary_tables(s)
    ret_tabs = _retention_tables()
    bias_tab = _diff_bias_tables(rel_bias, min(ATT_TQ, s))
    rw_pad = jnp.zeros((d, LANES), F32).at[:, :N_EXPERTS].set(router_w.astype(F32))

    x2 = x.reshape(t, d)
    for i in range(DEPTH):
        j = i // 2
        if i % 2 == 0:
            lam_init = 0.8 - 0.6 * math.exp(-0.3 * i)
            qa, ka, va, qb, kb, vb, gb = _even_inproj(x2, even_w_in[j].astype(BF16), cos_t, sin_t, s)
            sh = lambda a: a.reshape(b, s, a.shape[1])
            ya = _diff_attention(sh(qa), sh(ka), sh(va), bias_tab, even_lambda[j].astype(F32),
                                 even_diff_norm[j].reshape(1, -1).astype(F32), lam_init)
            yb = _retention(sh(qb), sh(kb), sh(vb), sh(gb), ret_tabs, even_ret_norm[j].reshape(1, -1).astype(F32))
            w_out = even_w_out[j].astype(BF16)
            n_a = ya.shape[2]
            ys = [ya.reshape(t, -1), yb.reshape(t, -1)]
            ws = [w_out[:n_a], w_out[n_a:]]
        else:
            w_in = odd_w_in[j]
            wf = jnp.zeros((d, LANES), BF16).at[:, :FOX_HEADS].set(w_in[:, 3 * d:].astype(BF16))
            bfg = jnp.zeros((1, LANES), F32).at[0, :FOX_HEADS].set(odd_b_forget[j].astype(F32))
            q, k, v, cum = _odd_inproj(x2, w_in[:, :3 * d].astype(BF16), wf, bfg, s)
            cum3 = cum.reshape(b, s, LANES)
            cum_t = jnp.transpose(cum3[:, :, :FOX_HEADS], (0, 2, 1))
            y = _fox_attention(q.reshape(b, s, d), k.reshape(b, s, d), v.reshape(b, s, d), cum3, cum_t)
            ys = [y.reshape(t, d)]
            ws = [odd_w_out[j].astype(BF16)]
        h, route = _outproj_router(ys, ws, x2, ln_mix_g[i].reshape(1, d), ln_mix_b[i].reshape(1, d), rw_pad)
        pos, src, tile_expert, n_tiles = _route_tables(route, MOE_TR)
        rows = _expert_mlps(tile_expert, n_tiles, src, h, moe_w_gate[i].astype(BF16),
                            moe_w_up[i].astype(BF16), moe_w_down[i].astype(BF16))
        x2 = _combine(pos, rows, h, route, p[i].reshape(t, -1), ln_ffn_g[i].reshape(1, d),
                      ln_ffn_b[i].reshape(1, d), ple_gate[i].astype(BF16), ple_proj[i].astype(BF16))
    return x2.reshape(b, s, d)
```

```python
import functools
import math

import numpy as np
import jax
import jax.numpy as jnp
from jax import lax
from jax.experimental import pallas as pl
from jax.experimental.pallas import tpu as pltpu

F32 = jnp.float32
BF16 = jnp.bfloat16
I32 = jnp.int32

DIFF_HEADS = 4
DIFF_DK = 64
RET_HEADS = 4
RET_DK = 64
RET_DV = 128
RET_CHUNK = 128
FOX_HEADS = 16
FOX_DH = 64
REL_BUCKETS = 32
REL_MAX_DIST = 128
N_GROUPS = 4
EXPERTS_PER_GROUP = 4
N_EXPERTS = 16
DEPTH = 2
DEEPNORM_ALPHA = (2 * DEPTH) ** 0.25
LN_EPS = 1e-5
ROPE_BASE = 10000.0
NEG_BIG = -1e30
LOG2E = math.log2(math.e)

VMEM_LIMIT_BYTES = 48 * 1024 * 1024
LANES = 128

PROJ_TM = 512
ATT_TQ = 512
FOX_TQ = 512
ATT_RQ = LANES
ATT_LOOKAHEAD = 3
MOE_TR = 256
COMB_TM = 256


def _cparams(*sem):
    return pltpu.CompilerParams(dimension_semantics=sem, vmem_limit_bytes=VMEM_LIMIT_BYTES)


def _dot(a, b):
    return jnp.dot(a, b, preferred_element_type=F32)


def _dot_nt(a, b):
    return lax.dot_general(a, b, (((1,), (1,)), ((), ())), preferred_element_type=F32)


def _layer_norm_rows(z, g, b):
    mu = jnp.mean(z, axis=-1, keepdims=True)
    zc = z - mu
    var = jnp.mean(zc * zc, axis=-1, keepdims=True)
    return zc * lax.rsqrt(var + LN_EPS) * g + b


def _silu(x):
    return x * (1.0 / (1.0 + jnp.exp(-x)))


def _sigmoid(x):
    return 1.0 / (1.0 + jnp.exp(-x))


def _even_inproj_kernel(x_ref, w_ref, cos_ref, sin_ref,
                        qa_ref, ka_ref, va_ref, qb_ref, kb_ref, vb_ref, gb_ref):
    x = x_ref[...].astype(BF16)

    def mm(c0, c1):
        return _dot(x, w_ref[:, c0:c1])

    qa_ref[...] = (mm(0, 512) * (DIFF_DK ** -0.5 * LOG2E)).astype(BF16)
    ka_ref[...] = mm(512, 1024).astype(BF16)
    va_ref[...] = mm(1024, 1536).astype(BF16)
    qk = mm(1536, 2048)
    cos = cos_ref[...]
    sin = sin_ref[...]
    lane = lax.broadcasted_iota(I32, cos.shape, 1)
    first_half = (lane % RET_DK) < (RET_DK // 2)

    def rot(t):
        sw = jnp.where(first_half, pltpu.roll(t, t.shape[1] - RET_DK // 2, 1),
                       pltpu.roll(t, RET_DK // 2, 1))
        return t * cos + sw * sin

    qb_ref[...] = rot(qk[:, :256]).astype(BF16)
    kb_ref[...] = (rot(qk[:, 256:]) * (RET_DK ** -0.5)).astype(BF16)
    vb_ref[...] = mm(2048, 2560).astype(BF16)
    gb_ref[...] = mm(2560, 3072).astype(BF16)


def _even_inproj(x2, w_bf, cos_t, sin_t, seq):
    t, d = x2.shape
    tm = min(PROJ_TM, seq)
    nblk_s = seq // tm
    widths = (512, 512, 512, 256, 256, 512, 512)
    row = lambda i: (i, 0)
    return pl.pallas_call(
        _even_inproj_kernel,
        grid=(t // tm,),
        in_specs=[
            pl.BlockSpec((tm, d), row),
            pl.BlockSpec(w_bf.shape, lambda i: (0, 0)),
            pl.BlockSpec((tm, 256), lambda i: (i % nblk_s, 0)),
            pl.BlockSpec((tm, 256), lambda i: (i % nblk_s, 0)),
        ],
        out_specs=[pl.BlockSpec((tm, w), row) for w in widths],
        out_shape=[jax.ShapeDtypeStruct((t, w), BF16) for w in widths],
        compiler_params=_cparams("parallel"),
        name="even_inproj",
    )(x2, w_bf, cos_t, sin_t)


def _run_chains(chains, lookahead, scores, finish):
    pending = [scores(c) for c in chains[:lookahead]]
    for n, chain in enumerate(chains):
        if n + lookahead < len(chains):
            pending.append(scores(chains[n + lookahead]))
        finish(chain, pending.pop(0))


def _diff_attn_kernel(lam_ref, q_ref, k_ref, v_ref, bias_ref, g_ref, o_ref, qm_ref, m_ref, l_ref, acc_ref,
                      *, lam_init):
    tq = q_ref.shape[0]
    tk = tq
    rq = ATT_RQ
    nr = tq // rq
    i = pl.program_id(2)
    q = q_ref[...]
    lane = lax.broadcasted_iota(I32, q.shape, 1)
    zero = jnp.zeros_like(q)
    qm_ref[0] = jnp.where(lane < DIFF_DK, q, zero)
    qm_ref[1] = jnp.where(lane >= DIFF_DK, q, zero)
    m_ref[...] = jnp.full(m_ref.shape, NEG_BIG, F32)
    l_ref[...] = jnp.zeros(l_ref.shape, F32)
    acc_ref[...] = jnp.zeros(acc_ref.shape, F32)

    def run(ii):
        chains = [(j, r, c) for j in range(ii + 1) for r in range(nr) for c in range(2)]

        def n_keys(j, r):
            return (r + 1) * rq if j == ii else tk

        def scores(chain):
            j, r, c = chain
            k = k_ref[j * tk:j * tk + n_keys(j, r), :]
            return _dot_nt(qm_ref[c, r * rq:(r + 1) * rq, :], k)

        def finish(chain, s):
            j, r, c = chain
            rows = slice(r * rq, (r + 1) * rq)
            nk = n_keys(j, r)
            v = v_ref[j * tk:j * tk + nk, :]
            sc = []
            for kc in range(nk // rq):
                t = s[:, kc * rq:(kc + 1) * rq]
                back = (ii * nr + r) - (j * nr + kc)
                if back <= 1:
                    t = t + bias_ref[back]
                sc.append(t)
            mx = sc[0]
            for t in sc[1:]:
                mx = jnp.maximum(mx, t)
            m_old = m_ref[c, rows, :]
            m_new = jnp.maximum(m_old, jnp.max(mx, axis=-1, keepdims=True))
            alpha = jnp.exp2(m_old - m_new)
            ps = [jnp.exp2(t - m_new) for t in sc]
            psum = ps[0]
            for t in ps[1:]:
                psum = psum + t
            l_ref[c, rows, :] = alpha * l_ref[c, rows, :] + psum
            p = jnp.concatenate([t.astype(BF16) for t in ps], axis=1)
            acc_ref[c, rows, :] = alpha * acc_ref[c, rows, :] + _dot(p, v)
            m_ref[c, rows, :] = m_new

        _run_chains(chains, ATT_LOOKAHEAD, scores, finish)

    for ii in range(k_ref.shape[0] // tq):
        pl.when(i == ii)(functools.partial(run, ii))

    lp = lam_ref[...]
    lam = (jnp.exp(jnp.sum(lp[0:1, :] * lp[1:2, :], axis=-1, keepdims=True))
           - jnp.exp(jnp.sum(lp[2:3, :] * lp[3:4, :], axis=-1, keepdims=True)) + lam_init)
    l0 = jnp.sum(l_ref[0], axis=-1, keepdims=True)
    l1 = jnp.sum(l_ref[1], axis=-1, keepdims=True)
    o = acc_ref[0] / l0 - lam * (acc_ref[1] / l1)
    o = o * lax.rsqrt(jnp.mean(o * o, axis=-1, keepdims=True) + LN_EPS)
    o_ref[...] = (o * g_ref[...] * (1.0 - lam_init)).astype(BF16)


def _diff_attention(qa, ka, va, bias_tab, lam_params, diff_g, lam_init):
    b, s, _ = qa.shape
    tq = min(ATT_TQ, s)
    kern = functools.partial(_diff_attn_kernel, lam_init=lam_init)
    return pl.pallas_call(
        kern,
        grid=(b, DIFF_HEADS, s // tq),
        in_specs=[
            pl.BlockSpec(lam_params.shape, lambda bi, h, i: (0, 0)),
            pl.BlockSpec((None, tq, LANES), lambda bi, h, i: (bi, i, h)),
            pl.BlockSpec((None, s, LANES), lambda bi, h, i: (bi, 0, h)),
            pl.BlockSpec((None, s, LANES), lambda bi, h, i: (bi, 0, h)),
            pl.BlockSpec((None, 2, ATT_RQ, ATT_RQ), lambda bi, h, i: (h, 0, 0, 0)),
            pl.BlockSpec((1, LANES), lambda bi, h, i: (0, 0)),
        ],
        out_specs=pl.BlockSpec((None, tq, LANES), lambda bi, h, i: (bi, i, h)),
        out_shape=jax.ShapeDtypeStruct((b, s, DIFF_HEADS * LANES), BF16),
        scratch_shapes=[pltpu.VMEM((2, tq, LANES), BF16), pltpu.VMEM((2, tq, LANES), F32),
                        pltpu.VMEM((2, tq, LANES), F32), pltpu.VMEM((2, tq, LANES), F32)],
        compiler_params=_cparams("parallel", "parallel", "parallel"),
        name="diff_attention",
    )(lam_params, qa, ka, va, bias_tab, diff_g)


def _retention_kernel(q_ref, k_ref, v_ref, gate_ref, din_ref, qd_ref, kd_ref, cd_ref, g_ref, o_ref):
    s = q_ref.shape[0]
    c = RET_CHUNK
    par = pl.program_id(1) % 2
    lane = lax.broadcasted_iota(I32, (c, LANES), 1)
    own = (lane // RET_DK) == par
    din = din_ref[...]
    qd = qd_ref[...]
    kd = kd_ref[...]
    cd = cd_ref[...]
    g = g_ref[...]

    def chunk(n, state):
        r = pl.ds(pl.multiple_of(n * c, c), c)
        q = jnp.where(own, q_ref[r, :].astype(F32), 0.0)
        k = jnp.where(own, k_ref[r, :].astype(F32), 0.0)
        v = v_ref[r, :]
        scores = _dot_nt(q.astype(BF16), k.astype(BF16)) * din
        inner = _dot(scores.astype(BF16), v)
        cross = _dot((q * qd).astype(BF16), state.astype(BF16))
        kv = _dot((k * kd).T.astype(BF16), v)
        y = inner + cross
        mu = jnp.mean(y, axis=-1, keepdims=True)
        yc = y - mu
        var = jnp.mean(yc * yc, axis=-1, keepdims=True)
        yn = yc * lax.rsqrt(var + LN_EPS) * g
        gate = gate_ref[r, :].astype(F32)
        o_ref[r, :] = (_silu(gate) * yn).astype(BF16)
        return cd * state + kv

    lax.fori_loop(0, s // c, chunk, jnp.zeros((LANES, RET_DV), F32))


def _retention(qb, kb, vb, gb, tabs, ret_g):
    b, s, _ = qb.shape
    din, qd, kd, cd = tabs
    pair = lambda bi, h: (bi, 0, h // 2)
    head = lambda bi, h: (bi, 0, h)
    tab = lambda bi, h: (h, 0, 0)
    return pl.pallas_call(
        _retention_kernel,
        grid=(b, RET_HEADS),
        in_specs=[
            pl.BlockSpec((None, s, LANES), pair),
            pl.BlockSpec((None, s, LANES), pair),
            pl.BlockSpec((None, s, RET_DV), head),
            pl.BlockSpec((None, s, RET_DV), head),
            pl.BlockSpec((None, RET_CHUNK, RET_CHUNK), tab),
            pl.BlockSpec((None, RET_CHUNK, LANES), tab),
            pl.BlockSpec((None, RET_CHUNK, LANES), tab),
            pl.BlockSpec((None, 1, LANES), tab),
            pl.BlockSpec((1, RET_DV), lambda bi, h: (0, 0)),
        ],
        out_specs=pl.BlockSpec((None, s, RET_DV), head),
        out_shape=jax.ShapeDtypeStruct((b, s, RET_HEADS * RET_DV), BF16),
        compiler_params=_cparams("parallel", "parallel"),
        name="retention",
    )(qb, kb, vb, gb, din, qd, kd, cd, ret_g)


def _odd_inproj_kernel(x_ref, w_ref, wf_ref, bf_ref, q_ref, k_ref, v_ref, cum_ref, carry_ref, *, nblk_s):
    i = pl.program_id(0)
    x = x_ref[...].astype(BF16)
    d = q_ref.shape[1]
    q_ref[...] = (_dot(x, w_ref[:, 0:d]) * (FOX_DH ** -0.5 * LOG2E)).astype(BF16)
    k_ref[...] = _dot(x, w_ref[:, d:2 * d]).astype(BF16)
    v_ref[...] = _dot(x, w_ref[:, 2 * d:3 * d]).astype(BF16)
    z = _dot(x, wf_ref[...]) + bf_ref[...]
    c = jnp.minimum(z, 0.0) - jnp.log1p(jnp.exp(-jnp.abs(z)))
    tm = c.shape[0]
    row = lax.broadcasted_iota(I32, c.shape, 0)
    step = 1
    while step < tm:
        c = c + jnp.where(row >= step, pltpu.roll(c, step, 0), 0.0)
        step *= 2

    @pl.when(i % nblk_s == 0)
    def _():
        carry_ref[...] = jnp.zeros_like(carry_ref)

    c = c + carry_ref[...]
    cum_ref[...] = c * LOG2E
    carry_ref[...] = c[tm - 1:tm, :]


def _odd_inproj(x2, w_bf, wf_bf, bfg, seq):
    t, d = x2.shape
    tm = min(PROJ_TM, seq)
    nblk_s = seq // tm
    row = lambda i: (i, 0)
    kern = functools.partial(_odd_inproj_kernel, nblk_s=nblk_s)
    return pl.pallas_call(
        kern,
        grid=(t // tm,),
        in_specs=[
            pl.BlockSpec((tm, d), row),
            pl.BlockSpec(w_bf.shape, lambda i: (0, 0)),
            pl.BlockSpec(wf_bf.shape, lambda i: (0, 0)),
            pl.BlockSpec(bfg.shape, lambda i: (0, 0)),
        ],
        out_specs=[pl.BlockSpec((tm, d), row)] * 3 + [pl.BlockSpec((tm, LANES), row)],
        out_shape=[jax.ShapeDtypeStruct((t, d), BF16)] * 3 + [jax.ShapeDtypeStruct((t, LANES), F32)],
        scratch_shapes=[pltpu.VMEM((1, LANES), F32)],
        compiler_params=_cparams("arbitrary"),
        name="odd_inproj",
    )(x2, w_bf, wf_bf, bfg)


def _fox_attn_kernel(q_ref, k_ref, v_ref, cq_ref, ck_ref, o_ref, qm_ref, cqc_ref, m_ref, acc_ref):
    tq = q_ref.shape[0]
    tk = tq
    hp = pl.program_id(1)
    i = pl.program_id(2)
    q = q_ref[...]
    lane = lax.broadcasted_iota(I32, (tq, LANES), 1)
    cq_all = cq_ref[...]
    for par in range(2):
        own = (lane // FOX_DH) == par
        qm_ref[par] = jnp.where(own, q, jnp.zeros_like(q))
        cq = jnp.sum(jnp.where(lane == 2 * hp + par, cq_all, 0.0), axis=-1, keepdims=True)
        cqc_ref[par] = jnp.broadcast_to(cq, (tq, LANES))
    m_ref[...] = jnp.full(m_ref.shape, NEG_BIG, F32)
    acc_ref[...] = jnp.zeros(acc_ref.shape, F32)
    rq = ATT_RQ
    nr = tq // rq
    upper = (lax.broadcasted_iota(I32, (rq, rq), 1) > lax.broadcasted_iota(I32, (rq, rq), 0))

    def run(ii):
        chains = [(j, r, par) for j in range(ii + 1) for r in range(nr) for par in range(2)]
        v_aug = {}

        def n_keys(j, r):
            return (r + 1) * rq if j == ii else tk

        def scores(chain):
            j, r, par = chain
            k = k_ref[j * tk:j * tk + n_keys(j, r), :]
            return _dot_nt(qm_ref[par, r * rq:(r + 1) * rq, :], k)

        def finish(chain, s):
            j, r, par = chain
            rows = slice(r * rq, (r + 1) * rq)
            nk = n_keys(j, r)
            if (j, par) not in v_aug:
                v = v_ref[j * tk:(j + 1) * tk, :]
                lane_k = lax.broadcasted_iota(I32, v.shape, 1)
                v_aug[j, par] = jnp.where((lane_k // FOX_DH) == par, v, jnp.ones_like(v))
            ck = ck_ref[par:par + 1, j * tk:j * tk + nk]
            cq = cqc_ref[par, rows, :]
            sc = []
            for kc in range(nk // rq):
                t = s[:, kc * rq:(kc + 1) * rq] - ck[:, kc * rq:(kc + 1) * rq]
                if j == ii and kc == r:
                    t = jnp.where(upper, NEG_BIG, t)
                sc.append(t)
            mx = sc[0]
            for t in sc[1:]:
                mx = jnp.maximum(mx, t)
            m_old = m_ref[par, rows, :]
            m_new = jnp.maximum(m_old, jnp.max(mx, axis=-1, keepdims=True) + cq)
            alpha = jnp.exp2(m_old - m_new)
            shift = m_new - cq
            p = jnp.concatenate([jnp.exp2(t - shift).astype(BF16) for t in sc], axis=1)
            acc_ref[par, rows, :] = alpha * acc_ref[par, rows, :] + _dot(p, v_aug[j, par][:nk])
            m_ref[par, rows, :] = m_new

        _run_chains(chains, ATT_LOOKAHEAD, scores, finish)

    for ii in range(k_ref.shape[0] // tq):
        pl.when(i == ii)(functools.partial(run, ii))

    acc0 = acc_ref[0]
    acc1 = acc_ref[1]
    out0 = acc0 / acc0[:, FOX_DH:FOX_DH + 1]
    out1 = acc1 / acc1[:, 0:1]
    o_ref[...] = jnp.where(lane < FOX_DH, out0, out1).astype(BF16)


def _fox_attention(q, k, v, cum, cum_t):
    b, s, d = q.shape
    tq = min(FOX_TQ, s)
    npair = d // LANES
    return pl.pallas_call(
        _fox_attn_kernel,
        grid=(b, npair, s // tq),
        in_specs=[
            pl.BlockSpec((None, tq, LANES), lambda bi, h, i: (bi, i, h)),
            pl.BlockSpec((None, s, LANES), lambda bi, h, i: (bi, 0, h)),
            pl.BlockSpec((None, s, LANES), lambda bi, h, i: (bi, 0, h)),
            pl.BlockSpec((None, tq, LANES), lambda bi, h, i: (bi, i, 0)),
            pl.BlockSpec((None, None, 2, s), lambda bi, h, i: (bi, h, 0, 0)),
        ],
        out_specs=pl.BlockSpec((None, tq, LANES), lambda bi, h, i: (bi, i, h)),
        out_shape=jax.ShapeDtypeStruct((b, s, d), BF16),
        scratch_shapes=[pltpu.VMEM((2, tq, LANES), BF16), pltpu.VMEM((2, tq, LANES), F32),
                        pltpu.VMEM((2, tq, LANES), F32), pltpu.VMEM((2, tq, LANES), F32)],
        compiler_params=_cparams("parallel", "parallel", "parallel"),
        name="fox_attention",
    )(q, k, v, cum, cum_t)


def _outproj_router_kernel(*refs, n_y):
    y_refs = refs[:n_y]
    w_refs = refs[n_y:2 * n_y]
    x_ref, g_ref, b_ref, rw_ref, h_ref, route_ref = refs[2 * n_y:]
    mix = _dot(y_refs[0][...], w_refs[0][...])
    for yr, wr in zip(y_refs[1:], w_refs[1:]):
        mix = mix + _dot(yr[...], wr[...])
    h = _layer_norm_rows(DEEPNORM_ALPHA * x_ref[...] + mix, g_ref[...], b_ref[...])
    h_ref[...] = h

    rw = rw_ref[...]
    rw_hi = rw.astype(BF16)
    rw_lo = (rw - rw_hi.astype(F32)).astype(BF16)
    h_hi = h.astype(BF16)
    h_lo = (h - h_hi.astype(F32)).astype(BF16)
    logits = _dot(h_hi, rw_hi) + (_dot(h_lo, rw_hi) + _dot(h_hi, rw_lo))
    tm = logits.shape[0]
    lane = lax.broadcasted_iota(I32, (tm, LANES), 1)
    valid = lane < N_EXPERTS
    logits = jnp.where(valid, logits, NEG_BIG)
    mx = jnp.max(logits, axis=-1, keepdims=True)
    ex = jnp.exp(logits - mx)
    probs = ex / jnp.sum(ex, axis=-1, keepdims=True)
    grp = lane // EXPERTS_PER_GROUP

    def top2(vals):
        v1 = jnp.max(vals, axis=-1, keepdims=True)
        i1 = jnp.min(jnp.where(vals == v1, lane, LANES), axis=-1, keepdims=True)
        rest = jnp.where(lane == i1, -2.0, vals)
        v2 = jnp.max(rest, axis=-1, keepdims=True)
        i2 = jnp.min(jnp.where(rest == v2, lane, LANES), axis=-1, keepdims=True)
        return v1, i1, v2, i2

    best_score = None
    best = None
    for gi in range(N_GROUPS):
        v1, _, v2, _ = top2(jnp.where(grp == gi, probs, -1.0))
        score = v1 + v2
        if gi == 0:
            best_score, best = score, jnp.zeros_like(score, dtype=I32)
        else:
            better = score > best_score
            best = jnp.where(better, gi, best)
            best_score = jnp.where(better, score, best_score)
    v1, i1, v2, i2 = top2(jnp.where(grp == best, probs, -1.0))
    tot = v1 + v2
    g1 = v1 / tot
    g2 = v2 / tot
    route = jnp.where(lane == i1, g1, jnp.where(lane == i2, g2, 0.0))
    route = jnp.where(lane == N_EXPERTS, i1.astype(F32), route)
    route = jnp.where(lane == N_EXPERTS + 1, i2.astype(F32), route)
    route = jnp.where(lane == N_EXPERTS + 2, g1, route)
    route = jnp.where(lane == N_EXPERTS + 3, g2, route)
    route_ref[...] = route


def _outproj_router(ys, ws, x2, ln_g, ln_b, rw_pad):
    t, d = x2.shape
    tm = min(PROJ_TM, t)
    row = lambda i: (i, 0)
    full = lambda i: (0, 0)
    n_y = len(ys)
    kern = functools.partial(_outproj_router_kernel, n_y=n_y)
    return pl.pallas_call(
        kern,
        grid=(t // tm,),
        in_specs=([pl.BlockSpec((tm, y.shape[1]), row) for y in ys]
                  + [pl.BlockSpec(w.shape, full) for w in ws]
                  + [pl.BlockSpec((tm, d), row), pl.BlockSpec((1, d), full), pl.BlockSpec((1, d), full),
                     pl.BlockSpec(rw_pad.shape, full)]),
        out_specs=[pl.BlockSpec((tm, d), row), pl.BlockSpec((tm, LANES), row)],
        out_shape=[jax.ShapeDtypeStruct((t, d), F32), jax.ShapeDtypeStruct((t, LANES), F32)],
        compiler_params=_cparams("parallel"),
        name="outproj_router",
    )(*ys, *ws, x2, ln_g, ln_b, rw_pad)


def _row_gather_copy(src_hbm, row, dst_vmem, slot, sem):
    return pltpu.make_async_copy(src_hbm.at[pl.ds(row, 1)], dst_vmem.at[pl.ds(slot, 1)], sem)


def _expert_kernel(te_ref, nt_ref, src_ref, h_hbm, wg_ref, wu_ref, wd_ref, o_ref, xbuf, sem):
    r = pl.program_id(0)
    tr = xbuf.shape[0]

    @pl.when(r < nt_ref[0])
    def _():
        base = r * tr

        def issue(i, c):
            _row_gather_copy(h_hbm, src_ref[base + i], xbuf, i, sem).start()
            return c

        lax.fori_loop(0, tr, issue, 0)
        pltpu.make_async_copy(h_hbm.at[pl.ds(0, tr)], xbuf, sem).wait()
        x = xbuf[...].astype(BF16)
        a = _silu(_dot(x, wg_ref[...])) * _dot(x, wu_ref[...])
        o_ref[...] = _dot(a.astype(BF16), wd_ref[...])

    @pl.when(r >= nt_ref[0])
    def _():
        o_ref[...] = jnp.zeros_like(o_ref)


def _expert_mlps(tile_expert, n_tiles, src_rows, h, wg, wu, wd):
    t, d = h.shape
    n_slots = src_rows.shape[0]
    tr = MOE_TR
    dff = wg.shape[2]
    grid_spec = pltpu.PrefetchScalarGridSpec(
        num_scalar_prefetch=3,
        grid=(n_slots // tr,),
        in_specs=[
            pl.BlockSpec(memory_space=pl.ANY),
            pl.BlockSpec((None, d, dff), lambda r, te, nt, src: (te[r], 0, 0)),
            pl.BlockSpec((None, d, dff), lambda r, te, nt, src: (te[r], 0, 0)),
            pl.BlockSpec((None, dff, d), lambda r, te, nt, src: (te[r], 0, 0)),
        ],
        out_specs=pl.BlockSpec((tr, d), lambda r, te, nt, src: (r, 0)),
        scratch_shapes=[pltpu.VMEM((tr, d), F32), pltpu.SemaphoreType.DMA],
    )
    return pl.pallas_call(
        _expert_kernel,
        grid_spec=grid_spec,
        out_shape=jax.ShapeDtypeStruct((n_slots, d), F32),
        compiler_params=_cparams("arbitrary"),
        name="expert_mlps",
    )(tile_expert, n_tiles, src_rows, h, wg, wu, wd)


def _combine_kernel(pos_ref, ys_hbm, h_ref, route_ref, p_ref, g_ref, b_ref, pg_ref, pp_ref,
                    o_ref, y1buf, y2buf, sem):
    i = pl.program_id(0)
    tm = h_ref.shape[0]
    base = i * tm

    def issue(r, c):
        _row_gather_copy(ys_hbm, pos_ref[2 * (base + r)], y1buf, r, sem).start()
        _row_gather_copy(ys_hbm, pos_ref[2 * (base + r) + 1], y2buf, r, sem).start()
        return c

    lax.fori_loop(0, tm, issue, 0)
    pltpu.make_async_copy(ys_hbm.at[pl.ds(0, tm)], y1buf, sem).wait()
    pltpu.make_async_copy(ys_hbm.at[pl.ds(0, tm)], y2buf, sem).wait()
    route = route_ref[...]
    g1 = route[:, N_EXPERTS + 2:N_EXPERTS + 3]
    g2 = route[:, N_EXPERTS + 3:N_EXPERTS + 4]
    ffn = g1 * y1buf[...] + g2 * y2buf[...]
    h2 = _layer_norm_rows(DEEPNORM_ALPHA * h_ref[...] + ffn, g_ref[...], b_ref[...])
    gate = _sigmoid(_dot(h2.astype(BF16), pg_ref[...]))
    pe = _dot(p_ref[...].astype(BF16), pp_ref[...])
    o_ref[...] = h2 + gate * pe


def _combine(pos, ys, h, route, p2, ln_g, ln_b, pg_bf, pp_bf):
    t, d = h.shape
    tm = min(COMB_TM, t)
    pdim = p2.shape[1]
    row = lambda i, pos: (i, 0)
    full = lambda i, pos: (0, 0)
    grid_spec = pltpu.PrefetchScalarGridSpec(
        num_scalar_prefetch=1,
        grid=(t // tm,),
        in_specs=[
            pl.BlockSpec(memory_space=pl.ANY),
            pl.BlockSpec((tm, d), row),
            pl.BlockSpec((tm, LANES), row),
            pl.BlockSpec((tm, pdim), row),
            pl.BlockSpec((1, d), full),
            pl.BlockSpec((1, d), full),
            pl.BlockSpec(pg_bf.shape, full),
            pl.BlockSpec(pp_bf.shape, full),
        ],
        out_specs=pl.BlockSpec((tm, d), row),
        scratch_shapes=[pltpu.VMEM((tm, d), F32), pltpu.VMEM((tm, d), F32), pltpu.SemaphoreType.DMA],
    )
    return pl.pallas_call(
        _combine_kernel,
        grid_spec=grid_spec,
        out_shape=jax.ShapeDtypeStruct((t, d), F32),
        compiler_params=_cparams("arbitrary"),
        name="moe_combine",
    )(pos, ys, h, route, p2, ln_g, ln_b, pg_bf, pp_bf)


def _rotary_tables(seq):
    half = RET_DK // 2
    inv = (np.float32(ROPE_BASE) ** (-np.arange(half, dtype=np.float32) / np.float32(half))).astype(np.float32)
    ang = (np.arange(seq, dtype=np.float32)[:, None] * inv[None, :]).astype(np.float32)
    cos = np.cos(ang.astype(np.float64))
    sin = np.sin(ang.astype(np.float64))
    cos_h = np.concatenate([cos, cos], axis=1)
    sin_h = np.concatenate([-sin, sin], axis=1)
    return (jnp.asarray(np.tile(cos_h, (1, RET_HEADS)), F32), jnp.asarray(np.tile(sin_h, (1, RET_HEADS)), F32))


def _retention_tables():
    c = RET_CHUNK
    h = np.arange(RET_HEADS, dtype=np.float64)
    log_g = np.log1p(-np.exp2(-5.0 - h))
    j = np.arange(c, dtype=np.float64)
    rel = j[:, None] - j[None, :]
    din = np.where(rel >= 0, np.exp(np.maximum(rel, 0.0)[None] * log_g[:, None, None]), 0.0)
    qd = np.exp((j + 1.0)[None] * log_g[:, None])
    kd = np.exp((c - 1.0 - j)[None] * log_g[:, None])
    cd = np.exp(c * log_g)
    qd = np.broadcast_to(qd[:, :, None], (RET_HEADS, c, LANES))
    kd = np.broadcast_to(kd[:, :, None], (RET_HEADS, c, LANES))
    cd = np.broadcast_to(cd[:, None, None], (RET_HEADS, 1, LANES))
    return tuple(jnp.asarray(a, F32) for a in (din, qd, kd, cd))


def _t5_bucket_np(dist):
    max_exact = REL_BUCKETS // 2
    d = np.maximum(dist, 1).astype(np.float32)
    large = max_exact + (np.log(d / np.float32(max_exact)) / np.float32(math.log(REL_MAX_DIST / max_exact))
                         * np.float32(REL_BUCKETS - max_exact)).astype(np.int32)
    large = np.minimum(large, REL_BUCKETS - 1)
    return np.where(dist < max_exact, dist, large)


def _diff_bias_tables(rel_bias, seq):
    c = ATT_RQ
    r = np.arange(c)
    dist0 = r[:, None] - r[None, :]
    far = REL_BUCKETS - 1
    assert np.all(_t5_bucket_np(np.arange(c + 1, max(seq, 2 * c))) == far)
    bidx = np.stack([_t5_bucket_np(np.maximum(dist0, 0)), _t5_bucket_np(dist0 + c)])
    rb = rel_bias.astype(F32).T
    shifted = (rb - rb[:, far:far + 1]) * LOG2E
    bidx = jnp.asarray(bidx, I32)[None]
    tab = jnp.zeros((rb.shape[0], 2, c, c), F32)
    for bucket in range(REL_BUCKETS - 1):
        tab = jnp.where(bidx == bucket, shifted[:, bucket][:, None, None, None], tab)
    causal = jnp.asarray(np.stack([dist0 >= 0, np.ones_like(dist0, bool)]))[None]
    return jnp.where(causal, tab, NEG_BIG)


def _route_tables(route, tr):
    t = route.shape[0]
    e = route[:, N_EXPERTS:N_EXPERTS + 2].astype(I32).reshape(-1)
    onehot = (e[:, None] == jnp.arange(N_EXPERTS, dtype=I32)[None, :]).astype(I32)
    incl = jnp.cumsum(onehot, axis=0)
    rank = jnp.sum((incl - onehot) * onehot, axis=1)
    counts = incl[-1]
    padded = ((counts + tr - 1) // tr) * tr
    ends = jnp.cumsum(padded)
    offs = ends - padded
    pos = (offs[e] + rank).astype(I32)
    n_slots = 2 * t + N_EXPERTS * tr
    src = jnp.zeros((n_slots,), I32).at[pos].set(jnp.arange(2 * t, dtype=I32) // 2)
    n_tiles = (ends[-1] // tr).astype(I32)
    tile_start = jnp.arange(n_slots // tr, dtype=I32) * tr
    tile_expert = jnp.sum((tile_start[:, None] >= ends[None, :]).astype(I32), axis=1)
    last = jnp.sum((((n_tiles - 1) * tr) >= ends).astype(I32))
    tile_expert = jnp.minimum(tile_expert, last).astype(I32)
    return pos, src, tile_expert, n_tiles.reshape(1)


def kernel(x, p, rel_bias, router_w, even_w_in, even_w_out, even_lambda, even_diff_norm, even_ret_norm,
           odd_w_in, odd_b_forget, odd_w_out, ln_mix_g, ln_mix_b, ln_ffn_g, ln_ffn_b,
           moe_w_gate, moe_w_up, moe_w_down, ple_proj, ple_gate):
    b, s, d = x.shape
    t = b * s
    assert d == 1024 and p.shape[0] == DEPTH and even_w_in.shape[2] == 3072
    assert odd_w_in.shape[2] == 3 * d + FOX_HEADS and moe_w_gate.shape[1] == N_EXPERTS
    assert s % RET_CHUNK == 0 and s % min(ATT_TQ, s) == 0 and s % min(FOX_TQ, s) == 0
    assert t % min(PROJ_TM, s) == 0

    cos_t, sin_t = _rotary_tables(s)
    ret_tabs = _retention_tables()
    bias_tab = _diff_bias_tables(rel_bias, s)
    rw_pad = jnp.zeros((d, LANES), F32).at[:, :N_EXPERTS].set(router_w.astype(F32))

    x2 = x.reshape(t, d)
    for i in range(DEPTH):
        j = i // 2
        if i % 2 == 0:
            lam_init = 0.8 - 0.6 * math.exp(-0.3 * i)
            qa, ka, va, qb, kb, vb, gb = _even_inproj(x2, even_w_in[j].astype(BF16), cos_t, sin_t, s)
            sh = lambda a: a.reshape(b, s, a.shape[1])
            ya = _diff_attention(sh(qa), sh(ka), sh(va), bias_tab, even_lambda[j].astype(F32),
                                 even_diff_norm[j].reshape(1, -1).astype(F32), lam_init)
            yb = _retention(sh(qb), sh(kb), sh(vb), sh(gb), ret_tabs, even_ret_norm[j].reshape(1, -1).astype(F32))
            w_out = even_w_out[j].astype(BF16)
            n_a = ya.shape[2]
            ys = [ya.reshape(t, -1), yb.reshape(t, -1)]
            ws = [w_out[:n_a], w_out[n_a:]]
        else:
            w_in = odd_w_in[j]
            wf = jnp.zeros((d, LANES), BF16).at[:, :FOX_HEADS].set(w_in[:, 3 * d:].astype(BF16))
            bfg = jnp.zeros((1, LANES), F32).at[0, :FOX_HEADS].set(odd_b_forget[j].astype(F32))
            q, k, v, cum = _odd_inproj(x2, w_in[:, :3 * d].astype(BF16), wf, bfg, s)
            cum3 = cum.reshape(b, s, LANES)
            cum_t = jnp.transpose(cum3[:, :, :FOX_HEADS], (0, 2, 1)).reshape(b, FOX_HEADS // 2, 2, s)
            y = _fox_attention(q.reshape(b, s, d), k.reshape(b, s, d), v.reshape(b, s, d), cum3, cum_t)
            ys = [y.reshape(t, d)]
            ws = [odd_w_out[j].astype(BF16)]
        h, route = _outproj_router(ys, ws, x2, ln_mix_g[i].reshape(1, d), ln_mix_b[i].reshape(1, d), rw_pad)
        pos, src, tile_expert, n_tiles = _route_tables(route, MOE_TR)
        rows = _expert_mlps(tile_expert, n_tiles, src, h, moe_w_gate[i].astype(BF16),
                            moe_w_up[i].astype(BF16), moe_w_down[i].astype(BF16))
        x2 = _combine(pos, rows, h, route, p[i].reshape(t, -1), ln_ffn_g[i].reshape(1, d),
                      ln_ffn_b[i].reshape(1, d), ple_gate[i].astype(BF16), ple_proj[i].astype(BF16))
    return x2.reshape(b, s, d)
```

```python
import functools
import math

import numpy as np
import jax
import jax.numpy as jnp
from jax import lax
from jax.experimental import pallas as pl
from jax.experimental.pallas import tpu as pltpu

F32 = jnp.float32
BF16 = jnp.bfloat16
I32 = jnp.int32

DIFF_HEADS = 4
DIFF_DK = 64
RET_HEADS = 4
RET_DK = 64
RET_DV = 128
RET_CHUNK = 128
FOX_HEADS = 16
FOX_DH = 64
REL_BUCKETS = 32
REL_MAX_DIST = 128
N_GROUPS = 4
EXPERTS_PER_GROUP = 4
N_EXPERTS = 16
DEPTH = 2
DEEPNORM_ALPHA = (2 * DEPTH) ** 0.25
LN_EPS = 1e-5
ROPE_BASE = 10000.0
NEG_BIG = -1e30
LOG2E = math.log2(math.e)

VMEM_LIMIT_BYTES = 48 * 1024 * 1024
LANES = 128

PROJ_TM = 512
ATT_TQ = 512
FOX_TQ = 512
ATT_RQ = LANES
ATT_LOOKAHEAD = 3
MOE_TR = 256
COMB_TM = 256


def _cparams(*sem):
    return pltpu.CompilerParams(dimension_semantics=sem, vmem_limit_bytes=VMEM_LIMIT_BYTES)


def _dot(a, b):
    return jnp.dot(a, b, preferred_element_type=F32)


def _dot_nt(a, b):
    return lax.dot_general(a, b, (((1,), (1,)), ((), ())), preferred_element_type=F32)


def _layer_norm_rows(z, g, b):
    mu = jnp.mean(z, axis=-1, keepdims=True)
    zc = z - mu
    var = jnp.mean(zc * zc, axis=-1, keepdims=True)
    return zc * lax.rsqrt(var + LN_EPS) * g + b


def _silu(x):
    return x * (1.0 / (1.0 + jnp.exp(-x)))


def _sigmoid(x):
    return 1.0 / (1.0 + jnp.exp(-x))


def _even_inproj_kernel(x_ref, w_ref, cos_ref, sin_ref,
                        qa_ref, ka_ref, va_ref, qb_ref, kb_ref, vb_ref, gb_ref):
    x = x_ref[...].astype(BF16)

    def mm(c0, c1):
        return _dot(x, w_ref[:, c0:c1])

    qa_ref[...] = (mm(0, 512) * (DIFF_DK ** -0.5 * LOG2E)).astype(BF16)
    ka_ref[...] = mm(512, 1024).astype(BF16)
    va_ref[...] = mm(1024, 1536).astype(BF16)
    qk = mm(1536, 2048)
    cos = cos_ref[...]
    sin = sin_ref[...]
    lane = lax.broadcasted_iota(I32, cos.shape, 1)
    first_half = (lane % RET_DK) < (RET_DK // 2)

    def rot(t):
        sw = jnp.where(first_half, pltpu.roll(t, t.shape[1] - RET_DK // 2, 1),
                       pltpu.roll(t, RET_DK // 2, 1))
        return t * cos + sw * sin

    qb_ref[...] = rot(qk[:, :256]).astype(BF16)
    kb_ref[...] = (rot(qk[:, 256:]) * (RET_DK ** -0.5)).astype(BF16)
    vb_ref[...] = mm(2048, 2560).astype(BF16)
    gb_ref[...] = mm(2560, 3072).astype(BF16)


def _even_inproj(x2, w_bf, cos_t, sin_t, seq):
    t, d = x2.shape
    tm = min(PROJ_TM, seq)
    nblk_s = seq // tm
    widths = (512, 512, 512, 256, 256, 512, 512)
    row = lambda i: (i, 0)
    return pl.pallas_call(
        _even_inproj_kernel,
        grid=(t // tm,),
        in_specs=[
            pl.BlockSpec((tm, d), row),
            pl.BlockSpec(w_bf.shape, lambda i: (0, 0)),
            pl.BlockSpec((tm, 256), lambda i: (i % nblk_s, 0)),
            pl.BlockSpec((tm, 256), lambda i: (i % nblk_s, 0)),
        ],
        out_specs=[pl.BlockSpec((tm, w), row) for w in widths],
        out_shape=[jax.ShapeDtypeStruct((t, w), BF16) for w in widths],
        compiler_params=_cparams("parallel"),
        name="even_inproj",
    )(x2, w_bf, cos_t, sin_t)


def _run_chains(chains, lookahead, scores, finish):
    pending = [scores(c) for c in chains[:lookahead]]
    for n, chain in enumerate(chains):
        if n + lookahead < len(chains):
            pending.append(scores(chains[n + lookahead]))
        finish(chain, pending.pop(0))


def _diff_attn_kernel(lam_ref, q_ref, k_ref, v_ref, bias_ref, g_ref, o_ref, qm_ref, m_ref, l_ref, acc_ref,
                      *, lam_init):
    tq = q_ref.shape[0]
    tk = tq
    rq = ATT_RQ
    nr = tq // rq
    i = pl.program_id(2)
    q = q_ref[...]
    lane = lax.broadcasted_iota(I32, q.shape, 1)
    zero = jnp.zeros_like(q)
    qm_ref[0] = jnp.where(lane < DIFF_DK, q, zero)
    qm_ref[1] = jnp.where(lane >= DIFF_DK, q, zero)
    m_ref[...] = jnp.full(m_ref.shape, NEG_BIG, F32)
    l_ref[...] = jnp.zeros(l_ref.shape, F32)
    acc_ref[...] = jnp.zeros(acc_ref.shape, F32)

    def run(ii):
        chains = [(j, r, c) for j in range(ii + 1) for r in range(nr) for c in range(2)]

        def n_keys(j, r):
            return (r + 1) * rq if j == ii else tk

        def scores(chain):
            j, r, c = chain
            k = k_ref[j * tk:j * tk + n_keys(j, r), :]
            return _dot_nt(qm_ref[c, r * rq:(r + 1) * rq, :], k)

        def finish(chain, s):
            j, r, c = chain
            rows = slice(r * rq, (r + 1) * rq)
            nk = n_keys(j, r)
            v = v_ref[j * tk:j * tk + nk, :]
            sc = []
            for kc in range(nk // rq):
                t = s[:, kc * rq:(kc + 1) * rq]
                back = (ii * nr + r) - (j * nr + kc)
                if back <= 1:
                    t = t + bias_ref[back]
                sc.append(t)
            mx = sc[0]
            for t in sc[1:]:
                mx = jnp.maximum(mx, t)
            m_old = m_ref[c, rows, :]
            m_new = jnp.maximum(m_old, jnp.max(mx, axis=-1, keepdims=True))
            alpha = jnp.exp2(m_old - m_new)
            ps = [jnp.exp2(t - m_new) for t in sc]
            psum = ps[0]
            for t in ps[1:]:
                psum = psum + t
            l_ref[c, rows, :] = alpha * l_ref[c, rows, :] + psum
            p = jnp.concatenate([t.astype(BF16) for t in ps], axis=1)
            acc_ref[c, rows, :] = alpha * acc_ref[c, rows, :] + _dot(p, v)
            m_ref[c, rows, :] = m_new

        _run_chains(chains, ATT_LOOKAHEAD, scores, finish)

    for ii in range(k_ref.shape[0] // tq):
        pl.when(i == ii)(functools.partial(run, ii))

    lp = lam_ref[...]
    lam = (jnp.exp(jnp.sum(lp[0:1, :] * lp[1:2, :], axis=-1, keepdims=True))
           - jnp.exp(jnp.sum(lp[2:3, :] * lp[3:4, :], axis=-1, keepdims=True)) + lam_init)
    l0 = jnp.sum(l_ref[0], axis=-1, keepdims=True)
    l1 = jnp.sum(l_ref[1], axis=-1, keepdims=True)
    o = acc_ref[0] / l0 - lam * (acc_ref[1] / l1)
    o = o * lax.rsqrt(jnp.mean(o * o, axis=-1, keepdims=True) + LN_EPS)
    o_ref[...] = (o * g_ref[...] * (1.0 - lam_init)).astype(BF16)


def _diff_attention(qa, ka, va, bias_tab, lam_params, diff_g, lam_init):
    b, s, _ = qa.shape
    tq = min(ATT_TQ, s)
    kern = functools.partial(_diff_attn_kernel, lam_init=lam_init)
    return pl.pallas_call(
        kern,
        grid=(b, DIFF_HEADS, s // tq),
        in_specs=[
            pl.BlockSpec(lam_params.shape, lambda bi, h, i: (0, 0)),
            pl.BlockSpec((None, tq, LANES), lambda bi, h, i: (bi, i, h)),
            pl.BlockSpec((None, s, LANES), lambda bi, h, i: (bi, 0, h)),
            pl.BlockSpec((None, s, LANES), lambda bi, h, i: (bi, 0, h)),
            pl.BlockSpec((None, 2, ATT_RQ, ATT_RQ), lambda bi, h, i: (h, 0, 0, 0)),
            pl.BlockSpec((1, LANES), lambda bi, h, i: (0, 0)),
        ],
        out_specs=pl.BlockSpec((None, tq, LANES), lambda bi, h, i: (bi, i, h)),
        out_shape=jax.ShapeDtypeStruct((b, s, DIFF_HEADS * LANES), BF16),
        scratch_shapes=[pltpu.VMEM((2, tq, LANES), BF16), pltpu.VMEM((2, tq, LANES), F32),
                        pltpu.VMEM((2, tq, LANES), F32), pltpu.VMEM((2, tq, LANES), F32)],
        compiler_params=_cparams("parallel", "parallel", "parallel"),
        name="diff_attention",
    )(lam_params, qa, ka, va, bias_tab, diff_g)


def _retention_kernel(q_ref, k_ref, v_ref, gate_ref, din_ref, qd_ref, kd_ref, cd_ref, g_ref, o_ref):
    s = q_ref.shape[0]
    c = RET_CHUNK
    par = pl.program_id(1) % 2
    lane = lax.broadcasted_iota(I32, (c, LANES), 1)
    own = (lane // RET_DK) == par
    din = din_ref[...]
    qd = qd_ref[...]
    kd = kd_ref[...]
    cd = cd_ref[...]
    g = g_ref[...]

    def chunk(n, state):
        r = pl.ds(pl.multiple_of(n * c, c), c)
        q = jnp.where(own, q_ref[r, :].astype(F32), 0.0)
        k = jnp.where(own, k_ref[r, :].astype(F32), 0.0)
        v = v_ref[r, :]
        scores = _dot_nt(q.astype(BF16), k.astype(BF16)) * din
        inner = _dot(scores.astype(BF16), v)
        cross = _dot((q * qd).astype(BF16), state.astype(BF16))
        kv = _dot((k * kd).T.astype(BF16), v)
        y = inner + cross
        mu = jnp.mean(y, axis=-1, keepdims=True)
        yc = y - mu
        var = jnp.mean(yc * yc, axis=-1, keepdims=True)
        yn = yc * lax.rsqrt(var + LN_EPS) * g
        gate = gate_ref[r, :].astype(F32)
        o_ref[r, :] = (_silu(gate) * yn).astype(BF16)
        return cd * state + kv

    lax.fori_loop(0, s // c, chunk, jnp.zeros((LANES, RET_DV), F32))


def _retention(qb, kb, vb, gb, tabs, ret_g):
    b, s, _ = qb.shape
    din, qd, kd, cd = tabs
    pair = lambda bi, h: (bi, 0, h // 2)
    head = lambda bi, h: (bi, 0, h)
    tab = lambda bi, h: (h, 0, 0)
    return pl.pallas_call(
        _retention_kernel,
        grid=(b, RET_HEADS),
        in_specs=[
            pl.BlockSpec((None, s, LANES), pair),
            pl.BlockSpec((None, s, LANES), pair),
            pl.BlockSpec((None, s, RET_DV), head),
            pl.BlockSpec((None, s, RET_DV), head),
            pl.BlockSpec((None, RET_CHUNK, RET_CHUNK), tab),
            pl.BlockSpec((None, RET_CHUNK, LANES), tab),
            pl.BlockSpec((None, RET_CHUNK, LANES), tab),
            pl.BlockSpec((None, 1, LANES), tab),
            pl.BlockSpec((1, RET_DV), lambda bi, h: (0, 0)),
        ],
        out_specs=pl.BlockSpec((None, s, RET_DV), head),
        out_shape=jax.ShapeDtypeStruct((b, s, RET_HEADS * RET_DV), BF16),
        compiler_params=_cparams("parallel", "parallel"),
        name="retention",
    )(qb, kb, vb, gb, din, qd, kd, cd, ret_g)


def _odd_inproj_kernel(x_ref, w_ref, wf_ref, bf_ref, q_ref, k_ref, v_ref, cum_ref, carry_ref, *, nblk_s):
    i = pl.program_id(0)
    x = x_ref[...].astype(BF16)
    d = q_ref.shape[1]
    q_ref[...] = (_dot(x, w_ref[:, 0:d]) * (FOX_DH ** -0.5 * LOG2E)).astype(BF16)
    k_ref[...] = _dot(x, w_ref[:, d:2 * d]).astype(BF16)
    v_ref[...] = _dot(x, w_ref[:, 2 * d:3 * d]).astype(BF16)
    z = _dot(x, wf_ref[...]) + bf_ref[...]
    c = jnp.minimum(z, 0.0) - jnp.log1p(jnp.exp(-jnp.abs(z)))
    tm = c.shape[0]
    row = lax.broadcasted_iota(I32, c.shape, 0)
    step = 1
    while step < tm:
        c = c + jnp.where(row >= step, pltpu.roll(c, step, 0), 0.0)
        step *= 2

    @pl.when(i % nblk_s == 0)
    def _():
        carry_ref[...] = jnp.zeros_like(carry_ref)

    c = c + carry_ref[...]
    cum_ref[...] = c * LOG2E
    carry_ref[...] = c[tm - 1:tm, :]


def _odd_inproj(x2, w_bf, wf_bf, bfg, seq):
    t, d = x2.shape
    tm = min(PROJ_TM, seq)
    nblk_s = seq // tm
    row = lambda i: (i, 0)
    kern = functools.partial(_odd_inproj_kernel, nblk_s=nblk_s)
    return pl.pallas_call(
        kern,
        grid=(t // tm,),
        in_specs=[
            pl.BlockSpec((tm, d), row),
            pl.BlockSpec(w_bf.shape, lambda i: (0, 0)),
            pl.BlockSpec(wf_bf.shape, lambda i: (0, 0)),
            pl.BlockSpec(bfg.shape, lambda i: (0, 0)),
        ],
        out_specs=[pl.BlockSpec((tm, d), row)] * 3 + [pl.BlockSpec((tm, LANES), row)],
        out_shape=[jax.ShapeDtypeStruct((t, d), BF16)] * 3 + [jax.ShapeDtypeStruct((t, LANES), F32)],
        scratch_shapes=[pltpu.VMEM((1, LANES), F32)],
        compiler_params=_cparams("arbitrary"),
        name="odd_inproj",
    )(x2, w_bf, wf_bf, bfg)


def _fox_attn_kernel(q_ref, k_ref, v_ref, cq_ref, ck_ref, o_ref, qm_ref, cqc_ref, m_ref, acc_ref):
    tq = q_ref.shape[0]
    tk = tq
    hp = pl.program_id(1)
    i = pl.program_id(2)
    q = q_ref[...]
    lane = lax.broadcasted_iota(I32, (tq, LANES), 1)
    cq_all = cq_ref[...]
    for par in range(2):
        own = (lane // FOX_DH) == par
        qm_ref[par] = jnp.where(own, q, jnp.zeros_like(q))
        cq = jnp.sum(jnp.where(lane == 2 * hp + par, cq_all, 0.0), axis=-1, keepdims=True)
        cqc_ref[par] = jnp.broadcast_to(cq, (tq, LANES))
    m_ref[...] = jnp.full(m_ref.shape, NEG_BIG, F32)
    acc_ref[...] = jnp.zeros(acc_ref.shape, F32)
    rq = ATT_RQ
    nr = tq // rq
    upper = (lax.broadcasted_iota(I32, (rq, rq), 1) > lax.broadcasted_iota(I32, (rq, rq), 0))

    def run(ii):
        chains = [(j, r, par) for j in range(ii + 1) for r in range(nr) for par in range(2)]
        v_aug = {}

        def n_keys(j, r):
            return (r + 1) * rq if j == ii else tk

        def scores(chain):
            j, r, par = chain
            k = k_ref[j * tk:j * tk + n_keys(j, r), :]
            return _dot_nt(qm_ref[par, r * rq:(r + 1) * rq, :], k)

        def finish(chain, s):
            j, r, par = chain
            rows = slice(r * rq, (r + 1) * rq)
            nk = n_keys(j, r)
            if (j, par) not in v_aug:
                v = v_ref[j * tk:(j + 1) * tk, :]
                lane_k = lax.broadcasted_iota(I32, v.shape, 1)
                v_aug[j, par] = jnp.where((lane_k // FOX_DH) == par, v, jnp.ones_like(v))
            ck = ck_ref[par:par + 1, j * tk:j * tk + nk]
            cq = cqc_ref[par, rows, :]
            sc = []
            for kc in range(nk // rq):
                t = s[:, kc * rq:(kc + 1) * rq] - ck[:, kc * rq:(kc + 1) * rq]
                if j == ii and kc == r:
                    t = jnp.where(upper, NEG_BIG, t)
                sc.append(t)
            mx = sc[0]
            for t in sc[1:]:
                mx = jnp.maximum(mx, t)
            m_old = m_ref[par, rows, :]
            m_new = jnp.maximum(m_old, jnp.max(mx, axis=-1, keepdims=True) + cq)
            alpha = jnp.exp2(m_old - m_new)
            shift = m_new - cq
            p = jnp.concatenate([jnp.exp2(t - shift).astype(BF16) for t in sc], axis=1)
            acc_ref[par, rows, :] = alpha * acc_ref[par, rows, :] + _dot(p, v_aug[j, par][:nk])
            m_ref[par, rows, :] = m_new

        _run_chains(chains, ATT_LOOKAHEAD, scores, finish)

    for ii in range(k_ref.shape[0] // tq):
        pl.when(i == ii)(functools.partial(run, ii))

    acc0 = acc_ref[0]
    acc1 = acc_ref[1]
    out0 = acc0 / acc0[:, FOX_DH:FOX_DH + 1]
    out1 = acc1 / acc1[:, 0:1]
    o_ref[...] = jnp.where(lane < FOX_DH, out0, out1).astype(BF16)


def _fox_attention(q, k, v, cum, cum_t):
    b, s, d = q.shape
    tq = min(FOX_TQ, s)
    npair = d // LANES
    return pl.pallas_call(
        _fox_attn_kernel,
        grid=(b, npair, s // tq),
        in_specs=[
            pl.BlockSpec((None, tq, LANES), lambda bi, h, i: (bi, i, h)),
            pl.BlockSpec((None, s, LANES), lambda bi, h, i: (bi, 0, h)),
            pl.BlockSpec((None, s, LANES), lambda bi, h, i: (bi, 0, h)),
            pl.BlockSpec((None, tq, LANES), lambda bi, h, i: (bi, i, 0)),
            pl.BlockSpec((None, None, 2, s), lambda bi, h, i: (bi, h, 0, 0)),
        ],
        out_specs=pl.BlockSpec((None, tq, LANES), lambda bi, h, i: (bi, i, h)),
        out_shape=jax.ShapeDtypeStruct((b, s, d), BF16),
        scratch_shapes=[pltpu.VMEM((2, tq, LANES), BF16), pltpu.VMEM((2, tq, LANES), F32),
                        pltpu.VMEM((2, tq, LANES), F32), pltpu.VMEM((2, tq, LANES), F32)],
        compiler_params=_cparams("parallel", "parallel", "parallel"),
        name="fox_attention",
    )(q, k, v, cum, cum_t)


def _outproj_router_kernel(*refs, n_y):
    y_refs = refs[:n_y]
    w_refs = refs[n_y:2 * n_y]
    x_ref, g_ref, b_ref, rw_ref, h_ref, route_ref = refs[2 * n_y:]
    mix = _dot(y_refs[0][...], w_refs[0][...])
    for yr, wr in zip(y_refs[1:], w_refs[1:]):
        mix = mix + _dot(yr[...], wr[...])
    h = _layer_norm_rows(DEEPNORM_ALPHA * x_ref[...] + mix, g_ref[...], b_ref[...])
    h_ref[...] = h

    rw = rw_ref[...]
    rw_hi = rw.astype(BF16)
    rw_lo = (rw - rw_hi.astype(F32)).astype(BF16)
    h_hi = h.astype(BF16)
    h_lo = (h - h_hi.astype(F32)).astype(BF16)
    logits = _dot(h_hi, rw_hi) + (_dot(h_lo, rw_hi) + _dot(h_hi, rw_lo))
    tm = logits.shape[0]
    lane = lax.broadcasted_iota(I32, (tm, LANES), 1)
    valid = lane < N_EXPERTS
    logits = jnp.where(valid, logits, NEG_BIG)
    mx = jnp.max(logits, axis=-1, keepdims=True)
    ex = jnp.exp(logits - mx)
    probs = ex / jnp.sum(ex, axis=-1, keepdims=True)
    grp = lane // EXPERTS_PER_GROUP

    def top2(vals):
        v1 = jnp.max(vals, axis=-1, keepdims=True)
        i1 = jnp.min(jnp.where(vals == v1, lane, LANES), axis=-1, keepdims=True)
        rest = jnp.where(lane == i1, -2.0, vals)
        v2 = jnp.max(rest, axis=-1, keepdims=True)
        i2 = jnp.min(jnp.where(rest == v2, lane, LANES), axis=-1, keepdims=True)
        return v1, i1, v2, i2

    best_score = None
    best = None
    for gi in range(N_GROUPS):
        v1, _, v2, _ = top2(jnp.where(grp == gi, probs, -1.0))
        score = v1 + v2
        if gi == 0:
            best_score, best = score, jnp.zeros_like(score, dtype=I32)
        else:
            better = score > best_score
            best = jnp.where(better, gi, best)
            best_score = jnp.where(better, score, best_score)
    v1, i1, v2, i2 = top2(jnp.where(grp == best, probs, -1.0))
    tot = v1 + v2
    g1 = v1 / tot
    g2 = v2 / tot
    route = jnp.where(lane == i1, g1, jnp.where(lane == i2, g2, 0.0))
    route = jnp.where(lane == N_EXPERTS, i1.astype(F32), route)
    route = jnp.where(lane == N_EXPERTS + 1, i2.astype(F32), route)
    route = jnp.where(lane == N_EXPERTS + 2, g1, route)
    route = jnp.where(lane == N_EXPERTS + 3, g2, route)
    route_ref[...] = route


def _outproj_router(ys, ws, x2, ln_g, ln_b, rw_pad):
    t, d = x2.shape
    tm = min(PROJ_TM, t)
    row = lambda i: (i, 0)
    full = lambda i: (0, 0)
    n_y = len(ys)
    kern = functools.partial(_outproj_router_kernel, n_y=n_y)
    return pl.pallas_call(
        kern,
        grid=(t // tm,),
        in_specs=([pl.BlockSpec((tm, y.shape[1]), row) for y in ys]
                  + [pl.BlockSpec(w.shape, full) for w in ws]
                  + [pl.BlockSpec((tm, d), row), pl.BlockSpec((1, d), full), pl.BlockSpec((1, d), full),
                     pl.BlockSpec(rw_pad.shape, full)]),
        out_specs=[pl.BlockSpec((tm, d), row), pl.BlockSpec((tm, LANES), row)],
        out_shape=[jax.ShapeDtypeStruct((t, d), F32), jax.ShapeDtypeStruct((t, LANES), F32)],
        compiler_params=_cparams("parallel"),
        name="outproj_router",
    )(*ys, *ws, x2, ln_g, ln_b, rw_pad)


def _row_gather_copy(src_hbm, row, dst_vmem, slot, sem):
    return pltpu.make_async_copy(src_hbm.at[pl.ds(row, 1)], dst_vmem.at[pl.ds(slot, 1)], sem)


def _issue_row_gathers(src_hbm, idx_ref, idx_base, idx_stride, dst_vmem, sem, n_rows):
    for i in range(n_rows):
        _row_gather_copy(src_hbm, idx_ref[idx_base + idx_stride * i], dst_vmem, i, sem).start()


def _wait_row_gathers(src_hbm, dst_vmem, sem):
    n_rows = dst_vmem.shape[0]
    pltpu.make_async_copy(src_hbm.at[pl.ds(0, n_rows)], dst_vmem, sem).wait()


def _expert_kernel(te_ref, nt_ref, src_ref, h_hbm, wg_ref, wu_ref, wd_ref, o_ref, xbuf, sem):
    r = pl.program_id(0)
    nt = nt_ref[0]
    tr = xbuf.shape[1]
    slot = r % 2

    @pl.when(r == 0)
    def _():
        _issue_row_gathers(h_hbm, src_ref, 0, 1, xbuf.at[0], sem.at[0], tr)

    @pl.when(r < nt)
    def _():
        _issue_row_gathers(h_hbm, src_ref, (r + 1) * tr, 1, xbuf.at[1 - slot], sem.at[1 - slot], tr)
        _wait_row_gathers(h_hbm, xbuf.at[slot], sem.at[slot])
        x = xbuf[slot].astype(BF16)
        a = _silu(_dot(x, wg_ref[...])) * _dot(x, wu_ref[...])
        o_ref[...] = _dot(a.astype(BF16), wd_ref[...])

    @pl.when(r >= nt)
    def _():
        o_ref[...] = jnp.zeros_like(o_ref)

    @pl.when(r == nt)
    def _():
        _wait_row_gathers(h_hbm, xbuf.at[slot], sem.at[slot])


def _expert_mlps(tile_expert, n_tiles, src_rows, h, wg, wu, wd):
    t, d = h.shape
    n_slots = src_rows.shape[0]
    tr = MOE_TR
    dff = wg.shape[2]
    grid_spec = pltpu.PrefetchScalarGridSpec(
        num_scalar_prefetch=3,
        grid=(n_slots // tr,),
        in_specs=[
            pl.BlockSpec(memory_space=pl.ANY),
            pl.BlockSpec((None, d, dff), lambda r, te, nt, src: (te[r], 0, 0)),
            pl.BlockSpec((None, d, dff), lambda r, te, nt, src: (te[r], 0, 0)),
            pl.BlockSpec((None, dff, d), lambda r, te, nt, src: (te[r], 0, 0)),
        ],
        out_specs=pl.BlockSpec((tr, d), lambda r, te, nt, src: (r, 0)),
        scratch_shapes=[pltpu.VMEM((2, tr, d), F32), pltpu.SemaphoreType.DMA((2,))],
    )
    return pl.pallas_call(
        _expert_kernel,
        grid_spec=grid_spec,
        out_shape=jax.ShapeDtypeStruct((n_slots, d), F32),
        compiler_params=_cparams("arbitrary"),
        name="expert_mlps",
    )(tile_expert, n_tiles, src_rows, h, wg, wu, wd)


def _combine_kernel(pos_ref, ys_hbm, h_ref, route_ref, p_ref, g_ref, b_ref, pg_ref, pp_ref,
                    o_ref, ybuf, sem):
    i = pl.program_id(0)
    n = pl.num_programs(0)
    tm = h_ref.shape[0]
    slot = i % 2

    def issue(tile, dst_slot):
        for k in range(2):
            _issue_row_gathers(ys_hbm, pos_ref, 2 * tile * tm + k, 2, ybuf.at[dst_slot, k], sem.at[dst_slot], tm)

    @pl.when(i == 0)
    def _():
        issue(0, 0)

    @pl.when(i + 1 < n)
    def _():
        issue(i + 1, 1 - slot)

    for k in range(2):
        _wait_row_gathers(ys_hbm, ybuf.at[slot, k], sem.at[slot])
    route = route_ref[...]
    g1 = route[:, N_EXPERTS + 2:N_EXPERTS + 3]
    g2 = route[:, N_EXPERTS + 3:N_EXPERTS + 4]
    ffn = g1 * ybuf[slot, 0] + g2 * ybuf[slot, 1]
    h2 = _layer_norm_rows(DEEPNORM_ALPHA * h_ref[...] + ffn, g_ref[...], b_ref[...])
    gate = _sigmoid(_dot(h2.astype(BF16), pg_ref[...]))
    pe = _dot(p_ref[...].astype(BF16), pp_ref[...])
    o_ref[...] = h2 + gate * pe


def _combine(pos, ys, h, route, p2, ln_g, ln_b, pg_bf, pp_bf):
    t, d = h.shape
    tm = min(COMB_TM, t)
    pdim = p2.shape[1]
    row = lambda i, pos: (i, 0)
    full = lambda i, pos: (0, 0)
    grid_spec = pltpu.PrefetchScalarGridSpec(
        num_scalar_prefetch=1,
        grid=(t // tm,),
        in_specs=[
            pl.BlockSpec(memory_space=pl.ANY),
            pl.BlockSpec((tm, d), row),
            pl.BlockSpec((tm, LANES), row),
            pl.BlockSpec((tm, pdim), row),
            pl.BlockSpec((1, d), full),
            pl.BlockSpec((1, d), full),
            pl.BlockSpec(pg_bf.shape, full),
            pl.BlockSpec(pp_bf.shape, full),
        ],
        out_specs=pl.BlockSpec((tm, d), row),
        scratch_shapes=[pltpu.VMEM((2, 2, tm, d), F32), pltpu.SemaphoreType.DMA((2,))],
    )
    return pl.pallas_call(
        _combine_kernel,
        grid_spec=grid_spec,
        out_shape=jax.ShapeDtypeStruct((t, d), F32),
        compiler_params=_cparams("arbitrary"),
        name="moe_combine",
    )(pos, ys, h, route, p2, ln_g, ln_b, pg_bf, pp_bf)


def _rotary_tables(seq):
    half = RET_DK // 2
    inv = (np.float32(ROPE_BASE) ** (-np.arange(half, dtype=np.float32) / np.float32(half))).astype(np.float32)
    ang = (np.arange(seq, dtype=np.float32)[:, None] * inv[None, :]).astype(np.float32)
    cos = np.cos(ang.astype(np.float64))
    sin = np.sin(ang.astype(np.float64))
    cos_h = np.concatenate([cos, cos], axis=1)
    sin_h = np.concatenate([-sin, sin], axis=1)
    return (jnp.asarray(np.tile(cos_h, (1, RET_HEADS)), F32), jnp.asarray(np.tile(sin_h, (1, RET_HEADS)), F32))


def _retention_tables():
    c = RET_CHUNK
    h = np.arange(RET_HEADS, dtype=np.float64)
    log_g = np.log1p(-np.exp2(-5.0 - h))
    j = np.arange(c, dtype=np.float64)
    rel = j[:, None] - j[None, :]
    din = np.where(rel >= 0, np.exp(np.maximum(rel, 0.0)[None] * log_g[:, None, None]), 0.0)
    qd = np.exp((j + 1.0)[None] * log_g[:, None])
    kd = np.exp((c - 1.0 - j)[None] * log_g[:, None])
    cd = np.exp(c * log_g)
    qd = np.broadcast_to(qd[:, :, None], (RET_HEADS, c, LANES))
    kd = np.broadcast_to(kd[:, :, None], (RET_HEADS, c, LANES))
    cd = np.broadcast_to(cd[:, None, None], (RET_HEADS, 1, LANES))
    return tuple(jnp.asarray(a, F32) for a in (din, qd, kd, cd))


def _t5_bucket_np(dist):
    max_exact = REL_BUCKETS // 2
    d = np.maximum(dist, 1).astype(np.float32)
    large = max_exact + (np.log(d / np.float32(max_exact)) / np.float32(math.log(REL_MAX_DIST / max_exact))
                         * np.float32(REL_BUCKETS - max_exact)).astype(np.int32)
    large = np.minimum(large, REL_BUCKETS - 1)
    return np.where(dist < max_exact, dist, large)


def _diff_bias_tables(rel_bias, seq):
    c = ATT_RQ
    r = np.arange(c)
    dist0 = r[:, None] - r[None, :]
    far = REL_BUCKETS - 1
    assert np.all(_t5_bucket_np(np.arange(c + 1, max(seq, 2 * c))) == far)
    bidx = np.stack([_t5_bucket_np(np.maximum(dist0, 0)), _t5_bucket_np(dist0 + c)])
    rb = rel_bias.astype(F32).T
    shifted = (rb - rb[:, far:far + 1]) * LOG2E
    bidx = jnp.asarray(bidx, I32)[None]
    tab = jnp.zeros((rb.shape[0], 2, c, c), F32)
    for bucket in range(REL_BUCKETS - 1):
        tab = jnp.where(bidx == bucket, shifted[:, bucket][:, None, None, None], tab)
    causal = jnp.asarray(np.stack([dist0 >= 0, np.ones_like(dist0, bool)]))[None]
    return jnp.where(causal, tab, NEG_BIG)


def _route_tables(route, tr):
    t = route.shape[0]
    e = route[:, N_EXPERTS:N_EXPERTS + 2].astype(I32).reshape(-1)
    onehot = (e[:, None] == jnp.arange(N_EXPERTS, dtype=I32)[None, :]).astype(I32)
    incl = jnp.cumsum(onehot, axis=0)
    rank = jnp.sum((incl - onehot) * onehot, axis=1)
    counts = incl[-1]
    padded = ((counts + tr - 1) // tr) * tr
    ends = jnp.cumsum(padded)
    offs = ends - padded
    pos = (offs[e] + rank).astype(I32)
    n_slots = 2 * t + N_EXPERTS * tr
    src = jnp.zeros((n_slots,), I32).at[pos].set(jnp.arange(2 * t, dtype=I32) // 2)
    n_tiles = (ends[-1] // tr).astype(I32)
    tile_start = jnp.arange(n_slots // tr, dtype=I32) * tr
    tile_expert = jnp.sum((tile_start[:, None] >= ends[None, :]).astype(I32), axis=1)
    last = jnp.sum((((n_tiles - 1) * tr) >= ends).astype(I32))
    tile_expert = jnp.minimum(tile_expert, last).astype(I32)
    return pos, src, tile_expert, n_tiles.reshape(1)


def kernel(x, p, rel_bias, router_w, even_w_in, even_w_out, even_lambda, even_diff_norm, even_ret_norm,
           odd_w_in, odd_b_forget, odd_w_out, ln_mix_g, ln_mix_b, ln_ffn_g, ln_ffn_b,
           moe_w_gate, moe_w_up, moe_w_down, ple_proj, ple_gate):
    b, s, d = x.shape
    t = b * s
    assert d == 1024 and p.shape[0] == DEPTH and even_w_in.shape[2] == 3072
    assert odd_w_in.shape[2] == 3 * d + FOX_HEADS and moe_w_gate.shape[1] == N_EXPERTS
    assert s % RET_CHUNK == 0 and s % min(ATT_TQ, s) == 0 and s % min(FOX_TQ, s) == 0
    assert t % min(PROJ_TM, s) == 0

    cos_t, sin_t = _rotary_tables(s)
    ret_tabs = _retention_tables()
    bias_tab = _diff_bias_tables(rel_bias, s)
    rw_pad = jnp.zeros((d, LANES), F32).at[:, :N_EXPERTS].set(router_w.astype(F32))

    x2 = x.reshape(t, d)
    for i in range(DEPTH):
        j = i // 2
        if i % 2 == 0:
            lam_init = 0.8 - 0.6 * math.exp(-0.3 * i)
            qa, ka, va, qb, kb, vb, gb = _even_inproj(x2, even_w_in[j].astype(BF16), cos_t, sin_t, s)
            sh = lambda a: a.reshape(b, s, a.shape[1])
            ya = _diff_attention(sh(qa), sh(ka), sh(va), bias_tab, even_lambda[j].astype(F32),
                                 even_diff_norm[j].reshape(1, -1).astype(F32), lam_init)
            yb = _retention(sh(qb), sh(kb), sh(vb), sh(gb), ret_tabs, even_ret_norm[j].reshape(1, -1).astype(F32))
            w_out = even_w_out[j].astype(BF16)
            n_a = ya.shape[2]
            ys = [ya.reshape(t, -1), yb.reshape(t, -1)]
            ws = [w_out[:n_a], w_out[n_a:]]
        else:
            w_in = odd_w_in[j]
            wf = jnp.zeros((d, LANES), BF16).at[:, :FOX_HEADS].set(w_in[:, 3 * d:].astype(BF16))
            bfg = jnp.zeros((1, LANES), F32).at[0, :FOX_HEADS].set(odd_b_forget[j].astype(F32))
            q, k, v, cum = _odd_inproj(x2, w_in[:, :3 * d].astype(BF16), wf, bfg, s)
            cum3 = cum.reshape(b, s, LANES)
            cum_t = jnp.transpose(cum3[:, :, :FOX_HEADS], (0, 2, 1)).reshape(b, FOX_HEADS // 2, 2, s)
            y = _fox_attention(q.reshape(b, s, d), k.reshape(b, s, d), v.reshape(b, s, d), cum3, cum_t)
            ys = [y.reshape(t, d)]
            ws = [odd_w_out[j].astype(BF16)]
        h, route = _outproj_router(ys, ws, x2, ln_mix_g[i].reshape(1, d), ln_mix_b[i].reshape(1, d), rw_pad)
        pos, src, tile_expert, n_tiles = _route_tables(route, MOE_TR)
        rows = _expert_mlps(tile_expert, n_tiles, src, h, moe_w_gate[i].astype(BF16),
                            moe_w_up[i].astype(BF16), moe_w_down[i].astype(BF16))
        x2 = _combine(pos, rows, h, route, p[i].reshape(t, -1), ln_ffn_g[i].reshape(1, d),
                      ln_ffn_b[i].reshape(1, d), ple_gate[i].astype(BF16), ple_proj[i].astype(BF16))
    return x2.reshape(b, s, d)
```

```python
import functools
import math

import numpy as np
import jax
import jax.numpy as jnp
from jax import lax
from jax.experimental import pallas as pl
from jax.experimental.pallas import tpu as pltpu

F32 = jnp.float32
BF16 = jnp.bfloat16
I32 = jnp.int32

DIFF_HEADS = 4
DIFF_DK = 64
RET_HEADS = 4
RET_DK = 64
RET_DV = 128
RET_CHUNK = 128
FOX_HEADS = 16
FOX_DH = 64
REL_BUCKETS = 32
REL_MAX_DIST = 128
N_GROUPS = 4
EXPERTS_PER_GROUP = 4
N_EXPERTS = 16
DEPTH = 2
DEEPNORM_ALPHA = (2 * DEPTH) ** 0.25
LN_EPS = 1e-5
ROPE_BASE = 10000.0
NEG_BIG = -1e30
LOG2E = math.log2(math.e)

VMEM_LIMIT_BYTES = 48 * 1024 * 1024
LANES = 128

PROJ_TM = 512
ATT_TQ = 512
FOX_TQ = 512
ATT_RQ = LANES
ATT_LOOKAHEAD = 3
MOE_TR = 256
MOE_TM = 512
SORT_ALIGN = 16
ROUTE_ROWS = 8


def _cparams(*sem):
    return pltpu.CompilerParams(dimension_semantics=sem, vmem_limit_bytes=VMEM_LIMIT_BYTES)


def _dot(a, b):
    return jnp.dot(a, b, preferred_element_type=F32)


def _dot_nt(a, b):
    return lax.dot_general(a, b, (((1,), (1,)), ((), ())), preferred_element_type=F32)


def _layer_norm_rows(z, g, b):
    mu = jnp.mean(z, axis=-1, keepdims=True)
    zc = z - mu
    var = jnp.mean(zc * zc, axis=-1, keepdims=True)
    return zc * lax.rsqrt(var + LN_EPS) * g + b


def _silu(x):
    return x * (1.0 / (1.0 + jnp.exp(-x)))


def _sigmoid(x):
    return 1.0 / (1.0 + jnp.exp(-x))


def _even_inproj_kernel(x_ref, w_ref, cos_ref, sin_ref,
                        qa_ref, ka_ref, va_ref, qb_ref, kb_ref, vb_ref, gb_ref):
    x = x_ref[...].astype(BF16)

    def mm(c0, c1):
        return _dot(x, w_ref[:, c0:c1])

    qa_ref[...] = (mm(0, 512) * (DIFF_DK ** -0.5 * LOG2E)).astype(BF16)
    ka_ref[...] = mm(512, 1024).astype(BF16)
    va_ref[...] = mm(1024, 1536).astype(BF16)
    qk = mm(1536, 2048)
    cos = cos_ref[...]
    sin = sin_ref[...]
    lane = lax.broadcasted_iota(I32, cos.shape, 1)
    first_half = (lane % RET_DK) < (RET_DK // 2)

    def rot(t):
        sw = jnp.where(first_half, pltpu.roll(t, t.shape[1] - RET_DK // 2, 1),
                       pltpu.roll(t, RET_DK // 2, 1))
        return t * cos + sw * sin

    qb_ref[...] = rot(qk[:, :256]).astype(BF16)
    kb_ref[...] = (rot(qk[:, 256:]) * (RET_DK ** -0.5)).astype(BF16)
    vb_ref[...] = mm(2048, 2560).astype(BF16)
    gb_ref[...] = mm(2560, 3072).astype(BF16)


def _even_inproj(x2, w_bf, cos_t, sin_t, seq):
    t, d = x2.shape
    tm = min(PROJ_TM, seq)
    nblk_s = seq // tm
    widths = (512, 512, 512, 256, 256, 512, 512)
    row = lambda i: (i, 0)
    return pl.pallas_call(
        _even_inproj_kernel,
        grid=(t // tm,),
        in_specs=[
            pl.BlockSpec((tm, d), row),
            pl.BlockSpec(w_bf.shape, lambda i: (0, 0)),
            pl.BlockSpec((tm, 256), lambda i: (i % nblk_s, 0)),
            pl.BlockSpec((tm, 256), lambda i: (i % nblk_s, 0)),
        ],
        out_specs=[pl.BlockSpec((tm, w), row) for w in widths],
        out_shape=[jax.ShapeDtypeStruct((t, w), BF16) for w in widths],
        compiler_params=_cparams("parallel"),
        name="even_inproj",
    )(x2, w_bf, cos_t, sin_t)


def _run_chains(chains, lookahead, scores, finish):
    pending = [scores(c) for c in chains[:lookahead]]
    for n, chain in enumerate(chains):
        if n + lookahead < len(chains):
            pending.append(scores(chains[n + lookahead]))
        finish(chain, pending.pop(0))


def _diff_attn_kernel(lam_ref, q_ref, k_ref, v_ref, bias_ref, g_ref, o_ref, qm_ref, m_ref, l_ref, acc_ref,
                      *, lam_init):
    tq = q_ref.shape[0]
    tk = tq
    rq = ATT_RQ
    nr = tq // rq
    i = pl.program_id(2)
    q = q_ref[...]
    lane = lax.broadcasted_iota(I32, q.shape, 1)
    zero = jnp.zeros_like(q)
    qm_ref[0] = jnp.where(lane < DIFF_DK, q, zero)
    qm_ref[1] = jnp.where(lane >= DIFF_DK, q, zero)
    m_ref[...] = jnp.full(m_ref.shape, NEG_BIG, F32)
    l_ref[...] = jnp.zeros(l_ref.shape, F32)
    acc_ref[...] = jnp.zeros(acc_ref.shape, F32)

    def run(ii):
        chains = [(j, r, c) for j in range(ii + 1) for r in range(nr) for c in range(2)]

        def n_keys(j, r):
            return (r + 1) * rq if j == ii else tk

        def scores(chain):
            j, r, c = chain
            k = k_ref[j * tk:j * tk + n_keys(j, r), :]
            return _dot_nt(qm_ref[c, r * rq:(r + 1) * rq, :], k)

        def finish(chain, s):
            j, r, c = chain
            rows = slice(r * rq, (r + 1) * rq)
            nk = n_keys(j, r)
            v = v_ref[j * tk:j * tk + nk, :]
            sc = []
            for kc in range(nk // rq):
                t = s[:, kc * rq:(kc + 1) * rq]
                back = (ii * nr + r) - (j * nr + kc)
                if back <= 1:
                    t = t + bias_ref[back]
                sc.append(t)
            mx = sc[0]
            for t in sc[1:]:
                mx = jnp.maximum(mx, t)
            m_old = m_ref[c, rows, :]
            m_new = jnp.maximum(m_old, jnp.max(mx, axis=-1, keepdims=True))
            alpha = jnp.exp2(m_old - m_new)
            ps = [jnp.exp2(t - m_new) for t in sc]
            psum = ps[0]
            for t in ps[1:]:
                psum = psum + t
            l_ref[c, rows, :] = alpha * l_ref[c, rows, :] + psum
            p = jnp.concatenate([t.astype(BF16) for t in ps], axis=1)
            acc_ref[c, rows, :] = alpha * acc_ref[c, rows, :] + _dot(p, v)
            m_ref[c, rows, :] = m_new

        _run_chains(chains, ATT_LOOKAHEAD, scores, finish)

    for ii in range(k_ref.shape[0] // tq):
        pl.when(i == ii)(functools.partial(run, ii))

    lp = lam_ref[...]
    lam = (jnp.exp(jnp.sum(lp[0:1, :] * lp[1:2, :], axis=-1, keepdims=True))
           - jnp.exp(jnp.sum(lp[2:3, :] * lp[3:4, :], axis=-1, keepdims=True)) + lam_init)
    l0 = jnp.sum(l_ref[0], axis=-1, keepdims=True)
    l1 = jnp.sum(l_ref[1], axis=-1, keepdims=True)
    o = acc_ref[0] / l0 - lam * (acc_ref[1] / l1)
    o = o * lax.rsqrt(jnp.mean(o * o, axis=-1, keepdims=True) + LN_EPS)
    o_ref[...] = (o * g_ref[...] * (1.0 - lam_init)).astype(BF16)


def _diff_attention(qa, ka, va, bias_tab, lam_params, diff_g, lam_init):
    b, s, _ = qa.shape
    tq = min(ATT_TQ, s)
    kern = functools.partial(_diff_attn_kernel, lam_init=lam_init)
    return pl.pallas_call(
        kern,
        grid=(b, DIFF_HEADS, s // tq),
        in_specs=[
            pl.BlockSpec(lam_params.shape, lambda bi, h, i: (0, 0)),
            pl.BlockSpec((None, tq, LANES), lambda bi, h, i: (bi, i, h)),
            pl.BlockSpec((None, s, LANES), lambda bi, h, i: (bi, 0, h)),
            pl.BlockSpec((None, s, LANES), lambda bi, h, i: (bi, 0, h)),
            pl.BlockSpec((None, 2, ATT_RQ, ATT_RQ), lambda bi, h, i: (h, 0, 0, 0)),
            pl.BlockSpec((1, LANES), lambda bi, h, i: (0, 0)),
        ],
        out_specs=pl.BlockSpec((None, tq, LANES), lambda bi, h, i: (bi, i, h)),
        out_shape=jax.ShapeDtypeStruct((b, s, DIFF_HEADS * LANES), BF16),
        scratch_shapes=[pltpu.VMEM((2, tq, LANES), BF16), pltpu.VMEM((2, tq, LANES), F32),
                        pltpu.VMEM((2, tq, LANES), F32), pltpu.VMEM((2, tq, LANES), F32)],
        compiler_params=_cparams("parallel", "parallel", "parallel"),
        name="diff_attention",
    )(lam_params, qa, ka, va, bias_tab, diff_g)


def _retention_kernel(q_ref, k_ref, v_ref, gate_ref, din_ref, qd_ref, kd_ref, cd_ref, g_ref, o_ref):
    s = q_ref.shape[0]
    c = RET_CHUNK
    lane = lax.broadcasted_iota(I32, (c, LANES), 1)
    g = g_ref[...]
    states = [jnp.zeros((LANES, RET_DV), F32) for _ in range(2)]
    for n in range(s // c):
        r = slice(n * c, (n + 1) * c)
        q_pair = q_ref[r, :].astype(F32)
        k_pair = k_ref[r, :].astype(F32)
        for par in range(2):
            own = (lane // RET_DK) == par
            cols = slice(par * RET_DV, (par + 1) * RET_DV)
            q = jnp.where(own, q_pair, 0.0)
            k = jnp.where(own, k_pair, 0.0)
            v = v_ref[r, cols]
            scores = _dot_nt(q.astype(BF16), k.astype(BF16)) * din_ref[par]
            inner = _dot(scores.astype(BF16), v)
            cross = _dot((q * qd_ref[par]).astype(BF16), states[par].astype(BF16))
            kv = _dot((k * kd_ref[par]).T.astype(BF16), v)
            states[par] = cd_ref[par] * states[par] + kv
            y = inner + cross
            mu = jnp.mean(y, axis=-1, keepdims=True)
            yc = y - mu
            var = jnp.mean(yc * yc, axis=-1, keepdims=True)
            yn = yc * lax.rsqrt(var + LN_EPS) * g
            gate = gate_ref[r, cols].astype(F32)
            o_ref[r, cols] = (_silu(gate) * yn).astype(BF16)


def _retention(qb, kb, vb, gb, tabs, ret_g):
    b, s, _ = qb.shape
    din, qd, kd, cd = tabs
    pair = lambda bi, hp: (bi, 0, hp)
    tab = lambda bi, hp: (hp, 0, 0)
    return pl.pallas_call(
        _retention_kernel,
        grid=(b, RET_HEADS // 2),
        in_specs=[
            pl.BlockSpec((None, s, LANES), pair),
            pl.BlockSpec((None, s, LANES), pair),
            pl.BlockSpec((None, s, 2 * RET_DV), pair),
            pl.BlockSpec((None, s, 2 * RET_DV), pair),
            pl.BlockSpec((2, RET_CHUNK, RET_CHUNK), tab),
            pl.BlockSpec((2, RET_CHUNK, LANES), tab),
            pl.BlockSpec((2, RET_CHUNK, LANES), tab),
            pl.BlockSpec((2, 1, LANES), tab),
            pl.BlockSpec((1, RET_DV), lambda bi, hp: (0, 0)),
        ],
        out_specs=pl.BlockSpec((None, s, 2 * RET_DV), pair),
        out_shape=jax.ShapeDtypeStruct((b, s, RET_HEADS * RET_DV), BF16),
        compiler_params=_cparams("parallel", "parallel"),
        name="retention",
    )(qb, kb, vb, gb, din, qd, kd, cd, ret_g)


def _odd_inproj_kernel(x_ref, w_ref, wf_ref, bf_ref, q_ref, k_ref, v_ref, cum_ref, carry_ref, *, nblk_s):
    i = pl.program_id(0)
    x = x_ref[...].astype(BF16)
    d = q_ref.shape[1]
    q_ref[...] = (_dot(x, w_ref[:, 0:d]) * (FOX_DH ** -0.5 * LOG2E)).astype(BF16)
    k_ref[...] = _dot(x, w_ref[:, d:2 * d]).astype(BF16)
    v_ref[...] = _dot(x, w_ref[:, 2 * d:3 * d]).astype(BF16)
    z = _dot(x, wf_ref[...]) + bf_ref[...]
    c = jnp.minimum(z, 0.0) - jnp.log1p(jnp.exp(-jnp.abs(z)))
    tm = c.shape[0]
    row = lax.broadcasted_iota(I32, c.shape, 0)
    step = 1
    while step < tm:
        c = c + jnp.where(row >= step, pltpu.roll(c, step, 0), 0.0)
        step *= 2

    @pl.when(i % nblk_s == 0)
    def _():
        carry_ref[...] = jnp.zeros_like(carry_ref)

    c = c + carry_ref[...]
    cum_ref[...] = c * LOG2E
    carry_ref[...] = c[tm - 1:tm, :]


def _odd_inproj(x2, w_bf, wf_bf, bfg, seq):
    t, d = x2.shape
    tm = min(PROJ_TM, seq)
    nblk_s = seq // tm
    row = lambda i: (i, 0)
    kern = functools.partial(_odd_inproj_kernel, nblk_s=nblk_s)
    return pl.pallas_call(
        kern,
        grid=(t // tm,),
        in_specs=[
            pl.BlockSpec((tm, d), row),
            pl.BlockSpec(w_bf.shape, lambda i: (0, 0)),
            pl.BlockSpec(wf_bf.shape, lambda i: (0, 0)),
            pl.BlockSpec(bfg.shape, lambda i: (0, 0)),
        ],
        out_specs=[pl.BlockSpec((tm, d), row)] * 3 + [pl.BlockSpec((tm, LANES), row)],
        out_shape=[jax.ShapeDtypeStruct((t, d), BF16)] * 3 + [jax.ShapeDtypeStruct((t, LANES), F32)],
        scratch_shapes=[pltpu.VMEM((1, LANES), F32)],
        compiler_params=_cparams("arbitrary"),
        name="odd_inproj",
    )(x2, w_bf, wf_bf, bfg)


def _fox_attn_kernel(q_ref, k_ref, v_ref, cq_ref, ck_ref, o_ref, qm_ref, cqc_ref, m_ref, acc_ref):
    tq = q_ref.shape[0]
    tk = tq
    hp = pl.program_id(1)
    i = pl.program_id(2)
    q = q_ref[...]
    lane = lax.broadcasted_iota(I32, (tq, LANES), 1)
    cq_all = cq_ref[...]
    for par in range(2):
        own = (lane // FOX_DH) == par
        qm_ref[par] = jnp.where(own, q, jnp.zeros_like(q))
        cq = jnp.sum(jnp.where(lane == 2 * hp + par, cq_all, 0.0), axis=-1, keepdims=True)
        cqc_ref[par] = jnp.broadcast_to(cq, (tq, LANES))
    m_ref[...] = jnp.full(m_ref.shape, NEG_BIG, F32)
    acc_ref[...] = jnp.zeros(acc_ref.shape, F32)
    rq = ATT_RQ
    nr = tq // rq
    upper = (lax.broadcasted_iota(I32, (rq, rq), 1) > lax.broadcasted_iota(I32, (rq, rq), 0))

    def run(ii):
        chains = [(j, r, par) for j in range(ii + 1) for r in range(nr) for par in range(2)]
        v_aug = {}

        def n_keys(j, r):
            return (r + 1) * rq if j == ii else tk

        def scores(chain):
            j, r, par = chain
            k = k_ref[j * tk:j * tk + n_keys(j, r), :]
            return _dot_nt(qm_ref[par, r * rq:(r + 1) * rq, :], k)

        def finish(chain, s):
            j, r, par = chain
            rows = slice(r * rq, (r + 1) * rq)
            nk = n_keys(j, r)
            if (j, par) not in v_aug:
                v = v_ref[j * tk:(j + 1) * tk, :]
                lane_k = lax.broadcasted_iota(I32, v.shape, 1)
                v_aug[j, par] = jnp.where((lane_k // FOX_DH) == par, v, jnp.ones_like(v))
            ck = ck_ref[par:par + 1, j * tk:j * tk + nk]
            cq = cqc_ref[par, rows, :]
            sc = []
            for kc in range(nk // rq):
                t = s[:, kc * rq:(kc + 1) * rq] - ck[:, kc * rq:(kc + 1) * rq]
                if j == ii and kc == r:
                    t = jnp.where(upper, NEG_BIG, t)
                sc.append(t)
            mx = sc[0]
            for t in sc[1:]:
                mx = jnp.maximum(mx, t)
            m_old = m_ref[par, rows, :]
            m_new = jnp.maximum(m_old, jnp.max(mx, axis=-1, keepdims=True) + cq)
            alpha = jnp.exp2(m_old - m_new)
            shift = m_new - cq
            p = jnp.concatenate([jnp.exp2(t - shift).astype(BF16) for t in sc], axis=1)
            acc_ref[par, rows, :] = alpha * acc_ref[par, rows, :] + _dot(p, v_aug[j, par][:nk])
            m_ref[par, rows, :] = m_new

        _run_chains(chains, ATT_LOOKAHEAD, scores, finish)

    for ii in range(k_ref.shape[0] // tq):
        pl.when(i == ii)(functools.partial(run, ii))

    acc0 = acc_ref[0]
    acc1 = acc_ref[1]
    out0 = acc0 / acc0[:, FOX_DH:FOX_DH + 1]
    out1 = acc1 / acc1[:, 0:1]
    o_ref[...] = jnp.where(lane < FOX_DH, out0, out1).astype(BF16)


def _fox_attention(q, k, v, cum, cum_t):
    b, s, d = q.shape
    tq = min(FOX_TQ, s)
    npair = d // LANES
    return pl.pallas_call(
        _fox_attn_kernel,
        grid=(b, npair, s // tq),
        in_specs=[
            pl.BlockSpec((None, tq, LANES), lambda bi, h, i: (bi, i, h)),
            pl.BlockSpec((None, s, LANES), lambda bi, h, i: (bi, 0, h)),
            pl.BlockSpec((None, s, LANES), lambda bi, h, i: (bi, 0, h)),
            pl.BlockSpec((None, tq, LANES), lambda bi, h, i: (bi, i, 0)),
            pl.BlockSpec((None, None, 2, s), lambda bi, h, i: (bi, h, 0, 0)),
        ],
        out_specs=pl.BlockSpec((None, tq, LANES), lambda bi, h, i: (bi, i, h)),
        out_shape=jax.ShapeDtypeStruct((b, s, d), BF16),
        scratch_shapes=[pltpu.VMEM((2, tq, LANES), BF16), pltpu.VMEM((2, tq, LANES), F32),
                        pltpu.VMEM((2, tq, LANES), F32), pltpu.VMEM((2, tq, LANES), F32)],
        compiler_params=_cparams("parallel", "parallel", "parallel"),
        name="fox_attention",
    )(q, k, v, cum, cum_t)


def _outproj_router_kernel(*refs, n_y):
    y_refs = refs[:n_y]
    w_refs = refs[n_y:2 * n_y]
    x_ref, g_ref, b_ref, rwt_ref, tri_ref, h_ref, route_ref, cnt_ref = refs[2 * n_y:]
    mix = _dot(y_refs[0][...], w_refs[0][...])
    for yr, wr in zip(y_refs[1:], w_refs[1:]):
        mix = mix + _dot(yr[...], wr[...])
    h = _layer_norm_rows(DEEPNORM_ALPHA * x_ref[...] + mix, g_ref[...], b_ref[...])
    h_ref[...] = h

    rwt = rwt_ref[...]
    rw_hi = rwt.astype(BF16)
    rw_lo = (rwt - rw_hi.astype(F32)).astype(BF16)
    h_hi = h.astype(BF16)
    h_lo = (h - h_hi.astype(F32)).astype(BF16)
    logits = _dot_nt(rw_hi, h_hi) + (_dot_nt(rw_lo, h_hi) + _dot_nt(rw_hi, h_lo))
    tm = logits.shape[1]
    row = lax.broadcasted_iota(I32, (N_EXPERTS, tm), 0)
    mx = jnp.max(logits, axis=0, keepdims=True)
    ex = jnp.exp(logits - mx)
    probs = ex / jnp.sum(ex, axis=0, keepdims=True)
    grp = row // EXPERTS_PER_GROUP

    def top2(vals):
        v1 = jnp.max(vals, axis=0, keepdims=True)
        i1 = jnp.min(jnp.where(vals == v1, row, N_EXPERTS), axis=0, keepdims=True)
        rest = jnp.where(row == i1, -2.0, vals)
        v2 = jnp.max(rest, axis=0, keepdims=True)
        i2 = jnp.min(jnp.where(rest == v2, row, N_EXPERTS), axis=0, keepdims=True)
        return v1, i1, v2, i2

    best_score = None
    best = None
    for gi in range(N_GROUPS):
        v1, _, v2, _ = top2(jnp.where(grp == gi, probs, -1.0))
        score = v1 + v2
        if gi == 0:
            best_score, best = score, jnp.zeros_like(score, dtype=I32)
        else:
            better = score > best_score
            best = jnp.where(better, gi, best)
            best_score = jnp.where(better, score, best_score)
    v1, i1, v2, i2 = top2(jnp.where(grp == best, probs, -1.0))
    tot = v1 + v2
    g1 = v1 / tot
    g2 = v2 / tot

    onehot = jnp.where((row == i1) | (row == i2), 1.0, 0.0)
    pref = _dot(onehot.astype(BF16), tri_ref[...])
    r1 = jnp.sum(jnp.where(row == i1, pref, 0.0), axis=0, keepdims=True)
    r2 = jnp.sum(jnp.where(row == i2, pref, 0.0), axis=0, keepdims=True)
    cnt_ref[...] = jnp.broadcast_to(jnp.sum(onehot, axis=1, keepdims=True), cnt_ref.shape)

    row8 = lax.broadcasted_iota(I32, (ROUTE_ROWS, tm), 0)
    fields = (i1.astype(F32), i2.astype(F32), g1, g2, r1, r2)
    route = jnp.zeros((ROUTE_ROWS, tm), F32)
    for n, f in enumerate(fields):
        route = jnp.where(row8 == n, f, route)
    route_ref[...] = route


def _outproj_router(ys, ws, x2, ln_g, ln_b, rwt, tri):
    t, d = x2.shape
    tm = tri.shape[0]
    row = lambda i: (i, 0)
    full = lambda i: (0, 0)
    n_y = len(ys)
    kern = functools.partial(_outproj_router_kernel, n_y=n_y)
    return pl.pallas_call(
        kern,
        grid=(t // tm,),
        in_specs=([pl.BlockSpec((tm, y.shape[1]), row) for y in ys]
                  + [pl.BlockSpec(w.shape, full) for w in ws]
                  + [pl.BlockSpec((tm, d), row), pl.BlockSpec((1, d), full), pl.BlockSpec((1, d), full),
                     pl.BlockSpec(rwt.shape, full), pl.BlockSpec(tri.shape, full)]),
        out_specs=[pl.BlockSpec((tm, d), row), pl.BlockSpec((ROUTE_ROWS, tm), lambda i: (0, i)),
                   pl.BlockSpec((N_EXPERTS, LANES), row)],
        out_shape=[jax.ShapeDtypeStruct((t, d), F32), jax.ShapeDtypeStruct((ROUTE_ROWS, t), F32),
                   jax.ShapeDtypeStruct((t // tm * N_EXPERTS, LANES), F32)],
        compiler_params=_cparams("parallel"),
        name="outproj_router",
    )(*ys, *ws, x2, ln_g, ln_b, rwt, tri)


def _group_copies(n_rows, local_start, global_start, local_ref, global_ref, sem, to_global):
    size = SORT_ALIGN
    while size <= MOE_TM:
        done = n_rows & ~(2 * size - 1)
        lo = pl.multiple_of(local_start + done, SORT_ALIGN)
        go = pl.multiple_of(global_start + done, SORT_ALIGN)

        @pl.when((n_rows & size) != 0)
        def _(lo=lo, go=go, size=size):
            loc = local_ref.at[pl.ds(lo, size)]
            glo = global_ref.at[pl.ds(go, size)]
            (pltpu.make_async_copy(loc, glo, sem) if to_global else pltpu.make_async_copy(glo, loc, sem)).start()

        size *= 2


def _wait_group_copies(total_rows, local_ref, global_ref, sem, to_global):
    loc = local_ref.at[pl.ds(0, total_rows)]
    glo = global_ref.at[pl.ds(0, total_rows)]
    (pltpu.make_async_copy(loc, glo, sem) if to_global else pltpu.make_async_copy(glo, loc, sem)).wait()


def _tile_groups(tile, loff_ref, rows_ref, gpos_ref, local_ref, global_ref, sem, to_global):
    for e in range(N_EXPERTS):
        n = tile * N_EXPERTS + e
        _group_copies(rows_ref[n], loff_ref[n], gpos_ref[n], local_ref, global_ref, sem, to_global)


def _dispatch_kernel(loff_ref, rows_ref, gpos_ref, tot_ref, tail_ref, nt_ref, h_ref, slot_ref, xs_hbm,
                     sbuf, zbuf, sem):
    i = pl.program_id(0)
    tm = h_ref.shape[0]
    n_local = sbuf.shape[0]

    @pl.when(i == 0)
    def _():
        zbuf[...] = jnp.zeros_like(zbuf)

        def zero_tile(start):
            return pltpu.make_async_copy(zbuf, xs_hbm.at[pl.ds(pl.multiple_of(start, MOE_TR), MOE_TR)], sem)

        for e in range(N_EXPERTS):
            @pl.when(tail_ref[e] >= 0)
            def _(e=e):
                zero_tile(tail_ref[e]).start()

        def start_unused(r, c):
            zero_tile(r * MOE_TR).start()
            return c

        def wait_unused(r, c):
            zero_tile(r * MOE_TR).wait()
            return c

        n_all = xs_hbm.shape[0] // MOE_TR
        lax.fori_loop(nt_ref[0], n_all, start_unused, 0)
        for e in range(N_EXPERTS):
            @pl.when(tail_ref[e] >= 0)
            def _(e=e):
                zero_tile(tail_ref[e]).wait()
        lax.fori_loop(nt_ref[0], n_all, wait_unused, 0)

    srow = lax.broadcasted_iota(I32, (n_local, tm), 0)
    perm = jnp.where((srow == slot_ref[0:1, :]) | (srow == slot_ref[1:2, :]), 1.0, 0.0).astype(BF16)
    sbuf[...] = _dot(perm, h_ref[...].astype(BF16)).astype(BF16)
    _tile_groups(i, loff_ref, rows_ref, gpos_ref, sbuf, xs_hbm, sem, True)
    _wait_group_copies(pl.multiple_of(tot_ref[i], SORT_ALIGN), sbuf, xs_hbm, sem, True)


def _dispatch(meta, h, slots, n_slots):
    t, d = h.shape
    tm = MOE_TM
    n_local = _local_rows(tm)
    idx = lambda i, *_: (i, 0)
    grid_spec = pltpu.PrefetchScalarGridSpec(
        num_scalar_prefetch=6,
        grid=(t // tm,),
        in_specs=[pl.BlockSpec((tm, d), idx), pl.BlockSpec((2, tm), lambda i, *_: (0, i))],
        out_specs=pl.BlockSpec(memory_space=pl.ANY),
        scratch_shapes=[pltpu.VMEM((n_local, d), BF16), pltpu.VMEM((MOE_TR, d), BF16), pltpu.SemaphoreType.DMA],
    )
    return pl.pallas_call(
        _dispatch_kernel,
        grid_spec=grid_spec,
        out_shape=jax.ShapeDtypeStruct((n_slots, d), BF16),
        compiler_params=_cparams("arbitrary"),
        name="moe_dispatch",
    )(meta["loff"], meta["rows"], meta["gpos"], meta["tot"], meta["tail"], meta["n_tiles"], h, slots)


def _expert_kernel(te_ref, nt_ref, x_ref, wg_ref, wu_ref, wd_ref, o_ref, wgb, wub, wdb):
    r = pl.program_id(0)

    @pl.when((r == 0) | (te_ref[r] != te_ref[jnp.maximum(r - 1, 0)]))
    def _():
        wgb[...] = wg_ref[...].astype(BF16)
        wub[...] = wu_ref[...].astype(BF16)
        wdb[...] = wd_ref[...].astype(BF16)

    @pl.when(r < nt_ref[0])
    def _():
        x = x_ref[...]
        a = _silu(_dot(x, wgb[...])) * _dot(x, wub[...])
        o_ref[...] = _dot(a.astype(BF16), wdb[...]).astype(BF16)

    @pl.when(r >= nt_ref[0])
    def _():
        o_ref[...] = jnp.zeros_like(o_ref)


def _expert_mlps(tile_expert, n_tiles, xs, wg, wu, wd):
    n_slots, d = xs.shape
    tr = MOE_TR
    dff = wg.shape[2]
    grid_spec = pltpu.PrefetchScalarGridSpec(
        num_scalar_prefetch=2,
        grid=(n_slots // tr,),
        in_specs=[
            pl.BlockSpec((tr, d), lambda r, te, nt: (jnp.where(r < nt[0], r, 0), 0)),
            pl.BlockSpec((None, d, dff), lambda r, te, nt: (te[r], 0, 0)),
            pl.BlockSpec((None, d, dff), lambda r, te, nt: (te[r], 0, 0)),
            pl.BlockSpec((None, dff, d), lambda r, te, nt: (te[r], 0, 0)),
        ],
        out_specs=pl.BlockSpec((tr, d), lambda r, te, nt: (r, 0)),
        scratch_shapes=[pltpu.VMEM((d, dff), BF16), pltpu.VMEM((d, dff), BF16), pltpu.VMEM((dff, d), BF16)],
    )
    return pl.pallas_call(
        _expert_kernel,
        grid_spec=grid_spec,
        out_shape=jax.ShapeDtypeStruct((n_slots, d), BF16),
        compiler_params=_cparams("arbitrary"),
        name="expert_mlps",
    )(tile_expert, n_tiles, xs, wg, wu, wd)


def _combine_kernel(loff_ref, rows_ref, gpos_ref, tot_ref, ys_hbm, h_ref, col_ref, p_ref, g_ref, b_ref,
                    pg_ref, pp_ref, o_ref, ybuf, sem):
    i = pl.program_id(0)
    n = pl.num_programs(0)
    tm = h_ref.shape[0]
    n_local = ybuf.shape[1]
    slot = i % 2

    @pl.when(i == 0)
    def _():
        ybuf[...] = jnp.zeros_like(ybuf)
        _tile_groups(0, loff_ref, rows_ref, gpos_ref, ybuf.at[0], ys_hbm, sem.at[0], False)

    @pl.when(i + 1 < n)
    def _():
        _tile_groups(i + 1, loff_ref, rows_ref, gpos_ref, ybuf.at[1 - slot], ys_hbm, sem.at[1 - slot], False)

    _wait_group_copies(pl.multiple_of(tot_ref[i], SORT_ALIGN), ybuf.at[slot], ys_hbm, sem.at[slot], False)
    cols = col_ref[...]
    scol = lax.broadcasted_iota(I32, (tm, n_local), 1).astype(F32)
    y = ybuf[slot]
    ffn = None
    for k in range(2):
        pick = jnp.where(scol == cols[:, k:k + 1], 1.0, 0.0).astype(BF16)
        term = cols[:, 2 + k:3 + k] * _dot(pick, y)
        ffn = term if ffn is None else ffn + term
    h2 = _layer_norm_rows(DEEPNORM_ALPHA * h_ref[...] + ffn, g_ref[...], b_ref[...])
    gate = _sigmoid(_dot(h2.astype(BF16), pg_ref[...]))
    pe = _dot(p_ref[...].astype(BF16), pp_ref[...])
    o_ref[...] = h2 + gate * pe


def _combine(meta, ys, h, cols, p2, ln_g, ln_b, pg_bf, pp_bf):
    t, d = h.shape
    tm = MOE_TM
    pdim = p2.shape[1]
    row = lambda i, *_: (i, 0)
    full = lambda i, *_: (0, 0)
    grid_spec = pltpu.PrefetchScalarGridSpec(
        num_scalar_prefetch=4,
        grid=(t // tm,),
        in_specs=[
            pl.BlockSpec(memory_space=pl.ANY),
            pl.BlockSpec((tm, d), row),
            pl.BlockSpec((tm, LANES), row),
            pl.BlockSpec((tm, pdim), row),
            pl.BlockSpec((1, d), full),
            pl.BlockSpec((1, d), full),
            pl.BlockSpec(pg_bf.shape, full),
            pl.BlockSpec(pp_bf.shape, full),
        ],
        out_specs=pl.BlockSpec((tm, d), row),
        scratch_shapes=[pltpu.VMEM((2, _local_rows(tm), d), BF16), pltpu.SemaphoreType.DMA((2,))],
    )
    return pl.pallas_call(
        _combine_kernel,
        grid_spec=grid_spec,
        out_shape=jax.ShapeDtypeStruct((t, d), F32),
        compiler_params=_cparams("arbitrary"),
        name="moe_combine",
    )(meta["loff"], meta["rows"], meta["gpos"], meta["tot"], ys, h, cols, p2, ln_g, ln_b, pg_bf, pp_bf)


def _rotary_tables(seq):
    half = RET_DK // 2
    inv = (np.float32(ROPE_BASE) ** (-np.arange(half, dtype=np.float32) / np.float32(half))).astype(np.float32)
    ang = (np.arange(seq, dtype=np.float32)[:, None] * inv[None, :]).astype(np.float32)
    cos = np.cos(ang.astype(np.float64))
    sin = np.sin(ang.astype(np.float64))
    cos_h = np.concatenate([cos, cos], axis=1)
    sin_h = np.concatenate([-sin, sin], axis=1)
    return (jnp.asarray(np.tile(cos_h, (1, RET_HEADS)), F32), jnp.asarray(np.tile(sin_h, (1, RET_HEADS)), F32))


def _retention_tables():
    c = RET_CHUNK
    h = np.arange(RET_HEADS, dtype=np.float64)
    log_g = np.log1p(-np.exp2(-5.0 - h))
    j = np.arange(c, dtype=np.float64)
    rel = j[:, None] - j[None, :]
    din = np.where(rel >= 0, np.exp(np.maximum(rel, 0.0)[None] * log_g[:, None, None]), 0.0)
    qd = np.exp((j + 1.0)[None] * log_g[:, None])
    kd = np.exp((c - 1.0 - j)[None] * log_g[:, None])
    cd = np.exp(c * log_g)
    qd = np.broadcast_to(qd[:, :, None], (RET_HEADS, c, LANES))
    kd = np.broadcast_to(kd[:, :, None], (RET_HEADS, c, LANES))
    cd = np.broadcast_to(cd[:, None, None], (RET_HEADS, 1, LANES))
    return tuple(jnp.asarray(a, F32) for a in (din, qd, kd, cd))


def _t5_bucket_np(dist):
    max_exact = REL_BUCKETS // 2
    d = np.maximum(dist, 1).astype(np.float32)
    large = max_exact + (np.log(d / np.float32(max_exact)) / np.float32(math.log(REL_MAX_DIST / max_exact))
                         * np.float32(REL_BUCKETS - max_exact)).astype(np.int32)
    large = np.minimum(large, REL_BUCKETS - 1)
    return np.where(dist < max_exact, dist, large)


def _diff_bias_tables(rel_bias, seq):
    c = ATT_RQ
    r = np.arange(c)
    dist0 = r[:, None] - r[None, :]
    far = REL_BUCKETS - 1
    assert np.all(_t5_bucket_np(np.arange(c + 1, max(seq, 2 * c))) == far)
    bidx = np.stack([_t5_bucket_np(np.maximum(dist0, 0)), _t5_bucket_np(dist0 + c)])
    rb = rel_bias.astype(F32).T
    shifted = (rb - rb[:, far:far + 1]) * LOG2E
    bidx = jnp.asarray(bidx, I32)[None]
    tab = jnp.zeros((rb.shape[0], 2, c, c), F32)
    for bucket in range(REL_BUCKETS - 1):
        tab = jnp.where(bidx == bucket, shifted[:, bucket][:, None, None, None], tab)
    causal = jnp.asarray(np.stack([dist0 >= 0, np.ones_like(dist0, bool)]))[None]
    return jnp.where(causal, tab, NEG_BIG)


def _local_rows(tm):
    need = 2 * tm + N_EXPERTS * (SORT_ALIGN - 1)
    return -(-need // LANES) * LANES


def _round_up(a, m):
    return ((a + m - 1) // m) * m


def _route_meta(route, cnt, t):
    tm, tr = MOE_TM, MOE_TR
    nt = t // tm
    counts = cnt.reshape(nt, N_EXPERTS, LANES)[:, :, 0].astype(I32)
    rows = _round_up(counts, SORT_ALIGN)
    loff = jnp.cumsum(rows, axis=1) - rows
    seg = jnp.sum(rows, axis=0)
    seg_pad = _round_up(seg, tr)
    ends = jnp.cumsum(seg_pad)
    offs = ends - seg_pad
    gpos = offs[None, :] + jnp.cumsum(rows, axis=0) - rows
    n_slots = 2 * t + nt * N_EXPERTS * (SORT_ALIGN - 1)
    n_slots = _round_up(n_slots, tr) + N_EXPERTS * tr
    n_tiles = (ends[-1] // tr).astype(I32)
    tile_start = jnp.arange(n_slots // tr, dtype=I32) * tr
    tile_expert = jnp.sum((tile_start[:, None] >= ends[None, :]).astype(I32), axis=1)
    last = jnp.sum((((n_tiles - 1) * tr) >= ends).astype(I32))
    meta = {
        "loff": loff.reshape(-1).astype(I32), "rows": rows.reshape(-1).astype(I32),
        "gpos": gpos.reshape(-1).astype(I32), "tot": jnp.sum(rows, axis=1).astype(I32),
        "tail": jnp.where(seg_pad > seg, ends - tr, -1).astype(I32),
        "tile_expert": jnp.minimum(tile_expert, last).astype(I32), "n_tiles": n_tiles.reshape(1),
    }
    experts = route[0:2].astype(I32).reshape(2, nt, tm)
    onehot = experts[..., None] == jnp.arange(N_EXPERTS, dtype=I32)
    slots = jnp.sum(jnp.where(onehot, loff[None, :, None, :], 0), axis=-1) + route[4:6].astype(I32).reshape(2, nt, tm)
    slots = slots.reshape(2, t)
    cols = jnp.concatenate([slots.astype(F32), route[2:4]], axis=0).T
    cols = jnp.pad(cols, ((0, 0), (0, LANES - cols.shape[1])))
    return meta, slots, cols, n_slots


def kernel(x, p, rel_bias, router_w, even_w_in, even_w_out, even_lambda, even_diff_norm, even_ret_norm,
           odd_w_in, odd_b_forget, odd_w_out, ln_mix_g, ln_mix_b, ln_ffn_g, ln_ffn_b,
           moe_w_gate, moe_w_up, moe_w_down, ple_proj, ple_gate):
    b, s, d = x.shape
    t = b * s
    assert d == 1024 and p.shape[0] == DEPTH and even_w_in.shape[2] == 3072
    assert odd_w_in.shape[2] == 3 * d + FOX_HEADS and moe_w_gate.shape[1] == N_EXPERTS
    assert s % RET_CHUNK == 0 and s % min(ATT_TQ, s) == 0 and s % min(FOX_TQ, s) == 0
    assert t % min(PROJ_TM, s) == 0 and t % MOE_TM == 0

    cos_t, sin_t = _rotary_tables(s)
    ret_tabs = _retention_tables()
    bias_tab = _diff_bias_tables(rel_bias, s)
    rwt = router_w.astype(F32).T
    tok = np.arange(MOE_TM)
    tri = jnp.asarray(tok[:, None] < tok[None, :], BF16)

    x2 = x.reshape(t, d)
    for i in range(DEPTH):
        j = i // 2
        if i % 2 == 0:
            lam_init = 0.8 - 0.6 * math.exp(-0.3 * i)
            qa, ka, va, qb, kb, vb, gb = _even_inproj(x2, even_w_in[j].astype(BF16), cos_t, sin_t, s)
            sh = lambda a: a.reshape(b, s, a.shape[1])
            ya = _diff_attention(sh(qa), sh(ka), sh(va), bias_tab, even_lambda[j].astype(F32),
                                 even_diff_norm[j].reshape(1, -1).astype(F32), lam_init)
            yb = _retention(sh(qb), sh(kb), sh(vb), sh(gb), ret_tabs, even_ret_norm[j].reshape(1, -1).astype(F32))
            w_out = even_w_out[j].astype(BF16)
            n_a = ya.shape[2]
            ys = [ya.reshape(t, -1), yb.reshape(t, -1)]
            ws = [w_out[:n_a], w_out[n_a:]]
        else:
            w_in = odd_w_in[j]
            wf = jnp.zeros((d, LANES), BF16).at[:, :FOX_HEADS].set(w_in[:, 3 * d:].astype(BF16))
            bfg = jnp.zeros((1, LANES), F32).at[0, :FOX_HEADS].set(odd_b_forget[j].astype(F32))
            q, k, v, cum = _odd_inproj(x2, w_in[:, :3 * d].astype(BF16), wf, bfg, s)
            cum3 = cum.reshape(b, s, LANES)
            cum_t = jnp.transpose(cum3[:, :, :FOX_HEADS], (0, 2, 1)).reshape(b, FOX_HEADS // 2, 2, s)
            y = _fox_attention(q.reshape(b, s, d), k.reshape(b, s, d), v.reshape(b, s, d), cum3, cum_t)
            ys = [y.reshape(t, d)]
            ws = [odd_w_out[j].astype(BF16)]
        h, route, cnt = _outproj_router(ys, ws, x2, ln_mix_g[i].reshape(1, d), ln_mix_b[i].reshape(1, d), rwt, tri)
        meta, slots, cols, n_slots = _route_meta(route, cnt, t)
        xs = _dispatch(meta, h, slots, n_slots)
        rows = _expert_mlps(meta["tile_expert"], meta["n_tiles"], xs, moe_w_gate[i], moe_w_up[i], moe_w_down[i])
        x2 = _combine(meta, rows, h, cols, p[i].reshape(t, -1), ln_ffn_g[i].reshape(1, d),
                      ln_ffn_b[i].reshape(1, d), ple_gate[i].astype(BF16), ple_proj[i].astype(BF16))
    return x2.reshape(b, s, d)
```

```python
import functools
import math

import numpy as np
import jax
import jax.numpy as jnp
from jax import lax
from jax.experimental import pallas as pl
from jax.experimental.pallas import tpu as pltpu

F32 = jnp.float32
BF16 = jnp.bfloat16
I32 = jnp.int32

DIFF_HEADS = 4
DIFF_DK = 64
RET_HEADS = 4
RET_DK = 64
RET_DV = 128
RET_CHUNK = 128
FOX_HEADS = 16
FOX_DH = 64
REL_BUCKETS = 32
REL_MAX_DIST = 128
N_GROUPS = 4
EXPERTS_PER_GROUP = 4
N_EXPERTS = 16
DEPTH = 2
DEEPNORM_ALPHA = (2 * DEPTH) ** 0.25
LN_EPS = 1e-5
ROPE_BASE = 10000.0
NEG_BIG = -1e30
LOG2E = math.log2(math.e)

VMEM_LIMIT_BYTES = 48 * 1024 * 1024
LANES = 128

PROJ_TM = 512
ATT_TQ = 512
FOX_TQ = 512
ATT_RQ = LANES
ATT_LOOKAHEAD = 3
MOE_TR = 512
MOE_TM = 512
SORT_ALIGN = 16
ROUTE_ROWS = 8


def _cparams(*sem):
    return pltpu.CompilerParams(dimension_semantics=sem, vmem_limit_bytes=VMEM_LIMIT_BYTES)


def _dot(a, b):
    return jnp.dot(a, b, preferred_element_type=F32)


def _dot_nt(a, b):
    return lax.dot_general(a, b, (((1,), (1,)), ((), ())), preferred_element_type=F32)


def _layer_norm_rows(z, g, b):
    mu = jnp.mean(z, axis=-1, keepdims=True)
    zc = z - mu
    var = jnp.mean(zc * zc, axis=-1, keepdims=True)
    return zc * lax.rsqrt(var + LN_EPS) * g + b


def _silu(x):
    return x * (1.0 / (1.0 + jnp.exp(-x)))


def _sigmoid(x):
    return 1.0 / (1.0 + jnp.exp(-x))


def _even_inproj_kernel(x_ref, w_ref, cos_ref, sin_ref,
                        qa_ref, ka_ref, va_ref, qb_ref, kb_ref, vb_ref, gb_ref):
    x = x_ref[...].astype(BF16)

    def mm(c0, c1):
        return _dot(x, w_ref[:, c0:c1])

    qa_ref[...] = (mm(0, 512) * (DIFF_DK ** -0.5 * LOG2E)).astype(BF16)
    ka_ref[...] = mm(512, 1024).astype(BF16)
    va_ref[...] = mm(1024, 1536).astype(BF16)
    qk = mm(1536, 2048)
    cos = cos_ref[...]
    sin = sin_ref[...]
    lane = lax.broadcasted_iota(I32, cos.shape, 1)
    first_half = (lane % RET_DK) < (RET_DK // 2)

    def rot(t):
        sw = jnp.where(first_half, pltpu.roll(t, t.shape[1] - RET_DK // 2, 1),
                       pltpu.roll(t, RET_DK // 2, 1))
        return t * cos + sw * sin

    qb_ref[...] = rot(qk[:, :256]).astype(BF16)
    kb_ref[...] = (rot(qk[:, 256:]) * (RET_DK ** -0.5)).astype(BF16)
    vb_ref[...] = mm(2048, 2560).astype(BF16)
    gb_ref[...] = mm(2560, 3072).astype(BF16)


def _even_inproj(x2, w_bf, cos_t, sin_t, seq):
    t, d = x2.shape
    tm = min(PROJ_TM, seq)
    nblk_s = seq // tm
    widths = (512, 512, 512, 256, 256, 512, 512)
    row = lambda i: (i, 0)
    return pl.pallas_call(
        _even_inproj_kernel,
        grid=(t // tm,),
        in_specs=[
            pl.BlockSpec((tm, d), row),
            pl.BlockSpec(w_bf.shape, lambda i: (0, 0)),
            pl.BlockSpec((tm, 256), lambda i: (i % nblk_s, 0)),
            pl.BlockSpec((tm, 256), lambda i: (i % nblk_s, 0)),
        ],
        out_specs=[pl.BlockSpec((tm, w), row) for w in widths],
        out_shape=[jax.ShapeDtypeStruct((t, w), BF16) for w in widths],
        compiler_params=_cparams("parallel"),
        name="even_inproj",
    )(x2, w_bf, cos_t, sin_t)


def _run_chains(chains, lookahead, scores, finish):
    pending = [scores(c) for c in chains[:lookahead]]
    for n, chain in enumerate(chains):
        if n + lookahead < len(chains):
            pending.append(scores(chains[n + lookahead]))
        finish(chain, pending.pop(0))


def _diff_attn_kernel(lam_ref, q_ref, k_ref, v_ref, bias_ref, g_ref, o_ref, qm_ref, m_ref, l_ref, acc_ref,
                      *, lam_init, tq):
    seq = q_ref.shape[0]
    tk = tq
    rq = ATT_RQ
    nr = tq // rq
    q = q_ref[...]
    lane = lax.broadcasted_iota(I32, q.shape, 1)
    zero = jnp.zeros_like(q)
    qm_ref[0] = jnp.where(lane < DIFF_DK, q, zero)
    qm_ref[1] = jnp.where(lane >= DIFF_DK, q, zero)
    m_ref[...] = jnp.full(m_ref.shape, NEG_BIG, F32)
    l_ref[...] = jnp.zeros(l_ref.shape, F32)
    acc_ref[...] = jnp.zeros(acc_ref.shape, F32)
    lp = lam_ref[...]
    lam = (jnp.exp(jnp.sum(lp[0:1, :] * lp[1:2, :], axis=-1, keepdims=True))
           - jnp.exp(jnp.sum(lp[2:3, :] * lp[3:4, :], axis=-1, keepdims=True)) + lam_init)

    chains = [(ii, j, r, c) for ii in range(seq // tq) for j in range(ii + 1)
              for r in range(nr) for c in range(2)]

    def n_keys(ii, j, r):
        return (r + 1) * rq if j == ii else tk

    def scores(chain):
        ii, j, r, c = chain
        k = k_ref[j * tk:j * tk + n_keys(ii, j, r), :]
        return _dot_nt(qm_ref[c, ii * tq + r * rq:ii * tq + (r + 1) * rq, :], k)

    def finish(chain, s):
        ii, j, r, c = chain
        rows = slice(ii * tq + r * rq, ii * tq + (r + 1) * rq)
        nk = n_keys(ii, j, r)
        v = v_ref[j * tk:j * tk + nk, :]
        sc = []
        for kc in range(nk // rq):
            t = s[:, kc * rq:(kc + 1) * rq]
            back = (ii * nr + r) - (j * nr + kc)
            if back <= 1:
                t = t + bias_ref[back]
            sc.append(t)
        mx = sc[0]
        for t in sc[1:]:
            mx = jnp.maximum(mx, t)
        m_old = m_ref[c, rows, :]
        m_new = jnp.maximum(m_old, jnp.max(mx, axis=-1, keepdims=True))
        alpha = jnp.exp2(m_old - m_new)
        ps = [jnp.exp2(t - m_new) for t in sc]
        psum = ps[0]
        for t in ps[1:]:
            psum = psum + t
        l_ref[c, rows, :] = alpha * l_ref[c, rows, :] + psum
        p = jnp.concatenate([t.astype(BF16) for t in ps], axis=1)
        acc_ref[c, rows, :] = alpha * acc_ref[c, rows, :] + _dot(p, v)
        m_ref[c, rows, :] = m_new
        if j == ii and r == nr - 1 and c == 1:
            blk = slice(ii * tq, (ii + 1) * tq)
            l0 = jnp.sum(l_ref[0, blk, :], axis=-1, keepdims=True)
            l1 = jnp.sum(l_ref[1, blk, :], axis=-1, keepdims=True)
            o = acc_ref[0, blk, :] / l0 - lam * (acc_ref[1, blk, :] / l1)
            o = o * lax.rsqrt(jnp.mean(o * o, axis=-1, keepdims=True) + LN_EPS)
            o_ref[blk, :] = (o * g_ref[...] * (1.0 - lam_init)).astype(BF16)

    _run_chains(chains, ATT_LOOKAHEAD, scores, finish)


def _diff_attention(qa, ka, va, bias_tab, lam_params, diff_g, lam_init):
    b, s, _ = qa.shape
    tq = min(ATT_TQ, s)
    kern = functools.partial(_diff_attn_kernel, lam_init=lam_init, tq=tq)
    seq_blk = lambda bi, h: (bi, 0, h)
    return pl.pallas_call(
        kern,
        grid=(b, DIFF_HEADS),
        in_specs=[
            pl.BlockSpec(lam_params.shape, lambda bi, h: (0, 0)),
            pl.BlockSpec((None, s, LANES), seq_blk),
            pl.BlockSpec((None, s, LANES), seq_blk),
            pl.BlockSpec((None, s, LANES), seq_blk),
            pl.BlockSpec((None, 2, ATT_RQ, ATT_RQ), lambda bi, h: (h, 0, 0, 0)),
            pl.BlockSpec((1, LANES), lambda bi, h: (0, 0)),
        ],
        out_specs=pl.BlockSpec((None, s, LANES), seq_blk),
        out_shape=jax.ShapeDtypeStruct((b, s, DIFF_HEADS * LANES), BF16),
        scratch_shapes=[pltpu.VMEM((2, s, LANES), BF16), pltpu.VMEM((2, s, LANES), F32),
                        pltpu.VMEM((2, s, LANES), F32), pltpu.VMEM((2, s, LANES), F32)],
        compiler_params=_cparams("parallel", "parallel"),
        name="diff_attention",
    )(lam_params, qa, ka, va, bias_tab, diff_g)


def _retention_kernel(q_ref, k_ref, v_ref, gate_ref, din_ref, qd_ref, kd_ref, cd_ref, g_ref, o_ref):
    s = q_ref.shape[0]
    c = RET_CHUNK
    lane = lax.broadcasted_iota(I32, (c, LANES), 1)
    g = g_ref[...]
    states = [jnp.zeros((LANES, RET_DV), F32) for _ in range(2)]
    for n in range(s // c):
        r = slice(n * c, (n + 1) * c)
        q_pair = q_ref[r, :].astype(F32)
        k_pair = k_ref[r, :].astype(F32)
        for par in range(2):
            own = (lane // RET_DK) == par
            cols = slice(par * RET_DV, (par + 1) * RET_DV)
            q = jnp.where(own, q_pair, 0.0)
            k = jnp.where(own, k_pair, 0.0)
            v = v_ref[r, cols]
            scores = _dot_nt(q.astype(BF16), k.astype(BF16)) * din_ref[par]
            inner = _dot(scores.astype(BF16), v)
            cross = _dot((q * qd_ref[par]).astype(BF16), states[par].astype(BF16))
            kv = _dot((k * kd_ref[par]).T.astype(BF16), v)
            states[par] = cd_ref[par] * states[par] + kv
            y = inner + cross
            mu = jnp.mean(y, axis=-1, keepdims=True)
            yc = y - mu
            var = jnp.mean(yc * yc, axis=-1, keepdims=True)
            yn = yc * lax.rsqrt(var + LN_EPS) * g
            gate = gate_ref[r, cols].astype(F32)
            o_ref[r, cols] = (_silu(gate) * yn).astype(BF16)


def _retention(qb, kb, vb, gb, tabs, ret_g):
    b, s, _ = qb.shape
    din, qd, kd, cd = tabs
    pair = lambda bi, hp: (bi, 0, hp)
    tab = lambda bi, hp: (hp, 0, 0)
    return pl.pallas_call(
        _retention_kernel,
        grid=(b, RET_HEADS // 2),
        in_specs=[
            pl.BlockSpec((None, s, LANES), pair),
            pl.BlockSpec((None, s, LANES), pair),
            pl.BlockSpec((None, s, 2 * RET_DV), pair),
            pl.BlockSpec((None, s, 2 * RET_DV), pair),
            pl.BlockSpec((2, RET_CHUNK, RET_CHUNK), tab),
            pl.BlockSpec((2, RET_CHUNK, LANES), tab),
            pl.BlockSpec((2, RET_CHUNK, LANES), tab),
            pl.BlockSpec((2, 1, LANES), tab),
            pl.BlockSpec((1, RET_DV), lambda bi, hp: (0, 0)),
        ],
        out_specs=pl.BlockSpec((None, s, 2 * RET_DV), pair),
        out_shape=jax.ShapeDtypeStruct((b, s, RET_HEADS * RET_DV), BF16),
        compiler_params=_cparams("parallel", "parallel"),
        name="retention",
    )(qb, kb, vb, gb, din, qd, kd, cd, ret_g)


def _odd_inproj_kernel(x_ref, w_ref, wf_ref, bf_ref, q_ref, k_ref, v_ref, cum_ref, carry_ref, *, nblk_s):
    i = pl.program_id(0)
    x = x_ref[...].astype(BF16)
    d = q_ref.shape[1]
    q_ref[...] = (_dot(x, w_ref[:, 0:d]) * (FOX_DH ** -0.5 * LOG2E)).astype(BF16)
    k_ref[...] = _dot(x, w_ref[:, d:2 * d]).astype(BF16)
    v_ref[...] = _dot(x, w_ref[:, 2 * d:3 * d]).astype(BF16)
    z = _dot(x, wf_ref[...]) + bf_ref[...]
    c = jnp.minimum(z, 0.0) - jnp.log1p(jnp.exp(-jnp.abs(z)))
    tm = c.shape[0]
    row = lax.broadcasted_iota(I32, c.shape, 0)
    step = 1
    while step < tm:
        c = c + jnp.where(row >= step, pltpu.roll(c, step, 0), 0.0)
        step *= 2

    @pl.when(i % nblk_s == 0)
    def _():
        carry_ref[...] = jnp.zeros_like(carry_ref)

    c = c + carry_ref[...]
    cum_ref[...] = c * LOG2E
    carry_ref[...] = c[tm - 1:tm, :]


def _odd_inproj(x2, w_bf, wf_bf, bfg, seq):
    t, d = x2.shape
    tm = min(PROJ_TM, seq)
    nblk_s = seq // tm
    row = lambda i: (i, 0)
    kern = functools.partial(_odd_inproj_kernel, nblk_s=nblk_s)
    return pl.pallas_call(
        kern,
        grid=(t // tm,),
        in_specs=[
            pl.BlockSpec((tm, d), row),
            pl.BlockSpec(w_bf.shape, lambda i: (0, 0)),
            pl.BlockSpec(wf_bf.shape, lambda i: (0, 0)),
            pl.BlockSpec(bfg.shape, lambda i: (0, 0)),
        ],
        out_specs=[pl.BlockSpec((tm, d), row)] * 3 + [pl.BlockSpec((tm, LANES), row)],
        out_shape=[jax.ShapeDtypeStruct((t, d), BF16)] * 3 + [jax.ShapeDtypeStruct((t, LANES), F32)],
        scratch_shapes=[pltpu.VMEM((1, LANES), F32)],
        compiler_params=_cparams("arbitrary"),
        name="odd_inproj",
    )(x2, w_bf, wf_bf, bfg)


def _fox_attn_kernel(q_ref, k_ref, v_ref, cq_ref, ck_ref, o_ref, qm_ref, va_ref, cqc_ref, m_ref, acc_ref, *, tq):
    seq = q_ref.shape[0]
    tk = tq
    hp = pl.program_id(1)
    q = q_ref[...]
    v = v_ref[...]
    lane = lax.broadcasted_iota(I32, (seq, LANES), 1)
    cq_all = cq_ref[...]
    for par in range(2):
        own = (lane // FOX_DH) == par
        qm_ref[par] = jnp.where(own, q, jnp.zeros_like(q))
        va_ref[par] = jnp.where(own, v, jnp.ones_like(v))
        cq = jnp.sum(jnp.where(lane == 2 * hp + par, cq_all, 0.0), axis=-1, keepdims=True)
        cqc_ref[par] = jnp.broadcast_to(cq, (seq, LANES))
    m_ref[...] = jnp.full(m_ref.shape, NEG_BIG, F32)
    acc_ref[...] = jnp.zeros(acc_ref.shape, F32)
    rq = ATT_RQ
    nr = tq // rq
    upper = (lax.broadcasted_iota(I32, (rq, rq), 1) > lax.broadcasted_iota(I32, (rq, rq), 0))
    lane_q = lax.broadcasted_iota(I32, (tq, LANES), 1)

    chains = [(ii, j, r, par) for ii in range(seq // tq) for j in range(ii + 1)
              for r in range(nr) for par in range(2)]

    def n_keys(ii, j, r):
        return (r + 1) * rq if j == ii else tk

    def scores(chain):
        ii, j, r, par = chain
        k = k_ref[j * tk:j * tk + n_keys(ii, j, r), :]
        return _dot_nt(qm_ref[par, ii * tq + r * rq:ii * tq + (r + 1) * rq, :], k)

    def finish(chain, s):
        ii, j, r, par = chain
        rows = slice(ii * tq + r * rq, ii * tq + (r + 1) * rq)
        nk = n_keys(ii, j, r)
        ck = ck_ref[par:par + 1, j * tk:j * tk + nk]
        cq = cqc_ref[par, rows, :]
        sc = []
        for kc in range(nk // rq):
            t = s[:, kc * rq:(kc + 1) * rq] - ck[:, kc * rq:(kc + 1) * rq]
            if j == ii and kc == r:
                t = jnp.where(upper, NEG_BIG, t)
            sc.append(t)
        mx = sc[0]
        for t in sc[1:]:
            mx = jnp.maximum(mx, t)
        m_old = m_ref[par, rows, :]
        m_new = jnp.maximum(m_old, jnp.max(mx, axis=-1, keepdims=True) + cq)
        alpha = jnp.exp2(m_old - m_new)
        shift = m_new - cq
        p = jnp.concatenate([jnp.exp2(t - shift).astype(BF16) for t in sc], axis=1)
        acc_ref[par, rows, :] = alpha * acc_ref[par, rows, :] + _dot(p, va_ref[par, j * tk:j * tk + nk, :])
        m_ref[par, rows, :] = m_new
        if j == ii and r == nr - 1 and par == 1:
            blk = slice(ii * tq, (ii + 1) * tq)
            acc0 = acc_ref[0, blk, :]
            acc1 = acc_ref[1, blk, :]
            out0 = acc0 / acc0[:, FOX_DH:FOX_DH + 1]
            out1 = acc1 / acc1[:, 0:1]
            o_ref[blk, :] = jnp.where(lane_q < FOX_DH, out0, out1).astype(BF16)

    _run_chains(chains, ATT_LOOKAHEAD, scores, finish)


def _fox_attention(q, k, v, cum, cum_t):
    b, s, d = q.shape
    tq = min(FOX_TQ, s)
    npair = d // LANES
    seq_blk = lambda bi, h: (bi, 0, h)
    return pl.pallas_call(
        functools.partial(_fox_attn_kernel, tq=tq),
        grid=(b, npair),
        in_specs=[
            pl.BlockSpec((None, s, LANES), seq_blk),
            pl.BlockSpec((None, s, LANES), seq_blk),
            pl.BlockSpec((None, s, LANES), seq_blk),
            pl.BlockSpec((None, s, LANES), lambda bi, h: (bi, 0, 0)),
            pl.BlockSpec((None, None, 2, s), lambda bi, h: (bi, h, 0, 0)),
        ],
        out_specs=pl.BlockSpec((None, s, LANES), seq_blk),
        out_shape=jax.ShapeDtypeStruct((b, s, d), BF16),
        scratch_shapes=[pltpu.VMEM((2, s, LANES), BF16), pltpu.VMEM((2, s, LANES), BF16),
                        pltpu.VMEM((2, s, LANES), F32), pltpu.VMEM((2, s, LANES), F32),
                        pltpu.VMEM((2, s, LANES), F32)],
        compiler_params=_cparams("parallel", "parallel"),
        name="fox_attention",
    )(q, k, v, cum, cum_t)


def _outproj_router_kernel(*refs, n_y):
    y_refs = refs[:n_y]
    w_refs = refs[n_y:2 * n_y]
    x_ref, g_ref, b_ref, rwt_ref, tri_ref, h_ref, route_ref, cnt_ref = refs[2 * n_y:]
    mix = _dot(y_refs[0][...], w_refs[0][...])
    for yr, wr in zip(y_refs[1:], w_refs[1:]):
        mix = mix + _dot(yr[...], wr[...])
    h = _layer_norm_rows(DEEPNORM_ALPHA * x_ref[...] + mix, g_ref[...], b_ref[...])
    h_ref[...] = h

    rwt = rwt_ref[...]
    rw_hi = rwt.astype(BF16)
    rw_lo = (rwt - rw_hi.astype(F32)).astype(BF16)
    h_hi = h.astype(BF16)
    h_lo = (h - h_hi.astype(F32)).astype(BF16)
    logits = _dot_nt(rw_hi, h_hi) + (_dot_nt(rw_lo, h_hi) + _dot_nt(rw_hi, h_lo))
    tm = logits.shape[1]
    row = lax.broadcasted_iota(I32, (N_EXPERTS, tm), 0)
    mx = jnp.max(logits, axis=0, keepdims=True)
    ex = jnp.exp(logits - mx)
    probs = ex / jnp.sum(ex, axis=0, keepdims=True)
    grp = row // EXPERTS_PER_GROUP

    def top2(vals):
        v1 = jnp.max(vals, axis=0, keepdims=True)
        i1 = jnp.min(jnp.where(vals == v1, row, N_EXPERTS), axis=0, keepdims=True)
        rest = jnp.where(row == i1, -2.0, vals)
        v2 = jnp.max(rest, axis=0, keepdims=True)
        i2 = jnp.min(jnp.where(rest == v2, row, N_EXPERTS), axis=0, keepdims=True)
        return v1, i1, v2, i2

    best_score = None
    best = None
    for gi in range(N_GROUPS):
        v1, _, v2, _ = top2(jnp.where(grp == gi, probs, -1.0))
        score = v1 + v2
        if gi == 0:
            best_score, best = score, jnp.zeros_like(score, dtype=I32)
        else:
            better = score > best_score
            best = jnp.where(better, gi, best)
            best_score = jnp.where(better, score, best_score)
    v1, i1, v2, i2 = top2(jnp.where(grp == best, probs, -1.0))
    tot = v1 + v2
    g1 = v1 / tot
    g2 = v2 / tot

    onehot = jnp.where((row == i1) | (row == i2), 1.0, 0.0)
    pref = _dot(onehot.astype(BF16), tri_ref[...])
    r1 = jnp.sum(jnp.where(row == i1, pref, 0.0), axis=0, keepdims=True)
    r2 = jnp.sum(jnp.where(row == i2, pref, 0.0), axis=0, keepdims=True)
    cnt_ref[...] = jnp.broadcast_to(jnp.sum(onehot, axis=1, keepdims=True), cnt_ref.shape)

    row8 = lax.broadcasted_iota(I32, (ROUTE_ROWS, tm), 0)
    fields = (i1.astype(F32), i2.astype(F32), g1, g2, r1, r2)
    route = jnp.zeros((ROUTE_ROWS, tm), F32)
    for n, f in enumerate(fields):
        route = jnp.where(row8 == n, f, route)
    route_ref[...] = route


def _outproj_router(ys, ws, x2, ln_g, ln_b, rwt, tri):
    t, d = x2.shape
    tm = tri.shape[0]
    row = lambda i: (i, 0)
    full = lambda i: (0, 0)
    n_y = len(ys)
    kern = functools.partial(_outproj_router_kernel, n_y=n_y)
    return pl.pallas_call(
        kern,
        grid=(t // tm,),
        in_specs=([pl.BlockSpec((tm, y.shape[1]), row) for y in ys]
                  + [pl.BlockSpec(w.shape, full) for w in ws]
                  + [pl.BlockSpec((tm, d), row), pl.BlockSpec((1, d), full), pl.BlockSpec((1, d), full),
                     pl.BlockSpec(rwt.shape, full), pl.BlockSpec(tri.shape, full)]),
        out_specs=[pl.BlockSpec((tm, d), row), pl.BlockSpec((ROUTE_ROWS, tm), lambda i: (0, i)),
                   pl.BlockSpec((N_EXPERTS, LANES), row)],
        out_shape=[jax.ShapeDtypeStruct((t, d), F32), jax.ShapeDtypeStruct((ROUTE_ROWS, t), F32),
                   jax.ShapeDtypeStruct((t // tm * N_EXPERTS, LANES), F32)],
        compiler_params=_cparams("parallel"),
        name="outproj_router",
    )(*ys, *ws, x2, ln_g, ln_b, rwt, tri)


def _group_copies(n_rows, local_start, global_start, local_ref, global_ref, sem, to_global):
    size = SORT_ALIGN
    while size <= MOE_TM:
        done = n_rows & ~(2 * size - 1)
        lo = pl.multiple_of(local_start + done, SORT_ALIGN)
        go = pl.multiple_of(global_start + done, SORT_ALIGN)

        @pl.when((n_rows & size) != 0)
        def _(lo=lo, go=go, size=size):
            loc = local_ref.at[pl.ds(lo, size)]
            glo = global_ref.at[pl.ds(go, size)]
            (pltpu.make_async_copy(loc, glo, sem) if to_global else pltpu.make_async_copy(glo, loc, sem)).start()

        size *= 2


def _wait_group_copies(total_rows, local_ref, global_ref, sem, to_global):
    loc = local_ref.at[pl.ds(0, total_rows)]
    glo = global_ref.at[pl.ds(0, total_rows)]
    (pltpu.make_async_copy(loc, glo, sem) if to_global else pltpu.make_async_copy(glo, loc, sem)).wait()


def _tile_groups(tile, loff_ref, rows_ref, gpos_ref, local_ref, global_ref, sem, to_global):
    for e in range(N_EXPERTS):
        n = tile * N_EXPERTS + e
        _group_copies(rows_ref[n], loff_ref[n], gpos_ref[n], local_ref, global_ref, sem, to_global)


def _dispatch_kernel(loff_ref, rows_ref, gpos_ref, tot_ref, tail_ref, nt_ref, h_ref, slot_ref, xs_hbm,
                     sbuf, zbuf, sem):
    i = pl.program_id(0)
    tm = h_ref.shape[0]
    n_local = sbuf.shape[0]

    @pl.when(i == 0)
    def _():
        zbuf[...] = jnp.zeros_like(zbuf)

        def zero_tile(start):
            return pltpu.make_async_copy(zbuf, xs_hbm.at[pl.ds(pl.multiple_of(start, MOE_TR), MOE_TR)], sem)

        for e in range(N_EXPERTS):
            @pl.when(tail_ref[e] >= 0)
            def _(e=e):
                zero_tile(tail_ref[e]).start()

        def start_unused(r, c):
            zero_tile(r * MOE_TR).start()
            return c

        def wait_unused(r, c):
            zero_tile(r * MOE_TR).wait()
            return c

        n_all = xs_hbm.shape[0] // MOE_TR
        lax.fori_loop(nt_ref[0], n_all, start_unused, 0)
        for e in range(N_EXPERTS):
            @pl.when(tail_ref[e] >= 0)
            def _(e=e):
                zero_tile(tail_ref[e]).wait()
        lax.fori_loop(nt_ref[0], n_all, wait_unused, 0)

    srow = lax.broadcasted_iota(I32, (n_local, tm), 0)
    perm = jnp.where((srow == slot_ref[0:1, :]) | (srow == slot_ref[1:2, :]), 1.0, 0.0).astype(BF16)
    sbuf[...] = _dot(perm, h_ref[...].astype(BF16)).astype(BF16)
    _tile_groups(i, loff_ref, rows_ref, gpos_ref, sbuf, xs_hbm, sem, True)
    _wait_group_copies(pl.multiple_of(tot_ref[i], SORT_ALIGN), sbuf, xs_hbm, sem, True)


def _dispatch(meta, h, slots, n_slots):
    t, d = h.shape
    tm = MOE_TM
    n_local = _local_rows(tm)
    idx = lambda i, *_: (i, 0)
    grid_spec = pltpu.PrefetchScalarGridSpec(
        num_scalar_prefetch=6,
        grid=(t // tm,),
        in_specs=[pl.BlockSpec((tm, d), idx), pl.BlockSpec((2, tm), lambda i, *_: (0, i))],
        out_specs=pl.BlockSpec(memory_space=pl.ANY),
        scratch_shapes=[pltpu.VMEM((n_local, d), BF16), pltpu.VMEM((MOE_TR, d), BF16), pltpu.SemaphoreType.DMA],
    )
    return pl.pallas_call(
        _dispatch_kernel,
        grid_spec=grid_spec,
        out_shape=jax.ShapeDtypeStruct((n_slots, d), BF16),
        compiler_params=_cparams("arbitrary"),
        name="moe_dispatch",
    )(meta["loff"], meta["rows"], meta["gpos"], meta["tot"], meta["tail"], meta["n_tiles"], h, slots)


def _expert_kernel(te_ref, nt_ref, x_ref, wg_ref, wu_ref, wd_ref, o_ref, wgb, wub, wdb):
    r = pl.program_id(0)

    @pl.when((r == 0) | (te_ref[r] != te_ref[jnp.maximum(r - 1, 0)]))
    def _():
        wgb[...] = wg_ref[...].astype(BF16)
        wub[...] = wu_ref[...].astype(BF16)
        wdb[...] = wd_ref[...].astype(BF16)

    @pl.when(r < nt_ref[0])
    def _():
        x = x_ref[...]
        a = _silu(_dot(x, wgb[...])) * _dot(x, wub[...])
        o_ref[...] = _dot(a.astype(BF16), wdb[...]).astype(BF16)

    @pl.when(r >= nt_ref[0])
    def _():
        o_ref[...] = jnp.zeros_like(o_ref)


def _expert_mlps(tile_expert, n_tiles, xs, wg, wu, wd, layer):
    n_slots, d = xs.shape
    tr = MOE_TR
    dff = wg.shape[3]
    grid_spec = pltpu.PrefetchScalarGridSpec(
        num_scalar_prefetch=2,
        grid=(n_slots // tr,),
        in_specs=[
            pl.BlockSpec((tr, d), lambda r, te, nt: (jnp.where(r < nt[0], r, 0), 0)),
            pl.BlockSpec((None, None, d, dff), lambda r, te, nt: (layer, te[r], 0, 0)),
            pl.BlockSpec((None, None, d, dff), lambda r, te, nt: (layer, te[r], 0, 0)),
            pl.BlockSpec((None, None, dff, d), lambda r, te, nt: (layer, te[r], 0, 0)),
        ],
        out_specs=pl.BlockSpec((tr, d), lambda r, te, nt: (r, 0)),
        scratch_shapes=[pltpu.VMEM((d, dff), BF16), pltpu.VMEM((d, dff), BF16), pltpu.VMEM((dff, d), BF16)],
    )
    return pl.pallas_call(
        _expert_kernel,
        grid_spec=grid_spec,
        out_shape=jax.ShapeDtypeStruct((n_slots, d), BF16),
        compiler_params=_cparams("arbitrary"),
        name="expert_mlps",
    )(tile_expert, n_tiles, xs, wg, wu, wd)


def _combine_kernel(loff_ref, rows_ref, gpos_ref, tot_ref, ys_hbm, h_ref, col_ref, p_ref, g_ref, b_ref,
                    pg_ref, pp_ref, o_ref, ybuf, sem):
    i = pl.program_id(0)
    n = pl.num_programs(0)
    tm = h_ref.shape[0]
    n_local = ybuf.shape[1]
    slot = i % 2

    @pl.when(i == 0)
    def _():
        ybuf[...] = jnp.zeros_like(ybuf)
        _tile_groups(0, loff_ref, rows_ref, gpos_ref, ybuf.at[0], ys_hbm, sem.at[0], False)

    @pl.when(i + 1 < n)
    def _():
        _tile_groups(i + 1, loff_ref, rows_ref, gpos_ref, ybuf.at[1 - slot], ys_hbm, sem.at[1 - slot], False)

    _wait_group_copies(pl.multiple_of(tot_ref[i], SORT_ALIGN), ybuf.at[slot], ys_hbm, sem.at[slot], False)
    cols = col_ref[...]
    scol = lax.broadcasted_iota(I32, (tm, n_local), 1).astype(F32)
    y = ybuf[slot]
    ffn = None
    for k in range(2):
        pick = jnp.where(scol == cols[:, k:k + 1], 1.0, 0.0).astype(BF16)
        term = cols[:, 2 + k:3 + k] * _dot(pick, y)
        ffn = term if ffn is None else ffn + term
    h2 = _layer_norm_rows(DEEPNORM_ALPHA * h_ref[...] + ffn, g_ref[...], b_ref[...])
    gate = _sigmoid(_dot(h2.astype(BF16), pg_ref[...]))
    pe = _dot(p_ref[...].astype(BF16), pp_ref[...])
    o_ref[...] = h2 + gate * pe


def _combine(meta, ys, h, cols, p2, ln_g, ln_b, pg_bf, pp_bf):
    t, d = h.shape
    tm = MOE_TM
    pdim = p2.shape[1]
    row = lambda i, *_: (i, 0)
    full = lambda i, *_: (0, 0)
    grid_spec = pltpu.PrefetchScalarGridSpec(
        num_scalar_prefetch=4,
        grid=(t // tm,),
        in_specs=[
            pl.BlockSpec(memory_space=pl.ANY),
            pl.BlockSpec((tm, d), row),
            pl.BlockSpec((tm, LANES), row),
            pl.BlockSpec((tm, pdim), row),
            pl.BlockSpec((1, d), full),
            pl.BlockSpec((1, d), full),
            pl.BlockSpec(pg_bf.shape, full),
            pl.BlockSpec(pp_bf.shape, full),
        ],
        out_specs=pl.BlockSpec((tm, d), row),
        scratch_shapes=[pltpu.VMEM((2, _local_rows(tm), d), BF16), pltpu.SemaphoreType.DMA((2,))],
    )
    return pl.pallas_call(
        _combine_kernel,
        grid_spec=grid_spec,
        out_shape=jax.ShapeDtypeStruct((t, d), F32),
        compiler_params=_cparams("arbitrary"),
        name="moe_combine",
    )(meta["loff"], meta["rows"], meta["gpos"], meta["tot"], ys, h, cols, p2, ln_g, ln_b, pg_bf, pp_bf)


def _rotary_tables(seq):
    half = RET_DK // 2
    inv = (np.float32(ROPE_BASE) ** (-np.arange(half, dtype=np.float32) / np.float32(half))).astype(np.float32)
    ang = (np.arange(seq, dtype=np.float32)[:, None] * inv[None, :]).astype(np.float32)
    cos = np.cos(ang.astype(np.float64))
    sin = np.sin(ang.astype(np.float64))
    cos_h = np.concatenate([cos, cos], axis=1)
    sin_h = np.concatenate([-sin, sin], axis=1)
    return (jnp.asarray(np.tile(cos_h, (1, RET_HEADS)), F32), jnp.asarray(np.tile(sin_h, (1, RET_HEADS)), F32))


def _retention_tables():
    c = RET_CHUNK
    h = np.arange(RET_HEADS, dtype=np.float64)
    log_g = np.log1p(-np.exp2(-5.0 - h))
    j = np.arange(c, dtype=np.float64)
    rel = j[:, None] - j[None, :]
    din = np.where(rel >= 0, np.exp(np.maximum(rel, 0.0)[None] * log_g[:, None, None]), 0.0)
    qd = np.exp((j + 1.0)[None] * log_g[:, None])
    kd = np.exp((c - 1.0 - j)[None] * log_g[:, None])
    cd = np.exp(c * log_g)
    qd = np.broadcast_to(qd[:, :, None], (RET_HEADS, c, LANES))
    kd = np.broadcast_to(kd[:, :, None], (RET_HEADS, c, LANES))
    cd = np.broadcast_to(cd[:, None, None], (RET_HEADS, 1, LANES))
    return tuple(jnp.asarray(a, F32) for a in (din, qd, kd, cd))


def _t5_bucket_np(dist):
    max_exact = REL_BUCKETS // 2
    d = np.maximum(dist, 1).astype(np.float32)
    large = max_exact + (np.log(d / np.float32(max_exact)) / np.float32(math.log(REL_MAX_DIST / max_exact))
                         * np.float32(REL_BUCKETS - max_exact)).astype(np.int32)
    large = np.minimum(large, REL_BUCKETS - 1)
    return np.where(dist < max_exact, dist, large)


def _diff_bias_tables(rel_bias, seq):
    c = ATT_RQ
    r = np.arange(c)
    dist0 = r[:, None] - r[None, :]
    far = REL_BUCKETS - 1
    assert np.all(_t5_bucket_np(np.arange(c + 1, max(seq, 2 * c))) == far)
    bidx = np.stack([_t5_bucket_np(np.maximum(dist0, 0)), _t5_bucket_np(dist0 + c)])
    rb = rel_bias.astype(F32).T
    shifted = (rb - rb[:, far:far + 1]) * LOG2E
    bidx = jnp.asarray(bidx, I32)[None]
    tab = jnp.zeros((rb.shape[0], 2, c, c), F32)
    for bucket in range(REL_BUCKETS - 1):
        tab = jnp.where(bidx == bucket, shifted[:, bucket][:, None, None, None], tab)
    causal = jnp.asarray(np.stack([dist0 >= 0, np.ones_like(dist0, bool)]))[None]
    return jnp.where(causal, tab, NEG_BIG)


def _local_rows(tm):
    need = 2 * tm + N_EXPERTS * (SORT_ALIGN - 1)
    return -(-need // LANES) * LANES


def _round_up(a, m):
    return ((a + m - 1) // m) * m


def _route_meta(route, cnt, t):
    tm, tr = MOE_TM, MOE_TR
    nt = t // tm
    counts = cnt.reshape(nt, N_EXPERTS, LANES)[:, :, 0].astype(I32)
    rows = _round_up(counts, SORT_ALIGN)
    loff = jnp.cumsum(rows, axis=1) - rows
    seg = jnp.sum(rows, axis=0)
    seg_pad = _round_up(seg, tr)
    ends = jnp.cumsum(seg_pad)
    offs = ends - seg_pad
    gpos = offs[None, :] + jnp.cumsum(rows, axis=0) - rows
    n_slots = 2 * t + nt * N_EXPERTS * (SORT_ALIGN - 1)
    n_slots = _round_up(n_slots, tr) + N_EXPERTS * tr
    n_tiles = (ends[-1] // tr).astype(I32)
    tile_start = jnp.arange(n_slots // tr, dtype=I32) * tr
    tile_expert = jnp.sum((tile_start[:, None] >= ends[None, :]).astype(I32), axis=1)
    last = jnp.sum((((n_tiles - 1) * tr) >= ends).astype(I32))
    meta = {
        "loff": loff.reshape(-1).astype(I32), "rows": rows.reshape(-1).astype(I32),
        "gpos": gpos.reshape(-1).astype(I32), "tot": jnp.sum(rows, axis=1).astype(I32),
        "tail": jnp.where(seg_pad > seg, ends - tr, -1).astype(I32),
        "tile_expert": jnp.minimum(tile_expert, last).astype(I32), "n_tiles": n_tiles.reshape(1),
    }
    experts = route[0:2].astype(I32).reshape(2, nt, tm)
    onehot = experts[..., None] == jnp.arange(N_EXPERTS, dtype=I32)
    slots = jnp.sum(jnp.where(onehot, loff[None, :, None, :], 0), axis=-1) + route[4:6].astype(I32).reshape(2, nt, tm)
    slots = slots.reshape(2, t)
    cols = jnp.concatenate([slots.astype(F32), route[2:4]], axis=0).T
    cols = jnp.pad(cols, ((0, 0), (0, LANES - cols.shape[1])))
    return meta, slots, cols, n_slots


def kernel(x, p, rel_bias, router_w, even_w_in, even_w_out, even_lambda, even_diff_norm, even_ret_norm,
           odd_w_in, odd_b_forget, odd_w_out, ln_mix_g, ln_mix_b, ln_ffn_g, ln_ffn_b,
           moe_w_gate, moe_w_up, moe_w_down, ple_proj, ple_gate):
    b, s, d = x.shape
    t = b * s
    assert d == 1024 and p.shape[0] == DEPTH and even_w_in.shape[2] == 3072
    assert odd_w_in.shape[2] == 3 * d + FOX_HEADS and moe_w_gate.shape[1] == N_EXPERTS
    assert s % RET_CHUNK == 0 and s % min(ATT_TQ, s) == 0 and s % min(FOX_TQ, s) == 0
    assert t % min(PROJ_TM, s) == 0 and t % MOE_TM == 0

    cos_t, sin_t = _rotary_tables(s)
    ret_tabs = _retention_tables()
    bias_tab = _diff_bias_tables(rel_bias, s)
    rwt = router_w.astype(F32).T
    tok = np.arange(MOE_TM)
    tri = jnp.asarray(tok[:, None] < tok[None, :], BF16)

    x2 = x.reshape(t, d)
    for i in range(DEPTH):
        j = i // 2
        if i % 2 == 0:
            lam_init = 0.8 - 0.6 * math.exp(-0.3 * i)
            qa, ka, va, qb, kb, vb, gb = _even_inproj(x2, even_w_in[j].astype(BF16), cos_t, sin_t, s)
            sh = lambda a: a.reshape(b, s, a.shape[1])
            ya = _diff_attention(sh(qa), sh(ka), sh(va), bias_tab, even_lambda[j].astype(F32),
                                 even_diff_norm[j].reshape(1, -1).astype(F32), lam_init)
            yb = _retention(sh(qb), sh(kb), sh(vb), sh(gb), ret_tabs, even_ret_norm[j].reshape(1, -1).astype(F32))
            w_out = even_w_out[j].astype(BF16)
            n_a = ya.shape[2]
            ys = [ya.reshape(t, -1), yb.reshape(t, -1)]
            ws = [w_out[:n_a], w_out[n_a:]]
        else:
            w_in = odd_w_in[j]
            wf = jnp.zeros((d, LANES), BF16).at[:, :FOX_HEADS].set(w_in[:, 3 * d:].astype(BF16))
            bfg = jnp.zeros((1, LANES), F32).at[0, :FOX_HEADS].set(odd_b_forget[j].astype(F32))
            q, k, v, cum = _odd_inproj(x2, w_in[:, :3 * d].astype(BF16), wf, bfg, s)
            cum3 = cum.reshape(b, s, LANES)
            cum_t = jnp.transpose(cum3[:, :, :FOX_HEADS], (0, 2, 1)).reshape(b, FOX_HEADS // 2, 2, s)
            y = _fox_attention(q.reshape(b, s, d), k.reshape(b, s, d), v.reshape(b, s, d), cum3, cum_t)
            ys = [y.reshape(t, d)]
            ws = [odd_w_out[j].astype(BF16)]
        h, route, cnt = _outproj_router(ys, ws, x2, ln_mix_g[i].reshape(1, d), ln_mix_b[i].reshape(1, d), rwt, tri)
        meta, slots, cols, n_slots = _route_meta(route, cnt, t)
        xs = _dispatch(meta, h, slots, n_slots)
        rows = _expert_mlps(meta["tile_expert"], meta["n_tiles"], xs, moe_w_gate, moe_w_up, moe_w_down, i)
        x2 = _combine(meta, rows, h, cols, p[i].reshape(t, -1), ln_ffn_g[i].reshape(1, d),
                      ln_ffn_b[i].reshape(1, d), ple_gate[i].astype(BF16), ple_proj[i].astype(BF16))
    return x2.reshape(b, s, d)
```

```python
import functools
import math

import numpy as np
import jax
import jax.numpy as jnp
from jax import lax
from jax.experimental import pallas as pl
from jax.experimental.pallas import tpu as pltpu

F32 = jnp.float32
BF16 = jnp.bfloat16
I32 = jnp.int32

DIFF_HEADS = 4
DIFF_DK = 64
RET_HEADS = 4
RET_DK = 64
RET_DV = 128
RET_CHUNK = 128
FOX_HEADS = 16
FOX_DH = 64
REL_BUCKETS = 32
REL_MAX_DIST = 128
N_GROUPS = 4
EXPERTS_PER_GROUP = 4
N_EXPERTS = 16
DEPTH = 2
DEEPNORM_ALPHA = (2 * DEPTH) ** 0.25
LN_EPS = 1e-5
ROPE_BASE = 10000.0
NEG_BIG = -1e30
LOG2E = math.log2(math.e)

VMEM_LIMIT_BYTES = 48 * 1024 * 1024
LANES = 128

PROJ_TM = 512
ATT_TQ = 512
FOX_TQ = 512
ATT_RQ = LANES
ATT_LOOKAHEAD = 3
MOE_TR = 512
MOE_TM = 512
SORT_ALIGN = 16
ROUTE_ROWS = 8
COMBINE_ROWS = 256


def _cparams(*sem):
    return pltpu.CompilerParams(dimension_semantics=sem, vmem_limit_bytes=VMEM_LIMIT_BYTES)


def _dot(a, b):
    return jnp.dot(a, b, preferred_element_type=F32)


def _dot_nt(a, b):
    return lax.dot_general(a, b, (((1,), (1,)), ((), ())), preferred_element_type=F32)


def _layer_norm_rows(z, g, b):
    mu = jnp.mean(z, axis=-1, keepdims=True)
    zc = z - mu
    var = jnp.mean(zc * zc, axis=-1, keepdims=True)
    return zc * lax.rsqrt(var + LN_EPS) * g + b


def _silu(x):
    return x * (1.0 / (1.0 + jnp.exp(-x)))


def _sigmoid(x):
    return 1.0 / (1.0 + jnp.exp(-x))


def _even_inproj_kernel(x_ref, w_ref, cos_ref, sin_ref,
                        qa_ref, ka_ref, va_ref, qb_ref, kb_ref, vb_ref, gb_ref):
    x = x_ref[...].astype(BF16)

    def mm(c0, c1):
        return _dot(x, w_ref[:, c0:c1])

    qa_ref[...] = (mm(0, 512) * (DIFF_DK ** -0.5 * LOG2E)).astype(BF16)
    ka_ref[...] = mm(512, 1024).astype(BF16)
    va_ref[...] = mm(1024, 1536).astype(BF16)
    qk = mm(1536, 2048)
    cos = cos_ref[...]
    sin = sin_ref[...]
    lane = lax.broadcasted_iota(I32, cos.shape, 1)
    first_half = (lane % RET_DK) < (RET_DK // 2)

    def rot(t):
        sw = jnp.where(first_half, pltpu.roll(t, t.shape[1] - RET_DK // 2, 1),
                       pltpu.roll(t, RET_DK // 2, 1))
        return t * cos + sw * sin

    qb_ref[...] = rot(qk[:, :256]).astype(BF16)
    kb_ref[...] = (rot(qk[:, 256:]) * (RET_DK ** -0.5)).astype(BF16)
    vb_ref[...] = mm(2048, 2560).astype(BF16)
    gb_ref[...] = mm(2560, 3072).astype(BF16)


def _even_inproj(x2, w_bf, cos_t, sin_t, seq):
    t, d = x2.shape
    tm = min(PROJ_TM, seq)
    nblk_s = seq // tm
    widths = (512, 512, 512, 256, 256, 512, 512)
    row = lambda i: (i, 0)
    return pl.pallas_call(
        _even_inproj_kernel,
        grid=(t // tm,),
        in_specs=[
            pl.BlockSpec((tm, d), row),
            pl.BlockSpec(w_bf.shape, lambda i: (0, 0)),
            pl.BlockSpec((tm, 256), lambda i: (i % nblk_s, 0)),
            pl.BlockSpec((tm, 256), lambda i: (i % nblk_s, 0)),
        ],
        out_specs=[pl.BlockSpec((tm, w), row) for w in widths],
        out_shape=[jax.ShapeDtypeStruct((t, w), BF16) for w in widths],
        compiler_params=_cparams("parallel"),
        name="even_inproj",
    )(x2, w_bf, cos_t, sin_t)


def _run_chains(chains, lookahead, scores, finish):
    pending = [scores(c) for c in chains[:lookahead]]
    for n, chain in enumerate(chains):
        if n + lookahead < len(chains):
            pending.append(scores(chains[n + lookahead]))
        finish(chain, pending.pop(0))


def _diff_attn_kernel(lam_ref, q_ref, k_ref, v_ref, bias_ref, g_ref, o_ref, qm_ref, m_ref, l_ref, acc_ref,
                      *, lam_init, tq):
    seq = q_ref.shape[0]
    tk = tq
    rq = ATT_RQ
    nr = tq // rq
    q = q_ref[...]
    lane = lax.broadcasted_iota(I32, q.shape, 1)
    zero = jnp.zeros_like(q)
    qm_ref[0] = jnp.where(lane < DIFF_DK, q, zero)
    qm_ref[1] = jnp.where(lane >= DIFF_DK, q, zero)
    m_ref[...] = jnp.full(m_ref.shape, NEG_BIG, F32)
    l_ref[...] = jnp.zeros(l_ref.shape, F32)
    acc_ref[...] = jnp.zeros(acc_ref.shape, F32)
    lp = lam_ref[...]
    lam = (jnp.exp(jnp.sum(lp[0:1, :] * lp[1:2, :], axis=-1, keepdims=True))
           - jnp.exp(jnp.sum(lp[2:3, :] * lp[3:4, :], axis=-1, keepdims=True)) + lam_init)

    chains = [(ii, j, r, c) for ii in range(seq // tq) for j in range(ii + 1)
              for r in range(nr) for c in range(2)]

    def n_keys(ii, j, r):
        return (r + 1) * rq if j == ii else tk

    def scores(chain):
        ii, j, r, c = chain
        k = k_ref[j * tk:j * tk + n_keys(ii, j, r), :]
        return _dot_nt(qm_ref[c, ii * tq + r * rq:ii * tq + (r + 1) * rq, :], k)

    def finish(chain, s):
        ii, j, r, c = chain
        rows = slice(ii * tq + r * rq, ii * tq + (r + 1) * rq)
        nk = n_keys(ii, j, r)
        v = v_ref[j * tk:j * tk + nk, :]
        sc = []
        for kc in range(nk // rq):
            t = s[:, kc * rq:(kc + 1) * rq]
            back = (ii * nr + r) - (j * nr + kc)
            if back <= 1:
                t = t + bias_ref[back]
            sc.append(t)
        mx = sc[0]
        for t in sc[1:]:
            mx = jnp.maximum(mx, t)
        m_old = m_ref[c, rows, :]
        m_new = jnp.maximum(m_old, jnp.max(mx, axis=-1, keepdims=True))
        alpha = jnp.exp2(m_old - m_new)
        ps = [jnp.exp2(t - m_new) for t in sc]
        psum = ps[0]
        for t in ps[1:]:
            psum = psum + t
        l_ref[c, rows, :] = alpha * l_ref[c, rows, :] + psum
        p = jnp.concatenate([t.astype(BF16) for t in ps], axis=1)
        acc_ref[c, rows, :] = alpha * acc_ref[c, rows, :] + _dot(p, v)
        m_ref[c, rows, :] = m_new
        if j == ii and r == nr - 1 and c == 1:
            blk = slice(ii * tq, (ii + 1) * tq)
            l0 = jnp.sum(l_ref[0, blk, :], axis=-1, keepdims=True)
            l1 = jnp.sum(l_ref[1, blk, :], axis=-1, keepdims=True)
            o = acc_ref[0, blk, :] / l0 - lam * (acc_ref[1, blk, :] / l1)
            o = o * lax.rsqrt(jnp.mean(o * o, axis=-1, keepdims=True) + LN_EPS)
            o_ref[blk, :] = (o * g_ref[...] * (1.0 - lam_init)).astype(BF16)

    _run_chains(chains, ATT_LOOKAHEAD, scores, finish)


def _diff_attention(qa, ka, va, bias_tab, lam_params, diff_g, lam_init):
    b, s, _ = qa.shape
    tq = min(ATT_TQ, s)
    kern = functools.partial(_diff_attn_kernel, lam_init=lam_init, tq=tq)
    seq_blk = lambda bi, h: (bi, 0, h)
    return pl.pallas_call(
        kern,
        grid=(b, DIFF_HEADS),
        in_specs=[
            pl.BlockSpec(lam_params.shape, lambda bi, h: (0, 0)),
            pl.BlockSpec((None, s, LANES), seq_blk),
            pl.BlockSpec((None, s, LANES), seq_blk),
            pl.BlockSpec((None, s, LANES), seq_blk),
            pl.BlockSpec((None, 2, ATT_RQ, ATT_RQ), lambda bi, h: (h, 0, 0, 0)),
            pl.BlockSpec((1, LANES), lambda bi, h: (0, 0)),
        ],
        out_specs=pl.BlockSpec((None, s, LANES), seq_blk),
        out_shape=jax.ShapeDtypeStruct((b, s, DIFF_HEADS * LANES), BF16),
        scratch_shapes=[pltpu.VMEM((2, s, LANES), BF16), pltpu.VMEM((2, s, LANES), F32),
                        pltpu.VMEM((2, s, LANES), F32), pltpu.VMEM((2, s, LANES), F32)],
        compiler_params=_cparams("parallel", "parallel"),
        name="diff_attention",
    )(lam_params, qa, ka, va, bias_tab, diff_g)


def _retention_kernel(q_ref, k_ref, v_ref, gate_ref, din_ref, qd_ref, kd_ref, cd_ref, g_ref, o_ref):
    s = q_ref.shape[0]
    c = RET_CHUNK
    lane = lax.broadcasted_iota(I32, (c, LANES), 1)
    g = g_ref[...]
    states = [jnp.zeros((LANES, RET_DV), F32) for _ in range(2)]
    for n in range(s // c):
        r = slice(n * c, (n + 1) * c)
        q_pair = q_ref[r, :].astype(F32)
        k_pair = k_ref[r, :].astype(F32)
        for par in range(2):
            own = (lane // RET_DK) == par
            cols = slice(par * RET_DV, (par + 1) * RET_DV)
            q = jnp.where(own, q_pair, 0.0)
            k = jnp.where(own, k_pair, 0.0)
            v = v_ref[r, cols]
            scores = _dot_nt(q.astype(BF16), k.astype(BF16)) * din_ref[par]
            inner = _dot(scores.astype(BF16), v)
            cross = _dot((q * qd_ref[par]).astype(BF16), states[par].astype(BF16))
            kv = _dot((k * kd_ref[par]).T.astype(BF16), v)
            states[par] = cd_ref[par] * states[par] + kv
            y = inner + cross
            mu = jnp.mean(y, axis=-1, keepdims=True)
            yc = y - mu
            var = jnp.mean(yc * yc, axis=-1, keepdims=True)
            yn = yc * lax.rsqrt(var + LN_EPS) * g
            gate = gate_ref[r, cols].astype(F32)
            o_ref[r, cols] = (_silu(gate) * yn).astype(BF16)


def _retention(qb, kb, vb, gb, tabs, ret_g):
    b, s, _ = qb.shape
    din, qd, kd, cd = tabs
    pair = lambda bi, hp: (bi, 0, hp)
    tab = lambda bi, hp: (hp, 0, 0)
    return pl.pallas_call(
        _retention_kernel,
        grid=(b, RET_HEADS // 2),
        in_specs=[
            pl.BlockSpec((None, s, LANES), pair),
            pl.BlockSpec((None, s, LANES), pair),
            pl.BlockSpec((None, s, 2 * RET_DV), pair),
            pl.BlockSpec((None, s, 2 * RET_DV), pair),
            pl.BlockSpec((2, RET_CHUNK, RET_CHUNK), tab),
            pl.BlockSpec((2, RET_CHUNK, LANES), tab),
            pl.BlockSpec((2, RET_CHUNK, LANES), tab),
            pl.BlockSpec((2, 1, LANES), tab),
            pl.BlockSpec((1, RET_DV), lambda bi, hp: (0, 0)),
        ],
        out_specs=pl.BlockSpec((None, s, 2 * RET_DV), pair),
        out_shape=jax.ShapeDtypeStruct((b, s, RET_HEADS * RET_DV), BF16),
        compiler_params=_cparams("parallel", "parallel"),
        name="retention",
    )(qb, kb, vb, gb, din, qd, kd, cd, ret_g)


def _odd_inproj_kernel(x_ref, w_ref, wf_ref, bf_ref, q_ref, k_ref, v_ref, cum_ref, carry_ref, *, nblk_s):
    i = pl.program_id(0)
    x = x_ref[...].astype(BF16)
    d = q_ref.shape[1]
    q_ref[...] = (_dot(x, w_ref[:, 0:d]) * (FOX_DH ** -0.5 * LOG2E)).astype(BF16)
    k_ref[...] = _dot(x, w_ref[:, d:2 * d]).astype(BF16)
    v_ref[...] = _dot(x, w_ref[:, 2 * d:3 * d]).astype(BF16)
    z = _dot(x, wf_ref[...]) + bf_ref[...]
    c = jnp.minimum(z, 0.0) - jnp.log1p(jnp.exp(-jnp.abs(z)))
    tm = c.shape[0]
    row = lax.broadcasted_iota(I32, c.shape, 0)
    step = 1
    while step < tm:
        c = c + jnp.where(row >= step, pltpu.roll(c, step, 0), 0.0)
        step *= 2

    @pl.when(i % nblk_s == 0)
    def _():
        carry_ref[...] = jnp.zeros_like(carry_ref)

    c = c + carry_ref[...]
    cum_ref[...] = c * LOG2E
    carry_ref[...] = c[tm - 1:tm, :]


def _odd_inproj(x2, w_bf, wf_bf, bfg, seq):
    t, d = x2.shape
    tm = min(PROJ_TM, seq)
    nblk_s = seq // tm
    row = lambda i: (i, 0)
    kern = functools.partial(_odd_inproj_kernel, nblk_s=nblk_s)
    return pl.pallas_call(
        kern,
        grid=(t // tm,),
        in_specs=[
            pl.BlockSpec((tm, d), row),
            pl.BlockSpec(w_bf.shape, lambda i: (0, 0)),
            pl.BlockSpec(wf_bf.shape, lambda i: (0, 0)),
            pl.BlockSpec(bfg.shape, lambda i: (0, 0)),
        ],
        out_specs=[pl.BlockSpec((tm, d), row)] * 3 + [pl.BlockSpec((tm, LANES), row)],
        out_shape=[jax.ShapeDtypeStruct((t, d), BF16)] * 3 + [jax.ShapeDtypeStruct((t, LANES), F32)],
        scratch_shapes=[pltpu.VMEM((1, LANES), F32)],
        compiler_params=_cparams("arbitrary"),
        name="odd_inproj",
    )(x2, w_bf, wf_bf, bfg)


def _fox_attn_kernel(q_ref, k_ref, v_ref, cq_ref, ck_ref, o_ref, qm_ref, va_ref, cqc_ref, m_ref, acc_ref, *, tq):
    seq = q_ref.shape[0]
    tk = tq
    hp = pl.program_id(1)
    q = q_ref[...]
    v = v_ref[...]
    lane = lax.broadcasted_iota(I32, (seq, LANES), 1)
    cq_all = cq_ref[...]
    for par in range(2):
        own = (lane // FOX_DH) == par
        qm_ref[par] = jnp.where(own, q, jnp.zeros_like(q))
        va_ref[par] = jnp.where(own, v, jnp.ones_like(v))
        cq = jnp.sum(jnp.where(lane == 2 * hp + par, cq_all, 0.0), axis=-1, keepdims=True)
        cqc_ref[par] = jnp.broadcast_to(cq, (seq, LANES))
    m_ref[...] = jnp.full(m_ref.shape, NEG_BIG, F32)
    acc_ref[...] = jnp.zeros(acc_ref.shape, F32)
    rq = ATT_RQ
    nr = tq // rq
    upper = (lax.broadcasted_iota(I32, (rq, rq), 1) > lax.broadcasted_iota(I32, (rq, rq), 0))
    lane_q = lax.broadcasted_iota(I32, (tq, LANES), 1)

    chains = [(ii, j, r, par) for ii in range(seq // tq) for j in range(ii + 1)
              for r in range(nr) for par in range(2)]

    def n_keys(ii, j, r):
        return (r + 1) * rq if j == ii else tk

    def scores(chain):
        ii, j, r, par = chain
        k = k_ref[j * tk:j * tk + n_keys(ii, j, r), :]
        return _dot_nt(qm_ref[par, ii * tq + r * rq:ii * tq + (r + 1) * rq, :], k)

    def finish(chain, s):
        ii, j, r, par = chain
        rows = slice(ii * tq + r * rq, ii * tq + (r + 1) * rq)
        nk = n_keys(ii, j, r)
        ck = ck_ref[par:par + 1, j * tk:j * tk + nk]
        cq = cqc_ref[par, rows, :]
        sc = []
        for kc in range(nk // rq):
            t = s[:, kc * rq:(kc + 1) * rq] - ck[:, kc * rq:(kc + 1) * rq]
            if j == ii and kc == r:
                t = jnp.where(upper, NEG_BIG, t)
            sc.append(t)
        mx = sc[0]
        for t in sc[1:]:
            mx = jnp.maximum(mx, t)
        m_old = m_ref[par, rows, :]
        m_new = jnp.maximum(m_old, jnp.max(mx, axis=-1, keepdims=True) + cq)
        alpha = jnp.exp2(m_old - m_new)
        shift = m_new - cq
        p = jnp.concatenate([jnp.exp2(t - shift).astype(BF16) for t in sc], axis=1)
        acc_ref[par, rows, :] = alpha * acc_ref[par, rows, :] + _dot(p, va_ref[par, j * tk:j * tk + nk, :])
        m_ref[par, rows, :] = m_new
        if j == ii and r == nr - 1 and par == 1:
            blk = slice(ii * tq, (ii + 1) * tq)
            acc0 = acc_ref[0, blk, :]
            acc1 = acc_ref[1, blk, :]
            out0 = acc0 / acc0[:, FOX_DH:FOX_DH + 1]
            out1 = acc1 / acc1[:, 0:1]
            o_ref[blk, :] = jnp.where(lane_q < FOX_DH, out0, out1).astype(BF16)

    _run_chains(chains, ATT_LOOKAHEAD, scores, finish)


def _fox_attention(q, k, v, cum, cum_t):
    b, s, d = q.shape
    tq = min(FOX_TQ, s)
    npair = d // LANES
    seq_blk = lambda bi, h: (bi, 0, h)
    return pl.pallas_call(
        functools.partial(_fox_attn_kernel, tq=tq),
        grid=(b, npair),
        in_specs=[
            pl.BlockSpec((None, s, LANES), seq_blk),
            pl.BlockSpec((None, s, LANES), seq_blk),
            pl.BlockSpec((None, s, LANES), seq_blk),
            pl.BlockSpec((None, s, LANES), lambda bi, h: (bi, 0, 0)),
            pl.BlockSpec((None, None, 2, s), lambda bi, h: (bi, h, 0, 0)),
        ],
        out_specs=pl.BlockSpec((None, s, LANES), seq_blk),
        out_shape=jax.ShapeDtypeStruct((b, s, d), BF16),
        scratch_shapes=[pltpu.VMEM((2, s, LANES), BF16), pltpu.VMEM((2, s, LANES), BF16),
                        pltpu.VMEM((2, s, LANES), F32), pltpu.VMEM((2, s, LANES), F32),
                        pltpu.VMEM((2, s, LANES), F32)],
        compiler_params=_cparams("parallel", "parallel"),
        name="fox_attention",
    )(q, k, v, cum, cum_t)


def _outproj_router_kernel(*refs, n_y):
    y_refs = refs[:n_y]
    w_refs = refs[n_y:2 * n_y]
    x_ref, g_ref, b_ref, rwt_ref, tri_ref, h_ref, route_ref, cnt_ref = refs[2 * n_y:]
    mix = _dot(y_refs[0][...], w_refs[0][...])
    for yr, wr in zip(y_refs[1:], w_refs[1:]):
        mix = mix + _dot(yr[...], wr[...])
    h = _layer_norm_rows(DEEPNORM_ALPHA * x_ref[...] + mix, g_ref[...], b_ref[...])
    h_ref[...] = h

    rw2 = rwt_ref[...]
    rw_hi = rw2.astype(BF16)
    rw_lo = (rw2 - rw_hi.astype(F32)).astype(BF16)
    lane_w = lax.broadcasted_iota(I32, rw2.shape, 1)
    w = jnp.where(lane_w < N_EXPERTS, rw_hi, rw_lo)
    h_hi = h.astype(BF16)
    h_lo = (h - h_hi.astype(F32)).astype(BF16)
    p_hi = _dot(h_hi, w)
    p_lo = _dot(h_lo, w)
    slab = p_hi + (pltpu.roll(p_hi, LANES - N_EXPERTS, 1) + p_lo)
    logits = slab.T[0:N_EXPERTS]
    tm = logits.shape[1]
    row = lax.broadcasted_iota(I32, (N_EXPERTS, tm), 0)
    mx = jnp.max(logits, axis=0, keepdims=True)
    ex = jnp.exp(logits - mx)
    probs = ex / jnp.sum(ex, axis=0, keepdims=True)
    grp = row // EXPERTS_PER_GROUP

    def top2(vals):
        v1 = jnp.max(vals, axis=0, keepdims=True)
        i1 = jnp.min(jnp.where(vals == v1, row, N_EXPERTS), axis=0, keepdims=True)
        rest = jnp.where(row == i1, -2.0, vals)
        v2 = jnp.max(rest, axis=0, keepdims=True)
        i2 = jnp.min(jnp.where(rest == v2, row, N_EXPERTS), axis=0, keepdims=True)
        return v1, i1, v2, i2

    best_score = None
    best = None
    for gi in range(N_GROUPS):
        v1, _, v2, _ = top2(jnp.where(grp == gi, probs, -1.0))
        score = v1 + v2
        if gi == 0:
            best_score, best = score, jnp.zeros_like(score, dtype=I32)
        else:
            better = score > best_score
            best = jnp.where(better, gi, best)
            best_score = jnp.where(better, score, best_score)
    v1, i1, v2, i2 = top2(jnp.where(grp == best, probs, -1.0))
    tot = v1 + v2
    g1 = v1 / tot
    g2 = v2 / tot

    onehot = jnp.where((row == i1) | (row == i2), 1.0, 0.0)
    pref = _dot(onehot.astype(BF16), tri_ref[...])
    r1 = jnp.sum(jnp.where(row == i1, pref, 0.0), axis=0, keepdims=True)
    r2 = jnp.sum(jnp.where(row == i2, pref, 0.0), axis=0, keepdims=True)
    cnt_ref[...] = jnp.broadcast_to(jnp.sum(onehot, axis=1, keepdims=True), cnt_ref.shape)

    row8 = lax.broadcasted_iota(I32, (ROUTE_ROWS, tm), 0)
    fields = (i1.astype(F32), i2.astype(F32), g1, g2, r1, r2)
    route = jnp.zeros((ROUTE_ROWS, tm), F32)
    for n, f in enumerate(fields):
        route = jnp.where(row8 == n, f, route)
    route_ref[...] = route


def _outproj_router(ys, ws, x2, ln_g, ln_b, rwt, tri):
    t, d = x2.shape
    tm = tri.shape[0]
    row = lambda i: (i, 0)
    full = lambda i: (0, 0)
    n_y = len(ys)
    kern = functools.partial(_outproj_router_kernel, n_y=n_y)
    return pl.pallas_call(
        kern,
        grid=(t // tm,),
        in_specs=([pl.BlockSpec((tm, y.shape[1]), row) for y in ys]
                  + [pl.BlockSpec(w.shape, full) for w in ws]
                  + [pl.BlockSpec((tm, d), row), pl.BlockSpec((1, d), full), pl.BlockSpec((1, d), full),
                     pl.BlockSpec(rwt.shape, full), pl.BlockSpec(tri.shape, full)]),
        out_specs=[pl.BlockSpec((tm, d), row), pl.BlockSpec((ROUTE_ROWS, tm), lambda i: (0, i)),
                   pl.BlockSpec((N_EXPERTS, LANES), row)],
        out_shape=[jax.ShapeDtypeStruct((t, d), F32), jax.ShapeDtypeStruct((ROUTE_ROWS, t), F32),
                   jax.ShapeDtypeStruct((t // tm * N_EXPERTS, LANES), F32)],
        compiler_params=_cparams("parallel"),
        name="outproj_router",
    )(*ys, *ws, x2, ln_g, ln_b, rwt, tri)


def _group_copies(n_rows, local_start, global_start, local_ref, global_ref, sem, to_global):
    @pl.when(n_rows > 0)
    def _():
        n = pl.multiple_of(n_rows, SORT_ALIGN)
        loc = local_ref.at[pl.ds(pl.multiple_of(local_start, SORT_ALIGN), n)]
        glo = global_ref.at[pl.ds(pl.multiple_of(global_start, SORT_ALIGN), n)]
        (pltpu.make_async_copy(loc, glo, sem) if to_global else pltpu.make_async_copy(glo, loc, sem)).start()


def _wait_group_copies(total_rows, local_ref, global_ref, sem, to_global):
    loc = local_ref.at[pl.ds(0, total_rows)]
    glo = global_ref.at[pl.ds(0, total_rows)]
    (pltpu.make_async_copy(loc, glo, sem) if to_global else pltpu.make_async_copy(glo, loc, sem)).wait()


def _tile_groups(tile, loff_ref, rows_ref, gpos_ref, local_ref, global_ref, sem, to_global):
    for e in range(N_EXPERTS):
        n = tile * N_EXPERTS + e
        _group_copies(rows_ref[n], loff_ref[n], gpos_ref[n], local_ref, global_ref, sem, to_global)


def _dispatch_kernel(loff_ref, rows_ref, gpos_ref, tot_ref, tail_ref, nt_ref, h_ref, slot_ref, xs_hbm,
                     sbuf, zbuf, sem):
    i = pl.program_id(0)
    tm = h_ref.shape[0]
    n_local = sbuf.shape[1]

    @pl.when(i == 0)
    def _():
        zbuf[...] = jnp.zeros_like(zbuf)

        def zero_tile(start):
            return pltpu.make_async_copy(zbuf, xs_hbm.at[pl.ds(pl.multiple_of(start, MOE_TR), MOE_TR)], sem.at[0])

        for e in range(N_EXPERTS):
            @pl.when(tail_ref[e] >= 0)
            def _(e=e):
                zero_tile(tail_ref[e]).start()

        def start_unused(r, c):
            zero_tile(r * MOE_TR).start()
            return c

        def wait_unused(r, c):
            zero_tile(r * MOE_TR).wait()
            return c

        n_all = xs_hbm.shape[0] // MOE_TR
        lax.fori_loop(nt_ref[0], n_all, start_unused, 0)
        for e in range(N_EXPERTS):
            @pl.when(tail_ref[e] >= 0)
            def _(e=e):
                zero_tile(tail_ref[e]).wait()
        lax.fori_loop(nt_ref[0], n_all, wait_unused, 0)

    srow = lax.broadcasted_iota(I32, (n_local, tm), 0)
    perm = jnp.where((srow == slot_ref[0:1, :]) | (srow == slot_ref[1:2, :]), 1.0, 0.0).astype(BF16)
    slot = i % 2
    sbuf[slot] = _dot(perm, h_ref[...].astype(BF16)).astype(BF16)
    _tile_groups(i, loff_ref, rows_ref, gpos_ref, sbuf.at[slot], xs_hbm, sem.at[slot], True)

    @pl.when(i > 0)
    def _():
        _wait_group_copies(pl.multiple_of(tot_ref[i - 1], SORT_ALIGN), sbuf.at[1 - slot], xs_hbm,
                           sem.at[1 - slot], True)

    @pl.when(i == pl.num_programs(0) - 1)
    def _():
        _wait_group_copies(pl.multiple_of(tot_ref[i], SORT_ALIGN), sbuf.at[slot], xs_hbm, sem.at[slot], True)


def _dispatch(meta, h, slots, n_slots):
    t, d = h.shape
    tm = MOE_TM
    n_local = _local_rows(tm)
    idx = lambda i, *_: (i, 0)
    grid_spec = pltpu.PrefetchScalarGridSpec(
        num_scalar_prefetch=6,
        grid=(t // tm,),
        in_specs=[pl.BlockSpec((tm, d), idx), pl.BlockSpec((2, tm), lambda i, *_: (0, i))],
        out_specs=pl.BlockSpec(memory_space=pl.ANY),
        scratch_shapes=[pltpu.VMEM((2, n_local, d), BF16), pltpu.VMEM((MOE_TR, d), BF16),
                        pltpu.SemaphoreType.DMA((2,))],
    )
    return pl.pallas_call(
        _dispatch_kernel,
        grid_spec=grid_spec,
        out_shape=jax.ShapeDtypeStruct((n_slots, d), BF16),
        compiler_params=_cparams("arbitrary"),
        name="moe_dispatch",
    )(meta["loff"], meta["rows"], meta["gpos"], meta["tot"], meta["tail"], meta["n_tiles"], h, slots)


def _expert_kernel(te_ref, nx_ref, nt_ref, x_ref, wg_hbm, wu_hbm, wd_hbm, o_ref,
                   wgs, wus, wds, wgb, wub, wdb, sem, nsw_ref, *, layer):
    r = pl.program_id(0)
    cur = te_ref[r]

    def weight_copies(e, slot):
        return [pltpu.make_async_copy(src.at[layer, e], dst.at[slot], sem.at[slot])
                for src, dst in ((wg_hbm, wgs), (wu_hbm, wus), (wd_hbm, wds))]

    @pl.when(r == 0)
    def _():
        nsw_ref[0] = 0
        for cp in weight_copies(cur, 0):
            cp.start()

    @pl.when((r == 0) | (cur != te_ref[jnp.maximum(r - 1, 0)]))
    def _():
        slot = nsw_ref[0] % 2
        nsw_ref[0] = nsw_ref[0] + 1
        for cp in weight_copies(cur, slot):
            cp.wait()

        @pl.when(nx_ref[r] != cur)
        def _():
            for cp in weight_copies(nx_ref[r], 1 - slot):
                cp.start()

        wgb[...] = wgs[slot].astype(BF16)
        wub[...] = wus[slot].astype(BF16)
        wdb[...] = wds[slot].astype(BF16)

    @pl.when(r < nt_ref[0])
    def _():
        x = x_ref[...]
        a = _silu(_dot(x, wgb[...])) * _dot(x, wub[...])
        o_ref[...] = _dot(a.astype(BF16), wdb[...]).astype(BF16)

    @pl.when(r >= nt_ref[0])
    def _():
        o_ref[...] = jnp.zeros_like(o_ref)


def _expert_mlps(tile_expert, next_expert, n_tiles, xs, wg, wu, wd, layer):
    n_slots, d = xs.shape
    tr = MOE_TR
    dff = wg.shape[3]
    grid_spec = pltpu.PrefetchScalarGridSpec(
        num_scalar_prefetch=3,
        grid=(n_slots // tr,),
        in_specs=[
            pl.BlockSpec((tr, d), lambda r, te, nx, nt: (jnp.where(r < nt[0], r, 0), 0)),
            pl.BlockSpec(memory_space=pl.ANY),
            pl.BlockSpec(memory_space=pl.ANY),
            pl.BlockSpec(memory_space=pl.ANY),
        ],
        out_specs=pl.BlockSpec((tr, d), lambda r, te, nx, nt: (r, 0)),
        scratch_shapes=[pltpu.VMEM((2, d, dff), F32), pltpu.VMEM((2, d, dff), F32), pltpu.VMEM((2, dff, d), F32),
                        pltpu.VMEM((d, dff), BF16), pltpu.VMEM((d, dff), BF16), pltpu.VMEM((dff, d), BF16),
                        pltpu.SemaphoreType.DMA((2,)), pltpu.SMEM((1,), I32)],
    )
    return pl.pallas_call(
        functools.partial(_expert_kernel, layer=layer),
        grid_spec=grid_spec,
        out_shape=jax.ShapeDtypeStruct((n_slots, d), BF16),
        compiler_params=_cparams("arbitrary"),
        name="expert_mlps",
    )(tile_expert, next_expert, n_tiles, xs, wg, wu, wd)


def _combine_kernel(loff_ref, rows_ref, gpos_ref, tot_ref, ys_hbm, h_ref, col_ref, p_ref, g_ref, b_ref,
                    pg_ref, pp_ref, o_ref, ybuf, sem):
    i = pl.program_id(0)
    n = pl.num_programs(0)
    tm = h_ref.shape[0]
    n_local = ybuf.shape[1]
    slot = i % 2

    @pl.when(i == 0)
    def _():
        ybuf[...] = jnp.zeros_like(ybuf)
        _tile_groups(0, loff_ref, rows_ref, gpos_ref, ybuf.at[0], ys_hbm, sem.at[0], False)

    @pl.when(i + 1 < n)
    def _():
        _tile_groups(i + 1, loff_ref, rows_ref, gpos_ref, ybuf.at[1 - slot], ys_hbm, sem.at[1 - slot], False)

    _wait_group_copies(pl.multiple_of(tot_ref[i], SORT_ALIGN), ybuf.at[slot], ys_hbm, sem.at[slot], False)
    rc = COMBINE_ROWS
    scol = lax.broadcasted_iota(I32, (rc, n_local), 1).astype(F32)

    def gather_rows(c):
        rows = slice(c * rc, (c + 1) * rc)
        cols = col_ref[rows, :]
        y = ybuf[slot]
        ffn = None
        for k in range(2):
            pick = jnp.where(scol == cols[:, k:k + 1], 1.0, 0.0).astype(BF16)
            term = cols[:, 2 + k:3 + k] * _dot(pick, y)
            ffn = term if ffn is None else ffn + term
        return ffn, _dot(p_ref[rows, :].astype(BF16), pp_ref[...])

    def finish(c, gathered):
        rows = slice(c * rc, (c + 1) * rc)
        ffn, pe = gathered
        h2 = _layer_norm_rows(DEEPNORM_ALPHA * h_ref[rows, :] + ffn, g_ref[...], b_ref[...])
        gate = _sigmoid(_dot(h2.astype(BF16), pg_ref[...]))
        o_ref[rows, :] = h2 + gate * pe

    _run_chains(list(range(tm // rc)), 1, gather_rows, finish)


def _combine(meta, ys, h, cols, p2, ln_g, ln_b, pg_bf, pp_bf):
    t, d = h.shape
    tm = MOE_TM
    pdim = p2.shape[1]
    row = lambda i, *_: (i, 0)
    full = lambda i, *_: (0, 0)
    grid_spec = pltpu.PrefetchScalarGridSpec(
        num_scalar_prefetch=4,
        grid=(t // tm,),
        in_specs=[
            pl.BlockSpec(memory_space=pl.ANY),
            pl.BlockSpec((tm, d), row),
            pl.BlockSpec((tm, LANES), row),
            pl.BlockSpec((tm, pdim), row),
            pl.BlockSpec((1, d), full),
            pl.BlockSpec((1, d), full),
            pl.BlockSpec(pg_bf.shape, full),
            pl.BlockSpec(pp_bf.shape, full),
        ],
        out_specs=pl.BlockSpec((tm, d), row),
        scratch_shapes=[pltpu.VMEM((2, _local_rows(tm), d), BF16), pltpu.SemaphoreType.DMA((2,))],
    )
    return pl.pallas_call(
        _combine_kernel,
        grid_spec=grid_spec,
        out_shape=jax.ShapeDtypeStruct((t, d), F32),
        compiler_params=_cparams("arbitrary"),
        name="moe_combine",
    )(meta["loff"], meta["rows"], meta["gpos"], meta["tot"], ys, h, cols, p2, ln_g, ln_b, pg_bf, pp_bf)


def _rotary_tables(seq):
    half = RET_DK // 2
    inv = (np.float32(ROPE_BASE) ** (-np.arange(half, dtype=np.float32) / np.float32(half))).astype(np.float32)
    ang = (np.arange(seq, dtype=np.float32)[:, None] * inv[None, :]).astype(np.float32)
    cos = np.cos(ang.astype(np.float64))
    sin = np.sin(ang.astype(np.float64))
    cos_h = np.concatenate([cos, cos], axis=1)
    sin_h = np.concatenate([-sin, sin], axis=1)
    return (jnp.asarray(np.tile(cos_h, (1, RET_HEADS)), F32), jnp.asarray(np.tile(sin_h, (1, RET_HEADS)), F32))


def _retention_tables():
    c = RET_CHUNK
    h = np.arange(RET_HEADS, dtype=np.float64)
    log_g = np.log1p(-np.exp2(-5.0 - h))
    j = np.arange(c, dtype=np.float64)
    rel = j[:, None] - j[None, :]
    din = np.where(rel >= 0, np.exp(np.maximum(rel, 0.0)[None] * log_g[:, None, None]), 0.0)
    qd = np.exp((j + 1.0)[None] * log_g[:, None])
    kd = np.exp((c - 1.0 - j)[None] * log_g[:, None])
    cd = np.exp(c * log_g)
    qd = np.broadcast_to(qd[:, :, None], (RET_HEADS, c, LANES))
    kd = np.broadcast_to(kd[:, :, None], (RET_HEADS, c, LANES))
    cd = np.broadcast_to(cd[:, None, None], (RET_HEADS, 1, LANES))
    return tuple(jnp.asarray(a, F32) for a in (din, qd, kd, cd))


def _t5_bucket_np(dist):
    max_exact = REL_BUCKETS // 2
    d = np.maximum(dist, 1).astype(np.float32)
    large = max_exact + (np.log(d / np.float32(max_exact)) / np.float32(math.log(REL_MAX_DIST / max_exact))
                         * np.float32(REL_BUCKETS - max_exact)).astype(np.int32)
    large = np.minimum(large, REL_BUCKETS - 1)
    return np.where(dist < max_exact, dist, large)


def _diff_bias_tables(rel_bias, seq):
    c = ATT_RQ
    r = np.arange(c)
    dist0 = r[:, None] - r[None, :]
    far = REL_BUCKETS - 1
    assert np.all(_t5_bucket_np(np.arange(c + 1, max(seq, 2 * c))) == far)
    bidx = np.stack([_t5_bucket_np(np.maximum(dist0, 0)), _t5_bucket_np(dist0 + c)])
    rb = rel_bias.astype(F32).T
    shifted = (rb - rb[:, far:far + 1]) * LOG2E
    bidx = jnp.asarray(bidx, I32)[None]
    tab = jnp.zeros((rb.shape[0], 2, c, c), F32)
    for bucket in range(REL_BUCKETS - 1):
        tab = jnp.where(bidx == bucket, shifted[:, bucket][:, None, None, None], tab)
    causal = jnp.asarray(np.stack([dist0 >= 0, np.ones_like(dist0, bool)]))[None]
    return jnp.where(causal, tab, NEG_BIG)


def _local_rows(tm):
    need = 2 * tm + N_EXPERTS * (SORT_ALIGN - 1)
    return -(-need // LANES) * LANES


def _round_up(a, m):
    return ((a + m - 1) // m) * m


def _route_meta(route, cnt, t):
    tm, tr = MOE_TM, MOE_TR
    nt = t // tm
    counts = cnt.reshape(nt, N_EXPERTS, LANES)[:, :, 0].astype(I32)
    rows = _round_up(counts, SORT_ALIGN)
    loff = jnp.cumsum(rows, axis=1) - rows
    seg = jnp.sum(rows, axis=0)
    seg_pad = _round_up(seg, tr)
    ends = jnp.cumsum(seg_pad)
    offs = ends - seg_pad
    gpos = offs[None, :] + jnp.cumsum(rows, axis=0) - rows
    n_slots = 2 * t + nt * N_EXPERTS * (SORT_ALIGN - 1)
    n_slots = _round_up(n_slots, tr) + N_EXPERTS * tr
    n_tiles = (ends[-1] // tr).astype(I32)
    tile_start = jnp.arange(n_slots // tr, dtype=I32) * tr
    tile_expert = jnp.sum((tile_start[:, None] >= ends[None, :]).astype(I32), axis=1)
    last = jnp.sum((((n_tiles - 1) * tr) >= ends).astype(I32))
    tile_expert = jnp.minimum(tile_expert, last).astype(I32)
    eid = jnp.arange(N_EXPERTS, dtype=I32)
    later = (eid[None, :] > eid[:, None]) & (seg_pad > 0)[None, :]
    nxt = jnp.min(jnp.where(later, eid[None, :], N_EXPERTS), axis=1)
    nxt = jnp.where(nxt == N_EXPERTS, eid, nxt)
    next_expert = jnp.sum(jnp.where(tile_expert[:, None] == eid[None, :], nxt[None, :], 0), axis=1).astype(I32)
    meta = {
        "loff": loff.reshape(-1).astype(I32), "rows": rows.reshape(-1).astype(I32),
        "gpos": gpos.reshape(-1).astype(I32), "tot": jnp.sum(rows, axis=1).astype(I32),
        "tail": jnp.where(seg_pad > seg, ends - tr, -1).astype(I32),
        "tile_expert": tile_expert, "next_expert": next_expert, "n_tiles": n_tiles.reshape(1),
    }
    experts = route[0:2].astype(I32).reshape(2, nt, tm)
    onehot = experts[..., None] == jnp.arange(N_EXPERTS, dtype=I32)
    slots = jnp.sum(jnp.where(onehot, loff[None, :, None, :], 0), axis=-1) + route[4:6].astype(I32).reshape(2, nt, tm)
    slots = slots.reshape(2, t)
    cols = jnp.concatenate([slots.astype(F32), route[2:4]], axis=0).T
    cols = jnp.pad(cols, ((0, 0), (0, LANES - cols.shape[1])))
    return meta, slots, cols, n_slots


def kernel(x, p, rel_bias, router_w, even_w_in, even_w_out, even_lambda, even_diff_norm, even_ret_norm,
           odd_w_in, odd_b_forget, odd_w_out, ln_mix_g, ln_mix_b, ln_ffn_g, ln_ffn_b,
           moe_w_gate, moe_w_up, moe_w_down, ple_proj, ple_gate):
    b, s, d = x.shape
    t = b * s
    assert d == 1024 and p.shape[0] == DEPTH and even_w_in.shape[2] == 3072
    assert odd_w_in.shape[2] == 3 * d + FOX_HEADS and moe_w_gate.shape[1] == N_EXPERTS
    assert s % RET_CHUNK == 0 and s % min(ATT_TQ, s) == 0 and s % min(FOX_TQ, s) == 0
    assert t % min(PROJ_TM, s) == 0 and t % MOE_TM == 0

    cos_t, sin_t = _rotary_tables(s)
    ret_tabs = _retention_tables()
    bias_tab = _diff_bias_tables(rel_bias, s)
    rw32 = router_w.astype(F32)
    rwt = jnp.zeros((d, LANES), F32).at[:, :N_EXPERTS].set(rw32).at[:, N_EXPERTS:2 * N_EXPERTS].set(rw32)
    tok = np.arange(MOE_TM)
    tri = jnp.asarray(tok[:, None] < tok[None, :], BF16)

    x2 = x.reshape(t, d)
    for i in range(DEPTH):
        j = i // 2
        if i % 2 == 0:
            lam_init = 0.8 - 0.6 * math.exp(-0.3 * i)
            qa, ka, va, qb, kb, vb, gb = _even_inproj(x2, even_w_in[j].astype(BF16), cos_t, sin_t, s)
            sh = lambda a: a.reshape(b, s, a.shape[1])
            ya = _diff_attention(sh(qa), sh(ka), sh(va), bias_tab, even_lambda[j].astype(F32),
                                 even_diff_norm[j].reshape(1, -1).astype(F32), lam_init)
            yb = _retention(sh(qb), sh(kb), sh(vb), sh(gb), ret_tabs, even_ret_norm[j].reshape(1, -1).astype(F32))
            w_out = even_w_out[j].astype(BF16)
            n_a = ya.shape[2]
            ys = [ya.reshape(t, -1), yb.reshape(t, -1)]
            ws = [w_out[:n_a], w_out[n_a:]]
        else:
            w_in = odd_w_in[j]
            wf = jnp.zeros((d, LANES), BF16).at[:, :FOX_HEADS].set(w_in[:, 3 * d:].astype(BF16))
            bfg = jnp.zeros((1, LANES), F32).at[0, :FOX_HEADS].set(odd_b_forget[j].astype(F32))
            q, k, v, cum = _odd_inproj(x2, w_in[:, :3 * d].astype(BF16), wf, bfg, s)
            cum3 = cum.reshape(b, s, LANES)
            cum_t = jnp.transpose(cum3[:, :, :FOX_HEADS], (0, 2, 1)).reshape(b, FOX_HEADS // 2, 2, s)
            y = _fox_attention(q.reshape(b, s, d), k.reshape(b, s, d), v.reshape(b, s, d), cum3, cum_t)
            ys = [y.reshape(t, d)]
            ws = [odd_w_out[j].astype(BF16)]
        h, route, cnt = _outproj_router(ys, ws, x2, ln_mix_g[i].reshape(1, d), ln_mix_b[i].reshape(1, d), rwt, tri)
        meta, slots, cols, n_slots = _route_meta(route, cnt, t)
        xs = _dispatch(meta, h, slots, n_slots)
        rows = _expert_mlps(meta["tile_expert"], meta["next_expert"], meta["n_tiles"], xs,
                            moe_w_gate, moe_w_up, moe_w_down, i)
        x2 = _combine(meta, rows, h, cols, p[i].reshape(t, -1), ln_ffn_g[i].reshape(1, d),
                      ln_ffn_b[i].reshape(1, d), ple_gate[i].astype(BF16), ple_proj[i].astype(BF16))
    return x2.reshape(b, s, d)
```

```python
import functools
import math

import numpy as np
import jax
import jax.numpy as jnp
from jax import lax
from jax.experimental import pallas as pl
from jax.experimental.pallas import tpu as pltpu

F32 = jnp.float32
BF16 = jnp.bfloat16
I32 = jnp.int32

DIFF_HEADS = 4
DIFF_DK = 64
RET_HEADS = 4
RET_DK = 64
RET_DV = 128
RET_CHUNK = 128
FOX_HEADS = 16
FOX_DH = 64
REL_BUCKETS = 32
REL_MAX_DIST = 128
N_GROUPS = 4
EXPERTS_PER_GROUP = 4
N_EXPERTS = 16
DEPTH = 2
DEEPNORM_ALPHA = (2 * DEPTH) ** 0.25
LN_EPS = 1e-5
ROPE_BASE = 10000.0
NEG_BIG = -1e30
LOG2E = math.log2(math.e)

VMEM_LIMIT_BYTES = 48 * 1024 * 1024
LANES = 128

PROJ_TM = 512
ATT_TQ = 512
FOX_TQ = 512
ATT_RQ = LANES
DIFF_LOOKAHEAD = 3
FOX_LOOKAHEAD = 4
MOE_TR = 512
MOE_TM = 512
SORT_ALIGN = 16
ROUTE_ROWS = 8
ROUTE_GATE = 2
ROUTE_SLOT = 4
COMBINE_ROWS = 256


def _cparams(*sem):
    return pltpu.CompilerParams(dimension_semantics=sem, vmem_limit_bytes=VMEM_LIMIT_BYTES)


def _dot(a, b):
    return jnp.dot(a, b, preferred_element_type=F32)


def _dot_nt(a, b):
    return lax.dot_general(a, b, (((1,), (1,)), ((), ())), preferred_element_type=F32)


def _layer_norm_rows(z, g, b):
    mu = jnp.mean(z, axis=-1, keepdims=True)
    zc = z - mu
    var = jnp.mean(zc * zc, axis=-1, keepdims=True)
    return zc * lax.rsqrt(var + LN_EPS) * g + b


def _silu(x):
    return x * (1.0 / (1.0 + jnp.exp(-x)))


def _sigmoid(x):
    return 1.0 / (1.0 + jnp.exp(-x))


def _even_inproj_kernel(x_ref, w_ref, cos_ref, sin_ref,
                        qa_ref, ka_ref, va_ref, qb_ref, kb_ref, vb_ref, gb_ref):
    x = x_ref[...].astype(BF16)

    def mm(c0, c1):
        return _dot(x, w_ref[:, c0:c1])

    qa_ref[...] = (mm(0, 512) * (DIFF_DK ** -0.5 * LOG2E)).astype(BF16)
    ka_ref[...] = mm(512, 1024).astype(BF16)
    va_ref[...] = mm(1024, 1536).astype(BF16)
    qk = mm(1536, 2048)
    cos = cos_ref[...]
    sin = sin_ref[...]
    lane = lax.broadcasted_iota(I32, cos.shape, 1)
    first_half = (lane % RET_DK) < (RET_DK // 2)

    def rot(t):
        sw = jnp.where(first_half, pltpu.roll(t, t.shape[1] - RET_DK // 2, 1),
                       pltpu.roll(t, RET_DK // 2, 1))
        return t * cos + sw * sin

    qb_ref[...] = rot(qk[:, :256]).astype(BF16)
    kb_ref[...] = (rot(qk[:, 256:]) * (RET_DK ** -0.5)).astype(BF16)
    vb_ref[...] = mm(2048, 2560).astype(BF16)
    gb_ref[...] = mm(2560, 3072).astype(BF16)


def _even_inproj(x2, w_bf, cos_t, sin_t, seq):
    t, d = x2.shape
    tm = min(PROJ_TM, seq)
    nblk_s = seq // tm
    widths = (512, 512, 512, 256, 256, 512, 512)
    row = lambda i: (i, 0)
    return pl.pallas_call(
        _even_inproj_kernel,
        grid=(t // tm,),
        in_specs=[
            pl.BlockSpec((tm, d), row),
            pl.BlockSpec(w_bf.shape, lambda i: (0, 0)),
            pl.BlockSpec((tm, 256), lambda i: (i % nblk_s, 0)),
            pl.BlockSpec((tm, 256), lambda i: (i % nblk_s, 0)),
        ],
        out_specs=[pl.BlockSpec((tm, w), row) for w in widths],
        out_shape=[jax.ShapeDtypeStruct((t, w), BF16) for w in widths],
        compiler_params=_cparams("parallel"),
        name="even_inproj",
    )(x2, w_bf, cos_t, sin_t)


def _run_chains(chains, lookahead, scores, finish):
    pending = [scores(c) for c in chains[:lookahead]]
    for n, chain in enumerate(chains):
        if n + lookahead < len(chains):
            pending.append(scores(chains[n + lookahead]))
        finish(chain, pending.pop(0))


def _diff_attn_kernel(lam_ref, q_ref, k_ref, v_ref, bias_ref, g_ref, o_ref, qm_ref, m_ref, l_ref, acc_ref,
                      *, lam_init, tq):
    seq = q_ref.shape[0]
    tk = tq
    rq = ATT_RQ
    nr = tq // rq
    q = q_ref[...]
    lane = lax.broadcasted_iota(I32, q.shape, 1)
    zero = jnp.zeros_like(q)
    qm_ref[0] = jnp.where(lane < DIFF_DK, q, zero)
    qm_ref[1] = jnp.where(lane >= DIFF_DK, q, zero)
    m_ref[...] = jnp.full(m_ref.shape, NEG_BIG, F32)
    l_ref[...] = jnp.zeros(l_ref.shape, F32)
    acc_ref[...] = jnp.zeros(acc_ref.shape, F32)
    lp = lam_ref[...]
    lam = (jnp.exp(jnp.sum(lp[0:1, :] * lp[1:2, :], axis=-1, keepdims=True))
           - jnp.exp(jnp.sum(lp[2:3, :] * lp[3:4, :], axis=-1, keepdims=True)) + lam_init)

    chains = [(ii, j, r, c) for ii in range(seq // tq) for j in range(ii + 1)
              for r in range(nr) for c in range(2)]

    def n_keys(ii, j, r):
        return (r + 1) * rq if j == ii else tk

    def scores(chain):
        ii, j, r, c = chain
        k = k_ref[j * tk:j * tk + n_keys(ii, j, r), :]
        return _dot_nt(qm_ref[c, ii * tq + r * rq:ii * tq + (r + 1) * rq, :], k)

    def finish(chain, s):
        ii, j, r, c = chain
        rows = slice(ii * tq + r * rq, ii * tq + (r + 1) * rq)
        nk = n_keys(ii, j, r)
        v = v_ref[j * tk:j * tk + nk, :]
        sc = []
        for kc in range(nk // rq):
            t = s[:, kc * rq:(kc + 1) * rq]
            back = (ii * nr + r) - (j * nr + kc)
            if back <= 1:
                t = t + bias_ref[back]
            sc.append(t)
        mx = sc[0]
        for t in sc[1:]:
            mx = jnp.maximum(mx, t)
        m_old = m_ref[c, rows, :]
        m_new = jnp.maximum(m_old, jnp.max(mx, axis=-1, keepdims=True))
        alpha = jnp.exp2(m_old - m_new)
        ps = [jnp.exp2(t - m_new) for t in sc]
        psum = ps[0]
        for t in ps[1:]:
            psum = psum + t
        l_ref[c, rows, :] = alpha * l_ref[c, rows, :] + psum
        p = jnp.concatenate([t.astype(BF16) for t in ps], axis=1)
        acc_ref[c, rows, :] = alpha * acc_ref[c, rows, :] + _dot(p, v)
        m_ref[c, rows, :] = m_new
        if j == ii and r == nr - 1 and c == 1:
            blk = slice(ii * tq, (ii + 1) * tq)
            l0 = jnp.sum(l_ref[0, blk, :], axis=-1, keepdims=True)
            l1 = jnp.sum(l_ref[1, blk, :], axis=-1, keepdims=True)
            o = acc_ref[0, blk, :] / l0 - lam * (acc_ref[1, blk, :] / l1)
            o = o * lax.rsqrt(jnp.mean(o * o, axis=-1, keepdims=True) + LN_EPS)
            o_ref[blk, :] = (o * g_ref[...] * (1.0 - lam_init)).astype(BF16)

    _run_chains(chains, DIFF_LOOKAHEAD, scores, finish)


def _diff_attention(qa, ka, va, bias_tab, lam_params, diff_g, lam_init):
    b, s, _ = qa.shape
    tq = min(ATT_TQ, s)
    kern = functools.partial(_diff_attn_kernel, lam_init=lam_init, tq=tq)
    seq_blk = lambda bi, h: (bi, 0, h)
    return pl.pallas_call(
        kern,
        grid=(b, DIFF_HEADS),
        in_specs=[
            pl.BlockSpec(lam_params.shape, lambda bi, h: (0, 0)),
            pl.BlockSpec((None, s, LANES), seq_blk),
            pl.BlockSpec((None, s, LANES), seq_blk),
            pl.BlockSpec((None, s, LANES), seq_blk),
            pl.BlockSpec((None, 2, ATT_RQ, ATT_RQ), lambda bi, h: (h, 0, 0, 0)),
            pl.BlockSpec((1, LANES), lambda bi, h: (0, 0)),
        ],
        out_specs=pl.BlockSpec((None, s, LANES), seq_blk),
        out_shape=jax.ShapeDtypeStruct((b, s, DIFF_HEADS * LANES), BF16),
        scratch_shapes=[pltpu.VMEM((2, s, LANES), BF16), pltpu.VMEM((2, s, LANES), F32),
                        pltpu.VMEM((2, s, LANES), F32), pltpu.VMEM((2, s, LANES), F32)],
        compiler_params=_cparams("parallel", "parallel"),
        name="diff_attention",
    )(lam_params, qa, ka, va, bias_tab, diff_g)


def _retention_kernel(q_ref, k_ref, v_ref, gate_ref, din_ref, qd_ref, kd_ref, cd_ref, g_ref, o_ref):
    s = q_ref.shape[0]
    c = RET_CHUNK
    lane = lax.broadcasted_iota(I32, (c, LANES), 1)
    g = g_ref[...]
    states = [jnp.zeros((LANES, RET_DV), F32) for _ in range(2)]
    for n in range(s // c):
        r = slice(n * c, (n + 1) * c)
        q_pair = q_ref[r, :].astype(F32)
        k_pair = k_ref[r, :].astype(F32)
        for par in range(2):
            own = (lane // RET_DK) == par
            cols = slice(par * RET_DV, (par + 1) * RET_DV)
            q = jnp.where(own, q_pair, 0.0)
            k = jnp.where(own, k_pair, 0.0)
            v = v_ref[r, cols]
            scores = _dot_nt(q.astype(BF16), k.astype(BF16)) * din_ref[par]
            inner = _dot(scores.astype(BF16), v)
            cross = _dot((q * qd_ref[par]).astype(BF16), states[par].astype(BF16))
            kv = _dot((k * kd_ref[par]).T.astype(BF16), v)
            states[par] = cd_ref[par] * states[par] + kv
            y = inner + cross
            mu = jnp.mean(y, axis=-1, keepdims=True)
            yc = y - mu
            var = jnp.mean(yc * yc, axis=-1, keepdims=True)
            yn = yc * lax.rsqrt(var + LN_EPS) * g
            gate = gate_ref[r, cols].astype(F32)
            o_ref[r, cols] = (_silu(gate) * yn).astype(BF16)


def _retention(qb, kb, vb, gb, tabs, ret_g):
    b, s, _ = qb.shape
    din, qd, kd, cd = tabs
    pair = lambda bi, hp: (bi, 0, hp)
    tab = lambda bi, hp: (hp, 0, 0)
    return pl.pallas_call(
        _retention_kernel,
        grid=(b, RET_HEADS // 2),
        in_specs=[
            pl.BlockSpec((None, s, LANES), pair),
            pl.BlockSpec((None, s, LANES), pair),
            pl.BlockSpec((None, s, 2 * RET_DV), pair),
            pl.BlockSpec((None, s, 2 * RET_DV), pair),
            pl.BlockSpec((2, RET_CHUNK, RET_CHUNK), tab),
            pl.BlockSpec((2, RET_CHUNK, LANES), tab),
            pl.BlockSpec((2, RET_CHUNK, LANES), tab),
            pl.BlockSpec((2, 1, LANES), tab),
            pl.BlockSpec((1, RET_DV), lambda bi, hp: (0, 0)),
        ],
        out_specs=pl.BlockSpec((None, s, 2 * RET_DV), pair),
        out_shape=jax.ShapeDtypeStruct((b, s, RET_HEADS * RET_DV), BF16),
        compiler_params=_cparams("parallel", "parallel"),
        name="retention",
    )(qb, kb, vb, gb, din, qd, kd, cd, ret_g)


def _odd_inproj_kernel(x_ref, w_ref, wf_ref, bf_ref, q_ref, k_ref, v_ref, cum_ref, carry_ref, *, nblk_s):
    i = pl.program_id(0)
    x = x_ref[...].astype(BF16)
    d = q_ref.shape[1]
    q_ref[...] = (_dot(x, w_ref[:, 0:d]) * (FOX_DH ** -0.5 * LOG2E)).astype(BF16)
    k_ref[...] = _dot(x, w_ref[:, d:2 * d]).astype(BF16)
    v_ref[...] = _dot(x, w_ref[:, 2 * d:3 * d]).astype(BF16)
    z = _dot(x, wf_ref[...]) + bf_ref[...]
    c = jnp.minimum(z, 0.0) - jnp.log1p(jnp.exp(-jnp.abs(z)))
    tm = c.shape[0]
    row = lax.broadcasted_iota(I32, c.shape, 0)
    step = 1
    while step < tm:
        c = c + jnp.where(row >= step, pltpu.roll(c, step, 0), 0.0)
        step *= 2

    @pl.when(i % nblk_s == 0)
    def _():
        carry_ref[...] = jnp.zeros_like(carry_ref)

    c = c + carry_ref[...]
    cum_ref[...] = c * LOG2E
    carry_ref[...] = c[tm - 1:tm, :]


def _odd_inproj(x2, w_bf, wf_bf, bfg, seq):
    t, d = x2.shape
    tm = min(PROJ_TM, seq)
    nblk_s = seq // tm
    row = lambda i: (i, 0)
    kern = functools.partial(_odd_inproj_kernel, nblk_s=nblk_s)
    return pl.pallas_call(
        kern,
        grid=(t // tm,),
        in_specs=[
            pl.BlockSpec((tm, d), row),
            pl.BlockSpec(w_bf.shape, lambda i: (0, 0)),
            pl.BlockSpec(wf_bf.shape, lambda i: (0, 0)),
            pl.BlockSpec(bfg.shape, lambda i: (0, 0)),
        ],
        out_specs=[pl.BlockSpec((tm, d), row)] * 3 + [pl.BlockSpec((tm, LANES), row)],
        out_shape=[jax.ShapeDtypeStruct((t, d), BF16)] * 3 + [jax.ShapeDtypeStruct((t, LANES), F32)],
        scratch_shapes=[pltpu.VMEM((1, LANES), F32)],
        compiler_params=_cparams("arbitrary"),
        name="odd_inproj",
    )(x2, w_bf, wf_bf, bfg)


def _fox_attn_kernel(q_ref, k_ref, v_ref, cq_ref, ck_ref, o_ref, qm_ref, va_ref, cqc_ref, m_ref, acc_ref, *, tq):
    seq = q_ref.shape[0]
    tk = tq
    hp = pl.program_id(1)
    q = q_ref[...]
    v = v_ref[...]
    lane = lax.broadcasted_iota(I32, (seq, LANES), 1)
    cq_all = cq_ref[...]
    for par in range(2):
        own = (lane // FOX_DH) == par
        qm_ref[par] = jnp.where(own, q, jnp.zeros_like(q))
        va_ref[par] = jnp.where(own, v, jnp.ones_like(v))
        cq = jnp.sum(jnp.where(lane == 2 * hp + par, cq_all, 0.0), axis=-1, keepdims=True)
        cqc_ref[par] = jnp.broadcast_to(cq, (seq, LANES))
    m_ref[...] = jnp.full(m_ref.shape, NEG_BIG, F32)
    acc_ref[...] = jnp.zeros(acc_ref.shape, F32)
    rq = ATT_RQ
    nr = tq // rq
    upper = (lax.broadcasted_iota(I32, (rq, rq), 1) > lax.broadcasted_iota(I32, (rq, rq), 0))
    lane_q = lax.broadcasted_iota(I32, (tq, LANES), 1)

    chains = [(ii, j, r, par) for ii in range(seq // tq) for j in range(ii + 1)
              for r in range(nr) for par in range(2)]

    def n_keys(ii, j, r):
        return (r + 1) * rq if j == ii else tk

    def scores(chain):
        ii, j, r, par = chain
        k = k_ref[j * tk:j * tk + n_keys(ii, j, r), :]
        return _dot_nt(qm_ref[par, ii * tq + r * rq:ii * tq + (r + 1) * rq, :], k)

    def finish(chain, s):
        ii, j, r, par = chain
        rows = slice(ii * tq + r * rq, ii * tq + (r + 1) * rq)
        nk = n_keys(ii, j, r)
        ck = ck_ref[par:par + 1, j * tk:j * tk + nk]
        cq = cqc_ref[par, rows, :]
        sc = []
        for kc in range(nk // rq):
            t = s[:, kc * rq:(kc + 1) * rq] - ck[:, kc * rq:(kc + 1) * rq]
            if j == ii and kc == r:
                t = jnp.where(upper, NEG_BIG, t)
            sc.append(t)
        mx = sc[0]
        for t in sc[1:]:
            mx = jnp.maximum(mx, t)
        m_old = m_ref[par, rows, :]
        m_new = jnp.maximum(m_old, jnp.max(mx, axis=-1, keepdims=True) + cq)
        alpha = jnp.exp2(m_old - m_new)
        shift = m_new - cq
        p = jnp.concatenate([jnp.exp2(t - shift).astype(BF16) for t in sc], axis=1)
        acc_ref[par, rows, :] = alpha * acc_ref[par, rows, :] + _dot(p, va_ref[par, j * tk:j * tk + nk, :])
        m_ref[par, rows, :] = m_new
        if j == ii and r == nr - 1 and par == 1:
            blk = slice(ii * tq, (ii + 1) * tq)
            acc0 = acc_ref[0, blk, :]
            acc1 = acc_ref[1, blk, :]
            out0 = acc0 / acc0[:, FOX_DH:FOX_DH + 1]
            out1 = acc1 / acc1[:, 0:1]
            o_ref[blk, :] = jnp.where(lane_q < FOX_DH, out0, out1).astype(BF16)

    _run_chains(chains, FOX_LOOKAHEAD, scores, finish)


def _fox_attention(q, k, v, cum, cum_t):
    b, s, d = q.shape
    tq = min(FOX_TQ, s)
    npair = d // LANES
    seq_blk = lambda bi, h: (bi, 0, h)
    return pl.pallas_call(
        functools.partial(_fox_attn_kernel, tq=tq),
        grid=(b, npair),
        in_specs=[
            pl.BlockSpec((None, s, LANES), seq_blk),
            pl.BlockSpec((None, s, LANES), seq_blk),
            pl.BlockSpec((None, s, LANES), seq_blk),
            pl.BlockSpec((None, s, LANES), lambda bi, h: (bi, 0, 0)),
            pl.BlockSpec((None, None, 2, s), lambda bi, h: (bi, h, 0, 0)),
        ],
        out_specs=pl.BlockSpec((None, s, LANES), seq_blk),
        out_shape=jax.ShapeDtypeStruct((b, s, d), BF16),
        scratch_shapes=[pltpu.VMEM((2, s, LANES), BF16), pltpu.VMEM((2, s, LANES), BF16),
                        pltpu.VMEM((2, s, LANES), F32), pltpu.VMEM((2, s, LANES), F32),
                        pltpu.VMEM((2, s, LANES), F32)],
        compiler_params=_cparams("parallel", "parallel"),
        name="fox_attention",
    )(q, k, v, cum, cum_t)


def _outproj_router_kernel(*refs, n_y):
    y_refs = refs[:n_y]
    w_refs = refs[n_y:2 * n_y]
    x_ref, g_ref, b_ref, rwt_ref, tri_ref, h_ref, route_ref, col_ref, cnt_ref = refs[2 * n_y:]
    mix = _dot(y_refs[0][...], w_refs[0][...])
    for yr, wr in zip(y_refs[1:], w_refs[1:]):
        mix = mix + _dot(yr[...], wr[...])
    h = _layer_norm_rows(DEEPNORM_ALPHA * x_ref[...] + mix, g_ref[...], b_ref[...])
    h_ref[...] = h

    rw2 = rwt_ref[...]
    rw_hi = rw2.astype(BF16)
    rw_lo = (rw2 - rw_hi.astype(F32)).astype(BF16)
    lane_w = lax.broadcasted_iota(I32, rw2.shape, 1)
    w = jnp.where(lane_w < N_EXPERTS, rw_hi, rw_lo)
    h_hi = h.astype(BF16)
    h_lo = (h - h_hi.astype(F32)).astype(BF16)
    p_hi = _dot(h_hi, w)
    p_lo = _dot(h_lo, w)
    slab = p_hi + (pltpu.roll(p_hi, LANES - N_EXPERTS, 1) + p_lo)
    logits = slab.T[0:N_EXPERTS]
    tm = logits.shape[1]
    row = lax.broadcasted_iota(I32, (N_EXPERTS, tm), 0)
    mx = jnp.max(logits, axis=0, keepdims=True)
    ex = jnp.exp(logits - mx)
    probs = ex / jnp.sum(ex, axis=0, keepdims=True)
    grp = row // EXPERTS_PER_GROUP

    def top2(vals):
        v1 = jnp.max(vals, axis=0, keepdims=True)
        i1 = jnp.min(jnp.where(vals == v1, row, N_EXPERTS), axis=0, keepdims=True)
        rest = jnp.where(row == i1, -2.0, vals)
        v2 = jnp.max(rest, axis=0, keepdims=True)
        i2 = jnp.min(jnp.where(rest == v2, row, N_EXPERTS), axis=0, keepdims=True)
        return v1, i1, v2, i2

    best_score = None
    best = None
    for gi in range(N_GROUPS):
        v1, _, v2, _ = top2(jnp.where(grp == gi, probs, -1.0))
        score = v1 + v2
        if gi == 0:
            best_score, best = score, jnp.zeros_like(score, dtype=I32)
        else:
            better = score > best_score
            best = jnp.where(better, gi, best)
            best_score = jnp.where(better, score, best_score)
    v1, i1, v2, i2 = top2(jnp.where(grp == best, probs, -1.0))
    tot = v1 + v2
    g1 = v1 / tot
    g2 = v2 / tot

    onehot = jnp.where((row == i1) | (row == i2), 1.0, 0.0)
    pref = _dot(onehot.astype(BF16), tri_ref[...])
    cnt = jnp.broadcast_to(jnp.sum(onehot, axis=1, keepdims=True), cnt_ref.shape)
    cnt_ref[...] = cnt
    grp_rows = jnp.floor((cnt + (SORT_ALIGN - 1)) * (1.0 / SORT_ALIGN)) * SORT_ALIGN
    row_c = lax.broadcasted_iota(I32, cnt.shape, 0)
    start = grp_rows
    step = 1
    while step < N_EXPERTS:
        start = start + jnp.where(row_c >= step, pltpu.roll(start, step, 0), 0.0)
        step *= 2
    start = (start - grp_rows)[:, 0:1]
    s1 = jnp.sum(jnp.where(row == i1, pref + start, 0.0), axis=0, keepdims=True)
    s2 = jnp.sum(jnp.where(row == i2, pref + start, 0.0), axis=0, keepdims=True)

    row8 = lax.broadcasted_iota(I32, (ROUTE_ROWS, tm), 0)
    fields = (i1.astype(F32), i2.astype(F32), g1, g2, s1, s2)
    route = jnp.zeros((ROUTE_ROWS, tm), F32)
    for n, f in enumerate(fields):
        route = jnp.where(row8 == n, f, route)
    route_ref[...] = route
    col_ref[...] = jnp.concatenate([route, jnp.zeros((LANES - ROUTE_ROWS, tm), F32)], axis=0).T


def _outproj_router(ys, ws, x2, ln_g, ln_b, rwt, tri):
    t, d = x2.shape
    tm = tri.shape[0]
    row = lambda i: (i, 0)
    full = lambda i: (0, 0)
    n_y = len(ys)
    kern = functools.partial(_outproj_router_kernel, n_y=n_y)
    return pl.pallas_call(
        kern,
        grid=(t // tm,),
        in_specs=([pl.BlockSpec((tm, y.shape[1]), row) for y in ys]
                  + [pl.BlockSpec(w.shape, full) for w in ws]
                  + [pl.BlockSpec((tm, d), row), pl.BlockSpec((1, d), full), pl.BlockSpec((1, d), full),
                     pl.BlockSpec(rwt.shape, full), pl.BlockSpec(tri.shape, full)]),
        out_specs=[pl.BlockSpec((tm, d), row), pl.BlockSpec((ROUTE_ROWS, tm), lambda i: (0, i)),
                   pl.BlockSpec((tm, LANES), row), pl.BlockSpec((N_EXPERTS, LANES), row)],
        out_shape=[jax.ShapeDtypeStruct((t, d), F32), jax.ShapeDtypeStruct((ROUTE_ROWS, t), F32),
                   jax.ShapeDtypeStruct((t, LANES), F32), jax.ShapeDtypeStruct((t // tm * N_EXPERTS, LANES), F32)],
        compiler_params=_cparams("parallel"),
        name="outproj_router",
    )(*ys, *ws, x2, ln_g, ln_b, rwt, tri)


def _group_copies(n_rows, local_start, global_start, local_ref, global_ref, sem, to_global):
    @pl.when(n_rows > 0)
    def _():
        n = pl.multiple_of(n_rows, SORT_ALIGN)
        loc = local_ref.at[pl.ds(pl.multiple_of(local_start, SORT_ALIGN), n)]
        glo = global_ref.at[pl.ds(pl.multiple_of(global_start, SORT_ALIGN), n)]
        (pltpu.make_async_copy(loc, glo, sem) if to_global else pltpu.make_async_copy(glo, loc, sem)).start()


def _wait_group_copies(total_rows, local_ref, global_ref, sem, to_global):
    loc = local_ref.at[pl.ds(0, total_rows)]
    glo = global_ref.at[pl.ds(0, total_rows)]
    (pltpu.make_async_copy(loc, glo, sem) if to_global else pltpu.make_async_copy(glo, loc, sem)).wait()


def _tile_groups(tile, loff_ref, rows_ref, gpos_ref, local_ref, global_ref, sem, to_global):
    for e in range(N_EXPERTS):
        n = tile * N_EXPERTS + e
        _group_copies(rows_ref[n], loff_ref[n], gpos_ref[n], local_ref, global_ref, sem, to_global)


def _dispatch_kernel(loff_ref, rows_ref, gpos_ref, tot_ref, tail_ref, nt_ref, h_ref, route_ref, xs_hbm,
                     sbuf, zbuf, sem):
    i = pl.program_id(0)
    tm = h_ref.shape[0]
    n_local = sbuf.shape[1]

    @pl.when(i == 0)
    def _():
        zbuf[...] = jnp.zeros_like(zbuf)

        def zero_tile(start):
            return pltpu.make_async_copy(zbuf, xs_hbm.at[pl.ds(pl.multiple_of(start, MOE_TR), MOE_TR)], sem.at[0])

        for e in range(N_EXPERTS):
            @pl.when(tail_ref[e] >= 0)
            def _(e=e):
                zero_tile(tail_ref[e]).start()

        def start_unused(r, c):
            zero_tile(r * MOE_TR).start()
            return c

        def wait_unused(r, c):
            zero_tile(r * MOE_TR).wait()
            return c

        n_all = xs_hbm.shape[0] // MOE_TR
        lax.fori_loop(nt_ref[0], n_all, start_unused, 0)
        for e in range(N_EXPERTS):
            @pl.when(tail_ref[e] >= 0)
            def _(e=e):
                zero_tile(tail_ref[e]).wait()
        lax.fori_loop(nt_ref[0], n_all, wait_unused, 0)

    srow = lax.broadcasted_iota(I32, (n_local, tm), 0).astype(F32)
    slot1 = route_ref[ROUTE_SLOT:ROUTE_SLOT + 1, :]
    slot2 = route_ref[ROUTE_SLOT + 1:ROUTE_SLOT + 2, :]
    perm = jnp.where((srow == slot1) | (srow == slot2), 1.0, 0.0).astype(BF16)
    slot = i % 2
    sbuf[slot] = _dot(perm, h_ref[...].astype(BF16)).astype(BF16)
    _tile_groups(i, loff_ref, rows_ref, gpos_ref, sbuf.at[slot], xs_hbm, sem.at[slot], True)

    @pl.when(i > 0)
    def _():
        _wait_group_copies(pl.multiple_of(tot_ref[i - 1], SORT_ALIGN), sbuf.at[1 - slot], xs_hbm,
                           sem.at[1 - slot], True)

    @pl.when(i == pl.num_programs(0) - 1)
    def _():
        _wait_group_copies(pl.multiple_of(tot_ref[i], SORT_ALIGN), sbuf.at[slot], xs_hbm, sem.at[slot], True)


def _dispatch(meta, h, route, n_slots):
    t, d = h.shape
    tm = MOE_TM
    n_local = _local_rows(tm)
    idx = lambda i, *_: (i, 0)
    grid_spec = pltpu.PrefetchScalarGridSpec(
        num_scalar_prefetch=6,
        grid=(t // tm,),
        in_specs=[pl.BlockSpec((tm, d), idx), pl.BlockSpec((ROUTE_ROWS, tm), lambda i, *_: (0, i))],
        out_specs=pl.BlockSpec(memory_space=pl.ANY),
        scratch_shapes=[pltpu.VMEM((2, n_local, d), BF16), pltpu.VMEM((MOE_TR, d), BF16),
                        pltpu.SemaphoreType.DMA((2,))],
    )
    return pl.pallas_call(
        _dispatch_kernel,
        grid_spec=grid_spec,
        out_shape=jax.ShapeDtypeStruct((n_slots, d), BF16),
        compiler_params=_cparams("arbitrary"),
        name="moe_dispatch",
    )(meta["loff"], meta["rows"], meta["gpos"], meta["tot"], meta["tail"], meta["n_tiles"], h, route)


def _expert_kernel(te_ref, nx_ref, nt_ref, x_ref, wg_hbm, wu_hbm, wd_hbm, o_ref,
                   wgs, wus, wds, wgb, wub, wdb, sem, nsw_ref, *, layer):
    r = pl.program_id(0)
    cur = te_ref[r]

    def weight_copies(e, slot):
        return [pltpu.make_async_copy(src.at[layer, e], dst.at[slot], sem.at[slot])
                for src, dst in ((wg_hbm, wgs), (wu_hbm, wus), (wd_hbm, wds))]

    @pl.when(r == 0)
    def _():
        nsw_ref[0] = 0
        for cp in weight_copies(cur, 0):
            cp.start()

    @pl.when((r == 0) | (cur != te_ref[jnp.maximum(r - 1, 0)]))
    def _():
        slot = nsw_ref[0] % 2
        nsw_ref[0] = nsw_ref[0] + 1
        for cp in weight_copies(cur, slot):
            cp.wait()

        @pl.when(nx_ref[r] != cur)
        def _():
            for cp in weight_copies(nx_ref[r], 1 - slot):
                cp.start()

        wgb[...] = wgs[slot].astype(BF16)
        wub[...] = wus[slot].astype(BF16)
        wdb[...] = wds[slot].astype(BF16)

    @pl.when(r < nt_ref[0])
    def _():
        x = x_ref[...]
        a = _silu(_dot(x, wgb[...])) * _dot(x, wub[...])
        o_ref[...] = _dot(a.astype(BF16), wdb[...]).astype(BF16)

    @pl.when(r >= nt_ref[0])
    def _():
        o_ref[...] = jnp.zeros_like(o_ref)


def _expert_mlps(tile_expert, next_expert, n_tiles, xs, wg, wu, wd, layer):
    n_slots, d = xs.shape
    tr = MOE_TR
    dff = wg.shape[3]
    grid_spec = pltpu.PrefetchScalarGridSpec(
        num_scalar_prefetch=3,
        grid=(n_slots // tr,),
        in_specs=[
            pl.BlockSpec((tr, d), lambda r, te, nx, nt: (jnp.where(r < nt[0], r, 0), 0)),
            pl.BlockSpec(memory_space=pl.ANY),
            pl.BlockSpec(memory_space=pl.ANY),
            pl.BlockSpec(memory_space=pl.ANY),
        ],
        out_specs=pl.BlockSpec((tr, d), lambda r, te, nx, nt: (r, 0)),
        scratch_shapes=[pltpu.VMEM((2, d, dff), F32), pltpu.VMEM((2, d, dff), F32), pltpu.VMEM((2, dff, d), F32),
                        pltpu.VMEM((d, dff), BF16), pltpu.VMEM((d, dff), BF16), pltpu.VMEM((dff, d), BF16),
                        pltpu.SemaphoreType.DMA((2,)), pltpu.SMEM((1,), I32)],
    )
    return pl.pallas_call(
        functools.partial(_expert_kernel, layer=layer),
        grid_spec=grid_spec,
        out_shape=jax.ShapeDtypeStruct((n_slots, d), BF16),
        compiler_params=_cparams("arbitrary"),
        name="expert_mlps",
    )(tile_expert, next_expert, n_tiles, xs, wg, wu, wd)


def _combine_kernel(loff_ref, rows_ref, gpos_ref, tot_ref, ys_hbm, h_ref, col_ref, p_ref, g_ref, b_ref,
                    pg_ref, pp_ref, o_ref, ybuf, sem):
    i = pl.program_id(0)
    n = pl.num_programs(0)
    tm = h_ref.shape[0]
    n_local = ybuf.shape[1]
    slot = i % 2

    @pl.when(i == 0)
    def _():
        ybuf[...] = jnp.zeros_like(ybuf)
        _tile_groups(0, loff_ref, rows_ref, gpos_ref, ybuf.at[0], ys_hbm, sem.at[0], False)

    @pl.when(i + 1 < n)
    def _():
        _tile_groups(i + 1, loff_ref, rows_ref, gpos_ref, ybuf.at[1 - slot], ys_hbm, sem.at[1 - slot], False)

    _wait_group_copies(pl.multiple_of(tot_ref[i], SORT_ALIGN), ybuf.at[slot], ys_hbm, sem.at[slot], False)
    rc = COMBINE_ROWS
    scol = lax.broadcasted_iota(I32, (rc, n_local), 1).astype(F32)

    def gather_rows(c):
        rows = slice(c * rc, (c + 1) * rc)
        cols = col_ref[rows, :]
        y = ybuf[slot]
        ffn = None
        for k in range(2):
            pick = jnp.where(scol == cols[:, ROUTE_SLOT + k:ROUTE_SLOT + k + 1], 1.0, 0.0).astype(BF16)
            term = cols[:, ROUTE_GATE + k:ROUTE_GATE + k + 1] * _dot(pick, y)
            ffn = term if ffn is None else ffn + term
        return ffn, _dot(p_ref[rows, :].astype(BF16), pp_ref[...])

    def finish(c, gathered):
        rows = slice(c * rc, (c + 1) * rc)
        ffn, pe = gathered
        h2 = _layer_norm_rows(DEEPNORM_ALPHA * h_ref[rows, :] + ffn, g_ref[...], b_ref[...])
        gate = _sigmoid(_dot(h2.astype(BF16), pg_ref[...]))
        o_ref[rows, :] = h2 + gate * pe

    _run_chains(list(range(tm // rc)), 1, gather_rows, finish)


def _combine(meta, ys, h, cols, p2, ln_g, ln_b, pg_bf, pp_bf):
    t, d = h.shape
    tm = MOE_TM
    pdim = p2.shape[1]
    row = lambda i, *_: (i, 0)
    full = lambda i, *_: (0, 0)
    grid_spec = pltpu.PrefetchScalarGridSpec(
        num_scalar_prefetch=4,
        grid=(t // tm,),
        in_specs=[
            pl.BlockSpec(memory_space=pl.ANY),
            pl.BlockSpec((tm, d), row),
            pl.BlockSpec((tm, LANES), row),
            pl.BlockSpec((tm, pdim), row),
            pl.BlockSpec((1, d), full),
            pl.BlockSpec((1, d), full),
            pl.BlockSpec(pg_bf.shape, full),
            pl.BlockSpec(pp_bf.shape, full),
        ],
        out_specs=pl.BlockSpec((tm, d), row),
        scratch_shapes=[pltpu.VMEM((2, _local_rows(tm), d), BF16), pltpu.SemaphoreType.DMA((2,))],
    )
    return pl.pallas_call(
        _combine_kernel,
        grid_spec=grid_spec,
        out_shape=jax.ShapeDtypeStruct((t, d), F32),
        compiler_params=_cparams("arbitrary"),
        name="moe_combine",
    )(meta["loff"], meta["rows"], meta["gpos"], meta["tot"], ys, h, cols, p2, ln_g, ln_b, pg_bf, pp_bf)


def _rotary_tables(seq):
    half = RET_DK // 2
    inv = (np.float32(ROPE_BASE) ** (-np.arange(half, dtype=np.float32) / np.float32(half))).astype(np.float32)
    ang = (np.arange(seq, dtype=np.float32)[:, None] * inv[None, :]).astype(np.float32)
    cos = np.cos(ang.astype(np.float64))
    sin = np.sin(ang.astype(np.float64))
    cos_h = np.concatenate([cos, cos], axis=1)
    sin_h = np.concatenate([-sin, sin], axis=1)
    return (jnp.asarray(np.tile(cos_h, (1, RET_HEADS)), F32), jnp.asarray(np.tile(sin_h, (1, RET_HEADS)), F32))


def _retention_tables():
    c = RET_CHUNK
    h = np.arange(RET_HEADS, dtype=np.float64)
    log_g = np.log1p(-np.exp2(-5.0 - h))
    j = np.arange(c, dtype=np.float64)
    rel = j[:, None] - j[None, :]
    din = np.where(rel >= 0, np.exp(np.maximum(rel, 0.0)[None] * log_g[:, None, None]), 0.0)
    qd = np.exp((j + 1.0)[None] * log_g[:, None])
    kd = np.exp((c - 1.0 - j)[None] * log_g[:, None])
    cd = np.exp(c * log_g)
    qd = np.broadcast_to(qd[:, :, None], (RET_HEADS, c, LANES))
    kd = np.broadcast_to(kd[:, :, None], (RET_HEADS, c, LANES))
    cd = np.broadcast_to(cd[:, None, None], (RET_HEADS, 1, LANES))
    return tuple(jnp.asarray(a, F32) for a in (din, qd, kd, cd))


def _t5_bucket_np(dist):
    max_exact = REL_BUCKETS // 2
    d = np.maximum(dist, 1).astype(np.float32)
    large = max_exact + (np.log(d / np.float32(max_exact)) / np.float32(math.log(REL_MAX_DIST / max_exact))
                         * np.float32(REL_BUCKETS - max_exact)).astype(np.int32)
    large = np.minimum(large, REL_BUCKETS - 1)
    return np.where(dist < max_exact, dist, large)


def _diff_bias_tables(rel_bias, seq):
    c = ATT_RQ
    r = np.arange(c)
    dist0 = r[:, None] - r[None, :]
    far = REL_BUCKETS - 1
    assert np.all(_t5_bucket_np(np.arange(c + 1, max(seq, 2 * c))) == far)
    bidx = np.stack([_t5_bucket_np(np.maximum(dist0, 0)), _t5_bucket_np(dist0 + c)])
    rb = rel_bias.astype(F32).T
    shifted = (rb - rb[:, far:far + 1]) * LOG2E
    bidx = jnp.asarray(bidx, I32)[None]
    tab = jnp.zeros((rb.shape[0], 2, c, c), F32)
    for bucket in range(REL_BUCKETS - 1):
        tab = jnp.where(bidx == bucket, shifted[:, bucket][:, None, None, None], tab)
    causal = jnp.asarray(np.stack([dist0 >= 0, np.ones_like(dist0, bool)]))[None]
    return jnp.where(causal, tab, NEG_BIG)


def _local_rows(tm):
    need = 2 * tm + N_EXPERTS * (SORT_ALIGN - 1)
    return -(-need // LANES) * LANES


def _round_up(a, m):
    return ((a + m - 1) // m) * m


def _route_meta(cnt, t):
    tm, tr = MOE_TM, MOE_TR
    nt = t // tm
    counts = cnt.reshape(nt, N_EXPERTS, LANES)[:, :, 0].astype(I32)
    rows = _round_up(counts, SORT_ALIGN)
    loff = jnp.cumsum(rows, axis=1) - rows
    seg = jnp.sum(rows, axis=0)
    seg_pad = _round_up(seg, tr)
    ends = jnp.cumsum(seg_pad)
    offs = ends - seg_pad
    gpos = offs[None, :] + jnp.cumsum(rows, axis=0) - rows
    n_slots = 2 * t + nt * N_EXPERTS * (SORT_ALIGN - 1)
    n_slots = _round_up(n_slots, tr) + N_EXPERTS * tr
    n_tiles = (ends[-1] // tr).astype(I32)
    tile_start = jnp.arange(n_slots // tr, dtype=I32) * tr
    tile_expert = jnp.sum((tile_start[:, None] >= ends[None, :]).astype(I32), axis=1)
    last = jnp.sum((((n_tiles - 1) * tr) >= ends).astype(I32))
    tile_expert = jnp.minimum(tile_expert, last).astype(I32)
    eid = jnp.arange(N_EXPERTS, dtype=I32)
    later = (eid[None, :] > eid[:, None]) & (seg_pad > 0)[None, :]
    nxt = jnp.min(jnp.where(later, eid[None, :], N_EXPERTS), axis=1)
    nxt = jnp.where(nxt == N_EXPERTS, eid, nxt)
    next_expert = jnp.sum(jnp.where(tile_expert[:, None] == eid[None, :], nxt[None, :], 0), axis=1).astype(I32)
    meta = {
        "loff": loff.reshape(-1).astype(I32), "rows": rows.reshape(-1).astype(I32),
        "gpos": gpos.reshape(-1).astype(I32), "tot": jnp.sum(rows, axis=1).astype(I32),
        "tail": jnp.where(seg_pad > seg, ends - tr, -1).astype(I32),
        "tile_expert": tile_expert, "next_expert": next_expert, "n_tiles": n_tiles.reshape(1),
    }
    return meta, n_slots


def kernel(x, p, rel_bias, router_w, even_w_in, even_w_out, even_lambda, even_diff_norm, even_ret_norm,
           odd_w_in, odd_b_forget, odd_w_out, ln_mix_g, ln_mix_b, ln_ffn_g, ln_ffn_b,
           moe_w_gate, moe_w_up, moe_w_down, ple_proj, ple_gate):
    b, s, d = x.shape
    t = b * s
    assert d == 1024 and p.shape[0] == DEPTH and even_w_in.shape[2] == 3072
    assert odd_w_in.shape[2] == 3 * d + FOX_HEADS and moe_w_gate.shape[1] == N_EXPERTS
    assert s % RET_CHUNK == 0 and s % min(ATT_TQ, s) == 0 and s % min(FOX_TQ, s) == 0
    assert t % min(PROJ_TM, s) == 0 and t % MOE_TM == 0

    cos_t, sin_t = _rotary_tables(s)
    ret_tabs = _retention_tables()
    bias_tab = _diff_bias_tables(rel_bias, s)
    rw32 = router_w.astype(F32)
    rwt = jnp.zeros((d, LANES), F32).at[:, :N_EXPERTS].set(rw32).at[:, N_EXPERTS:2 * N_EXPERTS].set(rw32)
    tok = np.arange(MOE_TM)
    tri = jnp.asarray(tok[:, None] < tok[None, :], BF16)

    x2 = x.reshape(t, d)
    for i in range(DEPTH):
        j = i // 2
        if i % 2 == 0:
            lam_init = 0.8 - 0.6 * math.exp(-0.3 * i)
            qa, ka, va, qb, kb, vb, gb = _even_inproj(x2, even_w_in[j].astype(BF16), cos_t, sin_t, s)
            sh = lambda a: a.reshape(b, s, a.shape[1])
            ya = _diff_attention(sh(qa), sh(ka), sh(va), bias_tab, even_lambda[j].astype(F32),
                                 even_diff_norm[j].reshape(1, -1).astype(F32), lam_init)
            yb = _retention(sh(qb), sh(kb), sh(vb), sh(gb), ret_tabs, even_ret_norm[j].reshape(1, -1).astype(F32))
            w_out = even_w_out[j].astype(BF16)
            n_a = ya.shape[2]
            ys = [ya.reshape(t, -1), yb.reshape(t, -1)]
            ws = [w_out[:n_a], w_out[n_a:]]
        else:
            w_in = odd_w_in[j]
            wf = jnp.zeros((d, LANES), BF16).at[:, :FOX_HEADS].set(w_in[:, 3 * d:].astype(BF16))
            bfg = jnp.zeros((1, LANES), F32).at[0, :FOX_HEADS].set(odd_b_forget[j].astype(F32))
            q, k, v, cum = _odd_inproj(x2, w_in[:, :3 * d].astype(BF16), wf, bfg, s)
            cum3 = cum.reshape(b, s, LANES)
            cum_t = jnp.transpose(cum3[:, :, :FOX_HEADS], (0, 2, 1)).reshape(b, FOX_HEADS // 2, 2, s)
            y = _fox_attention(q.reshape(b, s, d), k.reshape(b, s, d), v.reshape(b, s, d), cum3, cum_t)
            ys = [y.reshape(t, d)]
            ws = [odd_w_out[j].astype(BF16)]
        h, route, cols, cnt = _outproj_router(ys, ws, x2, ln_mix_g[i].reshape(1, d), ln_mix_b[i].reshape(1, d),
                                              rwt, tri)
        meta, n_slots = _route_meta(cnt, t)
        xs = _dispatch(meta, h, route, n_slots)
        rows = _expert_mlps(meta["tile_expert"], meta["next_expert"], meta["n_tiles"], xs,
                            moe_w_gate, moe_w_up, moe_w_down, i)
        x2 = _combine(meta, rows, h, cols, p[i].reshape(t, -1), ln_ffn_g[i].reshape(1, d),
                      ln_ffn_b[i].reshape(1, d), ple_gate[i].astype(BF16), ple_proj[i].astype(BF16))
    return x2.reshape(b, s, d)
```

```python
import functools
import math

import numpy as np
import jax
import jax.numpy as jnp
from jax import lax
from jax.experimental import pallas as pl
from jax.experimental.pallas import tpu as pltpu

F32 = jnp.float32
BF16 = jnp.bfloat16
I32 = jnp.int32

DIFF_HEADS = 4
DIFF_DK = 64
RET_HEADS = 4
RET_DK = 64
RET_DV = 128
RET_CHUNK = 128
FOX_HEADS = 16
FOX_DH = 64
REL_BUCKETS = 32
REL_MAX_DIST = 128
N_GROUPS = 4
EXPERTS_PER_GROUP = 4
N_EXPERTS = 16
DEPTH = 2
DEEPNORM_ALPHA = (2 * DEPTH) ** 0.25
LN_EPS = 1e-5
ROPE_BASE = 10000.0
NEG_BIG = -1e30
LOG2E = math.log2(math.e)

VMEM_LIMIT_BYTES = 48 * 1024 * 1024
LANES = 128

PROJ_TM = 512
ATT_TQ = 512
FOX_TQ = 512
ATT_RQ = LANES
DIFF_LOOKAHEAD = 3
FOX_LOOKAHEAD = 4
MOE_TR = 512
MOE_TM = 512
SORT_ALIGN = 16
ROUTE_ROWS = 8
ROUTE_GATE = 2
ROUTE_SLOT = 4
COMBINE_ROWS = 256
ROUTER_ROWS = 256


def _cparams(*sem):
    return pltpu.CompilerParams(dimension_semantics=sem, vmem_limit_bytes=VMEM_LIMIT_BYTES)


def _dot(a, b):
    return jnp.dot(a, b, preferred_element_type=F32)


def _dot_nt(a, b):
    return lax.dot_general(a, b, (((1,), (1,)), ((), ())), preferred_element_type=F32)


def _layer_norm_rows(z, g, b):
    mu = jnp.mean(z, axis=-1, keepdims=True)
    zc = z - mu
    var = jnp.mean(zc * zc, axis=-1, keepdims=True)
    return zc * lax.rsqrt(var + LN_EPS) * g + b


def _silu(x):
    return x * (1.0 / (1.0 + jnp.exp(-x)))


def _sigmoid(x):
    return 1.0 / (1.0 + jnp.exp(-x))


def _even_inproj_kernel(x_ref, w_ref, cos_ref, sin_ref,
                        qa_ref, ka_ref, va_ref, qb_ref, kb_ref, vb_ref, gb_ref):
    x = x_ref[...].astype(BF16)

    def mm(c0, c1):
        return _dot(x, w_ref[:, c0:c1])

    qa_ref[...] = (mm(0, 512) * (DIFF_DK ** -0.5 * LOG2E)).astype(BF16)
    ka_ref[...] = mm(512, 1024).astype(BF16)
    va_ref[...] = mm(1024, 1536).astype(BF16)
    qk = mm(1536, 2048)
    cos = cos_ref[...]
    sin = sin_ref[...]
    lane = lax.broadcasted_iota(I32, cos.shape, 1)
    first_half = (lane % RET_DK) < (RET_DK // 2)

    def rot(t):
        sw = jnp.where(first_half, pltpu.roll(t, t.shape[1] - RET_DK // 2, 1),
                       pltpu.roll(t, RET_DK // 2, 1))
        return t * cos + sw * sin

    qb_ref[...] = rot(qk[:, :256]).astype(BF16)
    kb_ref[...] = (rot(qk[:, 256:]) * (RET_DK ** -0.5)).astype(BF16)
    vb_ref[...] = mm(2048, 2560).astype(BF16)
    gb_ref[...] = mm(2560, 3072).astype(BF16)


def _even_inproj(x2, w_bf, cos_t, sin_t, seq):
    t, d = x2.shape
    tm = min(PROJ_TM, seq)
    nblk_s = seq // tm
    widths = (512, 512, 512, 256, 256, 512, 512)
    row = lambda i: (i, 0)
    return pl.pallas_call(
        _even_inproj_kernel,
        grid=(t // tm,),
        in_specs=[
            pl.BlockSpec((tm, d), row),
            pl.BlockSpec(w_bf.shape, lambda i: (0, 0)),
            pl.BlockSpec((tm, 256), lambda i: (i % nblk_s, 0)),
            pl.BlockSpec((tm, 256), lambda i: (i % nblk_s, 0)),
        ],
        out_specs=[pl.BlockSpec((tm, w), row) for w in widths],
        out_shape=[jax.ShapeDtypeStruct((t, w), BF16) for w in widths],
        compiler_params=_cparams("parallel"),
        name="even_inproj",
    )(x2, w_bf, cos_t, sin_t)


def _run_chains(chains, lookahead, scores, finish):
    pending = [scores(c) for c in chains[:lookahead]]
    for n, chain in enumerate(chains):
        if n + lookahead < len(chains):
            pending.append(scores(chains[n + lookahead]))
        finish(chain, pending.pop(0))


def _diff_attn_kernel(lam_ref, q_ref, k_ref, v_ref, bias_ref, g_ref, o_ref, qm_ref, m_ref, l_ref, acc_ref,
                      *, lam_init, tq):
    seq = q_ref.shape[0]
    tk = tq
    rq = ATT_RQ
    nr = tq // rq
    lane = lax.broadcasted_iota(I32, (rq, LANES), 1)
    for n in range(seq // rq):
        q = q_ref[n * rq:(n + 1) * rq, :]
        zero = jnp.zeros_like(q)
        qm_ref[n, 0:rq, :] = jnp.where(lane < DIFF_DK, q, zero)
        qm_ref[n, rq:2 * rq, :] = jnp.where(lane >= DIFF_DK, q, zero)
    m_ref[...] = jnp.full(m_ref.shape, NEG_BIG, F32)
    l_ref[...] = jnp.zeros(l_ref.shape, F32)
    acc_ref[...] = jnp.zeros(acc_ref.shape, F32)
    lp = lam_ref[...]
    lam = (jnp.exp(jnp.sum(lp[0:1, :] * lp[1:2, :], axis=-1, keepdims=True))
           - jnp.exp(jnp.sum(lp[2:3, :] * lp[3:4, :], axis=-1, keepdims=True)) + lam_init)
    bias2 = [jnp.concatenate([bias_ref[n], bias_ref[n]], axis=0) for n in range(2)]

    chains = [(ii, j, r) for ii in range(seq // tq) for j in range(ii + 1) for r in range(nr)]

    def n_keys(ii, j, r):
        return (r + 1) * rq if j == ii else tk

    def scores(chain):
        ii, j, r = chain
        k = k_ref[j * tk:j * tk + n_keys(ii, j, r), :]
        return _dot_nt(qm_ref[ii * nr + r], k)

    def finish(chain, s):
        ii, j, r = chain
        g = ii * nr + r
        nk = n_keys(ii, j, r)
        v = v_ref[j * tk:j * tk + nk, :]
        sc = []
        for kc in range(nk // rq):
            t = s[:, kc * rq:(kc + 1) * rq]
            back = g - (j * nr + kc)
            if back <= 1:
                t = t + bias2[back]
            sc.append(t)
        mx = sc[0]
        for t in sc[1:]:
            mx = jnp.maximum(mx, t)
        m_old = m_ref[g]
        m_new = jnp.maximum(m_old, jnp.max(mx, axis=-1, keepdims=True))
        alpha = jnp.exp2(m_old - m_new)
        ps = [jnp.exp2(t - m_new) for t in sc]
        psum = ps[0]
        for t in ps[1:]:
            psum = psum + t
        l_ref[g] = alpha * l_ref[g] + psum
        p = jnp.concatenate([t.astype(BF16) for t in ps], axis=1)
        acc_ref[g] = alpha * acc_ref[g] + _dot(p, v)
        m_ref[g] = m_new
        if j == ii:
            l_all = jnp.sum(l_ref[g], axis=-1, keepdims=True)
            a = acc_ref[g] / l_all
            o = a[0:rq] - lam * a[rq:2 * rq]
            o = o * lax.rsqrt(jnp.mean(o * o, axis=-1, keepdims=True) + LN_EPS)
            o_ref[g * rq:(g + 1) * rq, :] = (o * g_ref[...] * (1.0 - lam_init)).astype(BF16)

    _run_chains(chains, DIFF_LOOKAHEAD, scores, finish)


def _diff_attention(qa, ka, va, bias_tab, lam_params, diff_g, lam_init):
    b, s, _ = qa.shape
    tq = min(ATT_TQ, s)
    kern = functools.partial(_diff_attn_kernel, lam_init=lam_init, tq=tq)
    seq_blk = lambda bi, h: (bi, 0, h)
    return pl.pallas_call(
        kern,
        grid=(b, DIFF_HEADS),
        in_specs=[
            pl.BlockSpec(lam_params.shape, lambda bi, h: (0, 0)),
            pl.BlockSpec((None, s, LANES), seq_blk),
            pl.BlockSpec((None, s, LANES), seq_blk),
            pl.BlockSpec((None, s, LANES), seq_blk),
            pl.BlockSpec((None, 2, ATT_RQ, ATT_RQ), lambda bi, h: (h, 0, 0, 0)),
            pl.BlockSpec((1, LANES), lambda bi, h: (0, 0)),
        ],
        out_specs=pl.BlockSpec((None, s, LANES), seq_blk),
        out_shape=jax.ShapeDtypeStruct((b, s, DIFF_HEADS * LANES), BF16),
        scratch_shapes=[pltpu.VMEM((s // ATT_RQ, 2 * ATT_RQ, LANES), BF16)]
        + [pltpu.VMEM((s // ATT_RQ, 2 * ATT_RQ, LANES), F32)] * 3,
        compiler_params=_cparams("parallel", "parallel"),
        name="diff_attention",
    )(lam_params, qa, ka, va, bias_tab, diff_g)


def _retention_kernel(q_ref, k_ref, v_ref, gate_ref, din_ref, qd_ref, kd_ref, cd_ref, g_ref, o_ref):
    s = q_ref.shape[0]
    c = RET_CHUNK
    lane = lax.broadcasted_iota(I32, (c, LANES), 1)
    g = g_ref[...]
    states = [jnp.zeros((LANES, RET_DV), F32) for _ in range(2)]
    for n in range(s // c):
        r = slice(n * c, (n + 1) * c)
        q_pair = q_ref[r, :].astype(F32)
        k_pair = k_ref[r, :].astype(F32)
        for par in range(2):
            own = (lane // RET_DK) == par
            cols = slice(par * RET_DV, (par + 1) * RET_DV)
            q = jnp.where(own, q_pair, 0.0)
            k = jnp.where(own, k_pair, 0.0)
            v = v_ref[r, cols]
            scores = _dot_nt(q.astype(BF16), k.astype(BF16)) * din_ref[par]
            inner = _dot(scores.astype(BF16), v)
            cross = _dot((q * qd_ref[par]).astype(BF16), states[par].astype(BF16))
            kv = _dot((k * kd_ref[par]).T.astype(BF16), v)
            states[par] = cd_ref[par] * states[par] + kv
            y = inner + cross
            mu = jnp.mean(y, axis=-1, keepdims=True)
            yc = y - mu
            var = jnp.mean(yc * yc, axis=-1, keepdims=True)
            yn = yc * lax.rsqrt(var + LN_EPS) * g
            gate = gate_ref[r, cols].astype(F32)
            o_ref[r, cols] = (_silu(gate) * yn).astype(BF16)


def _retention(qb, kb, vb, gb, tabs, ret_g):
    b, s, _ = qb.shape
    din, qd, kd, cd = tabs
    pair = lambda bi, hp: (bi, 0, hp)
    tab = lambda bi, hp: (hp, 0, 0)
    return pl.pallas_call(
        _retention_kernel,
        grid=(b, RET_HEADS // 2),
        in_specs=[
            pl.BlockSpec((None, s, LANES), pair),
            pl.BlockSpec((None, s, LANES), pair),
            pl.BlockSpec((None, s, 2 * RET_DV), pair),
            pl.BlockSpec((None, s, 2 * RET_DV), pair),
            pl.BlockSpec((2, RET_CHUNK, RET_CHUNK), tab),
            pl.BlockSpec((2, RET_CHUNK, LANES), tab),
            pl.BlockSpec((2, RET_CHUNK, LANES), tab),
            pl.BlockSpec((2, 1, LANES), tab),
            pl.BlockSpec((1, RET_DV), lambda bi, hp: (0, 0)),
        ],
        out_specs=pl.BlockSpec((None, s, 2 * RET_DV), pair),
        out_shape=jax.ShapeDtypeStruct((b, s, RET_HEADS * RET_DV), BF16),
        compiler_params=_cparams("parallel", "parallel"),
        name="retention",
    )(qb, kb, vb, gb, din, qd, kd, cd, ret_g)


def _odd_inproj_kernel(x_ref, w_ref, wf_ref, bf_ref, q_ref, k_ref, v_ref, cum_ref, carry_ref, *, nblk_s):
    i = pl.program_id(0)
    x = x_ref[...].astype(BF16)
    d = q_ref.shape[1]
    q_ref[...] = (_dot(x, w_ref[:, 0:d]) * (FOX_DH ** -0.5 * LOG2E)).astype(BF16)
    k_ref[...] = _dot(x, w_ref[:, d:2 * d]).astype(BF16)
    v_ref[...] = _dot(x, w_ref[:, 2 * d:3 * d]).astype(BF16)
    z = _dot(x, wf_ref[...]) + bf_ref[...]
    c = jnp.minimum(z, 0.0) - jnp.log1p(jnp.exp(-jnp.abs(z)))
    tm = c.shape[0]
    row = lax.broadcasted_iota(I32, c.shape, 0)
    step = 1
    while step < tm:
        c = c + jnp.where(row >= step, pltpu.roll(c, step, 0), 0.0)
        step *= 2

    @pl.when(i % nblk_s == 0)
    def _():
        carry_ref[...] = jnp.zeros_like(carry_ref)

    c = c + carry_ref[...]
    cum_ref[...] = c * LOG2E
    carry_ref[...] = c[tm - 1:tm, :]


def _odd_inproj(x2, w_bf, wf_bf, bfg, seq):
    t, d = x2.shape
    tm = min(PROJ_TM, seq)
    nblk_s = seq // tm
    row = lambda i: (i, 0)
    kern = functools.partial(_odd_inproj_kernel, nblk_s=nblk_s)
    return pl.pallas_call(
        kern,
        grid=(t // tm,),
        in_specs=[
            pl.BlockSpec((tm, d), row),
            pl.BlockSpec(w_bf.shape, lambda i: (0, 0)),
            pl.BlockSpec(wf_bf.shape, lambda i: (0, 0)),
            pl.BlockSpec(bfg.shape, lambda i: (0, 0)),
        ],
        out_specs=[pl.BlockSpec((tm, d), row)] * 3 + [pl.BlockSpec((tm, LANES), row)],
        out_shape=[jax.ShapeDtypeStruct((t, d), BF16)] * 3 + [jax.ShapeDtypeStruct((t, LANES), F32)],
        scratch_shapes=[pltpu.VMEM((1, LANES), F32)],
        compiler_params=_cparams("arbitrary"),
        name="odd_inproj",
    )(x2, w_bf, wf_bf, bfg)


def _fox_attn_kernel(q_ref, k_ref, v_ref, cq_ref, ck_ref, o_ref, qm_ref, va_ref, cqc_ref, m_ref, acc_ref, *, tq):
    seq = q_ref.shape[0]
    tk = tq
    hp = pl.program_id(1)
    q = q_ref[...]
    v = v_ref[...]
    lane = lax.broadcasted_iota(I32, (seq, LANES), 1)
    cq_all = cq_ref[...]
    for par in range(2):
        own = (lane // FOX_DH) == par
        qm_ref[par] = jnp.where(own, q, jnp.zeros_like(q))
        va_ref[par] = jnp.where(own, v, jnp.ones_like(v))
        cq = jnp.sum(jnp.where(lane == 2 * hp + par, cq_all, 0.0), axis=-1, keepdims=True)
        cqc_ref[par] = jnp.broadcast_to(cq, (seq, LANES))
    m_ref[...] = jnp.full(m_ref.shape, NEG_BIG, F32)
    acc_ref[...] = jnp.zeros(acc_ref.shape, F32)
    rq = ATT_RQ
    nr = tq // rq
    upper = (lax.broadcasted_iota(I32, (rq, rq), 1) > lax.broadcasted_iota(I32, (rq, rq), 0))
    lane_q = lax.broadcasted_iota(I32, (tq, LANES), 1)

    chains = [(ii, j, r, par) for ii in range(seq // tq) for j in range(ii + 1)
              for r in range(nr) for par in range(2)]

    def n_keys(ii, j, r):
        return (r + 1) * rq if j == ii else tk

    def scores(chain):
        ii, j, r, par = chain
        k = k_ref[j * tk:j * tk + n_keys(ii, j, r), :]
        return _dot_nt(qm_ref[par, ii * tq + r * rq:ii * tq + (r + 1) * rq, :], k)

    def finish(chain, s):
        ii, j, r, par = chain
        rows = slice(ii * tq + r * rq, ii * tq + (r + 1) * rq)
        nk = n_keys(ii, j, r)
        ck = ck_ref[par:par + 1, j * tk:j * tk + nk]
        cq = cqc_ref[par, rows, :]
        sc = []
        for kc in range(nk // rq):
            t = s[:, kc * rq:(kc + 1) * rq] - ck[:, kc * rq:(kc + 1) * rq]
            if j == ii and kc == r:
                t = jnp.where(upper, NEG_BIG, t)
            sc.append(t)
        mx = sc[0]
        for t in sc[1:]:
            mx = jnp.maximum(mx, t)
        m_old = m_ref[par, rows, :]
        m_new = jnp.maximum(m_old, jnp.max(mx, axis=-1, keepdims=True) + cq)
        alpha = jnp.exp2(m_old - m_new)
        shift = m_new - cq
        p = jnp.concatenate([jnp.exp2(t - shift).astype(BF16) for t in sc], axis=1)
        acc_ref[par, rows, :] = alpha * acc_ref[par, rows, :] + _dot(p, va_ref[par, j * tk:j * tk + nk, :])
        m_ref[par, rows, :] = m_new
        if j == ii and r == nr - 1 and par == 1:
            blk = slice(ii * tq, (ii + 1) * tq)
            acc0 = acc_ref[0, blk, :]
            acc1 = acc_ref[1, blk, :]
            out0 = acc0 / acc0[:, FOX_DH:FOX_DH + 1]
            out1 = acc1 / acc1[:, 0:1]
            o_ref[blk, :] = jnp.where(lane_q < FOX_DH, out0, out1).astype(BF16)

    _run_chains(chains, FOX_LOOKAHEAD, scores, finish)


def _fox_attention(q, k, v, cum, cum_t):
    b, s, d = q.shape
    tq = min(FOX_TQ, s)
    npair = d // LANES
    seq_blk = lambda bi, h: (bi, 0, h)
    return pl.pallas_call(
        functools.partial(_fox_attn_kernel, tq=tq),
        grid=(b, npair),
        in_specs=[
            pl.BlockSpec((None, s, LANES), seq_blk),
            pl.BlockSpec((None, s, LANES), seq_blk),
            pl.BlockSpec((None, s, LANES), seq_blk),
            pl.BlockSpec((None, s, LANES), lambda bi, h: (bi, 0, 0)),
            pl.BlockSpec((None, None, 2, s), lambda bi, h: (bi, h, 0, 0)),
        ],
        out_specs=pl.BlockSpec((None, s, LANES), seq_blk),
        out_shape=jax.ShapeDtypeStruct((b, s, d), BF16),
        scratch_shapes=[pltpu.VMEM((2, s, LANES), BF16), pltpu.VMEM((2, s, LANES), BF16),
                        pltpu.VMEM((2, s, LANES), F32), pltpu.VMEM((2, s, LANES), F32),
                        pltpu.VMEM((2, s, LANES), F32)],
        compiler_params=_cparams("parallel", "parallel"),
        name="fox_attention",
    )(q, k, v, cum, cum_t)


def _outproj_router_kernel(*refs, n_y):
    y_refs = refs[:n_y]
    w_refs = refs[n_y:2 * n_y]
    x_ref, g_ref, b_ref, rwt_ref, tri_ref, h_ref, route_ref, col_ref, cnt_ref = refs[2 * n_y:]
    rw2 = rwt_ref[...]
    rw_hi = rw2.astype(BF16)
    rw_lo = (rw2 - rw_hi.astype(F32)).astype(BF16)
    lane_w = lax.broadcasted_iota(I32, rw2.shape, 1)
    w = jnp.where(lane_w < N_EXPERTS, rw_hi, rw_lo)
    tm = x_ref.shape[0]
    rc = ROUTER_ROWS
    logit_chunks = []

    def project(c):
        rows = slice(c * rc, (c + 1) * rc)
        mix = _dot(y_refs[0][rows, :], w_refs[0][...])
        for yr, wr in zip(y_refs[1:], w_refs[1:]):
            mix = mix + _dot(yr[rows, :], wr[...])
        return mix

    def norm_and_logits(c, mix):
        rows = slice(c * rc, (c + 1) * rc)
        h = _layer_norm_rows(DEEPNORM_ALPHA * x_ref[rows, :] + mix, g_ref[...], b_ref[...])
        h_ref[rows, :] = h
        h_hi = h.astype(BF16)
        h_lo = (h - h_hi.astype(F32)).astype(BF16)
        p_hi = _dot(h_hi, w)
        p_lo = _dot(h_lo, w)
        slab = p_hi + (pltpu.roll(p_hi, LANES - N_EXPERTS, 1) + p_lo)
        logit_chunks.append(slab.T[0:N_EXPERTS])

    _run_chains(list(range(tm // rc)), 1, project, norm_and_logits)

    logits = jnp.concatenate(logit_chunks, axis=1)
    row = lax.broadcasted_iota(I32, (N_EXPERTS, tm), 0)
    mx = jnp.max(logits, axis=0, keepdims=True)
    ex = jnp.exp(logits - mx)
    probs = ex / jnp.sum(ex, axis=0, keepdims=True)
    grp = row // EXPERTS_PER_GROUP

    def top2(vals):
        v1 = jnp.max(vals, axis=0, keepdims=True)
        i1 = jnp.min(jnp.where(vals == v1, row, N_EXPERTS), axis=0, keepdims=True)
        rest = jnp.where(row == i1, -2.0, vals)
        v2 = jnp.max(rest, axis=0, keepdims=True)
        i2 = jnp.min(jnp.where(rest == v2, row, N_EXPERTS), axis=0, keepdims=True)
        return v1, i1, v2, i2

    best_score = None
    best = None
    for gi in range(N_GROUPS):
        v1, _, v2, _ = top2(jnp.where(grp == gi, probs, -1.0))
        score = v1 + v2
        if gi == 0:
            best_score, best = score, jnp.zeros_like(score, dtype=I32)
        else:
            better = score > best_score
            best = jnp.where(better, gi, best)
            best_score = jnp.where(better, score, best_score)
    v1, i1, v2, i2 = top2(jnp.where(grp == best, probs, -1.0))
    tot = v1 + v2
    g1 = v1 / tot
    g2 = v2 / tot

    onehot = jnp.where((row == i1) | (row == i2), 1.0, 0.0)
    pref = _dot(onehot.astype(BF16), tri_ref[...])
    cnt = jnp.broadcast_to(jnp.sum(onehot, axis=1, keepdims=True), cnt_ref.shape)
    cnt_ref[...] = cnt
    grp_rows = jnp.floor((cnt + (SORT_ALIGN - 1)) * (1.0 / SORT_ALIGN)) * SORT_ALIGN
    row_c = lax.broadcasted_iota(I32, cnt.shape, 0)
    start = grp_rows
    step = 1
    while step < N_EXPERTS:
        start = start + jnp.where(row_c >= step, pltpu.roll(start, step, 0), 0.0)
        step *= 2
    start = (start - grp_rows)[:, 0:1]
    s1 = jnp.sum(jnp.where(row == i1, pref + start, 0.0), axis=0, keepdims=True)
    s2 = jnp.sum(jnp.where(row == i2, pref + start, 0.0), axis=0, keepdims=True)

    row8 = lax.broadcasted_iota(I32, (ROUTE_ROWS, tm), 0)
    fields = (i1.astype(F32), i2.astype(F32), g1, g2, s1, s2)
    route = jnp.zeros((ROUTE_ROWS, tm), F32)
    for n, f in enumerate(fields):
        route = jnp.where(row8 == n, f, route)
    route_ref[...] = route
    col_ref[...] = jnp.concatenate([route, jnp.zeros((LANES - ROUTE_ROWS, tm), F32)], axis=0).T


def _outproj_router(ys, ws, x2, ln_g, ln_b, rwt, tri):
    t, d = x2.shape
    tm = tri.shape[0]
    row = lambda i: (i, 0)
    full = lambda i: (0, 0)
    n_y = len(ys)
    kern = functools.partial(_outproj_router_kernel, n_y=n_y)
    return pl.pallas_call(
        kern,
        grid=(t // tm,),
        in_specs=([pl.BlockSpec((tm, y.shape[1]), row) for y in ys]
                  + [pl.BlockSpec(w.shape, full) for w in ws]
                  + [pl.BlockSpec((tm, d), row), pl.BlockSpec((1, d), full), pl.BlockSpec((1, d), full),
                     pl.BlockSpec(rwt.shape, full), pl.BlockSpec(tri.shape, full)]),
        out_specs=[pl.BlockSpec((tm, d), row), pl.BlockSpec((ROUTE_ROWS, tm), lambda i: (0, i)),
                   pl.BlockSpec((tm, LANES), row), pl.BlockSpec((N_EXPERTS, LANES), row)],
        out_shape=[jax.ShapeDtypeStruct((t, d), F32), jax.ShapeDtypeStruct((ROUTE_ROWS, t), F32),
                   jax.ShapeDtypeStruct((t, LANES), F32), jax.ShapeDtypeStruct((t // tm * N_EXPERTS, LANES), F32)],
        compiler_params=_cparams("parallel"),
        name="outproj_router",
    )(*ys, *ws, x2, ln_g, ln_b, rwt, tri)


def _group_copies(n_rows, local_start, global_start, local_ref, global_ref, sem, to_global):
    @pl.when(n_rows > 0)
    def _():
        n = pl.multiple_of(n_rows, SORT_ALIGN)
        loc = local_ref.at[pl.ds(pl.multiple_of(local_start, SORT_ALIGN), n)]
        glo = global_ref.at[pl.ds(pl.multiple_of(global_start, SORT_ALIGN), n)]
        (pltpu.make_async_copy(loc, glo, sem) if to_global else pltpu.make_async_copy(glo, loc, sem)).start()


def _wait_group_copies(total_rows, local_ref, global_ref, sem, to_global):
    loc = local_ref.at[pl.ds(0, total_rows)]
    glo = global_ref.at[pl.ds(0, total_rows)]
    (pltpu.make_async_copy(loc, glo, sem) if to_global else pltpu.make_async_copy(glo, loc, sem)).wait()


def _tile_groups(tile, loff_ref, rows_ref, gpos_ref, local_ref, global_ref, sem, to_global):
    for e in range(N_EXPERTS):
        n = tile * N_EXPERTS + e
        _group_copies(rows_ref[n], loff_ref[n], gpos_ref[n], local_ref, global_ref, sem, to_global)


def _dispatch_kernel(loff_ref, rows_ref, gpos_ref, tot_ref, tail_ref, nt_ref, h_ref, route_ref, xs_hbm,
                     sbuf, zbuf, sem):
    i = pl.program_id(0)
    tm = h_ref.shape[0]
    n_local = sbuf.shape[1]

    @pl.when(i == 0)
    def _():
        zbuf[...] = jnp.zeros_like(zbuf)

        def zero_tile(start):
            return pltpu.make_async_copy(zbuf, xs_hbm.at[pl.ds(pl.multiple_of(start, MOE_TR), MOE_TR)], sem.at[0])

        for e in range(N_EXPERTS):
            @pl.when(tail_ref[e] >= 0)
            def _(e=e):
                zero_tile(tail_ref[e]).start()

        def start_unused(r, c):
            zero_tile(r * MOE_TR).start()
            return c

        def wait_unused(r, c):
            zero_tile(r * MOE_TR).wait()
            return c

        n_all = xs_hbm.shape[0] // MOE_TR
        lax.fori_loop(nt_ref[0], n_all, start_unused, 0)
        for e in range(N_EXPERTS):
            @pl.when(tail_ref[e] >= 0)
            def _(e=e):
                zero_tile(tail_ref[e]).wait()
        lax.fori_loop(nt_ref[0], n_all, wait_unused, 0)

    srow = lax.broadcasted_iota(I32, (n_local, tm), 0).astype(F32)
    slot1 = route_ref[ROUTE_SLOT:ROUTE_SLOT + 1, :]
    slot2 = route_ref[ROUTE_SLOT + 1:ROUTE_SLOT + 2, :]
    perm = jnp.where((srow == slot1) | (srow == slot2), 1.0, 0.0).astype(BF16)
    slot = i % 2
    sbuf[slot] = _dot(perm, h_ref[...].astype(BF16)).astype(BF16)
    _tile_groups(i, loff_ref, rows_ref, gpos_ref, sbuf.at[slot], xs_hbm, sem.at[slot], True)

    @pl.when(i > 0)
    def _():
        _wait_group_copies(pl.multiple_of(tot_ref[i - 1], SORT_ALIGN), sbuf.at[1 - slot], xs_hbm,
                           sem.at[1 - slot], True)

    @pl.when(i == pl.num_programs(0) - 1)
    def _():
        _wait_group_copies(pl.multiple_of(tot_ref[i], SORT_ALIGN), sbuf.at[slot], xs_hbm, sem.at[slot], True)


def _dispatch(meta, h, route, n_slots):
    t, d = h.shape
    tm = MOE_TM
    n_local = _local_rows(tm)
    idx = lambda i, *_: (i, 0)
    grid_spec = pltpu.PrefetchScalarGridSpec(
        num_scalar_prefetch=6,
        grid=(t // tm,),
        in_specs=[pl.BlockSpec((tm, d), idx), pl.BlockSpec((ROUTE_ROWS, tm), lambda i, *_: (0, i))],
        out_specs=pl.BlockSpec(memory_space=pl.ANY),
        scratch_shapes=[pltpu.VMEM((2, n_local, d), BF16), pltpu.VMEM((MOE_TR, d), BF16),
                        pltpu.SemaphoreType.DMA((2,))],
    )
    return pl.pallas_call(
        _dispatch_kernel,
        grid_spec=grid_spec,
        out_shape=jax.ShapeDtypeStruct((n_slots, d), BF16),
        compiler_params=_cparams("arbitrary"),
        name="moe_dispatch",
    )(meta["loff"], meta["rows"], meta["gpos"], meta["tot"], meta["tail"], meta["n_tiles"], h, route)


def _expert_kernel(te_ref, nx_ref, nt_ref, x_ref, wg_hbm, wu_hbm, wd_hbm, o_ref,
                   wgs, wus, wds, wgb, wub, wdb, sem, nsw_ref, *, layer):
    r = pl.program_id(0)
    cur = te_ref[r]

    def weight_copies(e, slot):
        return [pltpu.make_async_copy(src.at[layer, e], dst.at[slot], sem.at[slot])
                for src, dst in ((wg_hbm, wgs), (wu_hbm, wus), (wd_hbm, wds))]

    @pl.when(r == 0)
    def _():
        nsw_ref[0] = 0
        for cp in weight_copies(cur, 0):
            cp.start()

    @pl.when((r == 0) | (cur != te_ref[jnp.maximum(r - 1, 0)]))
    def _():
        slot = nsw_ref[0] % 2
        nsw_ref[0] = nsw_ref[0] + 1
        for cp in weight_copies(cur, slot):
            cp.wait()

        @pl.when(nx_ref[r] != cur)
        def _():
            for cp in weight_copies(nx_ref[r], 1 - slot):
                cp.start()

        wgb[...] = wgs[slot].astype(BF16)
        wub[...] = wus[slot].astype(BF16)
        wdb[...] = wds[slot].astype(BF16)

    @pl.when(r < nt_ref[0])
    def _():
        x = x_ref[...]
        a = _silu(_dot(x, wgb[...])) * _dot(x, wub[...])
        o_ref[...] = _dot(a.astype(BF16), wdb[...]).astype(BF16)

    @pl.when(r >= nt_ref[0])
    def _():
        o_ref[...] = jnp.zeros_like(o_ref)


def _expert_mlps(tile_expert, next_expert, n_tiles, xs, wg, wu, wd, layer):
    n_slots, d = xs.shape
    tr = MOE_TR
    dff = wg.shape[3]
    grid_spec = pltpu.PrefetchScalarGridSpec(
        num_scalar_prefetch=3,
        grid=(n_slots // tr,),
        in_specs=[
            pl.BlockSpec((tr, d), lambda r, te, nx, nt: (jnp.where(r < nt[0], r, 0), 0)),
            pl.BlockSpec(memory_space=pl.ANY),
            pl.BlockSpec(memory_space=pl.ANY),
            pl.BlockSpec(memory_space=pl.ANY),
        ],
        out_specs=pl.BlockSpec((tr, d), lambda r, te, nx, nt: (r, 0)),
        scratch_shapes=[pltpu.VMEM((2, d, dff), F32), pltpu.VMEM((2, d, dff), F32), pltpu.VMEM((2, dff, d), F32),
                        pltpu.VMEM((d, dff), BF16), pltpu.VMEM((d, dff), BF16), pltpu.VMEM((dff, d), BF16),
                        pltpu.SemaphoreType.DMA((2,)), pltpu.SMEM((1,), I32)],
    )
    return pl.pallas_call(
        functools.partial(_expert_kernel, layer=layer),
        grid_spec=grid_spec,
        out_shape=jax.ShapeDtypeStruct((n_slots, d), BF16),
        compiler_params=_cparams("arbitrary"),
        name="expert_mlps",
    )(tile_expert, next_expert, n_tiles, xs, wg, wu, wd)


def _combine_kernel(loff_ref, rows_ref, gpos_ref, tot_ref, ys_hbm, h_ref, col_ref, p_ref, g_ref, b_ref,
                    pg_ref, pp_ref, o_ref, ybuf, sem):
    i = pl.program_id(0)
    n = pl.num_programs(0)
    tm = h_ref.shape[0]
    n_local = ybuf.shape[1]
    slot = i % 2

    @pl.when(i == 0)
    def _():
        ybuf[...] = jnp.zeros_like(ybuf)
        _tile_groups(0, loff_ref, rows_ref, gpos_ref, ybuf.at[0], ys_hbm, sem.at[0], False)

    @pl.when(i + 1 < n)
    def _():
        _tile_groups(i + 1, loff_ref, rows_ref, gpos_ref, ybuf.at[1 - slot], ys_hbm, sem.at[1 - slot], False)

    _wait_group_copies(pl.multiple_of(tot_ref[i], SORT_ALIGN), ybuf.at[slot], ys_hbm, sem.at[slot], False)
    rc = COMBINE_ROWS
    scol = lax.broadcasted_iota(I32, (rc, n_local), 1).astype(F32)

    def gather_rows(c):
        rows = slice(c * rc, (c + 1) * rc)
        cols = col_ref[rows, :]
        pick = jnp.zeros((rc, n_local), F32)
        for k in range(2):
            pick = jnp.where(scol == cols[:, ROUTE_SLOT + k:ROUTE_SLOT + k + 1],
                             cols[:, ROUTE_GATE + k:ROUTE_GATE + k + 1], pick)
        ffn = _dot(pick.astype(BF16), ybuf[slot])
        return ffn, _dot(p_ref[rows, :].astype(BF16), pp_ref[...])

    def finish(c, gathered):
        rows = slice(c * rc, (c + 1) * rc)
        ffn, pe = gathered
        h2 = _layer_norm_rows(DEEPNORM_ALPHA * h_ref[rows, :] + ffn, g_ref[...], b_ref[...])
        gate = _sigmoid(_dot(h2.astype(BF16), pg_ref[...]))
        o_ref[rows, :] = h2 + gate * pe

    _run_chains(list(range(tm // rc)), 1, gather_rows, finish)


def _combine(meta, ys, h, cols, p2, ln_g, ln_b, pg_bf, pp_bf):
    t, d = h.shape
    tm = MOE_TM
    pdim = p2.shape[1]
    row = lambda i, *_: (i, 0)
    full = lambda i, *_: (0, 0)
    grid_spec = pltpu.PrefetchScalarGridSpec(
        num_scalar_prefetch=4,
        grid=(t // tm,),
        in_specs=[
            pl.BlockSpec(memory_space=pl.ANY),
            pl.BlockSpec((tm, d), row),
            pl.BlockSpec((tm, LANES), row),
            pl.BlockSpec((tm, pdim), row),
            pl.BlockSpec((1, d), full),
            pl.BlockSpec((1, d), full),
            pl.BlockSpec(pg_bf.shape, full),
            pl.BlockSpec(pp_bf.shape, full),
        ],
        out_specs=pl.BlockSpec((tm, d), row),
        scratch_shapes=[pltpu.VMEM((2, _local_rows(tm), d), BF16), pltpu.SemaphoreType.DMA((2,))],
    )
    return pl.pallas_call(
        _combine_kernel,
        grid_spec=grid_spec,
        out_shape=jax.ShapeDtypeStruct((t, d), F32),
        compiler_params=_cparams("arbitrary"),
        name="moe_combine",
    )(meta["loff"], meta["rows"], meta["gpos"], meta["tot"], ys, h, cols, p2, ln_g, ln_b, pg_bf, pp_bf)


def _rotary_tables(seq):
    half = RET_DK // 2
    inv = (np.float32(ROPE_BASE) ** (-np.arange(half, dtype=np.float32) / np.float32(half))).astype(np.float32)
    ang = (np.arange(seq, dtype=np.float32)[:, None] * inv[None, :]).astype(np.float32)
    cos = np.cos(ang.astype(np.float64))
    sin = np.sin(ang.astype(np.float64))
    cos_h = np.concatenate([cos, cos], axis=1)
    sin_h = np.concatenate([-sin, sin], axis=1)
    return (jnp.asarray(np.tile(cos_h, (1, RET_HEADS)), F32), jnp.asarray(np.tile(sin_h, (1, RET_HEADS)), F32))


def _retention_tables():
    c = RET_CHUNK
    h = np.arange(RET_HEADS, dtype=np.float64)
    log_g = np.log1p(-np.exp2(-5.0 - h))
    j = np.arange(c, dtype=np.float64)
    rel = j[:, None] - j[None, :]
    din = np.where(rel >= 0, np.exp(np.maximum(rel, 0.0)[None] * log_g[:, None, None]), 0.0)
    qd = np.exp((j + 1.0)[None] * log_g[:, None])
    kd = np.exp((c - 1.0 - j)[None] * log_g[:, None])
    cd = np.exp(c * log_g)
    qd = np.broadcast_to(qd[:, :, None], (RET_HEADS, c, LANES))
    kd = np.broadcast_to(kd[:, :, None], (RET_HEADS, c, LANES))
    cd = np.broadcast_to(cd[:, None, None], (RET_HEADS, 1, LANES))
    return tuple(jnp.asarray(a, F32) for a in (din, qd, kd, cd))


def _t5_bucket_np(dist):
    max_exact = REL_BUCKETS // 2
    d = np.maximum(dist, 1).astype(np.float32)
    large = max_exact + (np.log(d / np.float32(max_exact)) / np.float32(math.log(REL_MAX_DIST / max_exact))
                         * np.float32(REL_BUCKETS - max_exact)).astype(np.int32)
    large = np.minimum(large, REL_BUCKETS - 1)
    return np.where(dist < max_exact, dist, large)


def _diff_bias_tables(rel_bias, seq):
    c = ATT_RQ
    r = np.arange(c)
    dist0 = r[:, None] - r[None, :]
    far = REL_BUCKETS - 1
    assert np.all(_t5_bucket_np(np.arange(c + 1, max(seq, 2 * c))) == far)
    bidx = np.stack([_t5_bucket_np(np.maximum(dist0, 0)), _t5_bucket_np(dist0 + c)])
    rb = rel_bias.astype(F32).T
    shifted = (rb - rb[:, far:far + 1]) * LOG2E
    bidx = jnp.asarray(bidx, I32)[None]
    tab = jnp.zeros((rb.shape[0], 2, c, c), F32)
    for bucket in range(REL_BUCKETS - 1):
        tab = jnp.where(bidx == bucket, shifted[:, bucket][:, None, None, None], tab)
    causal = jnp.asarray(np.stack([dist0 >= 0, np.ones_like(dist0, bool)]))[None]
    return jnp.where(causal, tab, NEG_BIG)


def _local_rows(tm):
    need = 2 * tm + N_EXPERTS * (SORT_ALIGN - 1)
    return -(-need // LANES) * LANES


def _round_up(a, m):
    return ((a + m - 1) // m) * m


def _route_meta(cnt, t):
    tm, tr = MOE_TM, MOE_TR
    nt = t // tm
    counts = cnt.reshape(nt, N_EXPERTS, LANES)[:, :, 0].astype(I32)
    rows = _round_up(counts, SORT_ALIGN)
    loff = jnp.cumsum(rows, axis=1) - rows
    seg = jnp.sum(rows, axis=0)
    seg_pad = _round_up(seg, tr)
    ends = jnp.cumsum(seg_pad)
    offs = ends - seg_pad
    gpos = offs[None, :] + jnp.cumsum(rows, axis=0) - rows
    n_slots = 2 * t + nt * N_EXPERTS * (SORT_ALIGN - 1)
    n_slots = _round_up(n_slots, tr) + N_EXPERTS * tr
    n_tiles = (ends[-1] // tr).astype(I32)
    tile_start = jnp.arange(n_slots // tr, dtype=I32) * tr
    tile_expert = jnp.sum((tile_start[:, None] >= ends[None, :]).astype(I32), axis=1)
    last = jnp.sum((((n_tiles - 1) * tr) >= ends).astype(I32))
    tile_expert = jnp.minimum(tile_expert, last).astype(I32)
    eid = jnp.arange(N_EXPERTS, dtype=I32)
    later = (eid[None, :] > eid[:, None]) & (seg_pad > 0)[None, :]
    nxt = jnp.min(jnp.where(later, eid[None, :], N_EXPERTS), axis=1)
    nxt = jnp.where(nxt == N_EXPERTS, eid, nxt)
    next_expert = jnp.sum(jnp.where(tile_expert[:, None] == eid[None, :], nxt[None, :], 0), axis=1).astype(I32)
    meta = {
        "loff": loff.reshape(-1).astype(I32), "rows": rows.reshape(-1).astype(I32),
        "gpos": gpos.reshape(-1).astype(I32), "tot": jnp.sum(rows, axis=1).astype(I32),
        "tail": jnp.where(seg_pad > seg, ends - tr, -1).astype(I32),
        "tile_expert": tile_expert, "next_expert": next_expert, "n_tiles": n_tiles.reshape(1),
    }
    return meta, n_slots


def kernel(x, p, rel_bias, router_w, even_w_in, even_w_out, even_lambda, even_diff_norm, even_ret_norm,
           odd_w_in, odd_b_forget, odd_w_out, ln_mix_g, ln_mix_b, ln_ffn_g, ln_ffn_b,
           moe_w_gate, moe_w_up, moe_w_down, ple_proj, ple_gate):
    b, s, d = x.shape
    t = b * s
    assert d == 1024 and p.shape[0] == DEPTH and even_w_in.shape[2] == 3072
    assert odd_w_in.shape[2] == 3 * d + FOX_HEADS and moe_w_gate.shape[1] == N_EXPERTS
    assert s % RET_CHUNK == 0 and s % min(ATT_TQ, s) == 0 and s % min(FOX_TQ, s) == 0
    assert t % min(PROJ_TM, s) == 0 and t % MOE_TM == 0

    cos_t, sin_t = _rotary_tables(s)
    ret_tabs = _retention_tables()
    bias_tab = _diff_bias_tables(rel_bias, s)
    rw32 = router_w.astype(F32)
    rwt = jnp.zeros((d, LANES), F32).at[:, :N_EXPERTS].set(rw32).at[:, N_EXPERTS:2 * N_EXPERTS].set(rw32)
    tok = np.arange(MOE_TM)
    tri = jnp.asarray(tok[:, None] < tok[None, :], BF16)

    x2 = x.reshape(t, d)
    for i in range(DEPTH):
        j = i // 2
        if i % 2 == 0:
            lam_init = 0.8 - 0.6 * math.exp(-0.3 * i)
            qa, ka, va, qb, kb, vb, gb = _even_inproj(x2, even_w_in[j].astype(BF16), cos_t, sin_t, s)
            sh = lambda a: a.reshape(b, s, a.shape[1])
            ya = _diff_attention(sh(qa), sh(ka), sh(va), bias_tab, even_lambda[j].astype(F32),
                                 even_diff_norm[j].reshape(1, -1).astype(F32), lam_init)
            yb = _retention(sh(qb), sh(kb), sh(vb), sh(gb), ret_tabs, even_ret_norm[j].reshape(1, -1).astype(F32))
            w_out = even_w_out[j].astype(BF16)
            n_a = ya.shape[2]
            ys = [ya.reshape(t, -1), yb.reshape(t, -1)]
            ws = [w_out[:n_a], w_out[n_a:]]
        else:
            w_in = odd_w_in[j]
            wf = jnp.zeros((d, LANES), BF16).at[:, :FOX_HEADS].set(w_in[:, 3 * d:].astype(BF16))
            bfg = jnp.zeros((1, LANES), F32).at[0, :FOX_HEADS].set(odd_b_forget[j].astype(F32))
            q, k, v, cum = _odd_inproj(x2, w_in[:, :3 * d].astype(BF16), wf, bfg, s)
            cum3 = cum.reshape(b, s, LANES)
            cum_t = jnp.transpose(cum3[:, :, :FOX_HEADS], (0, 2, 1)).reshape(b, FOX_HEADS // 2, 2, s)
            y = _fox_attention(q.reshape(b, s, d), k.reshape(b, s, d), v.reshape(b, s, d), cum3, cum_t)
            ys = [y.reshape(t, d)]
            ws = [odd_w_out[j].astype(BF16)]
        h, route, cols, cnt = _outproj_router(ys, ws, x2, ln_mix_g[i].reshape(1, d), ln_mix_b[i].reshape(1, d),
                                              rwt, tri)
        meta, n_slots = _route_meta(cnt, t)
        xs = _dispatch(meta, h, route, n_slots)
        rows = _expert_mlps(meta["tile_expert"], meta["next_expert"], meta["n_tiles"], xs,
                            moe_w_gate, moe_w_up, moe_w_down, i)
        x2 = _combine(meta, rows, h, cols, p[i].reshape(t, -1), ln_ffn_g[i].reshape(1, d),
                      ln_ffn_b[i].reshape(1, d), ple_gate[i].astype(BF16), ple_proj[i].astype(BF16))
    return x2.reshape(b, s, d)
```

```python
import functools
import math

import numpy as np
import jax
import jax.numpy as jnp
from jax import lax
from jax.experimental import pallas as pl
from jax.experimental.pallas import tpu as pltpu

F32 = jnp.float32
BF16 = jnp.bfloat16
I32 = jnp.int32

DIFF_HEADS = 4
DIFF_DK = 64
RET_HEADS = 4
RET_DK = 64
RET_DV = 128
RET_CHUNK = 128
FOX_HEADS = 16
FOX_DH = 64
REL_BUCKETS = 32
REL_MAX_DIST = 128
N_GROUPS = 4
EXPERTS_PER_GROUP = 4
N_EXPERTS = 16
DEPTH = 2
DEEPNORM_ALPHA = (2 * DEPTH) ** 0.25
LN_EPS = 1e-5
ROPE_BASE = 10000.0
NEG_BIG = -1e30
LOG2E = math.log2(math.e)

VMEM_LIMIT_BYTES = 48 * 1024 * 1024
LANES = 128

PROJ_TM = 1024
ATT_TQ = 512
FOX_TQ = 512
ATT_RQ = LANES
DIFF_LOOKAHEAD = 3
FOX_LOOKAHEAD = 4
MOE_TR = 512
MOE_TM = 512
SORT_ALIGN = 16
ROUTE_ROWS = 8
ROUTE_GATE = 2
ROUTE_SLOT = 4
COMBINE_ROWS = 256
ROUTER_ROWS = 256


def _cparams(*sem):
    return pltpu.CompilerParams(dimension_semantics=sem, vmem_limit_bytes=VMEM_LIMIT_BYTES)


def _dot(a, b):
    return jnp.dot(a, b, preferred_element_type=F32)


def _dot_nt(a, b):
    return lax.dot_general(a, b, (((1,), (1,)), ((), ())), preferred_element_type=F32)


def _layer_norm_rows(z, g, b):
    mu = jnp.mean(z, axis=-1, keepdims=True)
    zc = z - mu
    var = jnp.mean(zc * zc, axis=-1, keepdims=True)
    return zc * lax.rsqrt(var + LN_EPS) * g + b


def _silu(x):
    return x * (1.0 / (1.0 + jnp.exp(-x)))


def _sigmoid(x):
    return 1.0 / (1.0 + jnp.exp(-x))


def _even_inproj_kernel(x_ref, w_ref, cos_ref, sin_ref,
                        qa_ref, ka_ref, va_ref, qb_ref, kb_ref, vb_ref, gb_ref):
    x = x_ref[...].astype(BF16)

    def mm(c0, c1):
        return _dot(x, w_ref[:, c0:c1])

    qa_ref[...] = (mm(0, 512) * (DIFF_DK ** -0.5 * LOG2E)).astype(BF16)
    ka_ref[...] = mm(512, 1024).astype(BF16)
    va_ref[...] = mm(1024, 1536).astype(BF16)
    qk = mm(1536, 2048)
    cos = cos_ref[...]
    sin = sin_ref[...]
    lane = lax.broadcasted_iota(I32, cos.shape, 1)
    first_half = (lane % RET_DK) < (RET_DK // 2)

    def rot(t):
        sw = jnp.where(first_half, pltpu.roll(t, t.shape[1] - RET_DK // 2, 1),
                       pltpu.roll(t, RET_DK // 2, 1))
        return t * cos + sw * sin

    qb_ref[...] = rot(qk[:, :256]).astype(BF16)
    kb_ref[...] = (rot(qk[:, 256:]) * (RET_DK ** -0.5)).astype(BF16)
    vb_ref[...] = mm(2048, 2560).astype(BF16)
    gb_ref[...] = mm(2560, 3072).astype(BF16)


def _even_inproj(x2, w_bf, cos_t, sin_t, seq):
    t, d = x2.shape
    tm = min(PROJ_TM, seq)
    nblk_s = seq // tm
    widths = (512, 512, 512, 256, 256, 512, 512)
    row = lambda i: (i, 0)
    return pl.pallas_call(
        _even_inproj_kernel,
        grid=(t // tm,),
        in_specs=[
            pl.BlockSpec((tm, d), row),
            pl.BlockSpec(w_bf.shape, lambda i: (0, 0)),
            pl.BlockSpec((tm, 256), lambda i: (i % nblk_s, 0)),
            pl.BlockSpec((tm, 256), lambda i: (i % nblk_s, 0)),
        ],
        out_specs=[pl.BlockSpec((tm, w), row) for w in widths],
        out_shape=[jax.ShapeDtypeStruct((t, w), BF16) for w in widths],
        compiler_params=_cparams("parallel"),
        name="even_inproj",
    )(x2, w_bf, cos_t, sin_t)


def _run_chains(chains, lookahead, scores, finish):
    pending = [scores(c) for c in chains[:lookahead]]
    for n, chain in enumerate(chains):
        if n + lookahead < len(chains):
            pending.append(scores(chains[n + lookahead]))
        finish(chain, pending.pop(0))


def _diff_attn_kernel(lam_ref, q_ref, k_ref, v_ref, bias_ref, g_ref, o_ref, qm_ref, m_ref, l_ref, acc_ref,
                      *, lam_init, tq):
    seq = q_ref.shape[0]
    tk = tq
    rq = ATT_RQ
    nr = tq // rq
    lane = lax.broadcasted_iota(I32, (rq, LANES), 1)
    for n in range(seq // rq):
        q = q_ref[n * rq:(n + 1) * rq, :]
        zero = jnp.zeros_like(q)
        qm_ref[n, 0:rq, :] = jnp.where(lane < DIFF_DK, q, zero)
        qm_ref[n, rq:2 * rq, :] = jnp.where(lane >= DIFF_DK, q, zero)
    m_ref[...] = jnp.full(m_ref.shape, NEG_BIG, F32)
    l_ref[...] = jnp.zeros(l_ref.shape, F32)
    acc_ref[...] = jnp.zeros(acc_ref.shape, F32)
    lp = lam_ref[...]
    lam = (jnp.exp(jnp.sum(lp[0:1, :] * lp[1:2, :], axis=-1, keepdims=True))
           - jnp.exp(jnp.sum(lp[2:3, :] * lp[3:4, :], axis=-1, keepdims=True)) + lam_init)
    bias2 = [jnp.concatenate([bias_ref[n], bias_ref[n]], axis=0) for n in range(2)]

    chains = [(ii, j, r) for ii in range(seq // tq) for j in range(ii + 1) for r in range(nr)]

    def n_keys(ii, j, r):
        return (r + 1) * rq if j == ii else tk

    def scores(chain):
        ii, j, r = chain
        k = k_ref[j * tk:j * tk + n_keys(ii, j, r), :]
        return _dot_nt(qm_ref[ii * nr + r], k)

    def finish(chain, s):
        ii, j, r = chain
        g = ii * nr + r
        nk = n_keys(ii, j, r)
        v = v_ref[j * tk:j * tk + nk, :]
        sc = []
        for kc in range(nk // rq):
            t = s[:, kc * rq:(kc + 1) * rq]
            back = g - (j * nr + kc)
            if back <= 1:
                t = t + bias2[back]
            sc.append(t)
        mx = sc[0]
        for t in sc[1:]:
            mx = jnp.maximum(mx, t)
        m_old = m_ref[g]
        m_new = jnp.maximum(m_old, jnp.max(mx, axis=-1, keepdims=True))
        alpha = jnp.exp2(m_old - m_new)
        ps = [jnp.exp2(t - m_new) for t in sc]
        psum = ps[0]
        for t in ps[1:]:
            psum = psum + t
        l_ref[g] = alpha * l_ref[g] + psum
        p = jnp.concatenate([t.astype(BF16) for t in ps], axis=1)
        acc_ref[g] = alpha * acc_ref[g] + _dot(p, v)
        m_ref[g] = m_new
        if j == ii:
            l_all = jnp.sum(l_ref[g], axis=-1, keepdims=True)
            a = acc_ref[g] / l_all
            o = a[0:rq] - lam * a[rq:2 * rq]
            o = o * lax.rsqrt(jnp.mean(o * o, axis=-1, keepdims=True) + LN_EPS)
            o_ref[g * rq:(g + 1) * rq, :] = (o * g_ref[...] * (1.0 - lam_init)).astype(BF16)

    _run_chains(chains, DIFF_LOOKAHEAD, scores, finish)


def _diff_attention(qa, ka, va, bias_tab, lam_params, diff_g, lam_init):
    b, s, _ = qa.shape
    tq = min(ATT_TQ, s)
    kern = functools.partial(_diff_attn_kernel, lam_init=lam_init, tq=tq)
    seq_blk = lambda bi, h: (bi, 0, h)
    return pl.pallas_call(
        kern,
        grid=(b, DIFF_HEADS),
        in_specs=[
            pl.BlockSpec(lam_params.shape, lambda bi, h: (0, 0)),
            pl.BlockSpec((None, s, LANES), seq_blk),
            pl.BlockSpec((None, s, LANES), seq_blk),
            pl.BlockSpec((None, s, LANES), seq_blk),
            pl.BlockSpec((None, 2, ATT_RQ, ATT_RQ), lambda bi, h: (h, 0, 0, 0)),
            pl.BlockSpec((1, LANES), lambda bi, h: (0, 0)),
        ],
        out_specs=pl.BlockSpec((None, s, LANES), seq_blk),
        out_shape=jax.ShapeDtypeStruct((b, s, DIFF_HEADS * LANES), BF16),
        scratch_shapes=[pltpu.VMEM((s // ATT_RQ, 2 * ATT_RQ, LANES), BF16)]
        + [pltpu.VMEM((s // ATT_RQ, 2 * ATT_RQ, LANES), F32)] * 3,
        compiler_params=_cparams("parallel", "parallel"),
        name="diff_attention",
    )(lam_params, qa, ka, va, bias_tab, diff_g)


def _retention_kernel(q_ref, k_ref, v_ref, gate_ref, din_ref, qd_ref, kd_ref, cd_ref, g_ref, o_ref):
    s = q_ref.shape[0]
    c = RET_CHUNK
    lane = lax.broadcasted_iota(I32, (c, LANES), 1)
    g = g_ref[...]
    states = [jnp.zeros((LANES, RET_DV), F32) for _ in range(2)]
    for n in range(s // c):
        r = slice(n * c, (n + 1) * c)
        q_pair = q_ref[r, :].astype(F32)
        k_pair = k_ref[r, :].astype(F32)
        for par in range(2):
            own = (lane // RET_DK) == par
            cols = slice(par * RET_DV, (par + 1) * RET_DV)
            q = jnp.where(own, q_pair, 0.0)
            k = jnp.where(own, k_pair, 0.0)
            v = v_ref[r, cols]
            scores = _dot_nt(q.astype(BF16), k.astype(BF16)) * din_ref[par]
            inner = _dot(scores.astype(BF16), v)
            cross = _dot((q * qd_ref[par]).astype(BF16), states[par].astype(BF16))
            kv = _dot((k * kd_ref[par]).T.astype(BF16), v)
            states[par] = cd_ref[par] * states[par] + kv
            y = inner + cross
            mu = jnp.mean(y, axis=-1, keepdims=True)
            yc = y - mu
            var = jnp.mean(yc * yc, axis=-1, keepdims=True)
            yn = yc * lax.rsqrt(var + LN_EPS) * g
            gate = gate_ref[r, cols].astype(F32)
            o_ref[r, cols] = (_silu(gate) * yn).astype(BF16)


def _retention(qb, kb, vb, gb, tabs, ret_g):
    b, s, _ = qb.shape
    din, qd, kd, cd = tabs
    pair = lambda bi, hp: (bi, 0, hp)
    tab = lambda bi, hp: (hp, 0, 0)
    return pl.pallas_call(
        _retention_kernel,
        grid=(b, RET_HEADS // 2),
        in_specs=[
            pl.BlockSpec((None, s, LANES), pair),
            pl.BlockSpec((None, s, LANES), pair),
            pl.BlockSpec((None, s, 2 * RET_DV), pair),
            pl.BlockSpec((None, s, 2 * RET_DV), pair),
            pl.BlockSpec((2, RET_CHUNK, RET_CHUNK), tab),
            pl.BlockSpec((2, RET_CHUNK, LANES), tab),
            pl.BlockSpec((2, RET_CHUNK, LANES), tab),
            pl.BlockSpec((2, 1, LANES), tab),
            pl.BlockSpec((1, RET_DV), lambda bi, hp: (0, 0)),
        ],
        out_specs=pl.BlockSpec((None, s, 2 * RET_DV), pair),
        out_shape=jax.ShapeDtypeStruct((b, s, RET_HEADS * RET_DV), BF16),
        compiler_params=_cparams("parallel", "parallel"),
        name="retention",
    )(qb, kb, vb, gb, din, qd, kd, cd, ret_g)


def _odd_inproj_kernel(x_ref, w_ref, wf_ref, bf_ref, q_ref, k_ref, v_ref, cum_ref, carry_ref, *, nblk_s):
    i = pl.program_id(0)
    x = x_ref[...].astype(BF16)
    d = q_ref.shape[1]
    q_ref[...] = (_dot(x, w_ref[:, 0:d]) * (FOX_DH ** -0.5 * LOG2E)).astype(BF16)
    k_ref[...] = _dot(x, w_ref[:, d:2 * d]).astype(BF16)
    v_ref[...] = _dot(x, w_ref[:, 2 * d:3 * d]).astype(BF16)
    z = _dot(x, wf_ref[...]) + bf_ref[...]
    c = jnp.minimum(z, 0.0) - jnp.log1p(jnp.exp(-jnp.abs(z)))
    tm = c.shape[0]
    row = lax.broadcasted_iota(I32, c.shape, 0)
    step = 1
    while step < tm:
        c = c + jnp.where(row >= step, pltpu.roll(c, step, 0), 0.0)
        step *= 2

    @pl.when(i % nblk_s == 0)
    def _():
        carry_ref[...] = jnp.zeros_like(carry_ref)

    c = c + carry_ref[...]
    cum_ref[...] = c * LOG2E
    carry_ref[...] = c[tm - 1:tm, :]


def _odd_inproj(x2, w_bf, wf_bf, bfg, seq):
    t, d = x2.shape
    tm = min(PROJ_TM, seq)
    nblk_s = seq // tm
    row = lambda i: (i, 0)
    kern = functools.partial(_odd_inproj_kernel, nblk_s=nblk_s)
    return pl.pallas_call(
        kern,
        grid=(t // tm,),
        in_specs=[
            pl.BlockSpec((tm, d), row),
            pl.BlockSpec(w_bf.shape, lambda i: (0, 0)),
            pl.BlockSpec(wf_bf.shape, lambda i: (0, 0)),
            pl.BlockSpec(bfg.shape, lambda i: (0, 0)),
        ],
        out_specs=[pl.BlockSpec((tm, d), row)] * 3 + [pl.BlockSpec((tm, LANES), row)],
        out_shape=[jax.ShapeDtypeStruct((t, d), BF16)] * 3 + [jax.ShapeDtypeStruct((t, LANES), F32)],
        scratch_shapes=[pltpu.VMEM((1, LANES), F32)],
        compiler_params=_cparams("arbitrary"),
        name="odd_inproj",
    )(x2, w_bf, wf_bf, bfg)


def _fox_attn_kernel(q_ref, k_ref, v_ref, cq_ref, ck_ref, o_ref, qm_ref, va_ref, cqc_ref, m_ref, acc_ref, *, tq):
    seq = q_ref.shape[0]
    tk = tq
    hp = pl.program_id(1)
    q = q_ref[...]
    v = v_ref[...]
    lane = lax.broadcasted_iota(I32, (seq, LANES), 1)
    cq_all = cq_ref[...]
    for par in range(2):
        own = (lane // FOX_DH) == par
        qm_ref[par] = jnp.where(own, q, jnp.zeros_like(q))
        va_ref[par] = jnp.where(own, v, jnp.ones_like(v))
        cq = jnp.sum(jnp.where(lane == 2 * hp + par, cq_all, 0.0), axis=-1, keepdims=True)
        cqc_ref[par] = jnp.broadcast_to(cq, (seq, LANES))
    m_ref[...] = jnp.full(m_ref.shape, NEG_BIG, F32)
    acc_ref[...] = jnp.zeros(acc_ref.shape, F32)
    rq = ATT_RQ
    nr = tq // rq
    upper = (lax.broadcasted_iota(I32, (rq, rq), 1) > lax.broadcasted_iota(I32, (rq, rq), 0))
    lane_q = lax.broadcasted_iota(I32, (tq, LANES), 1)

    chains = [(ii, j, r, par) for ii in range(seq // tq) for j in range(ii + 1)
              for r in range(nr) for par in range(2)]

    def n_keys(ii, j, r):
        return (r + 1) * rq if j == ii else tk

    def scores(chain):
        ii, j, r, par = chain
        k = k_ref[j * tk:j * tk + n_keys(ii, j, r), :]
        return _dot_nt(qm_ref[par, ii * tq + r * rq:ii * tq + (r + 1) * rq, :], k)

    def finish(chain, s):
        ii, j, r, par = chain
        rows = slice(ii * tq + r * rq, ii * tq + (r + 1) * rq)
        nk = n_keys(ii, j, r)
        ck = ck_ref[par:par + 1, j * tk:j * tk + nk]
        cq = cqc_ref[par, rows, :]
        sc = []
        for kc in range(nk // rq):
            t = s[:, kc * rq:(kc + 1) * rq] - ck[:, kc * rq:(kc + 1) * rq]
            if j == ii and kc == r:
                t = jnp.where(upper, NEG_BIG, t)
            sc.append(t)
        mx = sc[0]
        for t in sc[1:]:
            mx = jnp.maximum(mx, t)
        m_old = m_ref[par, rows, :]
        m_new = jnp.maximum(m_old, jnp.max(mx, axis=-1, keepdims=True) + cq)
        alpha = jnp.exp2(m_old - m_new)
        shift = m_new - cq
        p = jnp.concatenate([jnp.exp2(t - shift).astype(BF16) for t in sc], axis=1)
        acc_ref[par, rows, :] = alpha * acc_ref[par, rows, :] + _dot(p, va_ref[par, j * tk:j * tk + nk, :])
        m_ref[par, rows, :] = m_new
        if j == ii and r == nr - 1 and par == 1:
            blk = slice(ii * tq, (ii + 1) * tq)
            acc0 = acc_ref[0, blk, :]
            acc1 = acc_ref[1, blk, :]
            out0 = acc0 / acc0[:, FOX_DH:FOX_DH + 1]
            out1 = acc1 / acc1[:, 0:1]
            o_ref[blk, :] = jnp.where(lane_q < FOX_DH, out0, out1).astype(BF16)

    _run_chains(chains, FOX_LOOKAHEAD, scores, finish)


def _fox_attention(q, k, v, cum, cum_t):
    b, s, d = q.shape
    tq = min(FOX_TQ, s)
    npair = d // LANES
    seq_blk = lambda bi, h: (bi, 0, h)
    return pl.pallas_call(
        functools.partial(_fox_attn_kernel, tq=tq),
        grid=(b, npair),
        in_specs=[
            pl.BlockSpec((None, s, LANES), seq_blk),
            pl.BlockSpec((None, s, LANES), seq_blk),
            pl.BlockSpec((None, s, LANES), seq_blk),
            pl.BlockSpec((None, s, LANES), lambda bi, h: (bi, 0, 0)),
            pl.BlockSpec((None, None, 2, s), lambda bi, h: (bi, h, 0, 0)),
        ],
        out_specs=pl.BlockSpec((None, s, LANES), seq_blk),
        out_shape=jax.ShapeDtypeStruct((b, s, d), BF16),
        scratch_shapes=[pltpu.VMEM((2, s, LANES), BF16), pltpu.VMEM((2, s, LANES), BF16),
                        pltpu.VMEM((2, s, LANES), F32), pltpu.VMEM((2, s, LANES), F32),
                        pltpu.VMEM((2, s, LANES), F32)],
        compiler_params=_cparams("parallel", "parallel"),
        name="fox_attention",
    )(q, k, v, cum, cum_t)


def _outproj_router_kernel(*refs, n_y):
    y_refs = refs[:n_y]
    w_refs = refs[n_y:2 * n_y]
    x_ref, g_ref, b_ref, rwt_ref, tri_ref, h_ref, route_ref, col_ref, cnt_ref = refs[2 * n_y:]
    rw2 = rwt_ref[...]
    rw_hi = rw2.astype(BF16)
    rw_lo = (rw2 - rw_hi.astype(F32)).astype(BF16)
    lane_w = lax.broadcasted_iota(I32, rw2.shape, 1)
    w = jnp.where(lane_w < N_EXPERTS, rw_hi, rw_lo)
    tm = x_ref.shape[0]
    rc = ROUTER_ROWS
    logit_chunks = []

    def project(c):
        rows = slice(c * rc, (c + 1) * rc)
        mix = _dot(y_refs[0][rows, :], w_refs[0][...])
        for yr, wr in zip(y_refs[1:], w_refs[1:]):
            mix = mix + _dot(yr[rows, :], wr[...])
        return mix

    def norm_and_logits(c, mix):
        rows = slice(c * rc, (c + 1) * rc)
        h = _layer_norm_rows(DEEPNORM_ALPHA * x_ref[rows, :] + mix, g_ref[...], b_ref[...])
        h_ref[rows, :] = h
        h_hi = h.astype(BF16)
        h_lo = (h - h_hi.astype(F32)).astype(BF16)
        p_hi = _dot(h_hi, w)
        p_lo = _dot(h_lo, w)
        slab = p_hi + (pltpu.roll(p_hi, LANES - N_EXPERTS, 1) + p_lo)
        logit_chunks.append(slab.T[0:N_EXPERTS])

    _run_chains(list(range(tm // rc)), 1, project, norm_and_logits)

    logits = jnp.concatenate(logit_chunks, axis=1)
    row = lax.broadcasted_iota(I32, (N_EXPERTS, tm), 0)
    mx = jnp.max(logits, axis=0, keepdims=True)
    ex = jnp.exp(logits - mx)
    probs = ex / jnp.sum(ex, axis=0, keepdims=True)
    grp = row // EXPERTS_PER_GROUP

    def top2(vals):
        v1 = jnp.max(vals, axis=0, keepdims=True)
        i1 = jnp.min(jnp.where(vals == v1, row, N_EXPERTS), axis=0, keepdims=True)
        rest = jnp.where(row == i1, -2.0, vals)
        v2 = jnp.max(rest, axis=0, keepdims=True)
        i2 = jnp.min(jnp.where(rest == v2, row, N_EXPERTS), axis=0, keepdims=True)
        return v1, i1, v2, i2

    best_score = None
    best = None
    for gi in range(N_GROUPS):
        v1, _, v2, _ = top2(jnp.where(grp == gi, probs, -1.0))
        score = v1 + v2
        if gi == 0:
            best_score, best = score, jnp.zeros_like(score, dtype=I32)
        else:
            better = score > best_score
            best = jnp.where(better, gi, best)
            best_score = jnp.where(better, score, best_score)
    v1, i1, v2, i2 = top2(jnp.where(grp == best, probs, -1.0))
    tot = v1 + v2
    g1 = v1 / tot
    g2 = v2 / tot

    onehot = jnp.where((row == i1) | (row == i2), 1.0, 0.0)
    pref = _dot(onehot.astype(BF16), tri_ref[...])
    cnt = jnp.broadcast_to(jnp.sum(onehot, axis=1, keepdims=True), cnt_ref.shape)
    cnt_ref[...] = cnt
    grp_rows = jnp.floor((cnt + (SORT_ALIGN - 1)) * (1.0 / SORT_ALIGN)) * SORT_ALIGN
    row_c = lax.broadcasted_iota(I32, cnt.shape, 0)
    start = grp_rows
    step = 1
    while step < N_EXPERTS:
        start = start + jnp.where(row_c >= step, pltpu.roll(start, step, 0), 0.0)
        step *= 2
    start = (start - grp_rows)[:, 0:1]
    s1 = jnp.sum(jnp.where(row == i1, pref + start, 0.0), axis=0, keepdims=True)
    s2 = jnp.sum(jnp.where(row == i2, pref + start, 0.0), axis=0, keepdims=True)

    row8 = lax.broadcasted_iota(I32, (ROUTE_ROWS, tm), 0)
    fields = (i1.astype(F32), i2.astype(F32), g1, g2, s1, s2)
    route = jnp.zeros((ROUTE_ROWS, tm), F32)
    for n, f in enumerate(fields):
        route = jnp.where(row8 == n, f, route)
    route_ref[...] = route
    col_ref[...] = jnp.concatenate([route, jnp.zeros((LANES - ROUTE_ROWS, tm), F32)], axis=0).T


def _outproj_router(ys, ws, x2, ln_g, ln_b, rwt, tri):
    t, d = x2.shape
    tm = tri.shape[0]
    row = lambda i: (i, 0)
    full = lambda i: (0, 0)
    n_y = len(ys)
    kern = functools.partial(_outproj_router_kernel, n_y=n_y)
    return pl.pallas_call(
        kern,
        grid=(t // tm,),
        in_specs=([pl.BlockSpec((tm, y.shape[1]), row) for y in ys]
                  + [pl.BlockSpec(w.shape, full) for w in ws]
                  + [pl.BlockSpec((tm, d), row), pl.BlockSpec((1, d), full), pl.BlockSpec((1, d), full),
                     pl.BlockSpec(rwt.shape, full), pl.BlockSpec(tri.shape, full)]),
        out_specs=[pl.BlockSpec((tm, d), row), pl.BlockSpec((ROUTE_ROWS, tm), lambda i: (0, i)),
                   pl.BlockSpec((tm, LANES), row), pl.BlockSpec((N_EXPERTS, LANES), row)],
        out_shape=[jax.ShapeDtypeStruct((t, d), F32), jax.ShapeDtypeStruct((ROUTE_ROWS, t), F32),
                   jax.ShapeDtypeStruct((t, LANES), F32), jax.ShapeDtypeStruct((t // tm * N_EXPERTS, LANES), F32)],
        compiler_params=_cparams("parallel"),
        name="outproj_router",
    )(*ys, *ws, x2, ln_g, ln_b, rwt, tri)


def _group_copies(n_rows, local_start, global_start, local_ref, global_ref, sem, to_global):
    @pl.when(n_rows > 0)
    def _():
        n = pl.multiple_of(n_rows, SORT_ALIGN)
        loc = local_ref.at[pl.ds(pl.multiple_of(local_start, SORT_ALIGN), n)]
        glo = global_ref.at[pl.ds(pl.multiple_of(global_start, SORT_ALIGN), n)]
        (pltpu.make_async_copy(loc, glo, sem) if to_global else pltpu.make_async_copy(glo, loc, sem)).start()


def _wait_group_copies(total_rows, local_ref, global_ref, sem, to_global):
    loc = local_ref.at[pl.ds(0, total_rows)]
    glo = global_ref.at[pl.ds(0, total_rows)]
    (pltpu.make_async_copy(loc, glo, sem) if to_global else pltpu.make_async_copy(glo, loc, sem)).wait()


def _tile_groups(tile, loff_ref, rows_ref, gpos_ref, local_ref, global_ref, sem, to_global):
    for e in range(N_EXPERTS):
        n = tile * N_EXPERTS + e
        _group_copies(rows_ref[n], loff_ref[n], gpos_ref[n], local_ref, global_ref, sem, to_global)


def _dispatch_kernel(loff_ref, rows_ref, gpos_ref, tot_ref, tail_ref, nt_ref, h_ref, route_ref, xs_hbm,
                     sbuf, zbuf, sem):
    i = pl.program_id(0)
    tm = h_ref.shape[0]
    n_local = sbuf.shape[1]

    @pl.when(i == 0)
    def _():
        zbuf[...] = jnp.zeros_like(zbuf)

        def zero_tile(start):
            return pltpu.make_async_copy(zbuf, xs_hbm.at[pl.ds(pl.multiple_of(start, MOE_TR), MOE_TR)], sem.at[0])

        for e in range(N_EXPERTS):
            @pl.when(tail_ref[e] >= 0)
            def _(e=e):
                zero_tile(tail_ref[e]).start()

        def start_unused(r, c):
            zero_tile(r * MOE_TR).start()
            return c

        def wait_unused(r, c):
            zero_tile(r * MOE_TR).wait()
            return c

        n_all = xs_hbm.shape[0] // MOE_TR
        lax.fori_loop(nt_ref[0], n_all, start_unused, 0)
        for e in range(N_EXPERTS):
            @pl.when(tail_ref[e] >= 0)
            def _(e=e):
                zero_tile(tail_ref[e]).wait()
        lax.fori_loop(nt_ref[0], n_all, wait_unused, 0)

    srow = lax.broadcasted_iota(I32, (n_local, tm), 0).astype(F32)
    slot1 = route_ref[ROUTE_SLOT:ROUTE_SLOT + 1, :]
    slot2 = route_ref[ROUTE_SLOT + 1:ROUTE_SLOT + 2, :]
    perm = jnp.where((srow == slot1) | (srow == slot2), 1.0, 0.0).astype(BF16)
    slot = i % 2
    sbuf[slot] = _dot(perm, h_ref[...].astype(BF16)).astype(BF16)
    _tile_groups(i, loff_ref, rows_ref, gpos_ref, sbuf.at[slot], xs_hbm, sem.at[slot], True)

    @pl.when(i > 0)
    def _():
        _wait_group_copies(pl.multiple_of(tot_ref[i - 1], SORT_ALIGN), sbuf.at[1 - slot], xs_hbm,
                           sem.at[1 - slot], True)

    @pl.when(i == pl.num_programs(0) - 1)
    def _():
        _wait_group_copies(pl.multiple_of(tot_ref[i], SORT_ALIGN), sbuf.at[slot], xs_hbm, sem.at[slot], True)


def _dispatch(meta, h, route, n_slots):
    t, d = h.shape
    tm = MOE_TM
    n_local = _local_rows(tm)
    idx = lambda i, *_: (i, 0)
    grid_spec = pltpu.PrefetchScalarGridSpec(
        num_scalar_prefetch=6,
        grid=(t // tm,),
        in_specs=[pl.BlockSpec((tm, d), idx), pl.BlockSpec((ROUTE_ROWS, tm), lambda i, *_: (0, i))],
        out_specs=pl.BlockSpec(memory_space=pl.ANY),
        scratch_shapes=[pltpu.VMEM((2, n_local, d), BF16), pltpu.VMEM((MOE_TR, d), BF16),
                        pltpu.SemaphoreType.DMA((2,))],
    )
    return pl.pallas_call(
        _dispatch_kernel,
        grid_spec=grid_spec,
        out_shape=jax.ShapeDtypeStruct((n_slots, d), BF16),
        compiler_params=_cparams("arbitrary"),
        name="moe_dispatch",
    )(meta["loff"], meta["rows"], meta["gpos"], meta["tot"], meta["tail"], meta["n_tiles"], h, route)


def _expert_kernel(te_ref, nx_ref, nt_ref, x_ref, wg_hbm, wu_hbm, wd_hbm, o_ref,
                   wgs, wus, wds, wgb, wub, wdb, sem, nsw_ref, *, layer):
    r = pl.program_id(0)
    cur = te_ref[r]

    def weight_copies(e, slot):
        return [pltpu.make_async_copy(src.at[layer, e], dst.at[slot], sem.at[slot])
                for src, dst in ((wg_hbm, wgs), (wu_hbm, wus), (wd_hbm, wds))]

    @pl.when(r == 0)
    def _():
        nsw_ref[0] = 0
        for cp in weight_copies(cur, 0):
            cp.start()

    @pl.when((r == 0) | (cur != te_ref[jnp.maximum(r - 1, 0)]))
    def _():
        slot = nsw_ref[0] % 2
        nsw_ref[0] = nsw_ref[0] + 1
        for cp in weight_copies(cur, slot):
            cp.wait()

        @pl.when(nx_ref[r] != cur)
        def _():
            for cp in weight_copies(nx_ref[r], 1 - slot):
                cp.start()

        wgb[...] = wgs[slot].astype(BF16)
        wub[...] = wus[slot].astype(BF16)
        wdb[...] = wds[slot].astype(BF16)

    @pl.when(r < nt_ref[0])
    def _():
        x = x_ref[...]
        a = _silu(_dot(x, wgb[...])) * _dot(x, wub[...])
        o_ref[...] = _dot(a.astype(BF16), wdb[...]).astype(BF16)

    @pl.when(r >= nt_ref[0])
    def _():
        o_ref[...] = jnp.zeros_like(o_ref)


def _expert_mlps(tile_expert, next_expert, n_tiles, xs, wg, wu, wd, layer):
    n_slots, d = xs.shape
    tr = MOE_TR
    dff = wg.shape[3]
    grid_spec = pltpu.PrefetchScalarGridSpec(
        num_scalar_prefetch=3,
        grid=(n_slots // tr,),
        in_specs=[
            pl.BlockSpec((tr, d), lambda r, te, nx, nt: (jnp.where(r < nt[0], r, 0), 0)),
            pl.BlockSpec(memory_space=pl.ANY),
            pl.BlockSpec(memory_space=pl.ANY),
            pl.BlockSpec(memory_space=pl.ANY),
        ],
        out_specs=pl.BlockSpec((tr, d), lambda r, te, nx, nt: (r, 0)),
        scratch_shapes=[pltpu.VMEM((2, d, dff), F32), pltpu.VMEM((2, d, dff), F32), pltpu.VMEM((2, dff, d), F32),
                        pltpu.VMEM((d, dff), BF16), pltpu.VMEM((d, dff), BF16), pltpu.VMEM((dff, d), BF16),
                        pltpu.SemaphoreType.DMA((2,)), pltpu.SMEM((1,), I32)],
    )
    return pl.pallas_call(
        functools.partial(_expert_kernel, layer=layer),
        grid_spec=grid_spec,
        out_shape=jax.ShapeDtypeStruct((n_slots, d), BF16),
        compiler_params=_cparams("arbitrary"),
        name="expert_mlps",
    )(tile_expert, next_expert, n_tiles, xs, wg, wu, wd)


def _combine_kernel(loff_ref, rows_ref, gpos_ref, tot_ref, ys_hbm, h_ref, col_ref, p_ref, g_ref, b_ref,
                    pg_ref, pp_ref, o_ref, ybuf, sem):
    i = pl.program_id(0)
    n = pl.num_programs(0)
    tm = h_ref.shape[0]
    n_local = ybuf.shape[1]
    slot = i % 2

    @pl.when(i == 0)
    def _():
        ybuf[...] = jnp.zeros_like(ybuf)
        _tile_groups(0, loff_ref, rows_ref, gpos_ref, ybuf.at[0], ys_hbm, sem.at[0], False)

    @pl.when(i + 1 < n)
    def _():
        _tile_groups(i + 1, loff_ref, rows_ref, gpos_ref, ybuf.at[1 - slot], ys_hbm, sem.at[1 - slot], False)

    _wait_group_copies(pl.multiple_of(tot_ref[i], SORT_ALIGN), ybuf.at[slot], ys_hbm, sem.at[slot], False)
    rc = COMBINE_ROWS
    scol = lax.broadcasted_iota(I32, (rc, n_local), 1).astype(F32)

    def gather_rows(c):
        rows = slice(c * rc, (c + 1) * rc)
        cols = col_ref[rows, :]
        pick = jnp.zeros((rc, n_local), F32)
        for k in range(2):
            pick = jnp.where(scol == cols[:, ROUTE_SLOT + k:ROUTE_SLOT + k + 1],
                             cols[:, ROUTE_GATE + k:ROUTE_GATE + k + 1], pick)
        ffn = _dot(pick.astype(BF16), ybuf[slot])
        return ffn, _dot(p_ref[rows, :].astype(BF16), pp_ref[...])

    def finish(c, gathered):
        rows = slice(c * rc, (c + 1) * rc)
        ffn, pe = gathered
        h2 = _layer_norm_rows(DEEPNORM_ALPHA * h_ref[rows, :] + ffn, g_ref[...], b_ref[...])
        gate = _sigmoid(_dot(h2.astype(BF16), pg_ref[...]))
        o_ref[rows, :] = h2 + gate * pe

    _run_chains(list(range(tm // rc)), 1, gather_rows, finish)


def _combine(meta, ys, h, cols, p3, layer, ln_g, ln_b, pg_bf, pp_bf):
    t, d = h.shape
    tm = MOE_TM
    pdim = p3.shape[2]
    row = lambda i, *_: (i, 0)
    full = lambda i, *_: (0, 0)
    grid_spec = pltpu.PrefetchScalarGridSpec(
        num_scalar_prefetch=4,
        grid=(t // tm,),
        in_specs=[
            pl.BlockSpec(memory_space=pl.ANY),
            pl.BlockSpec((tm, d), row),
            pl.BlockSpec((tm, LANES), row),
            pl.BlockSpec((None, tm, pdim), lambda i, *_: (layer, i, 0)),
            pl.BlockSpec((1, d), full),
            pl.BlockSpec((1, d), full),
            pl.BlockSpec(pg_bf.shape, full),
            pl.BlockSpec(pp_bf.shape, full),
        ],
        out_specs=pl.BlockSpec((tm, d), row),
        scratch_shapes=[pltpu.VMEM((2, _local_rows(tm), d), BF16), pltpu.SemaphoreType.DMA((2,))],
    )
    return pl.pallas_call(
        _combine_kernel,
        grid_spec=grid_spec,
        out_shape=jax.ShapeDtypeStruct((t, d), F32),
        compiler_params=_cparams("arbitrary"),
        name="moe_combine",
    )(meta["loff"], meta["rows"], meta["gpos"], meta["tot"], ys, h, cols, p3, ln_g, ln_b, pg_bf, pp_bf)


def _rotary_tables(seq):
    half = RET_DK // 2
    inv = (np.float32(ROPE_BASE) ** (-np.arange(half, dtype=np.float32) / np.float32(half))).astype(np.float32)
    ang = (np.arange(seq, dtype=np.float32)[:, None] * inv[None, :]).astype(np.float32)
    cos = np.cos(ang.astype(np.float64))
    sin = np.sin(ang.astype(np.float64))
    cos_h = np.concatenate([cos, cos], axis=1)
    sin_h = np.concatenate([-sin, sin], axis=1)
    return (jnp.asarray(np.tile(cos_h, (1, RET_HEADS)), F32), jnp.asarray(np.tile(sin_h, (1, RET_HEADS)), F32))


def _retention_tables():
    c = RET_CHUNK
    h = np.arange(RET_HEADS, dtype=np.float64)
    log_g = np.log1p(-np.exp2(-5.0 - h))
    j = np.arange(c, dtype=np.float64)
    rel = j[:, None] - j[None, :]
    din = np.where(rel >= 0, np.exp(np.maximum(rel, 0.0)[None] * log_g[:, None, None]), 0.0)
    qd = np.exp((j + 1.0)[None] * log_g[:, None])
    kd = np.exp((c - 1.0 - j)[None] * log_g[:, None])
    cd = np.exp(c * log_g)
    qd = np.broadcast_to(qd[:, :, None], (RET_HEADS, c, LANES))
    kd = np.broadcast_to(kd[:, :, None], (RET_HEADS, c, LANES))
    cd = np.broadcast_to(cd[:, None, None], (RET_HEADS, 1, LANES))
    return tuple(jnp.asarray(a, F32) for a in (din, qd, kd, cd))


def _t5_bucket_np(dist):
    max_exact = REL_BUCKETS // 2
    d = np.maximum(dist, 1).astype(np.float32)
    large = max_exact + (np.log(d / np.float32(max_exact)) / np.float32(math.log(REL_MAX_DIST / max_exact))
                         * np.float32(REL_BUCKETS - max_exact)).astype(np.int32)
    large = np.minimum(large, REL_BUCKETS - 1)
    return np.where(dist < max_exact, dist, large)


def _diff_bias_tables(rel_bias, seq):
    c = ATT_RQ
    r = np.arange(c)
    dist0 = r[:, None] - r[None, :]
    far = REL_BUCKETS - 1
    assert np.all(_t5_bucket_np(np.arange(c + 1, max(seq, 2 * c))) == far)
    bidx = np.stack([_t5_bucket_np(np.maximum(dist0, 0)), _t5_bucket_np(dist0 + c)])
    rb = rel_bias.astype(F32).T
    shifted = (rb - rb[:, far:far + 1]) * LOG2E
    bidx = jnp.asarray(bidx, I32)[None]
    tab = jnp.zeros((rb.shape[0], 2, c, c), F32)
    for bucket in range(REL_BUCKETS - 1):
        tab = jnp.where(bidx == bucket, shifted[:, bucket][:, None, None, None], tab)
    causal = jnp.asarray(np.stack([dist0 >= 0, np.ones_like(dist0, bool)]))[None]
    return jnp.where(causal, tab, NEG_BIG)


def _local_rows(tm):
    need = 2 * tm + N_EXPERTS * (SORT_ALIGN - 1)
    return -(-need // LANES) * LANES


def _round_up(a, m):
    return ((a + m - 1) // m) * m


def _route_meta(cnt, t):
    tm, tr = MOE_TM, MOE_TR
    nt = t // tm
    counts = cnt.reshape(nt, N_EXPERTS, LANES)[:, :, 0].astype(I32)
    rows = _round_up(counts, SORT_ALIGN)
    loff = jnp.cumsum(rows, axis=1) - rows
    seg = jnp.sum(rows, axis=0)
    seg_pad = _round_up(seg, tr)
    ends = jnp.cumsum(seg_pad)
    offs = ends - seg_pad
    gpos = offs[None, :] + jnp.cumsum(rows, axis=0) - rows
    n_slots = 2 * t + nt * N_EXPERTS * (SORT_ALIGN - 1)
    n_slots = _round_up(n_slots, tr) + N_EXPERTS * tr
    n_tiles = (ends[-1] // tr).astype(I32)
    tile_start = jnp.arange(n_slots // tr, dtype=I32) * tr
    tile_expert = jnp.sum((tile_start[:, None] >= ends[None, :]).astype(I32), axis=1)
    last = jnp.sum((((n_tiles - 1) * tr) >= ends).astype(I32))
    tile_expert = jnp.minimum(tile_expert, last).astype(I32)
    eid = jnp.arange(N_EXPERTS, dtype=I32)
    later = (eid[None, :] > eid[:, None]) & (seg_pad > 0)[None, :]
    nxt = jnp.min(jnp.where(later, eid[None, :], N_EXPERTS), axis=1)
    nxt = jnp.where(nxt == N_EXPERTS, eid, nxt)
    next_expert = jnp.sum(jnp.where(tile_expert[:, None] == eid[None, :], nxt[None, :], 0), axis=1).astype(I32)
    meta = {
        "loff": loff.reshape(-1).astype(I32), "rows": rows.reshape(-1).astype(I32),
        "gpos": gpos.reshape(-1).astype(I32), "tot": jnp.sum(rows, axis=1).astype(I32),
        "tail": jnp.where(seg_pad > seg, ends - tr, -1).astype(I32),
        "tile_expert": tile_expert, "next_expert": next_expert, "n_tiles": n_tiles.reshape(1),
    }
    return meta, n_slots


def kernel(x, p, rel_bias, router_w, even_w_in, even_w_out, even_lambda, even_diff_norm, even_ret_norm,
           odd_w_in, odd_b_forget, odd_w_out, ln_mix_g, ln_mix_b, ln_ffn_g, ln_ffn_b,
           moe_w_gate, moe_w_up, moe_w_down, ple_proj, ple_gate):
    b, s, d = x.shape
    t = b * s
    assert d == 1024 and p.shape[0] == DEPTH and even_w_in.shape[2] == 3072
    assert odd_w_in.shape[2] == 3 * d + FOX_HEADS and moe_w_gate.shape[1] == N_EXPERTS
    assert s % RET_CHUNK == 0 and s % min(ATT_TQ, s) == 0 and s % min(FOX_TQ, s) == 0
    assert t % min(PROJ_TM, s) == 0 and t % MOE_TM == 0

    cos_t, sin_t = _rotary_tables(s)
    ret_tabs = _retention_tables()
    bias_tab = _diff_bias_tables(rel_bias, s)
    rw32 = router_w.astype(F32)
    rwt = jnp.zeros((d, LANES), F32).at[:, :N_EXPERTS].set(rw32).at[:, N_EXPERTS:2 * N_EXPERTS].set(rw32)
    tok = np.arange(MOE_TM)
    tri = jnp.asarray(tok[:, None] < tok[None, :], BF16)

    x2 = x.reshape(t, d)
    for i in range(DEPTH):
        j = i // 2
        if i % 2 == 0:
            lam_init = 0.8 - 0.6 * math.exp(-0.3 * i)
            qa, ka, va, qb, kb, vb, gb = _even_inproj(x2, even_w_in[j].astype(BF16), cos_t, sin_t, s)
            sh = lambda a: a.reshape(b, s, a.shape[1])
            ya = _diff_attention(sh(qa), sh(ka), sh(va), bias_tab, even_lambda[j].astype(F32),
                                 even_diff_norm[j].reshape(1, -1).astype(F32), lam_init)
            yb = _retention(sh(qb), sh(kb), sh(vb), sh(gb), ret_tabs, even_ret_norm[j].reshape(1, -1).astype(F32))
            w_out = even_w_out[j].astype(BF16)
            n_a = ya.shape[2]
            ys = [ya.reshape(t, -1), yb.reshape(t, -1)]
            ws = [w_out[:n_a], w_out[n_a:]]
        else:
            w_in = odd_w_in[j]
            wf = jnp.zeros((d, LANES), BF16).at[:, :FOX_HEADS].set(w_in[:, 3 * d:].astype(BF16))
            bfg = jnp.zeros((1, LANES), F32).at[0, :FOX_HEADS].set(odd_b_forget[j].astype(F32))
            q, k, v, cum = _odd_inproj(x2, w_in[:, :3 * d].astype(BF16), wf, bfg, s)
            cum3 = cum.reshape(b, s, LANES)
            cum_t = jnp.transpose(cum3[:, :, :FOX_HEADS], (0, 2, 1)).reshape(b, FOX_HEADS // 2, 2, s)
            y = _fox_attention(q.reshape(b, s, d), k.reshape(b, s, d), v.reshape(b, s, d), cum3, cum_t)
            ys = [y.reshape(t, d)]
            ws = [odd_w_out[j].astype(BF16)]
        h, route, cols, cnt = _outproj_router(ys, ws, x2, ln_mix_g[i].reshape(1, d), ln_mix_b[i].reshape(1, d),
                                              rwt, tri)
        meta, n_slots = _route_meta(cnt, t)
        xs = _dispatch(meta, h, route, n_slots)
        rows = _expert_mlps(meta["tile_expert"], meta["next_expert"], meta["n_tiles"], xs,
                            moe_w_gate, moe_w_up, moe_w_down, i)
        x2 = _combine(meta, rows, h, cols, p.reshape(DEPTH, t, -1), i, ln_ffn_g[i].reshape(1, d),
                      ln_ffn_b[i].reshape(1, d), ple_gate[i].astype(BF16), ple_proj[i].astype(BF16))
    return x2.reshape(b, s, d)
```

```python
import functools
import math

import numpy as np
import jax
import jax.numpy as jnp
from jax import lax
from jax.experimental import pallas as pl
from jax.experimental.pallas import tpu as pltpu

F32 = jnp.float32
BF16 = jnp.bfloat16
I32 = jnp.int32

DIFF_HEADS = 4
DIFF_DK = 64
RET_HEADS = 4
RET_DK = 64
RET_DV = 128
RET_CHUNK = 128
FOX_HEADS = 16
FOX_DH = 64
REL_BUCKETS = 32
REL_MAX_DIST = 128
N_GROUPS = 4
EXPERTS_PER_GROUP = 4
N_EXPERTS = 16
DEPTH = 2
DEEPNORM_ALPHA = (2 * DEPTH) ** 0.25
LN_EPS = 1e-5
ROPE_BASE = 10000.0
NEG_BIG = -1e30
LOG2E = math.log2(math.e)

VMEM_LIMIT_BYTES = 48 * 1024 * 1024
LANES = 128

PROJ_TM = 1024
ATT_TQ = 512
FOX_TQ = 512
ATT_RQ = LANES
DIFF_LOOKAHEAD = 3
FOX_LOOKAHEAD = 4
MOE_TR = 512
MOE_TM = 512
SORT_ALIGN = 16
ROUTE_ROWS = 8
ROUTE_GATE = 2
ROUTE_SLOT = 4
COMBINE_ROWS = 256
ROUTER_ROWS = 256


def _cparams(*sem):
    return pltpu.CompilerParams(dimension_semantics=sem, vmem_limit_bytes=VMEM_LIMIT_BYTES)


def _dot(a, b):
    return jnp.dot(a, b, preferred_element_type=F32)


def _dot_nt(a, b):
    return lax.dot_general(a, b, (((1,), (1,)), ((), ())), preferred_element_type=F32)


def _layer_norm_rows(z, g, b):
    mu = jnp.mean(z, axis=-1, keepdims=True)
    zc = z - mu
    var = jnp.mean(zc * zc, axis=-1, keepdims=True)
    return zc * lax.rsqrt(var + LN_EPS) * g + b


def _silu(x):
    return x * (1.0 / (1.0 + jnp.exp(-x)))


def _sigmoid(x):
    return 1.0 / (1.0 + jnp.exp(-x))


def _even_inproj_kernel(x_ref, w_ref, cos_ref, sin_ref,
                        qa_ref, ka_ref, va_ref, qb_ref, kb_ref, vb_ref, gb_ref):
    x = x_ref[...].astype(BF16)

    def mm(c0, c1):
        return _dot(x, w_ref[:, c0:c1])

    qa_ref[...] = (mm(0, 512) * (DIFF_DK ** -0.5 * LOG2E)).astype(BF16)
    ka_ref[...] = mm(512, 1024).astype(BF16)
    va_ref[...] = mm(1024, 1536).astype(BF16)
    qk = mm(1536, 2048)
    cos = cos_ref[...]
    sin = sin_ref[...]
    lane = lax.broadcasted_iota(I32, cos.shape, 1)
    first_half = (lane % RET_DK) < (RET_DK // 2)

    def rot(t):
        sw = jnp.where(first_half, pltpu.roll(t, t.shape[1] - RET_DK // 2, 1),
                       pltpu.roll(t, RET_DK // 2, 1))
        return t * cos + sw * sin

    qb_ref[...] = rot(qk[:, :256]).astype(BF16)
    kb_ref[...] = (rot(qk[:, 256:]) * (RET_DK ** -0.5)).astype(BF16)
    vb_ref[...] = mm(2048, 2560).astype(BF16)
    gb_ref[...] = mm(2560, 3072).astype(BF16)


def _even_inproj(x2, w_bf, cos_t, sin_t, seq):
    t, d = x2.shape
    tm = min(PROJ_TM, seq)
    nblk_s = seq // tm
    widths = (512, 512, 512, 256, 256, 512, 512)
    row = lambda i: (i, 0)
    return pl.pallas_call(
        _even_inproj_kernel,
        grid=(t // tm,),
        in_specs=[
            pl.BlockSpec((tm, d), row),
            pl.BlockSpec(w_bf.shape, lambda i: (0, 0)),
            pl.BlockSpec((tm, 256), lambda i: (i % nblk_s, 0)),
            pl.BlockSpec((tm, 256), lambda i: (i % nblk_s, 0)),
        ],
        out_specs=[pl.BlockSpec((tm, w), row) for w in widths],
        out_shape=[jax.ShapeDtypeStruct((t, w), BF16) for w in widths],
        compiler_params=_cparams("parallel"),
        name="even_inproj",
    )(x2, w_bf, cos_t, sin_t)


def _run_chains(chains, lookahead, scores, finish):
    pending = [scores(c) for c in chains[:lookahead]]
    for n, chain in enumerate(chains):
        if n + lookahead < len(chains):
            pending.append(scores(chains[n + lookahead]))
        finish(chain, pending.pop(0))


def _diff_attn_kernel(lam_ref, q_ref, k_ref, v_ref, bias_ref, g_ref, o_ref, qm_ref, m_ref, l_ref, acc_ref,
                      *, lam_init, tq):
    seq = q_ref.shape[0]
    tk = tq
    rq = ATT_RQ
    nr = tq // rq
    lane = lax.broadcasted_iota(I32, (rq, LANES), 1)
    for n in range(seq // rq):
        q = q_ref[n * rq:(n + 1) * rq, :]
        zero = jnp.zeros_like(q)
        qm_ref[n, 0:rq, :] = jnp.where(lane < DIFF_DK, q, zero)
        qm_ref[n, rq:2 * rq, :] = jnp.where(lane >= DIFF_DK, q, zero)
    m_ref[...] = jnp.full(m_ref.shape, NEG_BIG, F32)
    l_ref[...] = jnp.zeros(l_ref.shape, F32)
    acc_ref[...] = jnp.zeros(acc_ref.shape, F32)
    lp = lam_ref[...]
    lam = (jnp.exp(jnp.sum(lp[0:1, :] * lp[1:2, :], axis=-1, keepdims=True))
           - jnp.exp(jnp.sum(lp[2:3, :] * lp[3:4, :], axis=-1, keepdims=True)) + lam_init)
    bias2 = [jnp.concatenate([bias_ref[n], bias_ref[n]], axis=0) for n in range(2)]

    chains = [(ii, j, r) for ii in range(seq // tq) for j in range(ii + 1) for r in range(nr)]

    def n_keys(ii, j, r):
        return (r + 1) * rq if j == ii else tk

    def scores(chain):
        ii, j, r = chain
        k = k_ref[j * tk:j * tk + n_keys(ii, j, r), :]
        return _dot_nt(qm_ref[ii * nr + r], k)

    def finish(chain, s):
        ii, j, r = chain
        g = ii * nr + r
        nk = n_keys(ii, j, r)
        v = v_ref[j * tk:j * tk + nk, :]
        sc = []
        for kc in range(nk // rq):
            t = s[:, kc * rq:(kc + 1) * rq]
            back = g - (j * nr + kc)
            if back <= 1:
                t = t + bias2[back]
            sc.append(t)
        mx = sc[0]
        for t in sc[1:]:
            mx = jnp.maximum(mx, t)
        m_old = m_ref[g]
        m_new = jnp.maximum(m_old, jnp.max(mx, axis=-1, keepdims=True))
        alpha = jnp.exp2(m_old - m_new)
        ps = [jnp.exp2(t - m_new) for t in sc]
        psum = ps[0]
        for t in ps[1:]:
            psum = psum + t
        l_ref[g] = alpha * l_ref[g] + psum
        p = jnp.concatenate([t.astype(BF16) for t in ps], axis=1)
        acc_ref[g] = alpha * acc_ref[g] + _dot(p, v)
        m_ref[g] = m_new
        if j == ii:
            l_all = jnp.sum(l_ref[g], axis=-1, keepdims=True)
            a = acc_ref[g] / l_all
            o = a[0:rq] - lam * a[rq:2 * rq]
            o = o * lax.rsqrt(jnp.mean(o * o, axis=-1, keepdims=True) + LN_EPS)
            o_ref[g * rq:(g + 1) * rq, :] = (o * g_ref[...] * (1.0 - lam_init)).astype(BF16)

    _run_chains(chains, DIFF_LOOKAHEAD, scores, finish)


def _diff_attention(qa, ka, va, bias_tab, lam_params, diff_g, lam_init):
    b, s, _ = qa.shape
    tq = min(ATT_TQ, s)
    kern = functools.partial(_diff_attn_kernel, lam_init=lam_init, tq=tq)
    seq_blk = lambda bi, h: (bi, 0, h)
    return pl.pallas_call(
        kern,
        grid=(b, DIFF_HEADS),
        in_specs=[
            pl.BlockSpec(lam_params.shape, lambda bi, h: (0, 0)),
            pl.BlockSpec((None, s, LANES), seq_blk),
            pl.BlockSpec((None, s, LANES), seq_blk),
            pl.BlockSpec((None, s, LANES), seq_blk),
            pl.BlockSpec((None, 2, ATT_RQ, ATT_RQ), lambda bi, h: (h, 0, 0, 0)),
            pl.BlockSpec((1, LANES), lambda bi, h: (0, 0)),
        ],
        out_specs=pl.BlockSpec((None, s, LANES), seq_blk),
        out_shape=jax.ShapeDtypeStruct((b, s, DIFF_HEADS * LANES), BF16),
        scratch_shapes=[pltpu.VMEM((s // ATT_RQ, 2 * ATT_RQ, LANES), BF16)]
        + [pltpu.VMEM((s // ATT_RQ, 2 * ATT_RQ, LANES), F32)] * 3,
        compiler_params=_cparams("parallel", "parallel"),
        name="diff_attention",
    )(lam_params, qa, ka, va, bias_tab, diff_g)


def _retention_kernel(q_ref, k_ref, v_ref, gate_ref, din_ref, qd_ref, kd_ref, cd_ref, g_ref, o_ref):
    s = q_ref.shape[0]
    c = RET_CHUNK
    lane = lax.broadcasted_iota(I32, (c, LANES), 1)
    g = g_ref[...]
    states = [jnp.zeros((LANES, RET_DV), F32) for _ in range(2)]
    for n in range(s // c):
        r = slice(n * c, (n + 1) * c)
        q_pair = q_ref[r, :].astype(F32)
        k_pair = k_ref[r, :].astype(F32)
        for par in range(2):
            own = (lane // RET_DK) == par
            cols = slice(par * RET_DV, (par + 1) * RET_DV)
            q = jnp.where(own, q_pair, 0.0)
            k = jnp.where(own, k_pair, 0.0)
            v = v_ref[r, cols]
            scores = _dot_nt(q.astype(BF16), k.astype(BF16)) * din_ref[par]
            inner = _dot(scores.astype(BF16), v)
            cross = _dot((q * qd_ref[par]).astype(BF16), states[par].astype(BF16))
            kv = _dot((k * kd_ref[par]).T.astype(BF16), v)
            states[par] = cd_ref[par] * states[par] + kv
            y = inner + cross
            mu = jnp.mean(y, axis=-1, keepdims=True)
            yc = y - mu
            var = jnp.mean(yc * yc, axis=-1, keepdims=True)
            yn = yc * lax.rsqrt(var + LN_EPS) * g
            gate = gate_ref[r, cols].astype(F32)
            o_ref[r, cols] = (_silu(gate) * yn).astype(BF16)


def _retention(qb, kb, vb, gb, tabs, ret_g):
    b, s, _ = qb.shape
    din, qd, kd, cd = tabs
    pair = lambda bi, hp: (bi, 0, hp)
    tab = lambda bi, hp: (hp, 0, 0)
    return pl.pallas_call(
        _retention_kernel,
        grid=(b, RET_HEADS // 2),
        in_specs=[
            pl.BlockSpec((None, s, LANES), pair),
            pl.BlockSpec((None, s, LANES), pair),
            pl.BlockSpec((None, s, 2 * RET_DV), pair),
            pl.BlockSpec((None, s, 2 * RET_DV), pair),
            pl.BlockSpec((2, RET_CHUNK, RET_CHUNK), tab),
            pl.BlockSpec((2, RET_CHUNK, LANES), tab),
            pl.BlockSpec((2, RET_CHUNK, LANES), tab),
            pl.BlockSpec((2, 1, LANES), tab),
            pl.BlockSpec((1, RET_DV), lambda bi, hp: (0, 0)),
        ],
        out_specs=pl.BlockSpec((None, s, 2 * RET_DV), pair),
        out_shape=jax.ShapeDtypeStruct((b, s, RET_HEADS * RET_DV), BF16),
        compiler_params=_cparams("parallel", "parallel"),
        name="retention",
    )(qb, kb, vb, gb, din, qd, kd, cd, ret_g)


def _odd_inproj_kernel(x_ref, w_ref, wf_ref, bf_ref, q_ref, k_ref, v_ref, cum_ref, carry_ref, *, nblk_s):
    i = pl.program_id(0)
    x = x_ref[...].astype(BF16)
    d = q_ref.shape[1]
    q_ref[...] = (_dot(x, w_ref[:, 0:d]) * (FOX_DH ** -0.5 * LOG2E)).astype(BF16)
    k_ref[...] = _dot(x, w_ref[:, d:2 * d]).astype(BF16)
    v_ref[...] = _dot(x, w_ref[:, 2 * d:3 * d]).astype(BF16)
    z = _dot(x, wf_ref[...]) + bf_ref[...]
    c = jnp.minimum(z, 0.0) - jnp.log1p(jnp.exp(-jnp.abs(z)))
    tm = c.shape[0]
    row = lax.broadcasted_iota(I32, c.shape, 0)
    step = 1
    while step < tm:
        c = c + jnp.where(row >= step, pltpu.roll(c, step, 0), 0.0)
        step *= 2

    @pl.when(i % nblk_s == 0)
    def _():
        carry_ref[...] = jnp.zeros_like(carry_ref)

    c = c + carry_ref[...]
    cum_ref[...] = c * LOG2E
    carry_ref[...] = c[tm - 1:tm, :]


def _odd_inproj(x2, w_bf, wf_bf, bfg, seq):
    t, d = x2.shape
    tm = min(PROJ_TM, seq)
    nblk_s = seq // tm
    row = lambda i: (i, 0)
    kern = functools.partial(_odd_inproj_kernel, nblk_s=nblk_s)
    return pl.pallas_call(
        kern,
        grid=(t // tm,),
        in_specs=[
            pl.BlockSpec((tm, d), row),
            pl.BlockSpec(w_bf.shape, lambda i: (0, 0)),
            pl.BlockSpec(wf_bf.shape, lambda i: (0, 0)),
            pl.BlockSpec(bfg.shape, lambda i: (0, 0)),
        ],
        out_specs=[pl.BlockSpec((tm, d), row)] * 3 + [pl.BlockSpec((tm, LANES), row)],
        out_shape=[jax.ShapeDtypeStruct((t, d), BF16)] * 3 + [jax.ShapeDtypeStruct((t, LANES), F32)],
        scratch_shapes=[pltpu.VMEM((1, LANES), F32)],
        compiler_params=_cparams("arbitrary"),
        name="odd_inproj",
    )(x2, w_bf, wf_bf, bfg)


def _fox_attn_kernel(q_ref, k_ref, v_ref, cq_ref, ck_ref, o_ref, qm_ref, va_ref, cqc_ref, m_ref, acc_ref, *, tq):
    seq = q_ref.shape[0]
    tk = tq
    hp = pl.program_id(1)
    q = q_ref[...]
    v = v_ref[...]
    lane = lax.broadcasted_iota(I32, (seq, LANES), 1)
    cq_all = cq_ref[...]
    for par in range(2):
        own = (lane // FOX_DH) == par
        qm_ref[par] = jnp.where(own, q, jnp.zeros_like(q))
        va_ref[par] = jnp.where(own, v, jnp.ones_like(v))
        cq = jnp.sum(jnp.where(lane == 2 * hp + par, cq_all, 0.0), axis=-1, keepdims=True)
        cqc_ref[par] = jnp.broadcast_to(cq, (seq, LANES))
    m_ref[...] = jnp.full(m_ref.shape, NEG_BIG, F32)
    acc_ref[...] = jnp.zeros(acc_ref.shape, F32)
    rq = ATT_RQ
    nr = tq // rq
    upper = (lax.broadcasted_iota(I32, (rq, rq), 1) > lax.broadcasted_iota(I32, (rq, rq), 0))
    lane_q = lax.broadcasted_iota(I32, (tq, LANES), 1)

    chains = [(ii, j, r, par) for ii in range(seq // tq) for j in range(ii + 1)
              for r in range(nr) for par in range(2)]

    def n_keys(ii, j, r):
        return (r + 1) * rq if j == ii else tk

    def scores(chain):
        ii, j, r, par = chain
        k = k_ref[j * tk:j * tk + n_keys(ii, j, r), :]
        return _dot_nt(qm_ref[par, ii * tq + r * rq:ii * tq + (r + 1) * rq, :], k)

    def finish(chain, s):
        ii, j, r, par = chain
        rows = slice(ii * tq + r * rq, ii * tq + (r + 1) * rq)
        nk = n_keys(ii, j, r)
        ck = ck_ref[par:par + 1, j * tk:j * tk + nk]
        cq = cqc_ref[par, rows, :]
        sc = []
        for kc in range(nk // rq):
            t = s[:, kc * rq:(kc + 1) * rq] - ck[:, kc * rq:(kc + 1) * rq]
            if j == ii and kc == r:
                t = jnp.where(upper, NEG_BIG, t)
            sc.append(t)
        mx = sc[0]
        for t in sc[1:]:
            mx = jnp.maximum(mx, t)
        m_old = m_ref[par, rows, :]
        m_new = jnp.maximum(m_old, jnp.max(mx, axis=-1, keepdims=True) + cq)
        alpha = jnp.exp2(m_old - m_new)
        shift = m_new - cq
        p = jnp.concatenate([jnp.exp2(t - shift).astype(BF16) for t in sc], axis=1)
        acc_ref[par, rows, :] = alpha * acc_ref[par, rows, :] + _dot(p, va_ref[par, j * tk:j * tk + nk, :])
        m_ref[par, rows, :] = m_new
        if j == ii and r == nr - 1 and par == 1:
            blk = slice(ii * tq, (ii + 1) * tq)
            acc0 = acc_ref[0, blk, :]
            acc1 = acc_ref[1, blk, :]
            out0 = acc0 / acc0[:, FOX_DH:FOX_DH + 1]
            out1 = acc1 / acc1[:, 0:1]
            o_ref[blk, :] = jnp.where(lane_q < FOX_DH, out0, out1).astype(BF16)

    _run_chains(chains, FOX_LOOKAHEAD, scores, finish)


def _fox_attention(q, k, v, cum, cum_t):
    b, s, d = q.shape
    tq = min(FOX_TQ, s)
    npair = d // LANES
    seq_blk = lambda bi, h: (bi, 0, h)
    return pl.pallas_call(
        functools.partial(_fox_attn_kernel, tq=tq),
        grid=(b, npair),
        in_specs=[
            pl.BlockSpec((None, s, LANES), seq_blk),
            pl.BlockSpec((None, s, LANES), seq_blk),
            pl.BlockSpec((None, s, LANES), seq_blk),
            pl.BlockSpec((None, s, LANES), lambda bi, h: (bi, 0, 0)),
            pl.BlockSpec((None, None, 2, s), lambda bi, h: (bi, h, 0, 0)),
        ],
        out_specs=pl.BlockSpec((None, s, LANES), seq_blk),
        out_shape=jax.ShapeDtypeStruct((b, s, d), BF16),
        scratch_shapes=[pltpu.VMEM((2, s, LANES), BF16), pltpu.VMEM((2, s, LANES), BF16),
                        pltpu.VMEM((2, s, LANES), F32), pltpu.VMEM((2, s, LANES), F32),
                        pltpu.VMEM((2, s, LANES), F32)],
        compiler_params=_cparams("parallel", "parallel"),
        name="fox_attention",
    )(q, k, v, cum, cum_t)


def _outproj_router_kernel(*refs, n_y):
    y_refs = refs[:n_y]
    w_refs = refs[n_y:2 * n_y]
    x_ref, g_ref, b_ref, rwt_ref, tri_ref, h_ref, route_ref, col_ref, cnt_ref, xs_ref, hbf_ref = refs[2 * n_y:]
    rw2 = rwt_ref[...]
    rw_hi = rw2.astype(BF16)
    rw_lo = (rw2 - rw_hi.astype(F32)).astype(BF16)
    lane_w = lax.broadcasted_iota(I32, rw2.shape, 1)
    w = jnp.where(lane_w < N_EXPERTS, rw_hi, rw_lo)
    tm = x_ref.shape[0]
    rc = ROUTER_ROWS
    logit_chunks = []

    def project(c):
        rows = slice(c * rc, (c + 1) * rc)
        mix = _dot(y_refs[0][rows, :], w_refs[0][...])
        for yr, wr in zip(y_refs[1:], w_refs[1:]):
            mix = mix + _dot(yr[rows, :], wr[...])
        return mix

    def norm_and_logits(c, mix):
        rows = slice(c * rc, (c + 1) * rc)
        h = _layer_norm_rows(DEEPNORM_ALPHA * x_ref[rows, :] + mix, g_ref[...], b_ref[...])
        h_ref[rows, :] = h
        h_hi = h.astype(BF16)
        hbf_ref[rows, :] = h_hi
        h_lo = (h - h_hi.astype(F32)).astype(BF16)
        p_hi = _dot(h_hi, w)
        p_lo = _dot(h_lo, w)
        slab = p_hi + (pltpu.roll(p_hi, LANES - N_EXPERTS, 1) + p_lo)
        logit_chunks.append(slab.T[0:N_EXPERTS])

    _run_chains(list(range(tm // rc)), 1, project, norm_and_logits)

    logits = jnp.concatenate(logit_chunks, axis=1)
    row = lax.broadcasted_iota(I32, (N_EXPERTS, tm), 0)
    mx = jnp.max(logits, axis=0, keepdims=True)
    ex = jnp.exp(logits - mx)
    probs = ex / jnp.sum(ex, axis=0, keepdims=True)
    grp = row // EXPERTS_PER_GROUP

    def top2(vals):
        v1 = jnp.max(vals, axis=0, keepdims=True)
        i1 = jnp.min(jnp.where(vals == v1, row, N_EXPERTS), axis=0, keepdims=True)
        rest = jnp.where(row == i1, -2.0, vals)
        v2 = jnp.max(rest, axis=0, keepdims=True)
        i2 = jnp.min(jnp.where(rest == v2, row, N_EXPERTS), axis=0, keepdims=True)
        return v1, i1, v2, i2

    best_score = None
    best = None
    for gi in range(N_GROUPS):
        v1, _, v2, _ = top2(jnp.where(grp == gi, probs, -1.0))
        score = v1 + v2
        if gi == 0:
            best_score, best = score, jnp.zeros_like(score, dtype=I32)
        else:
            better = score > best_score
            best = jnp.where(better, gi, best)
            best_score = jnp.where(better, score, best_score)
    v1, i1, v2, i2 = top2(jnp.where(grp == best, probs, -1.0))
    tot = v1 + v2
    g1 = v1 / tot
    g2 = v2 / tot

    onehot = jnp.where((row == i1) | (row == i2), 1.0, 0.0)
    pref = _dot(onehot.astype(BF16), tri_ref[...])
    cnt = jnp.broadcast_to(jnp.sum(onehot, axis=1, keepdims=True), cnt_ref.shape)
    cnt_ref[...] = cnt
    grp_rows = jnp.floor((cnt + (SORT_ALIGN - 1)) * (1.0 / SORT_ALIGN)) * SORT_ALIGN
    row_c = lax.broadcasted_iota(I32, cnt.shape, 0)
    start = grp_rows
    step = 1
    while step < N_EXPERTS:
        start = start + jnp.where(row_c >= step, pltpu.roll(start, step, 0), 0.0)
        step *= 2
    start = (start - grp_rows)[:, 0:1]
    s1 = jnp.sum(jnp.where(row == i1, pref + start, 0.0), axis=0, keepdims=True)
    s2 = jnp.sum(jnp.where(row == i2, pref + start, 0.0), axis=0, keepdims=True)

    row8 = lax.broadcasted_iota(I32, (ROUTE_ROWS, tm), 0)
    fields = (i1.astype(F32), i2.astype(F32), g1, g2, s1, s2)
    route = jnp.zeros((ROUTE_ROWS, tm), F32)
    for n, f in enumerate(fields):
        route = jnp.where(row8 == n, f, route)
    route_ref[...] = route
    col_ref[...] = jnp.concatenate([route, jnp.zeros((LANES - ROUTE_ROWS, tm), F32)], axis=0).T

    srow = lax.broadcasted_iota(I32, (xs_ref.shape[0], tm), 0).astype(F32)
    perm = jnp.where((srow == s1) | (srow == s2), 1.0, 0.0).astype(BF16)
    xs_ref[...] = _dot(perm, hbf_ref[...]).astype(BF16)


def _outproj_router(ys, ws, x2, ln_g, ln_b, rwt, tri):
    t, d = x2.shape
    tm = tri.shape[0]
    n_local = _local_rows(tm)
    row = lambda i: (i, 0)
    full = lambda i: (0, 0)
    n_y = len(ys)
    kern = functools.partial(_outproj_router_kernel, n_y=n_y)
    return pl.pallas_call(
        kern,
        grid=(t // tm,),
        in_specs=([pl.BlockSpec((tm, y.shape[1]), row) for y in ys]
                  + [pl.BlockSpec(w.shape, full) for w in ws]
                  + [pl.BlockSpec((tm, d), row), pl.BlockSpec((1, d), full), pl.BlockSpec((1, d), full),
                     pl.BlockSpec(rwt.shape, full), pl.BlockSpec(tri.shape, full)]),
        out_specs=[pl.BlockSpec((tm, d), row), pl.BlockSpec((ROUTE_ROWS, tm), lambda i: (0, i)),
                   pl.BlockSpec((tm, LANES), row), pl.BlockSpec((N_EXPERTS, LANES), row),
                   pl.BlockSpec((n_local, d), row)],
        out_shape=[jax.ShapeDtypeStruct((t, d), F32), jax.ShapeDtypeStruct((ROUTE_ROWS, t), F32),
                   jax.ShapeDtypeStruct((t, LANES), F32), jax.ShapeDtypeStruct((t // tm * N_EXPERTS, LANES), F32),
                   jax.ShapeDtypeStruct((t // tm * n_local, d), BF16)],
        scratch_shapes=[pltpu.VMEM((tm, d), BF16)],
        compiler_params=_cparams("parallel"),
        name="outproj_router",
    )(*ys, *ws, x2, ln_g, ln_b, rwt, tri)


def _group_copies(n_rows, local_start, global_start, local_ref, global_ref, sem, to_global):
    @pl.when(n_rows > 0)
    def _():
        n = pl.multiple_of(n_rows, SORT_ALIGN)
        loc = local_ref.at[pl.ds(pl.multiple_of(local_start, SORT_ALIGN), n)]
        glo = global_ref.at[pl.ds(pl.multiple_of(global_start, SORT_ALIGN), n)]
        (pltpu.make_async_copy(loc, glo, sem) if to_global else pltpu.make_async_copy(glo, loc, sem)).start()


def _wait_group_copies(total_rows, local_ref, global_ref, sem, to_global):
    loc = local_ref.at[pl.ds(0, total_rows)]
    glo = global_ref.at[pl.ds(0, total_rows)]
    (pltpu.make_async_copy(loc, glo, sem) if to_global else pltpu.make_async_copy(glo, loc, sem)).wait()


def _tile_groups(tile, loff_ref, rows_ref, gpos_ref, local_ref, global_ref, sem, to_global):
    for e in range(N_EXPERTS):
        n = tile * N_EXPERTS + e
        _group_copies(rows_ref[n], loff_ref[n], gpos_ref[n], local_ref, global_ref, sem, to_global)


def _dispatch_kernel(loff_ref, rows_ref, gpos_ref, tot_ref, tail_ref, nt_ref, h_ref, route_ref, xs_hbm,
                     sbuf, zbuf, sem):
    i = pl.program_id(0)
    tm = h_ref.shape[0]
    n_local = sbuf.shape[1]

    @pl.when(i == 0)
    def _():
        zbuf[...] = jnp.zeros_like(zbuf)

        def zero_tile(start):
            return pltpu.make_async_copy(zbuf, xs_hbm.at[pl.ds(pl.multiple_of(start, MOE_TR), MOE_TR)], sem.at[0])

        for e in range(N_EXPERTS):
            @pl.when(tail_ref[e] >= 0)
            def _(e=e):
                zero_tile(tail_ref[e]).start()

        def start_unused(r, c):
            zero_tile(r * MOE_TR).start()
            return c

        def wait_unused(r, c):
            zero_tile(r * MOE_TR).wait()
            return c

        n_all = xs_hbm.shape[0] // MOE_TR
        lax.fori_loop(nt_ref[0], n_all, start_unused, 0)
        for e in range(N_EXPERTS):
            @pl.when(tail_ref[e] >= 0)
            def _(e=e):
                zero_tile(tail_ref[e]).wait()
        lax.fori_loop(nt_ref[0], n_all, wait_unused, 0)

    srow = lax.broadcasted_iota(I32, (n_local, tm), 0).astype(F32)
    slot1 = route_ref[ROUTE_SLOT:ROUTE_SLOT + 1, :]
    slot2 = route_ref[ROUTE_SLOT + 1:ROUTE_SLOT + 2, :]
    perm = jnp.where((srow == slot1) | (srow == slot2), 1.0, 0.0).astype(BF16)
    slot = i % 2
    sbuf[slot] = _dot(perm, h_ref[...].astype(BF16)).astype(BF16)
    _tile_groups(i, loff_ref, rows_ref, gpos_ref, sbuf.at[slot], xs_hbm, sem.at[slot], True)

    @pl.when(i > 0)
    def _():
        _wait_group_copies(pl.multiple_of(tot_ref[i - 1], SORT_ALIGN), sbuf.at[1 - slot], xs_hbm,
                           sem.at[1 - slot], True)

    @pl.when(i == pl.num_programs(0) - 1)
    def _():
        _wait_group_copies(pl.multiple_of(tot_ref[i], SORT_ALIGN), sbuf.at[slot], xs_hbm, sem.at[slot], True)


def _dispatch(meta, h, route, n_slots):
    t, d = h.shape
    tm = MOE_TM
    n_local = _local_rows(tm)
    idx = lambda i, *_: (i, 0)
    grid_spec = pltpu.PrefetchScalarGridSpec(
        num_scalar_prefetch=6,
        grid=(t // tm,),
        in_specs=[pl.BlockSpec((tm, d), idx), pl.BlockSpec((ROUTE_ROWS, tm), lambda i, *_: (0, i))],
        out_specs=pl.BlockSpec(memory_space=pl.ANY),
        scratch_shapes=[pltpu.VMEM((2, n_local, d), BF16), pltpu.VMEM((MOE_TR, d), BF16),
                        pltpu.SemaphoreType.DMA((2,))],
    )
    return pl.pallas_call(
        _dispatch_kernel,
        grid_spec=grid_spec,
        out_shape=jax.ShapeDtypeStruct((n_slots, d), BF16),
        compiler_params=_cparams("arbitrary"),
        name="moe_dispatch",
    )(meta["loff"], meta["rows"], meta["gpos"], meta["tot"], meta["tail"], meta["n_tiles"], h, route)


def _expert_kernel(te_ref, nx_ref, nt_ref, g0_ref, g1_ref, valid_ref, loff_ref, rows_ref, gpos_ref,
                   xs_hbm, wg_hbm, wu_hbm, wd_hbm, o_ref,
                   xbuf, semx, wgs, wus, wds, wgb, wub, wdb, sem, nsw_ref, *, layer, n_local):
    r = pl.program_id(0)
    nt = nt_ref[0]
    cur = te_ref[r]
    tr = xbuf.shape[1]
    xslot = r % 2

    def start_pieces(tile, dst_slot):
        e = te_ref[tile]
        base = tile * tr

        def piece(tau, c):
            n = tau * N_EXPERTS + e
            lo = jnp.maximum(gpos_ref[n], base)
            hi = jnp.minimum(gpos_ref[n] + rows_ref[n], base + tr)

            @pl.when(hi > lo)
            def _():
                size = pl.multiple_of(hi - lo, SORT_ALIGN)
                src = pl.multiple_of(tau * n_local + loff_ref[n] + (lo - gpos_ref[n]), SORT_ALIGN)
                dst = pl.multiple_of(lo - base, SORT_ALIGN)
                pltpu.make_async_copy(xs_hbm.at[pl.ds(src, size)], xbuf.at[dst_slot, pl.ds(dst, size)],
                                      semx.at[dst_slot]).start()

            return c

        lax.fori_loop(g0_ref[tile], g1_ref[tile] + 1, piece, 0)

    @pl.when(r == 0)
    def _():
        xbuf[...] = jnp.zeros_like(xbuf)
        start_pieces(0, 0)

    @pl.when(r + 1 < nt)
    def _():
        start_pieces(r + 1, 1 - xslot)

    def weight_copies(e, slot):
        return [pltpu.make_async_copy(src.at[layer, e], dst.at[slot], sem.at[slot])
                for src, dst in ((wg_hbm, wgs), (wu_hbm, wus), (wd_hbm, wds))]

    @pl.when(r == 0)
    def _():
        nsw_ref[0] = 0
        for cp in weight_copies(cur, 0):
            cp.start()

    @pl.when((r == 0) | (cur != te_ref[jnp.maximum(r - 1, 0)]))
    def _():
        slot = nsw_ref[0] % 2
        nsw_ref[0] = nsw_ref[0] + 1
        for cp in weight_copies(cur, slot):
            cp.wait()

        @pl.when(nx_ref[r] != cur)
        def _():
            for cp in weight_copies(nx_ref[r], 1 - slot):
                cp.start()

        wgb[...] = wgs[slot].astype(BF16)
        wub[...] = wus[slot].astype(BF16)
        wdb[...] = wds[slot].astype(BF16)

    @pl.when(r < nt)
    def _():
        n_valid = pl.multiple_of(valid_ref[r], SORT_ALIGN)
        pltpu.make_async_copy(xs_hbm.at[pl.ds(0, n_valid)], xbuf.at[xslot, pl.ds(0, n_valid)],
                              semx.at[xslot]).wait()
        x = xbuf[xslot]
        a = _silu(_dot(x, wgb[...])) * _dot(x, wub[...])
        o_ref[...] = _dot(a.astype(BF16), wdb[...]).astype(BF16)

    @pl.when(r >= nt)
    def _():
        o_ref[...] = jnp.zeros_like(o_ref)


def _expert_mlps(meta, n_slots, xs_local, wg, wu, wd, layer):
    d = xs_local.shape[1]
    tr = MOE_TR
    dff = wg.shape[3]
    grid_spec = pltpu.PrefetchScalarGridSpec(
        num_scalar_prefetch=9,
        grid=(n_slots // tr,),
        in_specs=[pl.BlockSpec(memory_space=pl.ANY)] * 4,
        out_specs=pl.BlockSpec((tr, d), lambda r, *_: (r, 0)),
        scratch_shapes=[pltpu.VMEM((2, tr, d), BF16), pltpu.SemaphoreType.DMA((2,)),
                        pltpu.VMEM((2, d, dff), F32), pltpu.VMEM((2, d, dff), F32), pltpu.VMEM((2, dff, d), F32),
                        pltpu.VMEM((d, dff), BF16), pltpu.VMEM((d, dff), BF16), pltpu.VMEM((dff, d), BF16),
                        pltpu.SemaphoreType.DMA((2,)), pltpu.SMEM((1,), I32)],
    )
    return pl.pallas_call(
        functools.partial(_expert_kernel, layer=layer, n_local=_local_rows(MOE_TM)),
        grid_spec=grid_spec,
        out_shape=jax.ShapeDtypeStruct((n_slots, d), BF16),
        compiler_params=_cparams("arbitrary"),
        name="expert_mlps",
    )(meta["tile_expert"], meta["next_expert"], meta["n_tiles"], meta["first_group"], meta["last_group"],
      meta["valid"], meta["loff"], meta["rows"], meta["gpos"], xs_local, wg, wu, wd)


def _combine_kernel(loff_ref, rows_ref, gpos_ref, tot_ref, ys_hbm, h_ref, col_ref, p_ref, g_ref, b_ref,
                    pg_ref, pp_ref, o_ref, ybuf, sem):
    i = pl.program_id(0)
    n = pl.num_programs(0)
    tm = h_ref.shape[0]
    n_local = ybuf.shape[1]
    slot = i % 2

    @pl.when(i == 0)
    def _():
        ybuf[...] = jnp.zeros_like(ybuf)
        _tile_groups(0, loff_ref, rows_ref, gpos_ref, ybuf.at[0], ys_hbm, sem.at[0], False)

    @pl.when(i + 1 < n)
    def _():
        _tile_groups(i + 1, loff_ref, rows_ref, gpos_ref, ybuf.at[1 - slot], ys_hbm, sem.at[1 - slot], False)

    _wait_group_copies(pl.multiple_of(tot_ref[i], SORT_ALIGN), ybuf.at[slot], ys_hbm, sem.at[slot], False)
    rc = COMBINE_ROWS
    scol = lax.broadcasted_iota(I32, (rc, n_local), 1).astype(F32)

    def gather_rows(c):
        rows = slice(c * rc, (c + 1) * rc)
        cols = col_ref[rows, :]
        pick = jnp.zeros((rc, n_local), F32)
        for k in range(2):
            pick = jnp.where(scol == cols[:, ROUTE_SLOT + k:ROUTE_SLOT + k + 1],
                             cols[:, ROUTE_GATE + k:ROUTE_GATE + k + 1], pick)
        ffn = _dot(pick.astype(BF16), ybuf[slot])
        return ffn, _dot(p_ref[rows, :].astype(BF16), pp_ref[...])

    def finish(c, gathered):
        rows = slice(c * rc, (c + 1) * rc)
        ffn, pe = gathered
        h2 = _layer_norm_rows(DEEPNORM_ALPHA * h_ref[rows, :] + ffn, g_ref[...], b_ref[...])
        gate = _sigmoid(_dot(h2.astype(BF16), pg_ref[...]))
        o_ref[rows, :] = h2 + gate * pe

    _run_chains(list(range(tm // rc)), 1, gather_rows, finish)


def _combine(meta, ys, h, cols, p3, layer, ln_g, ln_b, pg_bf, pp_bf):
    t, d = h.shape
    tm = MOE_TM
    pdim = p3.shape[2]
    row = lambda i, *_: (i, 0)
    full = lambda i, *_: (0, 0)
    grid_spec = pltpu.PrefetchScalarGridSpec(
        num_scalar_prefetch=4,
        grid=(t // tm,),
        in_specs=[
            pl.BlockSpec(memory_space=pl.ANY),
            pl.BlockSpec((tm, d), row),
            pl.BlockSpec((tm, LANES), row),
            pl.BlockSpec((None, tm, pdim), lambda i, *_: (layer, i, 0)),
            pl.BlockSpec((1, d), full),
            pl.BlockSpec((1, d), full),
            pl.BlockSpec(pg_bf.shape, full),
            pl.BlockSpec(pp_bf.shape, full),
        ],
        out_specs=pl.BlockSpec((tm, d), row),
        scratch_shapes=[pltpu.VMEM((2, _local_rows(tm), d), BF16), pltpu.SemaphoreType.DMA((2,))],
    )
    return pl.pallas_call(
        _combine_kernel,
        grid_spec=grid_spec,
        out_shape=jax.ShapeDtypeStruct((t, d), F32),
        compiler_params=_cparams("arbitrary"),
        name="moe_combine",
    )(meta["loff"], meta["rows"], meta["gpos"], meta["tot"], ys, h, cols, p3, ln_g, ln_b, pg_bf, pp_bf)


def _rotary_tables(seq):
    half = RET_DK // 2
    inv = (np.float32(ROPE_BASE) ** (-np.arange(half, dtype=np.float32) / np.float32(half))).astype(np.float32)
    ang = (np.arange(seq, dtype=np.float32)[:, None] * inv[None, :]).astype(np.float32)
    cos = np.cos(ang.astype(np.float64))
    sin = np.sin(ang.astype(np.float64))
    cos_h = np.concatenate([cos, cos], axis=1)
    sin_h = np.concatenate([-sin, sin], axis=1)
    return (jnp.asarray(np.tile(cos_h, (1, RET_HEADS)), F32), jnp.asarray(np.tile(sin_h, (1, RET_HEADS)), F32))


def _retention_tables():
    c = RET_CHUNK
    h = np.arange(RET_HEADS, dtype=np.float64)
    log_g = np.log1p(-np.exp2(-5.0 - h))
    j = np.arange(c, dtype=np.float64)
    rel = j[:, None] - j[None, :]
    din = np.where(rel >= 0, np.exp(np.maximum(rel, 0.0)[None] * log_g[:, None, None]), 0.0)
    qd = np.exp((j + 1.0)[None] * log_g[:, None])
    kd = np.exp((c - 1.0 - j)[None] * log_g[:, None])
    cd = np.exp(c * log_g)
    qd = np.broadcast_to(qd[:, :, None], (RET_HEADS, c, LANES))
    kd = np.broadcast_to(kd[:, :, None], (RET_HEADS, c, LANES))
    cd = np.broadcast_to(cd[:, None, None], (RET_HEADS, 1, LANES))
    return tuple(jnp.asarray(a, F32) for a in (din, qd, kd, cd))


def _t5_bucket_np(dist):
    max_exact = REL_BUCKETS // 2
    d = np.maximum(dist, 1).astype(np.float32)
    large = max_exact + (np.log(d / np.float32(max_exact)) / np.float32(math.log(REL_MAX_DIST / max_exact))
                         * np.float32(REL_BUCKETS - max_exact)).astype(np.int32)
    large = np.minimum(large, REL_BUCKETS - 1)
    return np.where(dist < max_exact, dist, large)


def _diff_bias_tables(rel_bias, seq):
    c = ATT_RQ
    r = np.arange(c)
    dist0 = r[:, None] - r[None, :]
    far = REL_BUCKETS - 1
    assert np.all(_t5_bucket_np(np.arange(c + 1, max(seq, 2 * c))) == far)
    bidx = np.stack([_t5_bucket_np(np.maximum(dist0, 0)), _t5_bucket_np(dist0 + c)])
    rb = rel_bias.astype(F32).T
    shifted = (rb - rb[:, far:far + 1]) * LOG2E
    bidx = jnp.asarray(bidx, I32)[None]
    tab = jnp.zeros((rb.shape[0], 2, c, c), F32)
    for bucket in range(REL_BUCKETS - 1):
        tab = jnp.where(bidx == bucket, shifted[:, bucket][:, None, None, None], tab)
    causal = jnp.asarray(np.stack([dist0 >= 0, np.ones_like(dist0, bool)]))[None]
    return jnp.where(causal, tab, NEG_BIG)


def _local_rows(tm):
    need = 2 * tm + N_EXPERTS * (SORT_ALIGN - 1)
    return -(-need // LANES) * LANES


def _round_up(a, m):
    return ((a + m - 1) // m) * m


def _route_meta(cnt, t):
    tm, tr = MOE_TM, MOE_TR
    nt = t // tm
    counts = cnt.reshape(nt, N_EXPERTS, LANES)[:, :, 0].astype(I32)
    rows = _round_up(counts, SORT_ALIGN)
    loff = jnp.cumsum(rows, axis=1) - rows
    seg = jnp.sum(rows, axis=0)
    seg_pad = _round_up(seg, tr)
    ends = jnp.cumsum(seg_pad)
    offs = ends - seg_pad
    gpos = offs[None, :] + jnp.cumsum(rows, axis=0) - rows
    n_slots = 2 * t + nt * N_EXPERTS * (SORT_ALIGN - 1)
    n_slots = _round_up(n_slots, tr) + N_EXPERTS * tr
    n_tiles = (ends[-1] // tr).astype(I32)
    tile_start = jnp.arange(n_slots // tr, dtype=I32) * tr
    tile_expert = jnp.sum((tile_start[:, None] >= ends[None, :]).astype(I32), axis=1)
    last = jnp.sum((((n_tiles - 1) * tr) >= ends).astype(I32))
    tile_expert = jnp.minimum(tile_expert, last).astype(I32)
    eid = jnp.arange(N_EXPERTS, dtype=I32)
    later = (eid[None, :] > eid[:, None]) & (seg_pad > 0)[None, :]
    nxt = jnp.min(jnp.where(later, eid[None, :], N_EXPERTS), axis=1)
    nxt = jnp.where(nxt == N_EXPERTS, eid, nxt)
    next_expert = jnp.sum(jnp.where(tile_expert[:, None] == eid[None, :], nxt[None, :], 0), axis=1).astype(I32)
    own = tile_expert[:, None] == eid[None, :]
    pick = lambda tab: jnp.sum(jnp.where(own[:, None, :], tab[None, :, :], 0), axis=-1)
    g_start, g_end = pick(gpos), pick(gpos + rows)
    used = (tile_start < n_tiles * tr)[:, None]
    first_group = jnp.sum((g_end <= tile_start[:, None]).astype(I32), axis=1)
    last_group = jnp.sum(((g_start < tile_start[:, None] + tr) & used).astype(I32), axis=1) - 1
    seg_end = jnp.sum(jnp.where(own, (offs + seg)[None, :], 0), axis=1)
    valid = jnp.clip(seg_end - tile_start, 0, tr)
    meta = {
        "loff": loff.reshape(-1).astype(I32), "rows": rows.reshape(-1).astype(I32),
        "gpos": gpos.reshape(-1).astype(I32), "tot": jnp.sum(rows, axis=1).astype(I32),
        "tile_expert": tile_expert, "next_expert": next_expert, "n_tiles": n_tiles.reshape(1),
        "first_group": first_group.astype(I32), "last_group": last_group.astype(I32), "valid": valid.astype(I32),
    }
    return meta, n_slots


def kernel(x, p, rel_bias, router_w, even_w_in, even_w_out, even_lambda, even_diff_norm, even_ret_norm,
           odd_w_in, odd_b_forget, odd_w_out, ln_mix_g, ln_mix_b, ln_ffn_g, ln_ffn_b,
           moe_w_gate, moe_w_up, moe_w_down, ple_proj, ple_gate):
    b, s, d = x.shape
    t = b * s
    assert d == 1024 and p.shape[0] == DEPTH and even_w_in.shape[2] == 3072
    assert odd_w_in.shape[2] == 3 * d + FOX_HEADS and moe_w_gate.shape[1] == N_EXPERTS
    assert s % RET_CHUNK == 0 and s % min(ATT_TQ, s) == 0 and s % min(FOX_TQ, s) == 0
    assert t % min(PROJ_TM, s) == 0 and t % MOE_TM == 0

    cos_t, sin_t = _rotary_tables(s)
    ret_tabs = _retention_tables()
    bias_tab = _diff_bias_tables(rel_bias, s)
    rw32 = router_w.astype(F32)
    rwt = jnp.zeros((d, LANES), F32).at[:, :N_EXPERTS].set(rw32).at[:, N_EXPERTS:2 * N_EXPERTS].set(rw32)
    tok = np.arange(MOE_TM)
    tri = jnp.asarray(tok[:, None] < tok[None, :], BF16)

    x2 = x.reshape(t, d)
    for i in range(DEPTH):
        j = i // 2
        if i % 2 == 0:
            lam_init = 0.8 - 0.6 * math.exp(-0.3 * i)
            qa, ka, va, qb, kb, vb, gb = _even_inproj(x2, even_w_in[j].astype(BF16), cos_t, sin_t, s)
            sh = lambda a: a.reshape(b, s, a.shape[1])
            ya = _diff_attention(sh(qa), sh(ka), sh(va), bias_tab, even_lambda[j].astype(F32),
                                 even_diff_norm[j].reshape(1, -1).astype(F32), lam_init)
            yb = _retention(sh(qb), sh(kb), sh(vb), sh(gb), ret_tabs, even_ret_norm[j].reshape(1, -1).astype(F32))
            w_out = even_w_out[j].astype(BF16)
            n_a = ya.shape[2]
            ys = [ya.reshape(t, -1), yb.reshape(t, -1)]
            ws = [w_out[:n_a], w_out[n_a:]]
        else:
            w_in = odd_w_in[j]
            wf = jnp.zeros((d, LANES), BF16).at[:, :FOX_HEADS].set(w_in[:, 3 * d:].astype(BF16))
            bfg = jnp.zeros((1, LANES), F32).at[0, :FOX_HEADS].set(odd_b_forget[j].astype(F32))
            q, k, v, cum = _odd_inproj(x2, w_in[:, :3 * d].astype(BF16), wf, bfg, s)
            cum3 = cum.reshape(b, s, LANES)
            cum_t = jnp.transpose(cum3[:, :, :FOX_HEADS], (0, 2, 1)).reshape(b, FOX_HEADS // 2, 2, s)
            y = _fox_attention(q.reshape(b, s, d), k.reshape(b, s, d), v.reshape(b, s, d), cum3, cum_t)
            ys = [y.reshape(t, d)]
            ws = [odd_w_out[j].astype(BF16)]
        h, route, cols, cnt, xs_local = _outproj_router(ys, ws, x2, ln_mix_g[i].reshape(1, d),
                                                        ln_mix_b[i].reshape(1, d), rwt, tri)
        meta, n_slots = _route_meta(cnt, t)
        rows = _expert_mlps(meta, n_slots, xs_local, moe_w_gate, moe_w_up, moe_w_down, i)
        x2 = _combine(meta, rows, h, cols, p.reshape(DEPTH, t, -1), i, ln_ffn_g[i].reshape(1, d),
                      ln_ffn_b[i].reshape(1, d), ple_gate[i].astype(BF16), ple_proj[i].astype(BF16))
    return x2.reshape(b, s, d)
```

```python
import functools
import math

import numpy as np
import jax
import jax.numpy as jnp
from jax import lax
from jax.experimental import pallas as pl
from jax.experimental.pallas import tpu as pltpu

F32 = jnp.float32
BF16 = jnp.bfloat16
I32 = jnp.int32

DIFF_HEADS = 4
DIFF_DK = 64
RET_HEADS = 4
RET_DK = 64
RET_DV = 128
RET_CHUNK = 128
FOX_HEADS = 16
FOX_DH = 64
REL_BUCKETS = 32
REL_MAX_DIST = 128
N_GROUPS = 4
EXPERTS_PER_GROUP = 4
N_EXPERTS = 16
DEPTH = 2
DEEPNORM_ALPHA = (2 * DEPTH) ** 0.25
LN_EPS = 1e-5
ROPE_BASE = 10000.0
NEG_BIG = -1e30
LOG2E = math.log2(math.e)

VMEM_LIMIT_BYTES = 48 * 1024 * 1024
LANES = 128

PROJ_TM = 1024
ATT_TQ = 512
FOX_TQ = 512
ATT_RQ = LANES
DIFF_LOOKAHEAD = 3
FOX_LOOKAHEAD = 4
MOE_TR = 512
MOE_TM = 512
SORT_ALIGN = 16
ROUTE_ROWS = 8
ROUTE_GATE = 2
ROUTE_SLOT = 4
COMBINE_ROWS = 256
EXPERT_ROWS = 256
ROUTER_ROWS = 256


def _cparams(*sem):
    return pltpu.CompilerParams(dimension_semantics=sem, vmem_limit_bytes=VMEM_LIMIT_BYTES)


def _dot(a, b):
    return jnp.dot(a, b, preferred_element_type=F32)


def _dot_nt(a, b):
    return lax.dot_general(a, b, (((1,), (1,)), ((), ())), preferred_element_type=F32)


def _layer_norm_rows(z, g, b):
    mu = jnp.mean(z, axis=-1, keepdims=True)
    zc = z - mu
    var = jnp.mean(zc * zc, axis=-1, keepdims=True)
    return zc * lax.rsqrt(var + LN_EPS) * g + b


def _silu(x):
    return x * (1.0 / (1.0 + jnp.exp(-x)))


def _sigmoid(x):
    return 1.0 / (1.0 + jnp.exp(-x))


def _even_inproj_kernel(x_ref, w_ref, cos_ref, sin_ref,
                        qa_ref, ka_ref, va_ref, qb_ref, kb_ref, vb_ref, gb_ref):
    x = x_ref[...].astype(BF16)

    def mm(c0, c1):
        return _dot(x, w_ref[:, c0:c1].astype(BF16))

    qa_ref[...] = (mm(0, 512) * (DIFF_DK ** -0.5 * LOG2E)).astype(BF16)
    ka_ref[...] = mm(512, 1024).astype(BF16)
    va_ref[...] = mm(1024, 1536).astype(BF16)
    qk = mm(1536, 2048)
    cos = cos_ref[...]
    sin = sin_ref[...]
    lane = lax.broadcasted_iota(I32, cos.shape, 1)
    first_half = (lane % RET_DK) < (RET_DK // 2)

    def rot(t):
        sw = jnp.where(first_half, pltpu.roll(t, t.shape[1] - RET_DK // 2, 1),
                       pltpu.roll(t, RET_DK // 2, 1))
        return t * cos + sw * sin

    qb_ref[...] = rot(qk[:, :256]).astype(BF16)
    kb_ref[...] = (rot(qk[:, 256:]) * (RET_DK ** -0.5)).astype(BF16)
    vb_ref[...] = mm(2048, 2560).astype(BF16)
    gb_ref[...] = mm(2560, 3072).astype(BF16)


def _even_inproj(x2, w_bf, cos_t, sin_t, seq):
    t, d = x2.shape
    tm = min(PROJ_TM, seq)
    nblk_s = seq // tm
    widths = (512, 512, 512, 256, 256, 512, 512)
    row = lambda i: (i, 0)
    return pl.pallas_call(
        _even_inproj_kernel,
        grid=(t // tm,),
        in_specs=[
            pl.BlockSpec((tm, d), row),
            pl.BlockSpec(w_bf.shape, lambda i: (0, 0), pipeline_mode=pl.Buffered(1)),
            pl.BlockSpec((tm, 256), lambda i: (i % nblk_s, 0)),
            pl.BlockSpec((tm, 256), lambda i: (i % nblk_s, 0)),
        ],
        out_specs=[pl.BlockSpec((tm, w), row) for w in widths],
        out_shape=[jax.ShapeDtypeStruct((t, w), BF16) for w in widths],
        compiler_params=_cparams("parallel"),
        name="even_inproj",
    )(x2, w_bf, cos_t, sin_t)


def _run_chains(chains, lookahead, scores, finish):
    pending = [scores(c) for c in chains[:lookahead]]
    for n, chain in enumerate(chains):
        if n + lookahead < len(chains):
            pending.append(scores(chains[n + lookahead]))
        finish(chain, pending.pop(0))


def _diff_attn_kernel(lam_ref, q_ref, k_ref, v_ref, bias_ref, g_ref, o_ref, qm_ref, m_ref, l_ref, acc_ref,
                      *, lam_init, tq):
    seq = q_ref.shape[0]
    tk = tq
    rq = ATT_RQ
    nr = tq // rq
    lane = lax.broadcasted_iota(I32, (rq, LANES), 1)
    for n in range(seq // rq):
        q = q_ref[n * rq:(n + 1) * rq, :]
        zero = jnp.zeros_like(q)
        qm_ref[n, 0:rq, :] = jnp.where(lane < DIFF_DK, q, zero)
        qm_ref[n, rq:2 * rq, :] = jnp.where(lane >= DIFF_DK, q, zero)
    m_ref[...] = jnp.full(m_ref.shape, NEG_BIG, F32)
    l_ref[...] = jnp.zeros(l_ref.shape, F32)
    acc_ref[...] = jnp.zeros(acc_ref.shape, F32)
    lp = lam_ref[...]
    lam = (jnp.exp(jnp.sum(lp[0:1, :] * lp[1:2, :], axis=-1, keepdims=True))
           - jnp.exp(jnp.sum(lp[2:3, :] * lp[3:4, :], axis=-1, keepdims=True)) + lam_init)
    bias2 = [jnp.concatenate([bias_ref[n], bias_ref[n]], axis=0) for n in range(2)]

    chains = [(ii, j, r) for ii in range(seq // tq) for j in range(ii + 1) for r in range(nr)]

    def n_keys(ii, j, r):
        return (r + 1) * rq if j == ii else tk

    def scores(chain):
        ii, j, r = chain
        k = k_ref[j * tk:j * tk + n_keys(ii, j, r), :]
        return _dot_nt(qm_ref[ii * nr + r], k)

    def finish(chain, s):
        ii, j, r = chain
        g = ii * nr + r
        nk = n_keys(ii, j, r)
        v = v_ref[j * tk:j * tk + nk, :]
        sc = []
        for kc in range(nk // rq):
            t = s[:, kc * rq:(kc + 1) * rq]
            back = g - (j * nr + kc)
            if back <= 1:
                t = t + bias2[back]
            sc.append(t)
        mx = sc[0]
        for t in sc[1:]:
            mx = jnp.maximum(mx, t)
        m_old = m_ref[g]
        m_new = jnp.maximum(m_old, jnp.max(mx, axis=-1, keepdims=True))
        alpha = jnp.exp2(m_old - m_new)
        ps = [jnp.exp2(t - m_new) for t in sc]
        psum = ps[0]
        for t in ps[1:]:
            psum = psum + t
        l_ref[g] = alpha * l_ref[g] + psum
        p = jnp.concatenate([t.astype(BF16) for t in ps], axis=1)
        acc_ref[g] = alpha * acc_ref[g] + _dot(p, v)
        m_ref[g] = m_new
        if j == ii:
            l_all = jnp.sum(l_ref[g], axis=-1, keepdims=True)
            a = acc_ref[g] / l_all
            o = a[0:rq] - lam * a[rq:2 * rq]
            o = o * lax.rsqrt(jnp.mean(o * o, axis=-1, keepdims=True) + LN_EPS)
            o_ref[g * rq:(g + 1) * rq, :] = (o * g_ref[...] * (1.0 - lam_init)).astype(BF16)

    _run_chains(chains, DIFF_LOOKAHEAD, scores, finish)


def _diff_attention(qa, ka, va, bias_tab, lam_params, diff_g, lam_init):
    b, s, _ = qa.shape
    tq = min(ATT_TQ, s)
    kern = functools.partial(_diff_attn_kernel, lam_init=lam_init, tq=tq)
    seq_blk = lambda bi, h: (bi, 0, h)
    return pl.pallas_call(
        kern,
        grid=(b, DIFF_HEADS),
        in_specs=[
            pl.BlockSpec(lam_params.shape, lambda bi, h: (0, 0)),
            pl.BlockSpec((None, s, LANES), seq_blk),
            pl.BlockSpec((None, s, LANES), seq_blk),
            pl.BlockSpec((None, s, LANES), seq_blk),
            pl.BlockSpec((None, 2, ATT_RQ, ATT_RQ), lambda bi, h: (h, 0, 0, 0)),
            pl.BlockSpec((1, LANES), lambda bi, h: (0, 0)),
        ],
        out_specs=pl.BlockSpec((None, s, LANES), seq_blk),
        out_shape=jax.ShapeDtypeStruct((b, s, DIFF_HEADS * LANES), BF16),
        scratch_shapes=[pltpu.VMEM((s // ATT_RQ, 2 * ATT_RQ, LANES), BF16)]
        + [pltpu.VMEM((s // ATT_RQ, 2 * ATT_RQ, LANES), F32)] * 3,
        compiler_params=_cparams("parallel", "parallel"),
        name="diff_attention",
    )(lam_params, qa, ka, va, bias_tab, diff_g)


def _retention_kernel(q_ref, k_ref, v_ref, gate_ref, din_ref, qd_ref, kd_ref, cd_ref, g_ref, o_ref):
    s = q_ref.shape[0]
    c = RET_CHUNK
    lane = lax.broadcasted_iota(I32, (c, LANES), 1)
    g = g_ref[...]
    states = [jnp.zeros((LANES, RET_DV), F32) for _ in range(2)]
    for n in range(s // c):
        r = slice(n * c, (n + 1) * c)
        q_pair = q_ref[r, :].astype(F32)
        k_pair = k_ref[r, :].astype(F32)
        for par in range(2):
            own = (lane // RET_DK) == par
            cols = slice(par * RET_DV, (par + 1) * RET_DV)
            q = jnp.where(own, q_pair, 0.0)
            k = jnp.where(own, k_pair, 0.0)
            v = v_ref[r, cols]
            scores = _dot_nt(q.astype(BF16), k.astype(BF16)) * din_ref[par]
            inner = _dot(scores.astype(BF16), v)
            cross = _dot((q * qd_ref[par]).astype(BF16), states[par].astype(BF16))
            kv = _dot((k * kd_ref[par]).T.astype(BF16), v)
            states[par] = cd_ref[par] * states[par] + kv
            y = inner + cross
            mu = jnp.mean(y, axis=-1, keepdims=True)
            yc = y - mu
            var = jnp.mean(yc * yc, axis=-1, keepdims=True)
            yn = yc * lax.rsqrt(var + LN_EPS) * g
            gate = gate_ref[r, cols].astype(F32)
            o_ref[r, cols] = (_silu(gate) * yn).astype(BF16)


def _retention(qb, kb, vb, gb, tabs, ret_g):
    b, s, _ = qb.shape
    din, qd, kd, cd = tabs
    pair = lambda bi, hp: (bi, 0, hp)
    tab = lambda bi, hp: (hp, 0, 0)
    return pl.pallas_call(
        _retention_kernel,
        grid=(b, RET_HEADS // 2),
        in_specs=[
            pl.BlockSpec((None, s, LANES), pair),
            pl.BlockSpec((None, s, LANES), pair),
            pl.BlockSpec((None, s, 2 * RET_DV), pair),
            pl.BlockSpec((None, s, 2 * RET_DV), pair),
            pl.BlockSpec((2, RET_CHUNK, RET_CHUNK), tab),
            pl.BlockSpec((2, RET_CHUNK, LANES), tab),
            pl.BlockSpec((2, RET_CHUNK, LANES), tab),
            pl.BlockSpec((2, 1, LANES), tab),
            pl.BlockSpec((1, RET_DV), lambda bi, hp: (0, 0)),
        ],
        out_specs=pl.BlockSpec((None, s, 2 * RET_DV), pair),
        out_shape=jax.ShapeDtypeStruct((b, s, RET_HEADS * RET_DV), BF16),
        compiler_params=_cparams("parallel", "parallel"),
        name="retention",
    )(qb, kb, vb, gb, din, qd, kd, cd, ret_g)


def _odd_inproj_kernel(x_ref, w_ref, wf_ref, bf_ref, q_ref, k_ref, v_ref, cum_ref, carry_ref, *, nblk_s):
    i = pl.program_id(0)
    x = x_ref[...].astype(BF16)
    d = q_ref.shape[1]
    q_ref[...] = (_dot(x, w_ref[:, 0:d].astype(BF16)) * (FOX_DH ** -0.5 * LOG2E)).astype(BF16)
    k_ref[...] = _dot(x, w_ref[:, d:2 * d].astype(BF16)).astype(BF16)
    v_ref[...] = _dot(x, w_ref[:, 2 * d:3 * d].astype(BF16)).astype(BF16)
    z = _dot(x, wf_ref[...]) + bf_ref[...]
    c = jnp.minimum(z, 0.0) - jnp.log1p(jnp.exp(-jnp.abs(z)))
    tm = c.shape[0]
    row = lax.broadcasted_iota(I32, c.shape, 0)
    step = 1
    while step < tm:
        c = c + jnp.where(row >= step, pltpu.roll(c, step, 0), 0.0)
        step *= 2

    @pl.when(i % nblk_s == 0)
    def _():
        carry_ref[...] = jnp.zeros_like(carry_ref)

    c = c + carry_ref[...]
    cum_ref[...] = c * LOG2E
    carry_ref[...] = c[tm - 1:tm, :]


def _odd_inproj(x2, w_in, wf_bf, bfg, seq):
    t, d = x2.shape
    tm = min(PROJ_TM, seq)
    nblk_s = seq // tm
    row = lambda i: (i, 0)
    kern = functools.partial(_odd_inproj_kernel, nblk_s=nblk_s)
    return pl.pallas_call(
        kern,
        grid=(t // tm,),
        in_specs=[
            pl.BlockSpec((tm, d), row),
            pl.BlockSpec(w_in.shape, lambda i: (0, 0), pipeline_mode=pl.Buffered(1)),
            pl.BlockSpec(wf_bf.shape, lambda i: (0, 0)),
            pl.BlockSpec(bfg.shape, lambda i: (0, 0)),
        ],
        out_specs=[pl.BlockSpec((tm, d), row)] * 3 + [pl.BlockSpec((tm, LANES), row)],
        out_shape=[jax.ShapeDtypeStruct((t, d), BF16)] * 3 + [jax.ShapeDtypeStruct((t, LANES), F32)],
        scratch_shapes=[pltpu.VMEM((1, LANES), F32)],
        compiler_params=_cparams("arbitrary"),
        name="odd_inproj",
    )(x2, w_in, wf_bf, bfg)


def _fox_attn_kernel(q_ref, k_ref, v_ref, cq_ref, ck_ref, o_ref, qm_ref, va_ref, cqc_ref, m_ref, acc_ref, *, tq):
    seq = q_ref.shape[0]
    tk = tq
    hp = pl.program_id(1)
    q = q_ref[...]
    v = v_ref[...]
    lane = lax.broadcasted_iota(I32, (seq, LANES), 1)
    cq_all = cq_ref[...]
    for par in range(2):
        own = (lane // FOX_DH) == par
        qm_ref[par] = jnp.where(own, q, jnp.zeros_like(q))
        va_ref[par] = jnp.where(own, v, jnp.ones_like(v))
        cq = jnp.sum(jnp.where(lane == 2 * hp + par, cq_all, 0.0), axis=-1, keepdims=True)
        cqc_ref[par] = jnp.broadcast_to(cq, (seq, LANES))
    m_ref[...] = jnp.full(m_ref.shape, NEG_BIG, F32)
    acc_ref[...] = jnp.zeros(acc_ref.shape, F32)
    rq = ATT_RQ
    nr = tq // rq
    upper = (lax.broadcasted_iota(I32, (rq, rq), 1) > lax.broadcasted_iota(I32, (rq, rq), 0))
    lane_q = lax.broadcasted_iota(I32, (tq, LANES), 1)

    chains = [(ii, j, r, par) for ii in range(seq // tq) for j in range(ii + 1)
              for r in range(nr) for par in range(2)]

    def n_keys(ii, j, r):
        return (r + 1) * rq if j == ii else tk

    def scores(chain):
        ii, j, r, par = chain
        k = k_ref[j * tk:j * tk + n_keys(ii, j, r), :]
        return _dot_nt(qm_ref[par, ii * tq + r * rq:ii * tq + (r + 1) * rq, :], k)

    def finish(chain, s):
        ii, j, r, par = chain
        rows = slice(ii * tq + r * rq, ii * tq + (r + 1) * rq)
        nk = n_keys(ii, j, r)
        ck = ck_ref[par:par + 1, j * tk:j * tk + nk]
        cq = cqc_ref[par, rows, :]
        sc = []
        for kc in range(nk // rq):
            t = s[:, kc * rq:(kc + 1) * rq] - ck[:, kc * rq:(kc + 1) * rq]
            if j == ii and kc == r:
                t = jnp.where(upper, NEG_BIG, t)
            sc.append(t)
        mx = sc[0]
        for t in sc[1:]:
            mx = jnp.maximum(mx, t)
        m_old = m_ref[par, rows, :]
        m_new = jnp.maximum(m_old, jnp.max(mx, axis=-1, keepdims=True) + cq)
        alpha = jnp.exp2(m_old - m_new)
        shift = m_new - cq
        p = jnp.concatenate([jnp.exp2(t - shift).astype(BF16) for t in sc], axis=1)
        acc_ref[par, rows, :] = alpha * acc_ref[par, rows, :] + _dot(p, va_ref[par, j * tk:j * tk + nk, :])
        m_ref[par, rows, :] = m_new
        if j == ii and r == nr - 1 and par == 1:
            blk = slice(ii * tq, (ii + 1) * tq)
            acc0 = acc_ref[0, blk, :]
            acc1 = acc_ref[1, blk, :]
            out0 = acc0 / acc0[:, FOX_DH:FOX_DH + 1]
            out1 = acc1 / acc1[:, 0:1]
            o_ref[blk, :] = jnp.where(lane_q < FOX_DH, out0, out1).astype(BF16)

    _run_chains(chains, FOX_LOOKAHEAD, scores, finish)


def _fox_attention(q, k, v, cum, cum_t):
    b, s, d = q.shape
    tq = min(FOX_TQ, s)
    npair = d // LANES
    seq_blk = lambda bi, h: (bi, 0, h)
    return pl.pallas_call(
        functools.partial(_fox_attn_kernel, tq=tq),
        grid=(b, npair),
        in_specs=[
            pl.BlockSpec((None, s, LANES), seq_blk),
            pl.BlockSpec((None, s, LANES), seq_blk),
            pl.BlockSpec((None, s, LANES), seq_blk),
            pl.BlockSpec((None, s, LANES), lambda bi, h: (bi, 0, 0)),
            pl.BlockSpec((None, None, 2, s), lambda bi, h: (bi, h, 0, 0)),
        ],
        out_specs=pl.BlockSpec((None, s, LANES), seq_blk),
        out_shape=jax.ShapeDtypeStruct((b, s, d), BF16),
        scratch_shapes=[pltpu.VMEM((2, s, LANES), BF16), pltpu.VMEM((2, s, LANES), BF16),
                        pltpu.VMEM((2, s, LANES), F32), pltpu.VMEM((2, s, LANES), F32),
                        pltpu.VMEM((2, s, LANES), F32)],
        compiler_params=_cparams("parallel", "parallel"),
        name="fox_attention",
    )(q, k, v, cum, cum_t)


def _outproj_router_kernel(*refs, n_y):
    y_refs = refs[:n_y]
    (w_ref, x_ref, g_ref, b_ref, rwt_ref, tri_ref, h_ref, route_ref, col_ref, cnt_ref, xs_ref,
     hbf_ref) = refs[n_y:]
    rw2 = rwt_ref[...]
    rw_hi = rw2.astype(BF16)
    rw_lo = (rw2 - rw_hi.astype(F32)).astype(BF16)
    lane_w = lax.broadcasted_iota(I32, rw2.shape, 1)
    w = jnp.where(lane_w < N_EXPERTS, rw_hi, rw_lo)
    tm = x_ref.shape[0]
    rc = ROUTER_ROWS
    logit_chunks = []

    def project(c):
        rows = slice(c * rc, (c + 1) * rc)
        mix = None
        k0 = 0
        for yr in y_refs:
            k1 = k0 + yr.shape[1]
            part = _dot(yr[rows, :], w_ref[k0:k1, :].astype(BF16))
            mix = part if mix is None else mix + part
            k0 = k1
        return mix

    def norm_and_logits(c, mix):
        rows = slice(c * rc, (c + 1) * rc)
        h = _layer_norm_rows(DEEPNORM_ALPHA * x_ref[rows, :] + mix, g_ref[...], b_ref[...])
        h_ref[rows, :] = h
        h_hi = h.astype(BF16)
        hbf_ref[rows, :] = h_hi
        h_lo = (h - h_hi.astype(F32)).astype(BF16)
        p_hi = _dot(h_hi, w)
        p_lo = _dot(h_lo, w)
        slab = p_hi + (pltpu.roll(p_hi, LANES - N_EXPERTS, 1) + p_lo)
        logit_chunks.append(slab.T[0:N_EXPERTS])

    _run_chains(list(range(tm // rc)), 1, project, norm_and_logits)

    logits = jnp.concatenate(logit_chunks, axis=1)
    row = lax.broadcasted_iota(I32, (N_EXPERTS, tm), 0)
    mx = jnp.max(logits, axis=0, keepdims=True)
    ex = jnp.exp(logits - mx)
    probs = ex / jnp.sum(ex, axis=0, keepdims=True)
    grp = row // EXPERTS_PER_GROUP

    def top2(vals):
        v1 = jnp.max(vals, axis=0, keepdims=True)
        i1 = jnp.min(jnp.where(vals == v1, row, N_EXPERTS), axis=0, keepdims=True)
        rest = jnp.where(row == i1, -2.0, vals)
        v2 = jnp.max(rest, axis=0, keepdims=True)
        i2 = jnp.min(jnp.where(rest == v2, row, N_EXPERTS), axis=0, keepdims=True)
        return v1, i1, v2, i2

    best_score = None
    best = None
    for gi in range(N_GROUPS):
        v1, _, v2, _ = top2(jnp.where(grp == gi, probs, -1.0))
        score = v1 + v2
        if gi == 0:
            best_score, best = score, jnp.zeros_like(score, dtype=I32)
        else:
            better = score > best_score
            best = jnp.where(better, gi, best)
            best_score = jnp.where(better, score, best_score)
    v1, i1, v2, i2 = top2(jnp.where(grp == best, probs, -1.0))
    tot = v1 + v2
    g1 = v1 / tot
    g2 = v2 / tot

    onehot = jnp.where((row == i1) | (row == i2), 1.0, 0.0)
    pref = _dot(onehot.astype(BF16), tri_ref[...])
    cnt = jnp.broadcast_to(jnp.sum(onehot, axis=1, keepdims=True), cnt_ref.shape)
    cnt_ref[...] = cnt
    grp_rows = jnp.floor((cnt + (SORT_ALIGN - 1)) * (1.0 / SORT_ALIGN)) * SORT_ALIGN
    row_c = lax.broadcasted_iota(I32, cnt.shape, 0)
    start = grp_rows
    step = 1
    while step < N_EXPERTS:
        start = start + jnp.where(row_c >= step, pltpu.roll(start, step, 0), 0.0)
        step *= 2
    start = (start - grp_rows)[:, 0:1]
    s1 = jnp.sum(jnp.where(row == i1, pref + start, 0.0), axis=0, keepdims=True)
    s2 = jnp.sum(jnp.where(row == i2, pref + start, 0.0), axis=0, keepdims=True)

    row8 = lax.broadcasted_iota(I32, (ROUTE_ROWS, tm), 0)
    fields = (i1.astype(F32), i2.astype(F32), g1, g2, s1, s2)
    route = jnp.zeros((ROUTE_ROWS, tm), F32)
    for n, f in enumerate(fields):
        route = jnp.where(row8 == n, f, route)
    route_ref[...] = route
    col_ref[...] = jnp.concatenate([route, jnp.zeros((LANES - ROUTE_ROWS, tm), F32)], axis=0).T

    srow = lax.broadcasted_iota(I32, (xs_ref.shape[0], tm), 0).astype(F32)
    perm = jnp.where((srow == s1) | (srow == s2), 1.0, 0.0).astype(BF16)
    xs_ref[...] = _dot(perm, hbf_ref[...]).astype(BF16)


def _outproj_router(ys, w_out, x2, ln_g, ln_b, rwt, tri):
    t, d = x2.shape
    tm = tri.shape[0]
    n_local = _local_rows(tm)
    nt = t // tm
    row = lambda i: (i, 0)
    full = lambda i: (0, 0)
    n_y = len(ys)
    kern = functools.partial(_outproj_router_kernel, n_y=n_y)
    return pl.pallas_call(
        kern,
        grid=(nt,),
        in_specs=([pl.BlockSpec((tm, y.shape[1]), row) for y in ys]
                  + [pl.BlockSpec(w_out.shape, full, pipeline_mode=pl.Buffered(1))]
                  + [pl.BlockSpec((tm, d), row), pl.BlockSpec((1, d), full), pl.BlockSpec((1, d), full),
                     pl.BlockSpec(rwt.shape, full), pl.BlockSpec(tri.shape, full)]),
        out_specs=[pl.BlockSpec((tm, d), row), pl.BlockSpec((ROUTE_ROWS, tm), lambda i: (0, i)),
                   pl.BlockSpec((tm, LANES), row), pl.BlockSpec((N_EXPERTS, LANES), row),
                   pl.BlockSpec((n_local, d), row)],
        out_shape=[jax.ShapeDtypeStruct((t, d), F32), jax.ShapeDtypeStruct((ROUTE_ROWS, t), F32),
                   jax.ShapeDtypeStruct((t, LANES), F32), jax.ShapeDtypeStruct((nt * N_EXPERTS, LANES), F32),
                   jax.ShapeDtypeStruct((nt * n_local, d), BF16)],
        scratch_shapes=[pltpu.VMEM((tm, d), BF16)],
        compiler_params=_cparams("parallel"),
        name="outproj_router",
    )(*ys, w_out, x2, ln_g, ln_b, rwt, tri)


def _start_tile_gather(tile, loff_ref, rows_ref, gpos_ref, sorted_hbm, local_ref, sem):
    for e in range(N_EXPERTS):
        n = tile * N_EXPERTS + e

        @pl.when(rows_ref[n] > 0)
        def _(n=n):
            size = pl.multiple_of(rows_ref[n], SORT_ALIGN)
            src = sorted_hbm.at[pl.ds(pl.multiple_of(gpos_ref[n], SORT_ALIGN), size)]
            dst = local_ref.at[pl.ds(pl.multiple_of(loff_ref[n], SORT_ALIGN), size)]
            pltpu.make_async_copy(src, dst, sem).start()


def _wait_rows(n_rows, src_hbm, dst_ref, sem):
    n = pl.multiple_of(n_rows, SORT_ALIGN)
    pltpu.make_async_copy(src_hbm.at[pl.ds(0, n)], dst_ref.at[pl.ds(0, n)], sem).wait()


def _expert_kernel(te_ref, nx_ref, nt_ref, g0_ref, g1_ref, valid_ref, loff_ref, rows_ref, gpos_ref,
                   xs_hbm, wg_hbm, wu_hbm, wd_hbm, o_ref,
                   xbuf, semx, wgs, wus, wds, wgb, wub, wdb, sem, nsw_ref, *, layer, n_local):
    r = pl.program_id(0)
    nt = nt_ref[0]
    cur = te_ref[r]
    tr = xbuf.shape[1]
    xslot = r % 2

    def start_pieces(tile, dst_slot):
        e = te_ref[tile]
        base = tile * tr

        def piece(tau, c):
            n = tau * N_EXPERTS + e
            lo = jnp.maximum(gpos_ref[n], base)
            hi = jnp.minimum(gpos_ref[n] + rows_ref[n], base + tr)

            @pl.when(hi > lo)
            def _():
                size = pl.multiple_of(hi - lo, SORT_ALIGN)
                src = pl.multiple_of(tau * n_local + loff_ref[n] + (lo - gpos_ref[n]), SORT_ALIGN)
                dst = pl.multiple_of(lo - base, SORT_ALIGN)
                pltpu.make_async_copy(xs_hbm.at[pl.ds(src, size)], xbuf.at[dst_slot, pl.ds(dst, size)],
                                      semx.at[dst_slot]).start()

            return c

        lax.fori_loop(g0_ref[tile], g1_ref[tile] + 1, piece, 0)

    @pl.when(r == 0)
    def _():
        xbuf[...] = jnp.zeros_like(xbuf)
        start_pieces(0, 0)

    @pl.when(r + 1 < nt)
    def _():
        start_pieces(r + 1, 1 - xslot)

    def weight_copies(e, slot):
        return [pltpu.make_async_copy(src.at[layer, e], dst.at[slot], sem.at[slot])
                for src, dst in ((wg_hbm, wgs), (wu_hbm, wus), (wd_hbm, wds))]

    @pl.when(r == 0)
    def _():
        nsw_ref[0] = 0
        for cp in weight_copies(cur, 0):
            cp.start()

    @pl.when((r == 0) | (cur != te_ref[jnp.maximum(r - 1, 0)]))
    def _():
        slot = nsw_ref[0] % 2
        nsw_ref[0] = nsw_ref[0] + 1
        for cp in weight_copies(cur, slot):
            cp.wait()

        @pl.when(nx_ref[r] != cur)
        def _():
            for cp in weight_copies(nx_ref[r], 1 - slot):
                cp.start()

        wgb[...] = wgs[slot].astype(BF16)
        wub[...] = wus[slot].astype(BF16)
        wdb[...] = wds[slot].astype(BF16)

    @pl.when(r < nt)
    def _():
        _wait_rows(valid_ref[r], xs_hbm, xbuf.at[xslot], semx.at[xslot])
        rc = EXPERT_ROWS

        def up(c):
            x = xbuf[xslot, c * rc:(c + 1) * rc, :]
            return _dot(x, wgb[...]), _dot(x, wub[...])

        def down(c, gate_up):
            a = _silu(gate_up[0]) * gate_up[1]
            o_ref[c * rc:(c + 1) * rc, :] = _dot(a.astype(BF16), wdb[...]).astype(BF16)

        _run_chains(list(range(tr // rc)), 1, up, down)

    @pl.when(r >= nt)
    def _():
        o_ref[...] = jnp.zeros_like(o_ref)


def _expert_mlps(meta, n_slots, xs_local, wg, wu, wd, layer):
    d = xs_local.shape[1]
    tr = MOE_TR
    dff = wg.shape[3]
    grid_spec = pltpu.PrefetchScalarGridSpec(
        num_scalar_prefetch=9,
        grid=(n_slots // tr,),
        in_specs=[pl.BlockSpec(memory_space=pl.ANY)] * 4,
        out_specs=pl.BlockSpec((tr, d), lambda r, *_: (r, 0)),
        scratch_shapes=[pltpu.VMEM((2, tr, d), BF16), pltpu.SemaphoreType.DMA((2,)),
                        pltpu.VMEM((2, d, dff), F32), pltpu.VMEM((2, d, dff), F32), pltpu.VMEM((2, dff, d), F32),
                        pltpu.VMEM((d, dff), BF16), pltpu.VMEM((d, dff), BF16), pltpu.VMEM((dff, d), BF16),
                        pltpu.SemaphoreType.DMA((2,)), pltpu.SMEM((1,), I32)],
    )
    return pl.pallas_call(
        functools.partial(_expert_kernel, layer=layer, n_local=_local_rows(MOE_TM)),
        grid_spec=grid_spec,
        out_shape=jax.ShapeDtypeStruct((n_slots, d), BF16),
        compiler_params=_cparams("arbitrary"),
        name="expert_mlps",
    )(meta["tile_expert"], meta["next_expert"], meta["n_tiles"], meta["first_group"], meta["last_group"],
      meta["valid"], meta["loff"], meta["rows"], meta["gpos"], xs_local, wg, wu, wd)


def _combine_kernel(loff_ref, rows_ref, gpos_ref, tot_ref, ys_hbm, h_ref, col_ref, p_ref, g_ref, b_ref,
                    pg_ref, pp_ref, o_ref, ybuf, sem):
    i = pl.program_id(0)
    n = pl.num_programs(0)
    tm = h_ref.shape[0]
    n_local = ybuf.shape[1]
    slot = i % 2

    @pl.when(i == 0)
    def _():
        ybuf[...] = jnp.zeros_like(ybuf)
        _start_tile_gather(0, loff_ref, rows_ref, gpos_ref, ys_hbm, ybuf.at[0], sem.at[0])

    @pl.when(i + 1 < n)
    def _():
        _start_tile_gather(i + 1, loff_ref, rows_ref, gpos_ref, ys_hbm, ybuf.at[1 - slot], sem.at[1 - slot])

    _wait_rows(tot_ref[i], ys_hbm, ybuf.at[slot], sem.at[slot])
    rc = COMBINE_ROWS
    scol = lax.broadcasted_iota(I32, (rc, n_local), 1).astype(F32)

    def gather_rows(c):
        rows = slice(c * rc, (c + 1) * rc)
        cols = col_ref[rows, :]
        pick = jnp.zeros((rc, n_local), F32)
        for k in range(2):
            pick = jnp.where(scol == cols[:, ROUTE_SLOT + k:ROUTE_SLOT + k + 1],
                             cols[:, ROUTE_GATE + k:ROUTE_GATE + k + 1], pick)
        ffn = _dot(pick.astype(BF16), ybuf[slot])
        return ffn, _dot(p_ref[rows, :].astype(BF16), pp_ref[...].astype(BF16))

    def finish(c, gathered):
        rows = slice(c * rc, (c + 1) * rc)
        ffn, pe = gathered
        h2 = _layer_norm_rows(DEEPNORM_ALPHA * h_ref[rows, :] + ffn, g_ref[...], b_ref[...])
        gate = _sigmoid(_dot(h2.astype(BF16), pg_ref[...].astype(BF16)))
        o_ref[rows, :] = h2 + gate * pe

    _run_chains(list(range(tm // rc)), 1, gather_rows, finish)


def _combine(meta, ys, h, cols, p3, layer, ln_g, ln_b, ple_gate, ple_proj):
    t, d = h.shape
    tm = MOE_TM
    pdim = p3.shape[2]
    row = lambda i, *_: (i, 0)
    full = lambda i, *_: (0, 0)
    grid_spec = pltpu.PrefetchScalarGridSpec(
        num_scalar_prefetch=4,
        grid=(t // tm,),
        in_specs=[
            pl.BlockSpec(memory_space=pl.ANY),
            pl.BlockSpec((tm, d), row),
            pl.BlockSpec((tm, LANES), row),
            pl.BlockSpec((None, tm, pdim), lambda i, *_: (layer, i, 0)),
            pl.BlockSpec((1, d), full),
            pl.BlockSpec((1, d), full),
            pl.BlockSpec((None,) + ple_gate.shape[1:], lambda i, *_: (layer, 0, 0), pipeline_mode=pl.Buffered(1)),
            pl.BlockSpec((None,) + ple_proj.shape[1:], lambda i, *_: (layer, 0, 0), pipeline_mode=pl.Buffered(1)),
        ],
        out_specs=pl.BlockSpec((tm, d), row),
        scratch_shapes=[pltpu.VMEM((2, _local_rows(tm), d), BF16), pltpu.SemaphoreType.DMA((2,))],
    )
    return pl.pallas_call(
        _combine_kernel,
        grid_spec=grid_spec,
        out_shape=jax.ShapeDtypeStruct((t, d), F32),
        compiler_params=_cparams("arbitrary"),
        name="moe_combine",
    )(meta["loff"], meta["rows"], meta["gpos"], meta["tot"], ys, h, cols, p3, ln_g, ln_b, ple_gate, ple_proj)


def _rotary_tables(seq):
    half = RET_DK // 2
    inv = (np.float32(ROPE_BASE) ** (-np.arange(half, dtype=np.float32) / np.float32(half))).astype(np.float32)
    ang = (np.arange(seq, dtype=np.float32)[:, None] * inv[None, :]).astype(np.float32)
    cos = np.cos(ang.astype(np.float64))
    sin = np.sin(ang.astype(np.float64))
    cos_h = np.concatenate([cos, cos], axis=1)
    sin_h = np.concatenate([-sin, sin], axis=1)
    return (jnp.asarray(np.tile(cos_h, (1, RET_HEADS)), F32), jnp.asarray(np.tile(sin_h, (1, RET_HEADS)), F32))


def _retention_tables():
    c = RET_CHUNK
    h = np.arange(RET_HEADS, dtype=np.float64)
    log_g = np.log1p(-np.exp2(-5.0 - h))
    j = np.arange(c, dtype=np.float64)
    rel = j[:, None] - j[None, :]
    din = np.where(rel >= 0, np.exp(np.maximum(rel, 0.0)[None] * log_g[:, None, None]), 0.0)
    qd = np.exp((j + 1.0)[None] * log_g[:, None])
    kd = np.exp((c - 1.0 - j)[None] * log_g[:, None])
    cd = np.exp(c * log_g)
    qd = np.broadcast_to(qd[:, :, None], (RET_HEADS, c, LANES))
    kd = np.broadcast_to(kd[:, :, None], (RET_HEADS, c, LANES))
    cd = np.broadcast_to(cd[:, None, None], (RET_HEADS, 1, LANES))
    return tuple(jnp.asarray(a, F32) for a in (din, qd, kd, cd))


def _t5_bucket_np(dist):
    max_exact = REL_BUCKETS // 2
    d = np.maximum(dist, 1).astype(np.float32)
    large = max_exact + (np.log(d / np.float32(max_exact)) / np.float32(math.log(REL_MAX_DIST / max_exact))
                         * np.float32(REL_BUCKETS - max_exact)).astype(np.int32)
    large = np.minimum(large, REL_BUCKETS - 1)
    return np.where(dist < max_exact, dist, large)


def _diff_bias_tables(rel_bias, seq):
    c = ATT_RQ
    r = np.arange(c)
    dist0 = r[:, None] - r[None, :]
    far = REL_BUCKETS - 1
    assert np.all(_t5_bucket_np(np.arange(c + 1, max(seq, 2 * c))) == far)
    bidx = np.stack([_t5_bucket_np(np.maximum(dist0, 0)), _t5_bucket_np(dist0 + c)])
    rb = rel_bias.astype(F32).T
    shifted = (rb - rb[:, far:far + 1]) * LOG2E
    bidx = jnp.asarray(bidx, I32)[None]
    tab = jnp.zeros((rb.shape[0], 2, c, c), F32)
    for bucket in range(REL_BUCKETS - 1):
        tab = jnp.where(bidx == bucket, shifted[:, bucket][:, None, None, None], tab)
    causal = jnp.asarray(np.stack([dist0 >= 0, np.ones_like(dist0, bool)]))[None]
    return jnp.where(causal, tab, NEG_BIG)


def _local_rows(tm):
    need = 2 * tm + N_EXPERTS * (SORT_ALIGN - 1)
    return -(-need // LANES) * LANES


def _round_up(a, m):
    return ((a + m - 1) // m) * m


def _route_meta(cnt, t):
    tm, tr = MOE_TM, MOE_TR
    nt = t // tm
    counts = cnt.reshape(nt, N_EXPERTS, LANES)[:, :, 0].astype(I32)
    rows = _round_up(counts, SORT_ALIGN)
    loff = jnp.cumsum(rows, axis=1) - rows
    seg = jnp.sum(rows, axis=0)
    seg_pad = _round_up(seg, tr)
    ends = jnp.cumsum(seg_pad)
    offs = ends - seg_pad
    gpos = offs[None, :] + jnp.cumsum(rows, axis=0) - rows
    n_slots = 2 * t + nt * N_EXPERTS * (SORT_ALIGN - 1)
    n_slots = _round_up(n_slots, tr) + N_EXPERTS * tr
    n_tiles = (ends[-1] // tr).astype(I32)
    tile_start = jnp.arange(n_slots // tr, dtype=I32) * tr
    tile_expert = jnp.sum((tile_start[:, None] >= ends[None, :]).astype(I32), axis=1)
    last = jnp.sum((((n_tiles - 1) * tr) >= ends).astype(I32))
    tile_expert = jnp.minimum(tile_expert, last).astype(I32)
    eid = jnp.arange(N_EXPERTS, dtype=I32)
    later = (eid[None, :] > eid[:, None]) & (seg_pad > 0)[None, :]
    nxt = jnp.min(jnp.where(later, eid[None, :], N_EXPERTS), axis=1)
    nxt = jnp.where(nxt == N_EXPERTS, eid, nxt)
    next_expert = jnp.sum(jnp.where(tile_expert[:, None] == eid[None, :], nxt[None, :], 0), axis=1).astype(I32)
    own = tile_expert[:, None] == eid[None, :]
    pick = lambda tab: jnp.sum(jnp.where(own[:, None, :], tab[None, :, :], 0), axis=-1)
    g_start, g_end = pick(gpos), pick(gpos + rows)
    used = (tile_start < n_tiles * tr)[:, None]
    first_group = jnp.sum((g_end <= tile_start[:, None]).astype(I32), axis=1)
    last_group = jnp.sum(((g_start < tile_start[:, None] + tr) & used).astype(I32), axis=1) - 1
    seg_end = jnp.sum(jnp.where(own, (offs + seg)[None, :], 0), axis=1)
    valid = jnp.clip(seg_end - tile_start, 0, tr)
    meta = {
        "loff": loff.reshape(-1).astype(I32), "rows": rows.reshape(-1).astype(I32),
        "gpos": gpos.reshape(-1).astype(I32), "tot": jnp.sum(rows, axis=1).astype(I32),
        "tile_expert": tile_expert, "next_expert": next_expert, "n_tiles": n_tiles.reshape(1),
        "first_group": first_group.astype(I32), "last_group": last_group.astype(I32), "valid": valid.astype(I32),
    }
    return meta, n_slots


def kernel(x, p, rel_bias, router_w, even_w_in, even_w_out, even_lambda, even_diff_norm, even_ret_norm,
           odd_w_in, odd_b_forget, odd_w_out, ln_mix_g, ln_mix_b, ln_ffn_g, ln_ffn_b,
           moe_w_gate, moe_w_up, moe_w_down, ple_proj, ple_gate):
    b, s, d = x.shape
    t = b * s
    assert d == 1024 and p.shape[0] == DEPTH and even_w_in.shape[2] == 3072
    assert odd_w_in.shape[2] == 3 * d + FOX_HEADS and moe_w_gate.shape[1] == N_EXPERTS
    assert s % RET_CHUNK == 0 and s % min(ATT_TQ, s) == 0 and s % min(FOX_TQ, s) == 0
    assert t % min(PROJ_TM, s) == 0 and t % MOE_TM == 0

    cos_t, sin_t = _rotary_tables(s)
    ret_tabs = _retention_tables()
    bias_tab = _diff_bias_tables(rel_bias, s)
    rw32 = router_w.astype(F32)
    rwt = jnp.zeros((d, LANES), F32).at[:, :N_EXPERTS].set(rw32).at[:, N_EXPERTS:2 * N_EXPERTS].set(rw32)
    tok = np.arange(MOE_TM)
    tri = jnp.asarray(tok[:, None] < tok[None, :], BF16)

    x2 = x.reshape(t, d)
    for i in range(DEPTH):
        j = i // 2
        if i % 2 == 0:
            lam_init = 0.8 - 0.6 * math.exp(-0.3 * i)
            qa, ka, va, qb, kb, vb, gb = _even_inproj(x2, even_w_in[j], cos_t, sin_t, s)
            sh = lambda a: a.reshape(b, s, a.shape[1])
            ya = _diff_attention(sh(qa), sh(ka), sh(va), bias_tab, even_lambda[j].astype(F32),
                                 even_diff_norm[j].reshape(1, -1).astype(F32), lam_init)
            yb = _retention(sh(qb), sh(kb), sh(vb), sh(gb), ret_tabs, even_ret_norm[j].reshape(1, -1).astype(F32))
            ys = [ya.reshape(t, -1), yb.reshape(t, -1)]
            w_out = even_w_out[j]
        else:
            w_in = odd_w_in[j]
            wf = jnp.zeros((d, LANES), BF16).at[:, :FOX_HEADS].set(w_in[:, 3 * d:].astype(BF16))
            bfg = jnp.zeros((1, LANES), F32).at[0, :FOX_HEADS].set(odd_b_forget[j].astype(F32))
            q, k, v, cum = _odd_inproj(x2, w_in, wf, bfg, s)
            cum3 = cum.reshape(b, s, LANES)
            cum_t = jnp.transpose(cum3[:, :, :FOX_HEADS], (0, 2, 1)).reshape(b, FOX_HEADS // 2, 2, s)
            y = _fox_attention(q.reshape(b, s, d), k.reshape(b, s, d), v.reshape(b, s, d), cum3, cum_t)
            ys = [y.reshape(t, d)]
            w_out = odd_w_out[j]
        h, route, cols, cnt, xs_local = _outproj_router(ys, w_out, x2, ln_mix_g[i].reshape(1, d),
                                                        ln_mix_b[i].reshape(1, d), rwt, tri)
        meta, n_slots = _route_meta(cnt, t)
        rows = _expert_mlps(meta, n_slots, xs_local, moe_w_gate, moe_w_up, moe_w_down, i)
        x2 = _combine(meta, rows, h, cols, p.reshape(DEPTH, t, -1), i, ln_ffn_g[i].reshape(1, d),
                      ln_ffn_b[i].reshape(1, d), ple_gate, ple_proj)
    return x2.reshape(b, s, d)
```

```python
import functools
import math

import numpy as np
import jax
import jax.numpy as jnp
from jax import lax
from jax.experimental import pallas as pl
from jax.experimental.pallas import tpu as pltpu

F32 = jnp.float32
BF16 = jnp.bfloat16
I32 = jnp.int32

DIFF_HEADS = 4
DIFF_DK = 64
RET_HEADS = 4
RET_DK = 64
RET_DV = 128
RET_CHUNK = 128
FOX_HEADS = 16
FOX_DH = 64
REL_BUCKETS = 32
REL_MAX_DIST = 128
N_GROUPS = 4
EXPERTS_PER_GROUP = 4
N_EXPERTS = 16
DEPTH = 2
DEEPNORM_ALPHA = (2 * DEPTH) ** 0.25
LN_EPS = 1e-5
ROPE_BASE = 10000.0
NEG_BIG = -1e30
LOG2E = math.log2(math.e)

VMEM_LIMIT_BYTES = 48 * 1024 * 1024
LANES = 128

PROJ_TM = 1024
ATT_TQ = 512
FOX_TQ = 512
ATT_RQ = LANES
DIFF_LOOKAHEAD = 3
FOX_LOOKAHEAD = 4
MOE_TR = 512
MOE_TM = 512
SORT_ALIGN = 16
ROUTE_ROWS = 8
ROUTE_GATE = 2
ROUTE_SLOT = 4
COMBINE_ROWS = 256
EXPERT_ROWS = 256
ROUTER_ROWS = 256


def _cparams(*sem):
    return pltpu.CompilerParams(dimension_semantics=sem, vmem_limit_bytes=VMEM_LIMIT_BYTES)


def _dot(a, b):
    return jnp.dot(a, b, preferred_element_type=F32)


def _dot_nt(a, b):
    return lax.dot_general(a, b, (((1,), (1,)), ((), ())), preferred_element_type=F32)


def _layer_norm_rows(z, g, b):
    mu = jnp.mean(z, axis=-1, keepdims=True)
    zc = z - mu
    var = jnp.mean(zc * zc, axis=-1, keepdims=True)
    return zc * lax.rsqrt(var + LN_EPS) * g + b


def _silu(x):
    return x * (1.0 / (1.0 + jnp.exp(-x)))


def _sigmoid(x):
    return 1.0 / (1.0 + jnp.exp(-x))


def _even_inproj_kernel(x_ref, w_ref, cos_ref, sin_ref,
                        qa_ref, ka_ref, va_ref, qb_ref, kb_ref, vb_ref, gb_ref):
    x = x_ref[...].astype(BF16)

    def mm(c0, c1):
        return _dot(x, w_ref[:, c0:c1].astype(BF16))

    qa_ref[...] = (mm(0, 512) * (DIFF_DK ** -0.5 * LOG2E)).astype(BF16)
    ka_ref[...] = mm(512, 1024).astype(BF16)
    va_ref[...] = mm(1024, 1536).astype(BF16)
    qk = mm(1536, 2048)
    cos = cos_ref[...]
    sin = sin_ref[...]
    lane = lax.broadcasted_iota(I32, cos.shape, 1)
    first_half = (lane % RET_DK) < (RET_DK // 2)

    def rot(t):
        sw = jnp.where(first_half, pltpu.roll(t, t.shape[1] - RET_DK // 2, 1),
                       pltpu.roll(t, RET_DK // 2, 1))
        return t * cos + sw * sin

    qb_ref[...] = rot(qk[:, :256]).astype(BF16)
    kb_ref[...] = (rot(qk[:, 256:]) * (RET_DK ** -0.5)).astype(BF16)
    vb_ref[...] = mm(2048, 2560).astype(BF16)
    gb_ref[...] = mm(2560, 3072).astype(BF16)


def _even_inproj(x2, w_bf, cos_t, sin_t, seq):
    t, d = x2.shape
    tm = min(PROJ_TM, seq)
    nblk_s = seq // tm
    widths = (512, 512, 512, 256, 256, 512, 512)
    row = lambda i: (i, 0)
    return pl.pallas_call(
        _even_inproj_kernel,
        grid=(t // tm,),
        in_specs=[
            pl.BlockSpec((tm, d), row),
            pl.BlockSpec(w_bf.shape, lambda i: (0, 0), pipeline_mode=pl.Buffered(1)),
            pl.BlockSpec((tm, 256), lambda i: (i % nblk_s, 0)),
            pl.BlockSpec((tm, 256), lambda i: (i % nblk_s, 0)),
        ],
        out_specs=[pl.BlockSpec((tm, w), row) for w in widths],
        out_shape=[jax.ShapeDtypeStruct((t, w), BF16) for w in widths],
        compiler_params=_cparams("parallel"),
        name="even_inproj",
    )(x2, w_bf, cos_t, sin_t)


def _run_chains(chains, lookahead, scores, finish):
    pending = [scores(c) for c in chains[:lookahead]]
    for n, chain in enumerate(chains):
        if n + lookahead < len(chains):
            pending.append(scores(chains[n + lookahead]))
        finish(chain, pending.pop(0))


def _diff_attn_kernel(lam_ref, q_ref, k_ref, v_ref, bias_ref, g_ref, o_ref, qm_ref, m_ref, l_ref, acc_ref,
                      *, lam_init, tq):
    seq = q_ref.shape[0]
    tk = tq
    rq = ATT_RQ
    nr = tq // rq
    lane = lax.broadcasted_iota(I32, (rq, LANES), 1)
    for n in range(seq // rq):
        q = q_ref[n * rq:(n + 1) * rq, :]
        zero = jnp.zeros_like(q)
        qm_ref[n, 0:rq, :] = jnp.where(lane < DIFF_DK, q, zero)
        qm_ref[n, rq:2 * rq, :] = jnp.where(lane >= DIFF_DK, q, zero)
    m_ref[...] = jnp.full(m_ref.shape, NEG_BIG, F32)
    l_ref[...] = jnp.zeros(l_ref.shape, F32)
    acc_ref[...] = jnp.zeros(acc_ref.shape, F32)
    lp = lam_ref[...]
    lam = (jnp.exp(jnp.sum(lp[0:1, :] * lp[1:2, :], axis=-1, keepdims=True))
           - jnp.exp(jnp.sum(lp[2:3, :] * lp[3:4, :], axis=-1, keepdims=True)) + lam_init)
    bias2 = [jnp.concatenate([bias_ref[n], bias_ref[n]], axis=0) for n in range(2)]

    chains = [(ii, j, r) for j in range(seq // tq) for ii in range(j, seq // tq) for r in range(nr)]

    def n_keys(ii, j, r):
        return (r + 1) * rq if j == ii else tk

    def scores(chain):
        ii, j, r = chain
        k = k_ref[j * tk:j * tk + n_keys(ii, j, r), :]
        return _dot_nt(qm_ref[ii * nr + r], k)

    def finish(chain, s):
        ii, j, r = chain
        g = ii * nr + r
        nk = n_keys(ii, j, r)
        v = v_ref[j * tk:j * tk + nk, :]
        sc = []
        for kc in range(nk // rq):
            t = s[:, kc * rq:(kc + 1) * rq]
            back = g - (j * nr + kc)
            if back <= 1:
                t = t + bias2[back]
            sc.append(t)
        mx = sc[0]
        for t in sc[1:]:
            mx = jnp.maximum(mx, t)
        m_old = m_ref[g]
        m_new = jnp.maximum(m_old, jnp.max(mx, axis=-1, keepdims=True))
        alpha = jnp.exp2(m_old - m_new)
        ps = [jnp.exp2(t - m_new) for t in sc]
        psum = ps[0]
        for t in ps[1:]:
            psum = psum + t
        l_ref[g] = alpha * l_ref[g] + psum
        p = jnp.concatenate([t.astype(BF16) for t in ps], axis=1)
        acc_ref[g] = alpha * acc_ref[g] + _dot(p, v)
        m_ref[g] = m_new
        if j == ii:
            l_all = jnp.sum(l_ref[g], axis=-1, keepdims=True)
            a = acc_ref[g] / l_all
            o = a[0:rq] - lam * a[rq:2 * rq]
            o = o * lax.rsqrt(jnp.mean(o * o, axis=-1, keepdims=True) + LN_EPS)
            o_ref[g * rq:(g + 1) * rq, :] = (o * g_ref[...] * (1.0 - lam_init)).astype(BF16)

    _run_chains(chains, DIFF_LOOKAHEAD, scores, finish)


def _diff_attention(qa, ka, va, bias_tab, lam_params, diff_g, lam_init):
    b, s, _ = qa.shape
    tq = min(ATT_TQ, s)
    kern = functools.partial(_diff_attn_kernel, lam_init=lam_init, tq=tq)
    seq_blk = lambda bi, h: (bi, 0, h)
    return pl.pallas_call(
        kern,
        grid=(b, DIFF_HEADS),
        in_specs=[
            pl.BlockSpec(lam_params.shape, lambda bi, h: (0, 0)),
            pl.BlockSpec((None, s, LANES), seq_blk),
            pl.BlockSpec((None, s, LANES), seq_blk),
            pl.BlockSpec((None, s, LANES), seq_blk),
            pl.BlockSpec((None, 2, ATT_RQ, ATT_RQ), lambda bi, h: (h, 0, 0, 0)),
            pl.BlockSpec((1, LANES), lambda bi, h: (0, 0)),
        ],
        out_specs=pl.BlockSpec((None, s, LANES), seq_blk),
        out_shape=jax.ShapeDtypeStruct((b, s, DIFF_HEADS * LANES), BF16),
        scratch_shapes=[pltpu.VMEM((s // ATT_RQ, 2 * ATT_RQ, LANES), BF16)]
        + [pltpu.VMEM((s // ATT_RQ, 2 * ATT_RQ, LANES), F32)] * 3,
        compiler_params=_cparams("parallel", "parallel"),
        name="diff_attention",
    )(lam_params, qa, ka, va, bias_tab, diff_g)


def _retention_kernel(q_ref, k_ref, v_ref, gate_ref, din_ref, qd_ref, kd_ref, cd_ref, g_ref, o_ref):
    s = q_ref.shape[0]
    c = RET_CHUNK
    lane = lax.broadcasted_iota(I32, (c, LANES), 1)
    g = g_ref[...]
    states = [jnp.zeros((LANES, RET_DV), F32) for _ in range(2)]
    for n in range(s // c):
        r = slice(n * c, (n + 1) * c)
        q_pair = q_ref[r, :].astype(F32)
        k_pair = k_ref[r, :].astype(F32)
        for par in range(2):
            own = (lane // RET_DK) == par
            cols = slice(par * RET_DV, (par + 1) * RET_DV)
            q = jnp.where(own, q_pair, 0.0)
            k = jnp.where(own, k_pair, 0.0)
            v = v_ref[r, cols]
            scores = _dot_nt(q.astype(BF16), k.astype(BF16)) * din_ref[par]
            inner = _dot(scores.astype(BF16), v)
            cross = _dot((q * qd_ref[par]).astype(BF16), states[par].astype(BF16))
            kv = _dot((k * kd_ref[par]).T.astype(BF16), v)
            states[par] = cd_ref[par] * states[par] + kv
            y = inner + cross
            mu = jnp.mean(y, axis=-1, keepdims=True)
            yc = y - mu
            var = jnp.mean(yc * yc, axis=-1, keepdims=True)
            yn = yc * lax.rsqrt(var + LN_EPS) * g
            gate = gate_ref[r, cols].astype(F32)
            o_ref[r, cols] = (_silu(gate) * yn).astype(BF16)


def _retention(qb, kb, vb, gb, tabs, ret_g):
    b, s, _ = qb.shape
    din, qd, kd, cd = tabs
    pair = lambda bi, hp: (bi, 0, hp)
    tab = lambda bi, hp: (hp, 0, 0)
    return pl.pallas_call(
        _retention_kernel,
        grid=(b, RET_HEADS // 2),
        in_specs=[
            pl.BlockSpec((None, s, LANES), pair),
            pl.BlockSpec((None, s, LANES), pair),
            pl.BlockSpec((None, s, 2 * RET_DV), pair),
            pl.BlockSpec((None, s, 2 * RET_DV), pair),
            pl.BlockSpec((2, RET_CHUNK, RET_CHUNK), tab),
            pl.BlockSpec((2, RET_CHUNK, LANES), tab),
            pl.BlockSpec((2, RET_CHUNK, LANES), tab),
            pl.BlockSpec((2, 1, LANES), tab),
            pl.BlockSpec((1, RET_DV), lambda bi, hp: (0, 0)),
        ],
        out_specs=pl.BlockSpec((None, s, 2 * RET_DV), pair),
        out_shape=jax.ShapeDtypeStruct((b, s, RET_HEADS * RET_DV), BF16),
        compiler_params=_cparams("parallel", "parallel"),
        name="retention",
    )(qb, kb, vb, gb, din, qd, kd, cd, ret_g)


def _odd_inproj_kernel(x_ref, w_ref, wf_ref, bf_ref, q_ref, k_ref, v_ref, cum_ref, cumt_ref, carry_ref, *, nblk_s):
    i = pl.program_id(0)
    x = x_ref[...].astype(BF16)
    d = q_ref.shape[1]
    q_ref[...] = (_dot(x, w_ref[:, 0:d].astype(BF16)) * (FOX_DH ** -0.5 * LOG2E)).astype(BF16)
    k_ref[...] = _dot(x, w_ref[:, d:2 * d].astype(BF16)).astype(BF16)
    v_ref[...] = _dot(x, w_ref[:, 2 * d:3 * d].astype(BF16)).astype(BF16)
    z = _dot(x, wf_ref[...]) + bf_ref[...]
    c = jnp.minimum(z, 0.0) - jnp.log1p(jnp.exp(-jnp.abs(z)))
    tm = c.shape[0]
    row = lax.broadcasted_iota(I32, c.shape, 0)
    step = 1
    while step < tm:
        c = c + jnp.where(row >= step, pltpu.roll(c, step, 0), 0.0)
        step *= 2

    @pl.when(i % nblk_s == 0)
    def _():
        carry_ref[...] = jnp.zeros_like(carry_ref)

    c = c + carry_ref[...]
    cum = c * LOG2E
    cum_ref[...] = cum
    cumt_ref[...] = cum.T[0:FOX_HEADS]
    carry_ref[...] = c[tm - 1:tm, :]


def _odd_inproj(x2, w_in, wf_bf, bfg, seq):
    t, d = x2.shape
    tm = min(PROJ_TM, seq)
    nblk_s = seq // tm
    row = lambda i: (i, 0)
    kern = functools.partial(_odd_inproj_kernel, nblk_s=nblk_s)
    return pl.pallas_call(
        kern,
        grid=(t // tm,),
        in_specs=[
            pl.BlockSpec((tm, d), row),
            pl.BlockSpec(w_in.shape, lambda i: (0, 0), pipeline_mode=pl.Buffered(1)),
            pl.BlockSpec(wf_bf.shape, lambda i: (0, 0)),
            pl.BlockSpec(bfg.shape, lambda i: (0, 0)),
        ],
        out_specs=[pl.BlockSpec((tm, d), row)] * 3 + [
            pl.BlockSpec((tm, LANES), row),
            pl.BlockSpec((None, FOX_HEADS, tm), lambda i: (i // nblk_s, 0, i % nblk_s))],
        out_shape=[jax.ShapeDtypeStruct((t, d), BF16)] * 3 + [
            jax.ShapeDtypeStruct((t, LANES), F32), jax.ShapeDtypeStruct((t // seq, FOX_HEADS, seq), F32)],
        scratch_shapes=[pltpu.VMEM((1, LANES), F32)],
        compiler_params=_cparams("arbitrary"),
        name="odd_inproj",
    )(x2, w_in, wf_bf, bfg)


def _fox_attn_kernel(q_ref, k_ref, v_ref, cq_ref, ck_ref, o_ref, qm_ref, va_ref, cqc_ref, m_ref, acc_ref, *, tq):
    seq = q_ref.shape[0]
    tk = tq
    hp = pl.program_id(1)
    q = q_ref[...]
    v = v_ref[...]
    lane = lax.broadcasted_iota(I32, (seq, LANES), 1)
    cq_all = cq_ref[...]
    for par in range(2):
        own = (lane // FOX_DH) == par
        qm_ref[par] = jnp.where(own, q, jnp.zeros_like(q))
        va_ref[par] = jnp.where(own, v, jnp.ones_like(v))
        cq = jnp.sum(jnp.where(lane == 2 * hp + par, cq_all, 0.0), axis=-1, keepdims=True)
        cqc_ref[par] = jnp.broadcast_to(cq, (seq, LANES))
    m_ref[...] = jnp.full(m_ref.shape, NEG_BIG, F32)
    acc_ref[...] = jnp.zeros(acc_ref.shape, F32)
    rq = ATT_RQ
    upper = (lax.broadcasted_iota(I32, (rq, rq), 1) > lax.broadcasted_iota(I32, (rq, rq), 0))
    lane_q = lax.broadcasted_iota(I32, (rq, LANES), 1)

    def n_pieces(g):
        return -(-(g + 1) * rq // tk)

    n_chunks = seq // rq
    chains = [(g, kp, par) for kp in range(n_pieces(n_chunks - 1)) for g in range(n_chunks)
              if kp < n_pieces(g) for par in range(2)]

    def n_keys(g, kp):
        return min(tk, (g + 1) * rq - kp * tk)

    def scores(chain):
        g, kp, par = chain
        k = k_ref[kp * tk:kp * tk + n_keys(g, kp), :]
        return _dot_nt(qm_ref[par, g * rq:(g + 1) * rq, :], k)

    def finish(chain, s):
        g, kp, par = chain
        rows = slice(g * rq, (g + 1) * rq)
        nk = n_keys(g, kp)
        ck = ck_ref[par:par + 1, kp * tk:kp * tk + nk]
        cq = cqc_ref[par, rows, :]
        sc = []
        for kc in range(nk // rq):
            t = s[:, kc * rq:(kc + 1) * rq] - ck[:, kc * rq:(kc + 1) * rq]
            if kp * (tk // rq) + kc == g:
                t = jnp.where(upper, NEG_BIG, t)
            sc.append(t)
        mx = sc[0]
        for t in sc[1:]:
            mx = jnp.maximum(mx, t)
        m_old = m_ref[par, rows, :]
        m_new = jnp.maximum(m_old, jnp.max(mx, axis=-1, keepdims=True) + cq)
        alpha = jnp.exp2(m_old - m_new)
        shift = m_new - cq
        p = jnp.concatenate([jnp.exp2(t - shift).astype(BF16) for t in sc], axis=1)
        acc_ref[par, rows, :] = alpha * acc_ref[par, rows, :] + _dot(p, va_ref[par, kp * tk:kp * tk + nk, :])
        m_ref[par, rows, :] = m_new
        if kp == n_pieces(g) - 1 and par == 1:
            acc0 = acc_ref[0, rows, :]
            acc1 = acc_ref[1, rows, :]
            out0 = acc0 / acc0[:, FOX_DH:FOX_DH + 1]
            out1 = acc1 / acc1[:, 0:1]
            o_ref[rows, :] = jnp.where(lane_q < FOX_DH, out0, out1).astype(BF16)

    _run_chains(chains, FOX_LOOKAHEAD, scores, finish)


def _fox_attention(q, k, v, cum, cum_t):
    b, s, d = q.shape
    tq = min(FOX_TQ, s)
    npair = d // LANES
    seq_blk = lambda bi, h: (bi, 0, h)
    return pl.pallas_call(
        functools.partial(_fox_attn_kernel, tq=tq),
        grid=(b, npair),
        in_specs=[
            pl.BlockSpec((None, s, LANES), seq_blk),
            pl.BlockSpec((None, s, LANES), seq_blk),
            pl.BlockSpec((None, s, LANES), seq_blk),
            pl.BlockSpec((None, s, LANES), lambda bi, h: (bi, 0, 0)),
            pl.BlockSpec((None, None, 2, s), lambda bi, h: (bi, h, 0, 0)),
        ],
        out_specs=pl.BlockSpec((None, s, LANES), seq_blk),
        out_shape=jax.ShapeDtypeStruct((b, s, d), BF16),
        scratch_shapes=[pltpu.VMEM((2, s, LANES), BF16), pltpu.VMEM((2, s, LANES), BF16),
                        pltpu.VMEM((2, s, LANES), F32), pltpu.VMEM((2, s, LANES), F32),
                        pltpu.VMEM((2, s, LANES), F32)],
        compiler_params=_cparams("parallel", "parallel"),
        name="fox_attention",
    )(q, k, v, cum, cum_t)


def _outproj_router_kernel(*refs, n_y):
    y_refs = refs[:n_y]
    (w_ref, x_ref, g_ref, b_ref, rwt_ref, tri_ref, h_ref, route_ref, col_ref, cnt_ref, xs_ref,
     hbf_ref) = refs[n_y:]
    rw2 = rwt_ref[...]
    rw_hi = rw2.astype(BF16)
    rw_lo = (rw2 - rw_hi.astype(F32)).astype(BF16)
    lane_w = lax.broadcasted_iota(I32, rw2.shape, 1)
    w = jnp.where(lane_w < N_EXPERTS, rw_hi, rw_lo)
    tm = x_ref.shape[0]
    rc = ROUTER_ROWS
    logit_chunks = []

    def project(c):
        rows = slice(c * rc, (c + 1) * rc)
        mix = None
        k0 = 0
        for yr in y_refs:
            k1 = k0 + yr.shape[1]
            part = _dot(yr[rows, :], w_ref[k0:k1, :].astype(BF16))
            mix = part if mix is None else mix + part
            k0 = k1
        return mix

    def norm_and_logits(c, mix):
        rows = slice(c * rc, (c + 1) * rc)
        h = _layer_norm_rows(DEEPNORM_ALPHA * x_ref[rows, :] + mix, g_ref[...], b_ref[...])
        h_ref[rows, :] = h
        h_hi = h.astype(BF16)
        hbf_ref[rows, :] = h_hi
        h_lo = (h - h_hi.astype(F32)).astype(BF16)
        p_hi = _dot(h_hi, w)
        p_lo = _dot(h_lo, w)
        slab = p_hi + (pltpu.roll(p_hi, LANES - N_EXPERTS, 1) + p_lo)
        logit_chunks.append(slab.T[0:N_EXPERTS])

    _run_chains(list(range(tm // rc)), 1, project, norm_and_logits)

    logits = jnp.concatenate(logit_chunks, axis=1)
    row = lax.broadcasted_iota(I32, (N_EXPERTS, tm), 0)
    mx = jnp.max(logits, axis=0, keepdims=True)
    ex = jnp.exp(logits - mx)
    probs = ex / jnp.sum(ex, axis=0, keepdims=True)
    grp = row // EXPERTS_PER_GROUP

    def top2(vals):
        v1 = jnp.max(vals, axis=0, keepdims=True)
        i1 = jnp.min(jnp.where(vals == v1, row, N_EXPERTS), axis=0, keepdims=True)
        rest = jnp.where(row == i1, -2.0, vals)
        v2 = jnp.max(rest, axis=0, keepdims=True)
        i2 = jnp.min(jnp.where(rest == v2, row, N_EXPERTS), axis=0, keepdims=True)
        return v1, i1, v2, i2

    best_score = None
    best = None
    for gi in range(N_GROUPS):
        v1, _, v2, _ = top2(jnp.where(grp == gi, probs, -1.0))
        score = v1 + v2
        if gi == 0:
            best_score, best = score, jnp.zeros_like(score, dtype=I32)
        else:
            better = score > best_score
            best = jnp.where(better, gi, best)
            best_score = jnp.where(better, score, best_score)
    v1, i1, v2, i2 = top2(jnp.where(grp == best, probs, -1.0))
    tot = v1 + v2
    g1 = v1 / tot
    g2 = v2 / tot

    onehot = jnp.where((row == i1) | (row == i2), 1.0, 0.0)
    pref = _dot(onehot.astype(BF16), tri_ref[...])
    cnt = jnp.broadcast_to(jnp.sum(onehot, axis=1, keepdims=True), cnt_ref.shape)
    cnt_ref[...] = cnt
    grp_rows = jnp.floor((cnt + (SORT_ALIGN - 1)) * (1.0 / SORT_ALIGN)) * SORT_ALIGN
    row_c = lax.broadcasted_iota(I32, cnt.shape, 0)
    start = grp_rows
    step = 1
    while step < N_EXPERTS:
        start = start + jnp.where(row_c >= step, pltpu.roll(start, step, 0), 0.0)
        step *= 2
    start = (start - grp_rows)[:, 0:1]
    s1 = jnp.sum(jnp.where(row == i1, pref + start, 0.0), axis=0, keepdims=True)
    s2 = jnp.sum(jnp.where(row == i2, pref + start, 0.0), axis=0, keepdims=True)

    row8 = lax.broadcasted_iota(I32, (ROUTE_ROWS, tm), 0)
    fields = (i1.astype(F32), i2.astype(F32), g1, g2, s1, s2)
    route = jnp.zeros((ROUTE_ROWS, tm), F32)
    for n, f in enumerate(fields):
        route = jnp.where(row8 == n, f, route)
    route_ref[...] = route
    col_ref[...] = jnp.concatenate([route, jnp.zeros((LANES - ROUTE_ROWS, tm), F32)], axis=0).T

    srow = lax.broadcasted_iota(I32, (xs_ref.shape[0], tm), 0).astype(F32)
    perm = jnp.where((srow == s1) | (srow == s2), 1.0, 0.0).astype(BF16)
    xs_ref[...] = _dot(perm, hbf_ref[...]).astype(BF16)


def _outproj_router(ys, w_out, x2, ln_g, ln_b, rwt, tri):
    t, d = x2.shape
    tm = tri.shape[0]
    n_local = _local_rows(tm)
    nt = t // tm
    row = lambda i: (i, 0)
    full = lambda i: (0, 0)
    n_y = len(ys)
    kern = functools.partial(_outproj_router_kernel, n_y=n_y)
    return pl.pallas_call(
        kern,
        grid=(nt,),
        in_specs=([pl.BlockSpec((tm, y.shape[1]), row) for y in ys]
                  + [pl.BlockSpec(w_out.shape, full, pipeline_mode=pl.Buffered(1))]
                  + [pl.BlockSpec((tm, d), row), pl.BlockSpec((1, d), full), pl.BlockSpec((1, d), full),
                     pl.BlockSpec(rwt.shape, full), pl.BlockSpec(tri.shape, full)]),
        out_specs=[pl.BlockSpec((tm, d), row), pl.BlockSpec((ROUTE_ROWS, tm), lambda i: (0, i)),
                   pl.BlockSpec((tm, LANES), row), pl.BlockSpec((N_EXPERTS, LANES), row),
                   pl.BlockSpec((n_local, d), row)],
        out_shape=[jax.ShapeDtypeStruct((t, d), F32), jax.ShapeDtypeStruct((ROUTE_ROWS, t), F32),
                   jax.ShapeDtypeStruct((t, LANES), F32), jax.ShapeDtypeStruct((nt * N_EXPERTS, LANES), F32),
                   jax.ShapeDtypeStruct((nt * n_local, d), BF16)],
        scratch_shapes=[pltpu.VMEM((tm, d), BF16)],
        compiler_params=_cparams("parallel"),
        name="outproj_router",
    )(*ys, w_out, x2, ln_g, ln_b, rwt, tri)


def _start_tile_gather(tile, loff_ref, rows_ref, gpos_ref, sorted_hbm, local_ref, sem):
    for e in range(N_EXPERTS):
        n = tile * N_EXPERTS + e

        @pl.when(rows_ref[n] > 0)
        def _(n=n):
            size = pl.multiple_of(rows_ref[n], SORT_ALIGN)
            src = sorted_hbm.at[pl.ds(pl.multiple_of(gpos_ref[n], SORT_ALIGN), size)]
            dst = local_ref.at[pl.ds(pl.multiple_of(loff_ref[n], SORT_ALIGN), size)]
            pltpu.make_async_copy(src, dst, sem).start()


def _wait_rows(n_rows, src_hbm, dst_ref, sem):
    n = pl.multiple_of(n_rows, SORT_ALIGN)
    pltpu.make_async_copy(src_hbm.at[pl.ds(0, n)], dst_ref.at[pl.ds(0, n)], sem).wait()


def _expert_kernel(te_ref, nx_ref, nt_ref, g0_ref, g1_ref, valid_ref, loff_ref, rows_ref, gpos_ref,
                   xs_hbm, wg_hbm, wu_hbm, wd_hbm, o_ref,
                   xbuf, semx, wgs, wus, wds, wgb, wub, wdb, sem, nsw_ref, *, layer, n_local):
    r = pl.program_id(0)
    nt = nt_ref[0]
    cur = te_ref[r]
    tr = xbuf.shape[1]
    xslot = r % 2

    def start_pieces(tile, dst_slot):
        e = te_ref[tile]
        base = tile * tr

        def piece(tau, c):
            n = tau * N_EXPERTS + e
            lo = jnp.maximum(gpos_ref[n], base)
            hi = jnp.minimum(gpos_ref[n] + rows_ref[n], base + tr)

            @pl.when(hi > lo)
            def _():
                size = pl.multiple_of(hi - lo, SORT_ALIGN)
                src = pl.multiple_of(tau * n_local + loff_ref[n] + (lo - gpos_ref[n]), SORT_ALIGN)
                dst = pl.multiple_of(lo - base, SORT_ALIGN)
                pltpu.make_async_copy(xs_hbm.at[pl.ds(src, size)], xbuf.at[dst_slot, pl.ds(dst, size)],
                                      semx.at[dst_slot]).start()

            return c

        lax.fori_loop(g0_ref[tile], g1_ref[tile] + 1, piece, 0)

    @pl.when(r == 0)
    def _():
        xbuf[...] = jnp.zeros_like(xbuf)
        start_pieces(0, 0)

    @pl.when(r + 1 < nt)
    def _():
        start_pieces(r + 1, 1 - xslot)

    def weight_copies(e, slot):
        return [pltpu.make_async_copy(src.at[layer, e], dst.at[slot], sem.at[slot])
                for src, dst in ((wg_hbm, wgs), (wu_hbm, wus), (wd_hbm, wds))]

    @pl.when(r == 0)
    def _():
        nsw_ref[0] = 0
        for cp in weight_copies(cur, 0):
            cp.start()

    @pl.when((r == 0) | (cur != te_ref[jnp.maximum(r - 1, 0)]))
    def _():
        slot = nsw_ref[0] % 2
        nsw_ref[0] = nsw_ref[0] + 1
        for cp in weight_copies(cur, slot):
            cp.wait()

        @pl.when(nx_ref[r] != cur)
        def _():
            for cp in weight_copies(nx_ref[r], 1 - slot):
                cp.start()

        wgb[...] = wgs[slot].astype(BF16)
        wub[...] = wus[slot].astype(BF16)
        wdb[...] = wds[slot].astype(BF16)

    @pl.when(r < nt)
    def _():
        _wait_rows(valid_ref[r], xs_hbm, xbuf.at[xslot], semx.at[xslot])
        rc = EXPERT_ROWS

        def up(c):
            x = xbuf[xslot, c * rc:(c + 1) * rc, :]
            return _dot(x, wgb[...]), _dot(x, wub[...])

        def down(c, gate_up):
            a = _silu(gate_up[0]) * gate_up[1]
            o_ref[c * rc:(c + 1) * rc, :] = _dot(a.astype(BF16), wdb[...]).astype(BF16)

        _run_chains(list(range(tr // rc)), 1, up, down)

    @pl.when(r >= nt)
    def _():
        o_ref[...] = jnp.zeros_like(o_ref)


def _expert_mlps(meta, n_slots, xs_local, wg, wu, wd, layer):
    d = xs_local.shape[1]
    tr = MOE_TR
    dff = wg.shape[3]
    grid_spec = pltpu.PrefetchScalarGridSpec(
        num_scalar_prefetch=9,
        grid=(n_slots // tr,),
        in_specs=[pl.BlockSpec(memory_space=pl.ANY)] * 4,
        out_specs=pl.BlockSpec((tr, d), lambda r, *_: (r, 0)),
        scratch_shapes=[pltpu.VMEM((2, tr, d), BF16), pltpu.SemaphoreType.DMA((2,)),
                        pltpu.VMEM((2, d, dff), F32), pltpu.VMEM((2, d, dff), F32), pltpu.VMEM((2, dff, d), F32),
                        pltpu.VMEM((d, dff), BF16), pltpu.VMEM((d, dff), BF16), pltpu.VMEM((dff, d), BF16),
                        pltpu.SemaphoreType.DMA((2,)), pltpu.SMEM((1,), I32)],
    )
    return pl.pallas_call(
        functools.partial(_expert_kernel, layer=layer, n_local=_local_rows(MOE_TM)),
        grid_spec=grid_spec,
        out_shape=jax.ShapeDtypeStruct((n_slots, d), BF16),
        compiler_params=_cparams("arbitrary"),
        name="expert_mlps",
    )(meta["tile_expert"], meta["next_expert"], meta["n_tiles"], meta["first_group"], meta["last_group"],
      meta["valid"], meta["loff"], meta["rows"], meta["gpos"], xs_local, wg, wu, wd)


def _combine_kernel(loff_ref, rows_ref, gpos_ref, tot_ref, ys_hbm, h_ref, col_ref, p_ref, g_ref, b_ref,
                    pg_ref, pp_ref, o_ref, ybuf, sem):
    i = pl.program_id(0)
    n = pl.num_programs(0)
    tm = h_ref.shape[0]
    n_local = ybuf.shape[1]
    slot = i % 2

    @pl.when(i == 0)
    def _():
        ybuf[...] = jnp.zeros_like(ybuf)
        _start_tile_gather(0, loff_ref, rows_ref, gpos_ref, ys_hbm, ybuf.at[0], sem.at[0])

    @pl.when(i + 1 < n)
    def _():
        _start_tile_gather(i + 1, loff_ref, rows_ref, gpos_ref, ys_hbm, ybuf.at[1 - slot], sem.at[1 - slot])

    _wait_rows(tot_ref[i], ys_hbm, ybuf.at[slot], sem.at[slot])
    rc = COMBINE_ROWS
    scol = lax.broadcasted_iota(I32, (rc, n_local), 1).astype(F32)

    def gather_rows(c):
        rows = slice(c * rc, (c + 1) * rc)
        cols = col_ref[rows, :]
        pick = jnp.zeros((rc, n_local), F32)
        for k in range(2):
            pick = jnp.where(scol == cols[:, ROUTE_SLOT + k:ROUTE_SLOT + k + 1],
                             cols[:, ROUTE_GATE + k:ROUTE_GATE + k + 1], pick)
        ffn = _dot(pick.astype(BF16), ybuf[slot])
        return ffn, _dot(p_ref[rows, :].astype(BF16), pp_ref[...].astype(BF16))

    def finish(c, gathered):
        rows = slice(c * rc, (c + 1) * rc)
        ffn, pe = gathered
        h2 = _layer_norm_rows(DEEPNORM_ALPHA * h_ref[rows, :] + ffn, g_ref[...], b_ref[...])
        gate = _sigmoid(_dot(h2.astype(BF16), pg_ref[...].astype(BF16)))
        o_ref[rows, :] = h2 + gate * pe

    _run_chains(list(range(tm // rc)), 1, gather_rows, finish)


def _combine(meta, ys, h, cols, p3, layer, ln_g, ln_b, ple_gate, ple_proj):
    t, d = h.shape
    tm = MOE_TM
    pdim = p3.shape[2]
    row = lambda i, *_: (i, 0)
    full = lambda i, *_: (0, 0)
    grid_spec = pltpu.PrefetchScalarGridSpec(
        num_scalar_prefetch=4,
        grid=(t // tm,),
        in_specs=[
            pl.BlockSpec(memory_space=pl.ANY),
            pl.BlockSpec((tm, d), row),
            pl.BlockSpec((tm, LANES), row),
            pl.BlockSpec((None, tm, pdim), lambda i, *_: (layer, i, 0)),
            pl.BlockSpec((1, d), full),
            pl.BlockSpec((1, d), full),
            pl.BlockSpec((None,) + ple_gate.shape[1:], lambda i, *_: (layer, 0, 0), pipeline_mode=pl.Buffered(1)),
            pl.BlockSpec((None,) + ple_proj.shape[1:], lambda i, *_: (layer, 0, 0), pipeline_mode=pl.Buffered(1)),
        ],
        out_specs=pl.BlockSpec((tm, d), row),
        scratch_shapes=[pltpu.VMEM((2, _local_rows(tm), d), BF16), pltpu.SemaphoreType.DMA((2,))],
    )
    return pl.pallas_call(
        _combine_kernel,
        grid_spec=grid_spec,
        out_shape=jax.ShapeDtypeStruct((t, d), F32),
        compiler_params=_cparams("arbitrary"),
        name="moe_combine",
    )(meta["loff"], meta["rows"], meta["gpos"], meta["tot"], ys, h, cols, p3, ln_g, ln_b, ple_gate, ple_proj)


def _rotary_tables(seq):
    half = RET_DK // 2
    inv = (np.float32(ROPE_BASE) ** (-np.arange(half, dtype=np.float32) / np.float32(half))).astype(np.float32)
    ang = (np.arange(seq, dtype=np.float32)[:, None] * inv[None, :]).astype(np.float32)
    cos = np.cos(ang.astype(np.float64))
    sin = np.sin(ang.astype(np.float64))
    cos_h = np.concatenate([cos, cos], axis=1)
    sin_h = np.concatenate([-sin, sin], axis=1)
    return (jnp.asarray(np.tile(cos_h, (1, RET_HEADS)), F32), jnp.asarray(np.tile(sin_h, (1, RET_HEADS)), F32))


def _retention_tables():
    c = RET_CHUNK
    h = np.arange(RET_HEADS, dtype=np.float64)
    log_g = np.log1p(-np.exp2(-5.0 - h))
    j = np.arange(c, dtype=np.float64)
    rel = j[:, None] - j[None, :]
    din = np.where(rel >= 0, np.exp(np.maximum(rel, 0.0)[None] * log_g[:, None, None]), 0.0)
    qd = np.exp((j + 1.0)[None] * log_g[:, None])
    kd = np.exp((c - 1.0 - j)[None] * log_g[:, None])
    cd = np.exp(c * log_g)
    qd = np.broadcast_to(qd[:, :, None], (RET_HEADS, c, LANES))
    kd = np.broadcast_to(kd[:, :, None], (RET_HEADS, c, LANES))
    cd = np.broadcast_to(cd[:, None, None], (RET_HEADS, 1, LANES))
    return tuple(jnp.asarray(a, F32) for a in (din, qd, kd, cd))


def _t5_bucket_np(dist):
    max_exact = REL_BUCKETS // 2
    d = np.maximum(dist, 1).astype(np.float32)
    large = max_exact + (np.log(d / np.float32(max_exact)) / np.float32(math.log(REL_MAX_DIST / max_exact))
                         * np.float32(REL_BUCKETS - max_exact)).astype(np.int32)
    large = np.minimum(large, REL_BUCKETS - 1)
    return np.where(dist < max_exact, dist, large)


def _diff_bias_tables(rel_bias, seq):
    c = ATT_RQ
    r = np.arange(c)
    dist0 = r[:, None] - r[None, :]
    far = REL_BUCKETS - 1
    assert np.all(_t5_bucket_np(np.arange(c + 1, max(seq, 2 * c))) == far)
    bidx = np.stack([_t5_bucket_np(np.maximum(dist0, 0)), _t5_bucket_np(dist0 + c)])
    rb = rel_bias.astype(F32).T
    shifted = (rb - rb[:, far:far + 1]) * LOG2E
    bidx = jnp.asarray(bidx, I32)[None]
    tab = jnp.zeros((rb.shape[0], 2, c, c), F32)
    for bucket in range(REL_BUCKETS - 1):
        tab = jnp.where(bidx == bucket, shifted[:, bucket][:, None, None, None], tab)
    causal = jnp.asarray(np.stack([dist0 >= 0, np.ones_like(dist0, bool)]))[None]
    return jnp.where(causal, tab, NEG_BIG)


def _local_rows(tm):
    need = 2 * tm + N_EXPERTS * (SORT_ALIGN - 1)
    return -(-need // LANES) * LANES


def _round_up(a, m):
    return ((a + m - 1) // m) * m


def _route_meta(cnt, t):
    tm, tr = MOE_TM, MOE_TR
    nt = t // tm
    counts = cnt.reshape(nt, N_EXPERTS, LANES)[:, :, 0].astype(I32)
    rows = _round_up(counts, SORT_ALIGN)
    loff = jnp.cumsum(rows, axis=1) - rows
    seg = jnp.sum(rows, axis=0)
    seg_pad = _round_up(seg, tr)
    ends = jnp.cumsum(seg_pad)
    offs = ends - seg_pad
    gpos = offs[None, :] + jnp.cumsum(rows, axis=0) - rows
    n_slots = 2 * t + nt * N_EXPERTS * (SORT_ALIGN - 1)
    n_slots = _round_up(n_slots, tr) + N_EXPERTS * tr
    n_tiles = (ends[-1] // tr).astype(I32)
    tile_start = jnp.arange(n_slots // tr, dtype=I32) * tr
    tile_expert = jnp.sum((tile_start[:, None] >= ends[None, :]).astype(I32), axis=1)
    last = jnp.sum((((n_tiles - 1) * tr) >= ends).astype(I32))
    tile_expert = jnp.minimum(tile_expert, last).astype(I32)
    eid = jnp.arange(N_EXPERTS, dtype=I32)
    later = (eid[None, :] > eid[:, None]) & (seg_pad > 0)[None, :]
    nxt = jnp.min(jnp.where(later, eid[None, :], N_EXPERTS), axis=1)
    nxt = jnp.where(nxt == N_EXPERTS, eid, nxt)
    next_expert = jnp.sum(jnp.where(tile_expert[:, None] == eid[None, :], nxt[None, :], 0), axis=1).astype(I32)
    own = tile_expert[:, None] == eid[None, :]
    pick = lambda tab: jnp.sum(jnp.where(own[:, None, :], tab[None, :, :], 0), axis=-1)
    g_start, g_end = pick(gpos), pick(gpos + rows)
    used = (tile_start < n_tiles * tr)[:, None]
    first_group = jnp.sum((g_end <= tile_start[:, None]).astype(I32), axis=1)
    last_group = jnp.sum(((g_start < tile_start[:, None] + tr) & used).astype(I32), axis=1) - 1
    seg_end = jnp.sum(jnp.where(own, (offs + seg)[None, :], 0), axis=1)
    valid = jnp.clip(seg_end - tile_start, 0, tr)
    meta = {
        "loff": loff.reshape(-1).astype(I32), "rows": rows.reshape(-1).astype(I32),
        "gpos": gpos.reshape(-1).astype(I32), "tot": jnp.sum(rows, axis=1).astype(I32),
        "tile_expert": tile_expert, "next_expert": next_expert, "n_tiles": n_tiles.reshape(1),
        "first_group": first_group.astype(I32), "last_group": last_group.astype(I32), "valid": valid.astype(I32),
    }
    return meta, n_slots


def kernel(x, p, rel_bias, router_w, even_w_in, even_w_out, even_lambda, even_diff_norm, even_ret_norm,
           odd_w_in, odd_b_forget, odd_w_out, ln_mix_g, ln_mix_b, ln_ffn_g, ln_ffn_b,
           moe_w_gate, moe_w_up, moe_w_down, ple_proj, ple_gate):
    b, s, d = x.shape
    t = b * s
    assert d == 1024 and p.shape[0] == DEPTH and even_w_in.shape[2] == 3072
    assert odd_w_in.shape[2] == 3 * d + FOX_HEADS and moe_w_gate.shape[1] == N_EXPERTS
    assert s % RET_CHUNK == 0 and s % min(ATT_TQ, s) == 0 and s % min(FOX_TQ, s) == 0
    assert t % min(PROJ_TM, s) == 0 and t % MOE_TM == 0

    cos_t, sin_t = _rotary_tables(s)
    ret_tabs = _retention_tables()
    bias_tab = _diff_bias_tables(rel_bias, s)
    rw32 = router_w.astype(F32)
    rwt = jnp.zeros((d, LANES), F32).at[:, :N_EXPERTS].set(rw32).at[:, N_EXPERTS:2 * N_EXPERTS].set(rw32)
    tok = np.arange(MOE_TM)
    tri = jnp.asarray(tok[:, None] < tok[None, :], BF16)

    x2 = x.reshape(t, d)
    for i in range(DEPTH):
        j = i // 2
        if i % 2 == 0:
            lam_init = 0.8 - 0.6 * math.exp(-0.3 * i)
            qa, ka, va, qb, kb, vb, gb = _even_inproj(x2, even_w_in[j], cos_t, sin_t, s)
            sh = lambda a: a.reshape(b, s, a.shape[1])
            ya = _diff_attention(sh(qa), sh(ka), sh(va), bias_tab, even_lambda[j].astype(F32),
                                 even_diff_norm[j].reshape(1, -1).astype(F32), lam_init)
            yb = _retention(sh(qb), sh(kb), sh(vb), sh(gb), ret_tabs, even_ret_norm[j].reshape(1, -1).astype(F32))
            ys = [ya.reshape(t, -1), yb.reshape(t, -1)]
            w_out = even_w_out[j]
        else:
            w_in = odd_w_in[j]
            wf = jnp.zeros((d, LANES), BF16).at[:, :FOX_HEADS].set(w_in[:, 3 * d:].astype(BF16))
            bfg = jnp.zeros((1, LANES), F32).at[0, :FOX_HEADS].set(odd_b_forget[j].astype(F32))
            q, k, v, cum, cum_t = _odd_inproj(x2, w_in, wf, bfg, s)
            cum3 = cum.reshape(b, s, LANES)
            cum_t = cum_t.reshape(b, FOX_HEADS // 2, 2, s)
            y = _fox_attention(q.reshape(b, s, d), k.reshape(b, s, d), v.reshape(b, s, d), cum3, cum_t)
            ys = [y.reshape(t, d)]
            w_out = odd_w_out[j]
        h, route, cols, cnt, xs_local = _outproj_router(ys, w_out, x2, ln_mix_g[i].reshape(1, d),
                                                        ln_mix_b[i].reshape(1, d), rwt, tri)
        meta, n_slots = _route_meta(cnt, t)
        rows = _expert_mlps(meta, n_slots, xs_local, moe_w_gate, moe_w_up, moe_w_down, i)
        x2 = _combine(meta, rows, h, cols, p.reshape(DEPTH, t, -1), i, ln_ffn_g[i].reshape(1, d),
                      ln_ffn_b[i].reshape(1, d), ple_gate, ple_proj)
    return x2.reshape(b, s, d)
```

```python
import functools
import math

import numpy as np
import jax
import jax.numpy as jnp
from jax import lax
from jax.experimental import pallas as pl
from jax.experimental.pallas import tpu as pltpu

F32 = jnp.float32
BF16 = jnp.bfloat16
I32 = jnp.int32

DIFF_HEADS = 4
DIFF_DK = 64
RET_HEADS = 4
RET_DK = 64
RET_DV = 128
RET_CHUNK = 128
FOX_HEADS = 16
FOX_DH = 64
REL_BUCKETS = 32
REL_MAX_DIST = 128
N_GROUPS = 4
EXPERTS_PER_GROUP = 4
N_EXPERTS = 16
DEPTH = 2
DEEPNORM_ALPHA = (2 * DEPTH) ** 0.25
LN_EPS = 1e-5
ROPE_BASE = 10000.0
NEG_BIG = -1e30
LOG2E = math.log2(math.e)

VMEM_LIMIT_BYTES = 48 * 1024 * 1024
LANES = 128

PROJ_TM = 1024
ATT_TQ = 512
FOX_TQ = 512
ATT_RQ = LANES
DIFF_LOOKAHEAD = 3
FOX_LOOKAHEAD = 4
MOE_TR = 512
MOE_TM = 512
SORT_ALIGN = 16
ROUTE_ROWS = 8
ROUTE_GATE = 2
ROUTE_SLOT = 4
COMBINE_ROWS = 256
EXPERT_ROWS = 256
ROUTER_ROWS = 256


def _cparams(*sem):
    return pltpu.CompilerParams(dimension_semantics=sem, vmem_limit_bytes=VMEM_LIMIT_BYTES)


def _dot(a, b):
    return jnp.dot(a, b, preferred_element_type=F32)


def _dot_nt(a, b):
    return lax.dot_general(a, b, (((1,), (1,)), ((), ())), preferred_element_type=F32)


def _layer_norm_rows(z, g, b):
    mu = jnp.mean(z, axis=-1, keepdims=True)
    zc = z - mu
    var = jnp.mean(zc * zc, axis=-1, keepdims=True)
    return zc * lax.rsqrt(var + LN_EPS) * g + b


def _silu(x):
    return x * (1.0 / (1.0 + jnp.exp(-x)))


def _sigmoid(x):
    return 1.0 / (1.0 + jnp.exp(-x))


def _even_inproj_kernel(x_ref, w_ref, cos_ref, sin_ref,
                        qa_ref, ka_ref, va_ref, qb_ref, kb_ref, vb_ref, gb_ref):
    x = x_ref[...].astype(BF16)

    def mm(c0, c1):
        return _dot(x, w_ref[:, c0:c1].astype(BF16))

    qa_ref[...] = (mm(0, 512) * (DIFF_DK ** -0.5 * LOG2E)).astype(BF16)
    ka_ref[...] = mm(512, 1024).astype(BF16)
    va_ref[...] = mm(1024, 1536).astype(BF16)
    qk = mm(1536, 2048)
    cos = cos_ref[...]
    sin = sin_ref[...]
    lane = lax.broadcasted_iota(I32, cos.shape, 1)
    first_half = (lane % RET_DK) < (RET_DK // 2)

    def rot(t):
        sw = jnp.where(first_half, pltpu.roll(t, t.shape[1] - RET_DK // 2, 1),
                       pltpu.roll(t, RET_DK // 2, 1))
        return t * cos + sw * sin

    qb_ref[...] = rot(qk[:, :256]).astype(BF16)
    kb_ref[...] = (rot(qk[:, 256:]) * (RET_DK ** -0.5)).astype(BF16)
    vb_ref[...] = mm(2048, 2560).astype(BF16)
    gb_ref[...] = mm(2560, 3072).astype(BF16)


def _layer_weight_spec(w_all, layer):
    return pl.BlockSpec((None,) + w_all.shape[1:], lambda *_: (layer, 0, 0), pipeline_mode=pl.Buffered(1))


def _even_inproj(x2, w_all, layer, cos_t, sin_t, seq):
    t, d = x2.shape
    tm = min(PROJ_TM, seq)
    nblk_s = seq // tm
    widths = (512, 512, 512, 256, 256, 512, 512)
    row = lambda i: (i, 0)
    return pl.pallas_call(
        _even_inproj_kernel,
        grid=(t // tm,),
        in_specs=[
            pl.BlockSpec((tm, d), row),
            _layer_weight_spec(w_all, layer),
            pl.BlockSpec((tm, 256), lambda i: (i % nblk_s, 0)),
            pl.BlockSpec((tm, 256), lambda i: (i % nblk_s, 0)),
        ],
        out_specs=[pl.BlockSpec((tm, w), row) for w in widths],
        out_shape=[jax.ShapeDtypeStruct((t, w), BF16) for w in widths],
        compiler_params=_cparams("parallel"),
        name="even_inproj",
    )(x2, w_all, cos_t, sin_t)


def _run_chains(chains, lookahead, scores, finish):
    pending = [scores(c) for c in chains[:lookahead]]
    for n, chain in enumerate(chains):
        if n + lookahead < len(chains):
            pending.append(scores(chains[n + lookahead]))
        finish(chain, pending.pop(0))


def _diff_attn_kernel(lam_ref, q_ref, k_ref, v_ref, bias_ref, g_ref, o_ref, qm_ref, m_ref, l_ref, acc_ref,
                      *, lam_init, tq):
    seq = q_ref.shape[0]
    tk = tq
    rq = ATT_RQ
    nr = tq // rq
    lane = lax.broadcasted_iota(I32, (rq, LANES), 1)
    for n in range(seq // rq):
        q = q_ref[n * rq:(n + 1) * rq, :]
        zero = jnp.zeros_like(q)
        qm_ref[n, 0:rq, :] = jnp.where(lane < DIFF_DK, q, zero)
        qm_ref[n, rq:2 * rq, :] = jnp.where(lane >= DIFF_DK, q, zero)
    m_ref[...] = jnp.full(m_ref.shape, NEG_BIG, F32)
    l_ref[...] = jnp.zeros(l_ref.shape, F32)
    acc_ref[...] = jnp.zeros(acc_ref.shape, F32)
    lp = lam_ref[...]
    lam = (jnp.exp(jnp.sum(lp[0:1, :] * lp[1:2, :], axis=-1, keepdims=True))
           - jnp.exp(jnp.sum(lp[2:3, :] * lp[3:4, :], axis=-1, keepdims=True)) + lam_init)
    bias2 = [jnp.concatenate([bias_ref[n], bias_ref[n]], axis=0) for n in range(2)]

    chains = [(ii, j, r) for j in range(seq // tq) for ii in range(j, seq // tq) for r in range(nr)]

    def n_keys(ii, j, r):
        return (r + 1) * rq if j == ii else tk

    def scores(chain):
        ii, j, r = chain
        k = k_ref[j * tk:j * tk + n_keys(ii, j, r), :]
        return _dot_nt(qm_ref[ii * nr + r], k)

    def finish(chain, s):
        ii, j, r = chain
        g = ii * nr + r
        nk = n_keys(ii, j, r)
        v = v_ref[j * tk:j * tk + nk, :]
        sc = []
        for kc in range(nk // rq):
            t = s[:, kc * rq:(kc + 1) * rq]
            back = g - (j * nr + kc)
            if back <= 1:
                t = t + bias2[back]
            sc.append(t)
        mx = sc[0]
        for t in sc[1:]:
            mx = jnp.maximum(mx, t)
        m_old = m_ref[g]
        m_new = jnp.maximum(m_old, jnp.max(mx, axis=-1, keepdims=True))
        alpha = jnp.exp2(m_old - m_new)
        ps = [jnp.exp2(t - m_new) for t in sc]
        psum = ps[0]
        for t in ps[1:]:
            psum = psum + t
        l_ref[g] = alpha * l_ref[g] + psum
        p = jnp.concatenate([t.astype(BF16) for t in ps], axis=1)
        acc_ref[g] = alpha * acc_ref[g] + _dot(p, v)
        m_ref[g] = m_new
        if j == ii:
            l_all = jnp.sum(l_ref[g], axis=-1, keepdims=True)
            a = acc_ref[g] / l_all
            o = a[0:rq] - lam * a[rq:2 * rq]
            o = o * lax.rsqrt(jnp.mean(o * o, axis=-1, keepdims=True) + LN_EPS)
            o_ref[g * rq:(g + 1) * rq, :] = (o * g_ref[...] * (1.0 - lam_init)).astype(BF16)

    _run_chains(chains, DIFF_LOOKAHEAD, scores, finish)


def _diff_attention(qa, ka, va, bias_tab, lam_params, diff_g, lam_init):
    b, s, _ = qa.shape
    tq = min(ATT_TQ, s)
    kern = functools.partial(_diff_attn_kernel, lam_init=lam_init, tq=tq)
    seq_blk = lambda bi, h: (bi, 0, h)
    return pl.pallas_call(
        kern,
        grid=(b, DIFF_HEADS),
        in_specs=[
            pl.BlockSpec(lam_params.shape, lambda bi, h: (0, 0)),
            pl.BlockSpec((None, s, LANES), seq_blk),
            pl.BlockSpec((None, s, LANES), seq_blk),
            pl.BlockSpec((None, s, LANES), seq_blk),
            pl.BlockSpec((None, 2, ATT_RQ, ATT_RQ), lambda bi, h: (h, 0, 0, 0)),
            pl.BlockSpec((1, LANES), lambda bi, h: (0, 0)),
        ],
        out_specs=pl.BlockSpec((None, s, LANES), seq_blk),
        out_shape=jax.ShapeDtypeStruct((b, s, DIFF_HEADS * LANES), BF16),
        scratch_shapes=[pltpu.VMEM((s // ATT_RQ, 2 * ATT_RQ, LANES), BF16)]
        + [pltpu.VMEM((s // ATT_RQ, 2 * ATT_RQ, LANES), F32)] * 3,
        compiler_params=_cparams("parallel", "parallel"),
        name="diff_attention",
    )(lam_params, qa, ka, va, bias_tab, diff_g)


def _retention_kernel(q_ref, k_ref, v_ref, gate_ref, din_ref, qd_ref, kd_ref, cd_ref, g_ref, o_ref):
    s = q_ref.shape[0]
    c = RET_CHUNK
    lane = lax.broadcasted_iota(I32, (c, LANES), 1)
    g = g_ref[...]
    states = [jnp.zeros((LANES, RET_DV), F32) for _ in range(2)]
    for n in range(s // c):
        r = slice(n * c, (n + 1) * c)
        q_pair = q_ref[r, :].astype(F32)
        k_pair = k_ref[r, :].astype(F32)
        for par in range(2):
            own = (lane // RET_DK) == par
            cols = slice(par * RET_DV, (par + 1) * RET_DV)
            q = jnp.where(own, q_pair, 0.0)
            k = jnp.where(own, k_pair, 0.0)
            v = v_ref[r, cols]
            scores = _dot_nt(q.astype(BF16), k.astype(BF16)) * din_ref[par]
            inner = _dot(scores.astype(BF16), v)
            cross = _dot((q * qd_ref[par]).astype(BF16), states[par].astype(BF16))
            kv = _dot((k * kd_ref[par]).T.astype(BF16), v)
            states[par] = cd_ref[par] * states[par] + kv
            y = inner + cross
            mu = jnp.mean(y, axis=-1, keepdims=True)
            yc = y - mu
            var = jnp.mean(yc * yc, axis=-1, keepdims=True)
            yn = yc * lax.rsqrt(var + LN_EPS) * g
            gate = gate_ref[r, cols].astype(F32)
            o_ref[r, cols] = (_silu(gate) * yn).astype(BF16)


def _retention(qb, kb, vb, gb, tabs, ret_g):
    b, s, _ = qb.shape
    din, qd, kd, cd = tabs
    pair = lambda bi, hp: (bi, 0, hp)
    tab = lambda bi, hp: (hp, 0, 0)
    return pl.pallas_call(
        _retention_kernel,
        grid=(b, RET_HEADS // 2),
        in_specs=[
            pl.BlockSpec((None, s, LANES), pair),
            pl.BlockSpec((None, s, LANES), pair),
            pl.BlockSpec((None, s, 2 * RET_DV), pair),
            pl.BlockSpec((None, s, 2 * RET_DV), pair),
            pl.BlockSpec((2, RET_CHUNK, RET_CHUNK), tab),
            pl.BlockSpec((2, RET_CHUNK, LANES), tab),
            pl.BlockSpec((2, RET_CHUNK, LANES), tab),
            pl.BlockSpec((2, 1, LANES), tab),
            pl.BlockSpec((1, RET_DV), lambda bi, hp: (0, 0)),
        ],
        out_specs=pl.BlockSpec((None, s, 2 * RET_DV), pair),
        out_shape=jax.ShapeDtypeStruct((b, s, RET_HEADS * RET_DV), BF16),
        compiler_params=_cparams("parallel", "parallel"),
        name="retention",
    )(qb, kb, vb, gb, din, qd, kd, cd, ret_g)


def _odd_inproj_kernel(x_ref, w_ref, wf_ref, bf_ref, q_ref, k_ref, v_ref, cum_ref, cumt_ref, carry_ref, *, nblk_s):
    i = pl.program_id(0)
    x = x_ref[...].astype(BF16)
    d = q_ref.shape[1]
    q_ref[...] = (_dot(x, w_ref[:, 0:d].astype(BF16)) * (FOX_DH ** -0.5 * LOG2E)).astype(BF16)
    k_ref[...] = _dot(x, w_ref[:, d:2 * d].astype(BF16)).astype(BF16)
    v_ref[...] = _dot(x, w_ref[:, 2 * d:3 * d].astype(BF16)).astype(BF16)
    z = _dot(x, wf_ref[...]) + bf_ref[...]
    c = jnp.minimum(z, 0.0) - jnp.log1p(jnp.exp(-jnp.abs(z)))
    tm = c.shape[0]
    row = lax.broadcasted_iota(I32, c.shape, 0)
    step = 1
    while step < tm:
        c = c + jnp.where(row >= step, pltpu.roll(c, step, 0), 0.0)
        step *= 2

    @pl.when(i % nblk_s == 0)
    def _():
        carry_ref[...] = jnp.zeros_like(carry_ref)

    c = c + carry_ref[...]
    cum = c * LOG2E
    cum_ref[...] = cum
    cumt_ref[...] = cum.T[0:FOX_HEADS]
    carry_ref[...] = c[tm - 1:tm, :]


def _odd_inproj(x2, w_all, layer, wf_bf, bfg, seq):
    t, d = x2.shape
    tm = min(PROJ_TM, seq)
    nblk_s = seq // tm
    row = lambda i: (i, 0)
    kern = functools.partial(_odd_inproj_kernel, nblk_s=nblk_s)
    return pl.pallas_call(
        kern,
        grid=(t // tm,),
        in_specs=[
            pl.BlockSpec((tm, d), row),
            _layer_weight_spec(w_all, layer),
            pl.BlockSpec(wf_bf.shape, lambda i: (0, 0)),
            pl.BlockSpec(bfg.shape, lambda i: (0, 0)),
        ],
        out_specs=[pl.BlockSpec((tm, d), row)] * 3 + [
            pl.BlockSpec((tm, LANES), row),
            pl.BlockSpec((None, FOX_HEADS, tm), lambda i: (i // nblk_s, 0, i % nblk_s))],
        out_shape=[jax.ShapeDtypeStruct((t, d), BF16)] * 3 + [
            jax.ShapeDtypeStruct((t, LANES), F32), jax.ShapeDtypeStruct((t // seq, FOX_HEADS, seq), F32)],
        scratch_shapes=[pltpu.VMEM((1, LANES), F32)],
        compiler_params=_cparams("arbitrary"),
        name="odd_inproj",
    )(x2, w_all, wf_bf, bfg)


def _fox_attn_kernel(q_ref, k_ref, v_ref, cq_ref, ck_ref, o_ref, qm_ref, va_ref, cqc_ref, m_ref, acc_ref, *, tq):
    seq = q_ref.shape[0]
    tk = tq
    hp = pl.program_id(1)
    q = q_ref[...]
    v = v_ref[...]
    lane = lax.broadcasted_iota(I32, (seq, LANES), 1)
    cq_all = cq_ref[...]
    for par in range(2):
        own = (lane // FOX_DH) == par
        qm_ref[par] = jnp.where(own, q, jnp.zeros_like(q))
        va_ref[par] = jnp.where(own, v, jnp.ones_like(v))
        cq = jnp.sum(jnp.where(lane == 2 * hp + par, cq_all, 0.0), axis=-1, keepdims=True)
        cqc_ref[par] = jnp.broadcast_to(cq, (seq, LANES))
    m_ref[...] = jnp.full(m_ref.shape, NEG_BIG, F32)
    acc_ref[...] = jnp.zeros(acc_ref.shape, F32)
    rq = ATT_RQ
    upper = (lax.broadcasted_iota(I32, (rq, rq), 1) > lax.broadcasted_iota(I32, (rq, rq), 0))
    lane_q = lax.broadcasted_iota(I32, (rq, LANES), 1)

    def n_pieces(g):
        return -(-(g + 1) * rq // tk)

    n_chunks = seq // rq
    chains = [(g, kp, par) for kp in range(n_pieces(n_chunks - 1)) for g in range(n_chunks)
              if kp < n_pieces(g) for par in range(2)]

    def n_keys(g, kp):
        return min(tk, (g + 1) * rq - kp * tk)

    def scores(chain):
        g, kp, par = chain
        k = k_ref[kp * tk:kp * tk + n_keys(g, kp), :]
        return _dot_nt(qm_ref[par, g * rq:(g + 1) * rq, :], k)

    def finish(chain, s):
        g, kp, par = chain
        rows = slice(g * rq, (g + 1) * rq)
        nk = n_keys(g, kp)
        ck = ck_ref[par:par + 1, kp * tk:kp * tk + nk]
        cq = cqc_ref[par, rows, :]
        sc = []
        for kc in range(nk // rq):
            t = s[:, kc * rq:(kc + 1) * rq] - ck[:, kc * rq:(kc + 1) * rq]
            if kp * (tk // rq) + kc == g:
                t = jnp.where(upper, NEG_BIG, t)
            sc.append(t)
        mx = sc[0]
        for t in sc[1:]:
            mx = jnp.maximum(mx, t)
        m_old = m_ref[par, rows, :]
        m_new = jnp.maximum(m_old, jnp.max(mx, axis=-1, keepdims=True) + cq)
        alpha = jnp.exp2(m_old - m_new)
        shift = m_new - cq
        p = jnp.concatenate([jnp.exp2(t - shift).astype(BF16) for t in sc], axis=1)
        acc_ref[par, rows, :] = alpha * acc_ref[par, rows, :] + _dot(p, va_ref[par, kp * tk:kp * tk + nk, :])
        m_ref[par, rows, :] = m_new
        if kp == n_pieces(g) - 1 and par == 1:
            acc0 = acc_ref[0, rows, :]
            acc1 = acc_ref[1, rows, :]
            out0 = acc0 / acc0[:, FOX_DH:FOX_DH + 1]
            out1 = acc1 / acc1[:, 0:1]
            o_ref[rows, :] = jnp.where(lane_q < FOX_DH, out0, out1).astype(BF16)

    _run_chains(chains, FOX_LOOKAHEAD, scores, finish)


def _fox_attention(q, k, v, cum, cum_t):
    b, s, d = q.shape
    tq = min(FOX_TQ, s)
    npair = d // LANES
    seq_blk = lambda bi, h: (bi, 0, h)
    return pl.pallas_call(
        functools.partial(_fox_attn_kernel, tq=tq),
        grid=(b, npair),
        in_specs=[
            pl.BlockSpec((None, s, LANES), seq_blk),
            pl.BlockSpec((None, s, LANES), seq_blk),
            pl.BlockSpec((None, s, LANES), seq_blk),
            pl.BlockSpec((None, s, LANES), lambda bi, h: (bi, 0, 0)),
            pl.BlockSpec((None, None, 2, s), lambda bi, h: (bi, h, 0, 0)),
        ],
        out_specs=pl.BlockSpec((None, s, LANES), seq_blk),
        out_shape=jax.ShapeDtypeStruct((b, s, d), BF16),
        scratch_shapes=[pltpu.VMEM((2, s, LANES), BF16), pltpu.VMEM((2, s, LANES), BF16),
                        pltpu.VMEM((2, s, LANES), F32), pltpu.VMEM((2, s, LANES), F32),
                        pltpu.VMEM((2, s, LANES), F32)],
        compiler_params=_cparams("parallel", "parallel"),
        name="fox_attention",
    )(q, k, v, cum, cum_t)


def _outproj_router_kernel(*refs, n_y):
    y_refs = refs[:n_y]
    (w_ref, x_ref, g_ref, b_ref, rwt_ref, tri_ref, h_ref, route_ref, col_ref, cnt_ref, xs_ref,
     hbf_ref) = refs[n_y:]
    rw2 = rwt_ref[...]
    rw_hi = rw2.astype(BF16)
    rw_lo = (rw2 - rw_hi.astype(F32)).astype(BF16)
    lane_w = lax.broadcasted_iota(I32, rw2.shape, 1)
    w = jnp.where(lane_w < N_EXPERTS, rw_hi, rw_lo)
    tm = x_ref.shape[0]
    rc = ROUTER_ROWS
    logit_chunks = []

    def project(c):
        rows = slice(c * rc, (c + 1) * rc)
        mix = None
        k0 = 0
        for yr in y_refs:
            k1 = k0 + yr.shape[1]
            part = _dot(yr[rows, :], w_ref[k0:k1, :].astype(BF16))
            mix = part if mix is None else mix + part
            k0 = k1
        return mix

    def norm_and_logits(c, mix):
        rows = slice(c * rc, (c + 1) * rc)
        h = _layer_norm_rows(DEEPNORM_ALPHA * x_ref[rows, :] + mix, g_ref[...], b_ref[...])
        h_ref[rows, :] = h
        h_hi = h.astype(BF16)
        hbf_ref[rows, :] = h_hi
        h_lo = (h - h_hi.astype(F32)).astype(BF16)
        p_hi = _dot(h_hi, w)
        p_lo = _dot(h_lo, w)
        slab = p_hi + (pltpu.roll(p_hi, LANES - N_EXPERTS, 1) + p_lo)
        logit_chunks.append(slab.T[0:N_EXPERTS])

    _run_chains(list(range(tm // rc)), 1, project, norm_and_logits)

    logits = jnp.concatenate(logit_chunks, axis=1)
    row = lax.broadcasted_iota(I32, (N_EXPERTS, tm), 0)
    mx = jnp.max(logits, axis=0, keepdims=True)
    ex = jnp.exp(logits - mx)
    probs = ex / jnp.sum(ex, axis=0, keepdims=True)
    grp = row // EXPERTS_PER_GROUP

    def top2(vals):
        v1 = jnp.max(vals, axis=0, keepdims=True)
        i1 = jnp.min(jnp.where(vals == v1, row, N_EXPERTS), axis=0, keepdims=True)
        rest = jnp.where(row == i1, -2.0, vals)
        v2 = jnp.max(rest, axis=0, keepdims=True)
        i2 = jnp.min(jnp.where(rest == v2, row, N_EXPERTS), axis=0, keepdims=True)
        return v1, i1, v2, i2

    best_score = None
    best = None
    for gi in range(N_GROUPS):
        v1, _, v2, _ = top2(jnp.where(grp == gi, probs, -1.0))
        score = v1 + v2
        if gi == 0:
            best_score, best = score, jnp.zeros_like(score, dtype=I32)
        else:
            better = score > best_score
            best = jnp.where(better, gi, best)
            best_score = jnp.where(better, score, best_score)
    v1, i1, v2, i2 = top2(jnp.where(grp == best, probs, -1.0))
    tot = v1 + v2
    g1 = v1 / tot
    g2 = v2 / tot

    onehot = jnp.where((row == i1) | (row == i2), 1.0, 0.0)
    pref = _dot(onehot.astype(BF16), tri_ref[...])
    cnt = jnp.broadcast_to(jnp.sum(onehot, axis=1, keepdims=True), cnt_ref.shape)
    cnt_ref[...] = cnt
    grp_rows = jnp.floor((cnt + (SORT_ALIGN - 1)) * (1.0 / SORT_ALIGN)) * SORT_ALIGN
    row_c = lax.broadcasted_iota(I32, cnt.shape, 0)
    start = grp_rows
    step = 1
    while step < N_EXPERTS:
        start = start + jnp.where(row_c >= step, pltpu.roll(start, step, 0), 0.0)
        step *= 2
    start = (start - grp_rows)[:, 0:1]
    s1 = jnp.sum(jnp.where(row == i1, pref + start, 0.0), axis=0, keepdims=True)
    s2 = jnp.sum(jnp.where(row == i2, pref + start, 0.0), axis=0, keepdims=True)

    row8 = lax.broadcasted_iota(I32, (ROUTE_ROWS, tm), 0)
    fields = (i1.astype(F32), i2.astype(F32), g1, g2, s1, s2)
    route = jnp.zeros((ROUTE_ROWS, tm), F32)
    for n, f in enumerate(fields):
        route = jnp.where(row8 == n, f, route)
    route_ref[...] = route
    col_ref[...] = jnp.concatenate([route, jnp.zeros((LANES - ROUTE_ROWS, tm), F32)], axis=0).T

    srow = lax.broadcasted_iota(I32, (xs_ref.shape[0], tm), 0).astype(F32)
    perm = jnp.where((srow == s1) | (srow == s2), 1.0, 0.0).astype(BF16)
    xs_ref[...] = _dot(perm, hbf_ref[...]).astype(BF16)


def _outproj_router(ys, w_all, layer, x2, ln_g, ln_b, rwt, tri):
    t, d = x2.shape
    tm = tri.shape[0]
    n_local = _local_rows(tm)
    nt = t // tm
    row = lambda i: (i, 0)
    full = lambda i: (0, 0)
    n_y = len(ys)
    kern = functools.partial(_outproj_router_kernel, n_y=n_y)
    return pl.pallas_call(
        kern,
        grid=(nt,),
        in_specs=([pl.BlockSpec((tm, y.shape[1]), row) for y in ys]
                  + [_layer_weight_spec(w_all, layer)]
                  + [pl.BlockSpec((tm, d), row), pl.BlockSpec((1, d), full), pl.BlockSpec((1, d), full),
                     pl.BlockSpec(rwt.shape, full), pl.BlockSpec(tri.shape, full)]),
        out_specs=[pl.BlockSpec((tm, d), row), pl.BlockSpec((ROUTE_ROWS, tm), lambda i: (0, i)),
                   pl.BlockSpec((tm, LANES), row), pl.BlockSpec((N_EXPERTS, LANES), row),
                   pl.BlockSpec((n_local, d), row)],
        out_shape=[jax.ShapeDtypeStruct((t, d), F32), jax.ShapeDtypeStruct((ROUTE_ROWS, t), F32),
                   jax.ShapeDtypeStruct((t, LANES), F32), jax.ShapeDtypeStruct((nt * N_EXPERTS, LANES), F32),
                   jax.ShapeDtypeStruct((nt * n_local, d), BF16)],
        scratch_shapes=[pltpu.VMEM((tm, d), BF16)],
        compiler_params=_cparams("parallel"),
        name="outproj_router",
    )(*ys, w_all, x2, ln_g, ln_b, rwt, tri)


def _start_tile_gather(tile, loff_ref, rows_ref, gpos_ref, sorted_hbm, local_ref, sem):
    for e in range(N_EXPERTS):
        n = tile * N_EXPERTS + e

        @pl.when(rows_ref[n] > 0)
        def _(n=n):
            size = pl.multiple_of(rows_ref[n], SORT_ALIGN)
            src = sorted_hbm.at[pl.ds(pl.multiple_of(gpos_ref[n], SORT_ALIGN), size)]
            dst = local_ref.at[pl.ds(pl.multiple_of(loff_ref[n], SORT_ALIGN), size)]
            pltpu.make_async_copy(src, dst, sem).start()


def _wait_rows(n_rows, src_hbm, dst_ref, sem):
    n = pl.multiple_of(n_rows, SORT_ALIGN)
    pltpu.make_async_copy(src_hbm.at[pl.ds(0, n)], dst_ref.at[pl.ds(0, n)], sem).wait()


def _expert_kernel(te_ref, nx_ref, nt_ref, g0_ref, g1_ref, valid_ref, loff_ref, rows_ref, gpos_ref,
                   xs_hbm, wg_hbm, wu_hbm, wd_hbm, o_ref,
                   xbuf, semx, wgs, wus, wds, wgb, wub, wdb, sem, nsw_ref, *, layer, n_local):
    r = pl.program_id(0)
    nt = nt_ref[0]
    cur = te_ref[r]
    tr = xbuf.shape[1]
    xslot = r % 2

    def start_pieces(tile, dst_slot):
        e = te_ref[tile]
        base = tile * tr

        def piece(tau, c):
            n = tau * N_EXPERTS + e
            lo = jnp.maximum(gpos_ref[n], base)
            hi = jnp.minimum(gpos_ref[n] + rows_ref[n], base + tr)

            @pl.when(hi > lo)
            def _():
                size = pl.multiple_of(hi - lo, SORT_ALIGN)
                src = pl.multiple_of(tau * n_local + loff_ref[n] + (lo - gpos_ref[n]), SORT_ALIGN)
                dst = pl.multiple_of(lo - base, SORT_ALIGN)
                pltpu.make_async_copy(xs_hbm.at[pl.ds(src, size)], xbuf.at[dst_slot, pl.ds(dst, size)],
                                      semx.at[dst_slot]).start()

            return c

        lax.fori_loop(g0_ref[tile], g1_ref[tile] + 1, piece, 0)

    @pl.when(r == 0)
    def _():
        xbuf[...] = jnp.zeros_like(xbuf)
        start_pieces(0, 0)

    @pl.when(r + 1 < nt)
    def _():
        start_pieces(r + 1, 1 - xslot)

    def weight_copies(e, slot):
        return [pltpu.make_async_copy(src.at[layer, e], dst.at[slot], sem.at[slot])
                for src, dst in ((wg_hbm, wgs), (wu_hbm, wus), (wd_hbm, wds))]

    @pl.when(r == 0)
    def _():
        nsw_ref[0] = 0
        for cp in weight_copies(cur, 0):
            cp.start()

    @pl.when((r == 0) | (cur != te_ref[jnp.maximum(r - 1, 0)]))
    def _():
        slot = nsw_ref[0] % 2
        nsw_ref[0] = nsw_ref[0] + 1
        for cp in weight_copies(cur, slot):
            cp.wait()

        @pl.when(nx_ref[r] != cur)
        def _():
            for cp in weight_copies(nx_ref[r], 1 - slot):
                cp.start()

        wgb[...] = wgs[slot].astype(BF16)
        wub[...] = wus[slot].astype(BF16)
        wdb[...] = wds[slot].astype(BF16)

    @pl.when(r < nt)
    def _():
        _wait_rows(valid_ref[r], xs_hbm, xbuf.at[xslot], semx.at[xslot])
        rc = EXPERT_ROWS

        def up(c):
            x = xbuf[xslot, c * rc:(c + 1) * rc, :]
            return _dot(x, wgb[...]), _dot(x, wub[...])

        def down(c, gate_up):
            a = _silu(gate_up[0]) * gate_up[1]
            o_ref[c * rc:(c + 1) * rc, :] = _dot(a.astype(BF16), wdb[...]).astype(BF16)

        _run_chains(list(range(tr // rc)), 1, up, down)

    @pl.when(r >= nt)
    def _():
        o_ref[...] = jnp.zeros_like(o_ref)


def _expert_mlps(meta, n_slots, xs_local, wg, wu, wd, layer):
    d = xs_local.shape[1]
    tr = MOE_TR
    dff = wg.shape[3]
    grid_spec = pltpu.PrefetchScalarGridSpec(
        num_scalar_prefetch=9,
        grid=(n_slots // tr,),
        in_specs=[pl.BlockSpec(memory_space=pl.ANY)] * 4,
        out_specs=pl.BlockSpec((tr, d), lambda r, *_: (r, 0)),
        scratch_shapes=[pltpu.VMEM((2, tr, d), BF16), pltpu.SemaphoreType.DMA((2,)),
                        pltpu.VMEM((2, d, dff), F32), pltpu.VMEM((2, d, dff), F32), pltpu.VMEM((2, dff, d), F32),
                        pltpu.VMEM((d, dff), BF16), pltpu.VMEM((d, dff), BF16), pltpu.VMEM((dff, d), BF16),
                        pltpu.SemaphoreType.DMA((2,)), pltpu.SMEM((1,), I32)],
    )
    return pl.pallas_call(
        functools.partial(_expert_kernel, layer=layer, n_local=_local_rows(MOE_TM)),
        grid_spec=grid_spec,
        out_shape=jax.ShapeDtypeStruct((n_slots, d), BF16),
        compiler_params=_cparams("arbitrary"),
        name="expert_mlps",
    )(meta["tile_expert"], meta["next_expert"], meta["n_tiles"], meta["first_group"], meta["last_group"],
      meta["valid"], meta["loff"], meta["rows"], meta["gpos"], xs_local, wg, wu, wd)


def _combine_kernel(loff_ref, rows_ref, gpos_ref, tot_ref, ys_hbm, h_ref, col_ref, p_ref, g_ref, b_ref,
                    pg_ref, pp_ref, o_ref, ybuf, sem):
    i = pl.program_id(0)
    n = pl.num_programs(0)
    tm = h_ref.shape[0]
    n_local = ybuf.shape[1]
    slot = i % 2

    @pl.when(i == 0)
    def _():
        ybuf[...] = jnp.zeros_like(ybuf)
        _start_tile_gather(0, loff_ref, rows_ref, gpos_ref, ys_hbm, ybuf.at[0], sem.at[0])

    @pl.when(i + 1 < n)
    def _():
        _start_tile_gather(i + 1, loff_ref, rows_ref, gpos_ref, ys_hbm, ybuf.at[1 - slot], sem.at[1 - slot])

    _wait_rows(tot_ref[i], ys_hbm, ybuf.at[slot], sem.at[slot])
    rc = COMBINE_ROWS
    scol = lax.broadcasted_iota(I32, (rc, n_local), 1).astype(F32)

    def gather_rows(c):
        rows = slice(c * rc, (c + 1) * rc)
        cols = col_ref[rows, :]
        pick = jnp.zeros((rc, n_local), F32)
        for k in range(2):
            pick = jnp.where(scol == cols[:, ROUTE_SLOT + k:ROUTE_SLOT + k + 1],
                             cols[:, ROUTE_GATE + k:ROUTE_GATE + k + 1], pick)
        ffn = _dot(pick.astype(BF16), ybuf[slot])
        return ffn, _dot(p_ref[rows, :].astype(BF16), pp_ref[...].astype(BF16))

    def finish(c, gathered):
        rows = slice(c * rc, (c + 1) * rc)
        ffn, pe = gathered
        h2 = _layer_norm_rows(DEEPNORM_ALPHA * h_ref[rows, :] + ffn, g_ref[...], b_ref[...])
        gate = _sigmoid(_dot(h2.astype(BF16), pg_ref[...].astype(BF16)))
        o_ref[rows, :] = h2 + gate * pe

    _run_chains(list(range(tm // rc)), 1, gather_rows, finish)


def _combine(meta, ys, h, cols, p3, layer, ln_g, ln_b, ple_gate, ple_proj):
    t, d = h.shape
    tm = MOE_TM
    pdim = p3.shape[2]
    row = lambda i, *_: (i, 0)
    full = lambda i, *_: (0, 0)
    grid_spec = pltpu.PrefetchScalarGridSpec(
        num_scalar_prefetch=4,
        grid=(t // tm,),
        in_specs=[
            pl.BlockSpec(memory_space=pl.ANY),
            pl.BlockSpec((tm, d), row),
            pl.BlockSpec((tm, LANES), row),
            pl.BlockSpec((None, tm, pdim), lambda i, *_: (layer, i, 0)),
            pl.BlockSpec((1, d), full),
            pl.BlockSpec((1, d), full),
            pl.BlockSpec((None,) + ple_gate.shape[1:], lambda i, *_: (layer, 0, 0), pipeline_mode=pl.Buffered(1)),
            pl.BlockSpec((None,) + ple_proj.shape[1:], lambda i, *_: (layer, 0, 0), pipeline_mode=pl.Buffered(1)),
        ],
        out_specs=pl.BlockSpec((tm, d), row),
        scratch_shapes=[pltpu.VMEM((2, _local_rows(tm), d), BF16), pltpu.SemaphoreType.DMA((2,))],
    )
    return pl.pallas_call(
        _combine_kernel,
        grid_spec=grid_spec,
        out_shape=jax.ShapeDtypeStruct((t, d), F32),
        compiler_params=_cparams("arbitrary"),
        name="moe_combine",
    )(meta["loff"], meta["rows"], meta["gpos"], meta["tot"], ys, h, cols, p3, ln_g, ln_b, ple_gate, ple_proj)


def _rotary_tables(seq):
    half = RET_DK // 2
    inv = (np.float32(ROPE_BASE) ** (-np.arange(half, dtype=np.float32) / np.float32(half))).astype(np.float32)
    ang = (np.arange(seq, dtype=np.float32)[:, None] * inv[None, :]).astype(np.float32)
    cos = np.cos(ang.astype(np.float64))
    sin = np.sin(ang.astype(np.float64))
    cos_h = np.concatenate([cos, cos], axis=1)
    sin_h = np.concatenate([-sin, sin], axis=1)
    return (jnp.asarray(np.tile(cos_h, (1, RET_HEADS)), F32), jnp.asarray(np.tile(sin_h, (1, RET_HEADS)), F32))


def _retention_tables():
    c = RET_CHUNK
    h = np.arange(RET_HEADS, dtype=np.float64)
    log_g = np.log1p(-np.exp2(-5.0 - h))
    j = np.arange(c, dtype=np.float64)
    rel = j[:, None] - j[None, :]
    din = np.where(rel >= 0, np.exp(np.maximum(rel, 0.0)[None] * log_g[:, None, None]), 0.0)
    qd = np.exp((j + 1.0)[None] * log_g[:, None])
    kd = np.exp((c - 1.0 - j)[None] * log_g[:, None])
    cd = np.exp(c * log_g)
    qd = np.broadcast_to(qd[:, :, None], (RET_HEADS, c, LANES))
    kd = np.broadcast_to(kd[:, :, None], (RET_HEADS, c, LANES))
    cd = np.broadcast_to(cd[:, None, None], (RET_HEADS, 1, LANES))
    return tuple(jnp.asarray(a, F32) for a in (din, qd, kd, cd))


def _t5_bucket_np(dist):
    max_exact = REL_BUCKETS // 2
    d = np.maximum(dist, 1).astype(np.float32)
    large = max_exact + (np.log(d / np.float32(max_exact)) / np.float32(math.log(REL_MAX_DIST / max_exact))
                         * np.float32(REL_BUCKETS - max_exact)).astype(np.int32)
    large = np.minimum(large, REL_BUCKETS - 1)
    return np.where(dist < max_exact, dist, large)


def _diff_bias_tables(rel_bias, seq):
    c = ATT_RQ
    r = np.arange(c)
    dist0 = r[:, None] - r[None, :]
    far = REL_BUCKETS - 1
    assert np.all(_t5_bucket_np(np.arange(c + 1, max(seq, 2 * c))) == far)
    bidx = np.stack([_t5_bucket_np(np.maximum(dist0, 0)), _t5_bucket_np(dist0 + c)])
    rb = rel_bias.astype(F32).T
    shifted = (rb - rb[:, far:far + 1]) * LOG2E
    bidx = jnp.asarray(bidx, I32)[None]
    tab = jnp.zeros((rb.shape[0], 2, c, c), F32)
    for bucket in range(REL_BUCKETS - 1):
        tab = jnp.where(bidx == bucket, shifted[:, bucket][:, None, None, None], tab)
    causal = jnp.asarray(np.stack([dist0 >= 0, np.ones_like(dist0, bool)]))[None]
    return jnp.where(causal, tab, NEG_BIG)


def _local_rows(tm):
    need = 2 * tm + N_EXPERTS * (SORT_ALIGN - 1)
    return -(-need // LANES) * LANES


def _round_up(a, m):
    return ((a + m - 1) // m) * m


def _route_meta(cnt, t):
    tm, tr = MOE_TM, MOE_TR
    nt = t // tm
    counts = cnt.reshape(nt, N_EXPERTS, LANES)[:, :, 0].astype(I32)
    rows = _round_up(counts, SORT_ALIGN)
    loff = jnp.cumsum(rows, axis=1) - rows
    seg = jnp.sum(rows, axis=0)
    seg_pad = _round_up(seg, tr)
    ends = jnp.cumsum(seg_pad)
    offs = ends - seg_pad
    gpos = offs[None, :] + jnp.cumsum(rows, axis=0) - rows
    n_slots = 2 * t + nt * N_EXPERTS * (SORT_ALIGN - 1)
    n_slots = _round_up(n_slots, tr) + N_EXPERTS * tr
    n_tiles = (ends[-1] // tr).astype(I32)
    tile_start = jnp.arange(n_slots // tr, dtype=I32) * tr
    tile_expert = jnp.sum((tile_start[:, None] >= ends[None, :]).astype(I32), axis=1)
    last = jnp.sum((((n_tiles - 1) * tr) >= ends).astype(I32))
    tile_expert = jnp.minimum(tile_expert, last).astype(I32)
    eid = jnp.arange(N_EXPERTS, dtype=I32)
    later = (eid[None, :] > eid[:, None]) & (seg_pad > 0)[None, :]
    nxt = jnp.min(jnp.where(later, eid[None, :], N_EXPERTS), axis=1)
    nxt = jnp.where(nxt == N_EXPERTS, eid, nxt)
    next_expert = jnp.sum(jnp.where(tile_expert[:, None] == eid[None, :], nxt[None, :], 0), axis=1).astype(I32)
    own = tile_expert[:, None] == eid[None, :]
    pick = lambda tab: jnp.sum(jnp.where(own[:, None, :], tab[None, :, :], 0), axis=-1)
    g_start, g_end = pick(gpos), pick(gpos + rows)
    used = (tile_start < n_tiles * tr)[:, None]
    first_group = jnp.sum((g_end <= tile_start[:, None]).astype(I32), axis=1)
    last_group = jnp.sum(((g_start < tile_start[:, None] + tr) & used).astype(I32), axis=1) - 1
    seg_end = jnp.sum(jnp.where(own, (offs + seg)[None, :], 0), axis=1)
    valid = jnp.clip(seg_end - tile_start, 0, tr)
    meta = {
        "loff": loff.reshape(-1).astype(I32), "rows": rows.reshape(-1).astype(I32),
        "gpos": gpos.reshape(-1).astype(I32), "tot": jnp.sum(rows, axis=1).astype(I32),
        "tile_expert": tile_expert, "next_expert": next_expert, "n_tiles": n_tiles.reshape(1),
        "first_group": first_group.astype(I32), "last_group": last_group.astype(I32), "valid": valid.astype(I32),
    }
    return meta, n_slots


def kernel(x, p, rel_bias, router_w, even_w_in, even_w_out, even_lambda, even_diff_norm, even_ret_norm,
           odd_w_in, odd_b_forget, odd_w_out, ln_mix_g, ln_mix_b, ln_ffn_g, ln_ffn_b,
           moe_w_gate, moe_w_up, moe_w_down, ple_proj, ple_gate):
    b, s, d = x.shape
    t = b * s
    assert d == 1024 and p.shape[0] == DEPTH and even_w_in.shape[2] == 3072
    assert odd_w_in.shape[2] == 3 * d + FOX_HEADS and moe_w_gate.shape[1] == N_EXPERTS
    assert s % RET_CHUNK == 0 and s % min(ATT_TQ, s) == 0 and s % min(FOX_TQ, s) == 0
    assert t % min(PROJ_TM, s) == 0 and t % MOE_TM == 0

    cos_t, sin_t = _rotary_tables(s)
    ret_tabs = _retention_tables()
    bias_tab = _diff_bias_tables(rel_bias, s)
    rw32 = router_w.astype(F32)
    rwt = jnp.zeros((d, LANES), F32).at[:, :N_EXPERTS].set(rw32).at[:, N_EXPERTS:2 * N_EXPERTS].set(rw32)
    tok = np.arange(MOE_TM)
    tri = jnp.asarray(tok[:, None] < tok[None, :], BF16)

    x2 = x.reshape(t, d)
    for i in range(DEPTH):
        j = i // 2
        if i % 2 == 0:
            lam_init = 0.8 - 0.6 * math.exp(-0.3 * i)
            qa, ka, va, qb, kb, vb, gb = _even_inproj(x2, even_w_in, j, cos_t, sin_t, s)
            sh = lambda a: a.reshape(b, s, a.shape[1])
            ya = _diff_attention(sh(qa), sh(ka), sh(va), bias_tab, even_lambda[j].astype(F32),
                                 even_diff_norm[j].reshape(1, -1).astype(F32), lam_init)
            yb = _retention(sh(qb), sh(kb), sh(vb), sh(gb), ret_tabs, even_ret_norm[j].reshape(1, -1).astype(F32))
            ys = [ya.reshape(t, -1), yb.reshape(t, -1)]
            w_out = even_w_out
        else:
            wf = jnp.zeros((d, LANES), BF16).at[:, :FOX_HEADS].set(odd_w_in[j, :, 3 * d:].astype(BF16))
            bfg = jnp.zeros((1, LANES), F32).at[0, :FOX_HEADS].set(odd_b_forget[j].astype(F32))
            q, k, v, cum, cum_t = _odd_inproj(x2, odd_w_in, j, wf, bfg, s)
            cum3 = cum.reshape(b, s, LANES)
            cum_t = cum_t.reshape(b, FOX_HEADS // 2, 2, s)
            y = _fox_attention(q.reshape(b, s, d), k.reshape(b, s, d), v.reshape(b, s, d), cum3, cum_t)
            ys = [y.reshape(t, d)]
            w_out = odd_w_out
        h, route, cols, cnt, xs_local = _outproj_router(ys, w_out, j, x2, ln_mix_g[i].reshape(1, d),
                                                        ln_mix_b[i].reshape(1, d), rwt, tri)
        meta, n_slots = _route_meta(cnt, t)
        rows = _expert_mlps(meta, n_slots, xs_local, moe_w_gate, moe_w_up, moe_w_down, i)
        x2 = _combine(meta, rows, h, cols, p.reshape(DEPTH, t, -1), i, ln_ffn_g[i].reshape(1, d),
                      ln_ffn_b[i].reshape(1, d), ple_gate, ple_proj)
    return x2.reshape(b, s, d)
```

```python
import functools
import math

import numpy as np
import jax
import jax.numpy as jnp
from jax import lax
from jax.experimental import pallas as pl
from jax.experimental.pallas import tpu as pltpu

F32 = jnp.float32
BF16 = jnp.bfloat16
I32 = jnp.int32

DIFF_HEADS = 4
DIFF_DK = 64
RET_HEADS = 4
RET_DK = 64
RET_DV = 128
RET_CHUNK = 128
FOX_HEADS = 16
FOX_DH = 64
REL_BUCKETS = 32
REL_MAX_DIST = 128
N_GROUPS = 4
EXPERTS_PER_GROUP = 4
N_EXPERTS = 16
DEPTH = 2
DEEPNORM_ALPHA = (2 * DEPTH) ** 0.25
LN_EPS = 1e-5
ROPE_BASE = 10000.0
NEG_BIG = -1e30
LOG2E = math.log2(math.e)

VMEM_LIMIT_BYTES = 48 * 1024 * 1024
LANES = 128

PROJ_TM = 1024
ATT_TQ = 512
FOX_TQ = 512
ATT_RQ = LANES
DIFF_LOOKAHEAD = 3
FOX_LOOKAHEAD = 4
MOE_TR = 512
MOE_TM = 512
SORT_ALIGN = 16
ROUTE_ROWS = 8
ROUTE_GATE = 2
ROUTE_SLOT = 4
COMBINE_ROWS = 256
EXPERT_ROWS = 256
ROUTER_ROWS = 256


def _cparams(*sem):
    return pltpu.CompilerParams(dimension_semantics=sem, vmem_limit_bytes=VMEM_LIMIT_BYTES)


def _dot(a, b):
    return jnp.dot(a, b, preferred_element_type=F32)


def _dot_nt(a, b):
    return lax.dot_general(a, b, (((1,), (1,)), ((), ())), preferred_element_type=F32)


def _layer_norm_rows(z, g, b):
    mu = jnp.mean(z, axis=-1, keepdims=True)
    zc = z - mu
    var = jnp.mean(zc * zc, axis=-1, keepdims=True)
    return zc * lax.rsqrt(var + LN_EPS) * g + b


def _silu(x):
    return x * (1.0 / (1.0 + jnp.exp(-x)))


def _sigmoid(x):
    return 1.0 / (1.0 + jnp.exp(-x))


def _even_inproj_kernel(x_ref, w_ref, cos_ref, sin_ref,
                        qa_ref, ka_ref, va_ref, qb_ref, kb_ref, vb_ref, gb_ref):
    x = x_ref[...].astype(BF16)

    def mm(c0, c1):
        return _dot(x, w_ref[:, c0:c1].astype(BF16))

    qa_ref[...] = (mm(0, 512) * (DIFF_DK ** -0.5 * LOG2E)).astype(BF16)
    ka_ref[...] = mm(512, 1024).astype(BF16)
    va_ref[...] = mm(1024, 1536).astype(BF16)
    qk = mm(1536, 2048)
    cos = cos_ref[...]
    sin = sin_ref[...]
    lane = lax.broadcasted_iota(I32, cos.shape, 1)
    first_half = (lane % RET_DK) < (RET_DK // 2)

    def rot(t):
        sw = jnp.where(first_half, pltpu.roll(t, t.shape[1] - RET_DK // 2, 1),
                       pltpu.roll(t, RET_DK // 2, 1))
        return t * cos + sw * sin

    qb_ref[...] = rot(qk[:, :256]).astype(BF16)
    kb_ref[...] = (rot(qk[:, 256:]) * (RET_DK ** -0.5)).astype(BF16)
    vb_ref[...] = mm(2048, 2560).astype(BF16)
    gb_ref[...] = mm(2560, 3072).astype(BF16)


def _layer_weight_spec(w_all, layer):
    return pl.BlockSpec((None,) + w_all.shape[1:], lambda *_: (layer, 0, 0), pipeline_mode=pl.Buffered(1))


def _even_inproj(x2, w_all, layer, cos_t, sin_t, seq):
    t, d = x2.shape
    tm = min(PROJ_TM, seq)
    nblk_s = seq // tm
    widths = (512, 512, 512, 256, 256, 512, 512)
    row = lambda i: (i, 0)
    return pl.pallas_call(
        _even_inproj_kernel,
        grid=(t // tm,),
        in_specs=[
            pl.BlockSpec((tm, d), row),
            _layer_weight_spec(w_all, layer),
            pl.BlockSpec((tm, 256), lambda i: (i % nblk_s, 0)),
            pl.BlockSpec((tm, 256), lambda i: (i % nblk_s, 0)),
        ],
        out_specs=[pl.BlockSpec((tm, w), row) for w in widths],
        out_shape=[jax.ShapeDtypeStruct((t, w), BF16) for w in widths],
        compiler_params=_cparams("parallel"),
        name="even_inproj",
    )(x2, w_all, cos_t, sin_t)


def _run_chains(chains, lookahead, scores, finish):
    pending = [scores(c) for c in chains[:lookahead]]
    for n, chain in enumerate(chains):
        if n + lookahead < len(chains):
            pending.append(scores(chains[n + lookahead]))
        finish(chain, pending.pop(0))


def _diff_attn_kernel(lam_ref, q_ref, k_ref, v_ref, bias_ref, g_ref, o_ref, qm_ref, m_ref, l_ref, acc_ref,
                      *, lam_init, tq):
    seq = q_ref.shape[0]
    tk = tq
    rq = ATT_RQ
    nr = tq // rq
    lane = lax.broadcasted_iota(I32, (rq, LANES), 1)
    for n in range(seq // rq):
        q = q_ref[n * rq:(n + 1) * rq, :]
        zero = jnp.zeros_like(q)
        qm_ref[n, 0:rq, :] = jnp.where(lane < DIFF_DK, q, zero)
        qm_ref[n, rq:2 * rq, :] = jnp.where(lane >= DIFF_DK, q, zero)
    m_ref[...] = jnp.full(m_ref.shape, NEG_BIG, F32)
    l_ref[...] = jnp.zeros(l_ref.shape, F32)
    acc_ref[...] = jnp.zeros(acc_ref.shape, F32)
    lp = lam_ref[...]
    lam = (jnp.exp(jnp.sum(lp[0:1, :] * lp[1:2, :], axis=-1, keepdims=True))
           - jnp.exp(jnp.sum(lp[2:3, :] * lp[3:4, :], axis=-1, keepdims=True)) + lam_init)
    bias2 = [jnp.concatenate([bias_ref[n], bias_ref[n]], axis=0) for n in range(2)]

    chains = [(ii, j, r) for j in range(seq // tq) for ii in range(j, seq // tq) for r in range(nr)]

    def n_keys(ii, j, r):
        return (r + 1) * rq if j == ii else tk

    def scores(chain):
        ii, j, r = chain
        k = k_ref[j * tk:j * tk + n_keys(ii, j, r), :]
        return _dot_nt(qm_ref[ii * nr + r], k)

    def finish(chain, s):
        ii, j, r = chain
        g = ii * nr + r
        nk = n_keys(ii, j, r)
        v = v_ref[j * tk:j * tk + nk, :]
        sc = []
        for kc in range(nk // rq):
            t = s[:, kc * rq:(kc + 1) * rq]
            back = g - (j * nr + kc)
            if back <= 1:
                t = t + bias2[back]
            sc.append(t)
        mx = sc[0]
        for t in sc[1:]:
            mx = jnp.maximum(mx, t)
        m_old = m_ref[g]
        m_new = jnp.maximum(m_old, jnp.max(mx, axis=-1, keepdims=True))
        alpha = jnp.exp2(m_old - m_new)
        ps = [jnp.exp2(t - m_new) for t in sc]
        psum = ps[0]
        for t in ps[1:]:
            psum = psum + t
        l_ref[g] = alpha * l_ref[g] + psum
        p = jnp.concatenate([t.astype(BF16) for t in ps], axis=1)
        acc_ref[g] = alpha * acc_ref[g] + _dot(p, v)
        m_ref[g] = m_new
        if j == ii:
            l_all = jnp.sum(l_ref[g], axis=-1, keepdims=True)
            a = acc_ref[g] / l_all
            o = a[0:rq] - lam * a[rq:2 * rq]
            o = o * lax.rsqrt(jnp.mean(o * o, axis=-1, keepdims=True) + LN_EPS)
            o_ref[g * rq:(g + 1) * rq, :] = (o * g_ref[...] * (1.0 - lam_init)).astype(BF16)

    _run_chains(chains, DIFF_LOOKAHEAD, scores, finish)


def _diff_attention(qa, ka, va, bias_tab, lam_params, diff_g, lam_init):
    b, s, _ = qa.shape
    tq = min(ATT_TQ, s)
    kern = functools.partial(_diff_attn_kernel, lam_init=lam_init, tq=tq)
    seq_blk = lambda bi, h: (bi, 0, h)
    return pl.pallas_call(
        kern,
        grid=(b, DIFF_HEADS),
        in_specs=[
            pl.BlockSpec(lam_params.shape, lambda bi, h: (0, 0)),
            pl.BlockSpec((None, s, LANES), seq_blk),
            pl.BlockSpec((None, s, LANES), seq_blk),
            pl.BlockSpec((None, s, LANES), seq_blk),
            pl.BlockSpec((None, 2, ATT_RQ, ATT_RQ), lambda bi, h: (h, 0, 0, 0)),
            pl.BlockSpec((1, LANES), lambda bi, h: (0, 0)),
        ],
        out_specs=pl.BlockSpec((None, s, LANES), seq_blk),
        out_shape=jax.ShapeDtypeStruct((b, s, DIFF_HEADS * LANES), BF16),
        scratch_shapes=[pltpu.VMEM((s // ATT_RQ, 2 * ATT_RQ, LANES), BF16)]
        + [pltpu.VMEM((s // ATT_RQ, 2 * ATT_RQ, LANES), F32)] * 3,
        compiler_params=_cparams("parallel", "parallel"),
        name="diff_attention",
    )(lam_params, qa, ka, va, bias_tab, diff_g)


def _retention_kernel(q_ref, k_ref, v_ref, gate_ref, din_ref, qd_ref, kd_ref, cd_ref, g_ref, o_ref):
    s = q_ref.shape[0]
    c = RET_CHUNK
    lane = lax.broadcasted_iota(I32, (c, LANES), 1)
    g = g_ref[...]
    states = [jnp.zeros((LANES, RET_DV), F32) for _ in range(2)]
    for n in range(s // c):
        r = slice(n * c, (n + 1) * c)
        q_pair = q_ref[r, :].astype(F32)
        k_pair = k_ref[r, :].astype(F32)
        for par in range(2):
            own = (lane // RET_DK) == par
            cols = slice(par * RET_DV, (par + 1) * RET_DV)
            q = jnp.where(own, q_pair, 0.0)
            k = jnp.where(own, k_pair, 0.0)
            v = v_ref[r, cols]
            scores = _dot_nt(q.astype(BF16), k.astype(BF16)) * din_ref[par]
            inner = _dot(scores.astype(BF16), v)
            cross = _dot((q * qd_ref[par]).astype(BF16), states[par].astype(BF16))
            kv = _dot((k * kd_ref[par]).T.astype(BF16), v)
            states[par] = cd_ref[par] * states[par] + kv
            y = inner + cross
            mu = jnp.mean(y, axis=-1, keepdims=True)
            yc = y - mu
            var = jnp.mean(yc * yc, axis=-1, keepdims=True)
            yn = yc * lax.rsqrt(var + LN_EPS) * g
            gate = gate_ref[r, cols].astype(F32)
            o_ref[r, cols] = (_silu(gate) * yn).astype(BF16)


def _retention(qb, kb, vb, gb, tabs, ret_g):
    b, s, _ = qb.shape
    din, qd, kd, cd = tabs
    pair = lambda bi, hp: (bi, 0, hp)
    tab = lambda bi, hp: (hp, 0, 0)
    return pl.pallas_call(
        _retention_kernel,
        grid=(b, RET_HEADS // 2),
        in_specs=[
            pl.BlockSpec((None, s, LANES), pair),
            pl.BlockSpec((None, s, LANES), pair),
            pl.BlockSpec((None, s, 2 * RET_DV), pair),
            pl.BlockSpec((None, s, 2 * RET_DV), pair),
            pl.BlockSpec((2, RET_CHUNK, RET_CHUNK), tab),
            pl.BlockSpec((2, RET_CHUNK, LANES), tab),
            pl.BlockSpec((2, RET_CHUNK, LANES), tab),
            pl.BlockSpec((2, 1, LANES), tab),
            pl.BlockSpec((1, RET_DV), lambda bi, hp: (0, 0)),
        ],
        out_specs=pl.BlockSpec((None, s, 2 * RET_DV), pair),
        out_shape=jax.ShapeDtypeStruct((b, s, RET_HEADS * RET_DV), BF16),
        compiler_params=_cparams("parallel", "parallel"),
        name="retention",
    )(qb, kb, vb, gb, din, qd, kd, cd, ret_g)


def _odd_inproj_kernel(x_ref, w_ref, wf_ref, bf_ref, q_ref, k_ref, v_ref, cum_ref, cumt_ref, carry_ref, *, nblk_s):
    i = pl.program_id(0)
    x = x_ref[...].astype(BF16)
    d = q_ref.shape[1]
    q_ref[...] = (_dot(x, w_ref[:, 0:d].astype(BF16)) * (FOX_DH ** -0.5 * LOG2E)).astype(BF16)
    k_ref[...] = _dot(x, w_ref[:, d:2 * d].astype(BF16)).astype(BF16)
    v_ref[...] = _dot(x, w_ref[:, 2 * d:3 * d].astype(BF16)).astype(BF16)
    z = _dot(x, wf_ref[...]) + bf_ref[...]
    c = jnp.minimum(z, 0.0) - jnp.log1p(jnp.exp(-jnp.abs(z)))
    tm = c.shape[0]
    row = lax.broadcasted_iota(I32, c.shape, 0)
    step = 1
    while step < tm:
        c = c + jnp.where(row >= step, pltpu.roll(c, step, 0), 0.0)
        step *= 2

    @pl.when(i % nblk_s == 0)
    def _():
        carry_ref[...] = jnp.zeros_like(carry_ref)

    c = c + carry_ref[...]
    cum = c * LOG2E
    cum_ref[...] = cum
    cum_t = cum.T
    for hp in range(cumt_ref.shape[0]):
        cumt_ref[hp] = cum_t[2 * hp:2 * hp + 2, :]
    carry_ref[...] = c[tm - 1:tm, :]


def _odd_inproj(x2, w_all, layer, wf_bf, bfg, seq):
    t, d = x2.shape
    tm = min(PROJ_TM, seq)
    nblk_s = seq // tm
    row = lambda i: (i, 0)
    kern = functools.partial(_odd_inproj_kernel, nblk_s=nblk_s)
    return pl.pallas_call(
        kern,
        grid=(t // tm,),
        in_specs=[
            pl.BlockSpec((tm, d), row),
            _layer_weight_spec(w_all, layer),
            pl.BlockSpec(wf_bf.shape, lambda i: (0, 0)),
            pl.BlockSpec(bfg.shape, lambda i: (0, 0)),
        ],
        out_specs=[pl.BlockSpec((tm, d), row)] * 3 + [
            pl.BlockSpec((tm, LANES), row),
            pl.BlockSpec((None, FOX_HEADS // 2, 2, tm), lambda i: (i // nblk_s, 0, 0, i % nblk_s))],
        out_shape=[jax.ShapeDtypeStruct((t, d), BF16)] * 3 + [
            jax.ShapeDtypeStruct((t, LANES), F32), jax.ShapeDtypeStruct((t // seq, FOX_HEADS // 2, 2, seq), F32)],
        scratch_shapes=[pltpu.VMEM((1, LANES), F32)],
        compiler_params=_cparams("arbitrary"),
        name="odd_inproj",
    )(x2, w_all, wf_bf, bfg)


def _fox_attn_kernel(q_ref, k_ref, v_ref, cq_ref, ck_ref, o_ref, qm_ref, va_ref, cqc_ref, m_ref, acc_ref, *, tq):
    seq = q_ref.shape[0]
    tk = tq
    hp = pl.program_id(1)
    q = q_ref[...]
    v = v_ref[...]
    lane = lax.broadcasted_iota(I32, (seq, LANES), 1)
    cq_all = cq_ref[...]
    for par in range(2):
        own = (lane // FOX_DH) == par
        qm_ref[par] = jnp.where(own, q, jnp.zeros_like(q))
        va_ref[par] = jnp.where(own, v, jnp.ones_like(v))
        cq = jnp.sum(jnp.where(lane == 2 * hp + par, cq_all, 0.0), axis=-1, keepdims=True)
        cqc_ref[par] = jnp.broadcast_to(cq, (seq, LANES))
    m_ref[...] = jnp.full(m_ref.shape, NEG_BIG, F32)
    acc_ref[...] = jnp.zeros(acc_ref.shape, F32)
    rq = ATT_RQ
    upper = (lax.broadcasted_iota(I32, (rq, rq), 1) > lax.broadcasted_iota(I32, (rq, rq), 0))
    lane_q = lax.broadcasted_iota(I32, (rq, LANES), 1)

    def n_pieces(g):
        return -(-(g + 1) * rq // tk)

    n_chunks = seq // rq
    chains = [(g, kp, par) for kp in range(n_pieces(n_chunks - 1)) for g in range(n_chunks)
              if kp < n_pieces(g) for par in range(2)]

    def n_keys(g, kp):
        return min(tk, (g + 1) * rq - kp * tk)

    def scores(chain):
        g, kp, par = chain
        k = k_ref[kp * tk:kp * tk + n_keys(g, kp), :]
        return _dot_nt(qm_ref[par, g * rq:(g + 1) * rq, :], k)

    def finish(chain, s):
        g, kp, par = chain
        rows = slice(g * rq, (g + 1) * rq)
        nk = n_keys(g, kp)
        ck = ck_ref[par:par + 1, kp * tk:kp * tk + nk]
        cq = cqc_ref[par, rows, :]
        sc = []
        for kc in range(nk // rq):
            t = s[:, kc * rq:(kc + 1) * rq] - ck[:, kc * rq:(kc + 1) * rq]
            if kp * (tk // rq) + kc == g:
                t = jnp.where(upper, NEG_BIG, t)
            sc.append(t)
        mx = sc[0]
        for t in sc[1:]:
            mx = jnp.maximum(mx, t)
        m_old = m_ref[par, rows, :]
        m_new = jnp.maximum(m_old, jnp.max(mx, axis=-1, keepdims=True) + cq)
        alpha = jnp.exp2(m_old - m_new)
        shift = m_new - cq
        p = jnp.concatenate([jnp.exp2(t - shift).astype(BF16) for t in sc], axis=1)
        acc_ref[par, rows, :] = alpha * acc_ref[par, rows, :] + _dot(p, va_ref[par, kp * tk:kp * tk + nk, :])
        m_ref[par, rows, :] = m_new
        if kp == n_pieces(g) - 1 and par == 1:
            acc0 = acc_ref[0, rows, :]
            acc1 = acc_ref[1, rows, :]
            out0 = acc0 / acc0[:, FOX_DH:FOX_DH + 1]
            out1 = acc1 / acc1[:, 0:1]
            o_ref[rows, :] = jnp.where(lane_q < FOX_DH, out0, out1).astype(BF16)

    _run_chains(chains, FOX_LOOKAHEAD, scores, finish)


def _fox_attention(q, k, v, cum, cum_t):
    b, s, d = q.shape
    tq = min(FOX_TQ, s)
    npair = d // LANES
    seq_blk = lambda bi, h: (bi, 0, h)
    return pl.pallas_call(
        functools.partial(_fox_attn_kernel, tq=tq),
        grid=(b, npair),
        in_specs=[
            pl.BlockSpec((None, s, LANES), seq_blk),
            pl.BlockSpec((None, s, LANES), seq_blk),
            pl.BlockSpec((None, s, LANES), seq_blk),
            pl.BlockSpec((None, s, LANES), lambda bi, h: (bi, 0, 0)),
            pl.BlockSpec((None, None, 2, s), lambda bi, h: (bi, h, 0, 0)),
        ],
        out_specs=pl.BlockSpec((None, s, LANES), seq_blk),
        out_shape=jax.ShapeDtypeStruct((b, s, d), BF16),
        scratch_shapes=[pltpu.VMEM((2, s, LANES), BF16), pltpu.VMEM((2, s, LANES), BF16),
                        pltpu.VMEM((2, s, LANES), F32), pltpu.VMEM((2, s, LANES), F32),
                        pltpu.VMEM((2, s, LANES), F32)],
        compiler_params=_cparams("parallel", "parallel"),
        name="fox_attention",
    )(q, k, v, cum, cum_t)


def _outproj_router_kernel(*refs, n_y):
    y_refs = refs[:n_y]
    (w_ref, x_ref, g_ref, b_ref, rwt_ref, tri_ref, h_ref, route_ref, col_ref, cnt_ref, xs_ref,
     hbf_ref) = refs[n_y:]
    rw2 = rwt_ref[...]
    rw_hi = rw2.astype(BF16)
    rw_lo = (rw2 - rw_hi.astype(F32)).astype(BF16)
    lane_w = lax.broadcasted_iota(I32, rw2.shape, 1)
    w = jnp.where(lane_w < N_EXPERTS, rw_hi, rw_lo)
    tm = x_ref.shape[0]
    rc = ROUTER_ROWS
    logit_chunks = []

    def project(c):
        rows = slice(c * rc, (c + 1) * rc)
        mix = None
        k0 = 0
        for yr in y_refs:
            k1 = k0 + yr.shape[1]
            part = _dot(yr[rows, :], w_ref[k0:k1, :].astype(BF16))
            mix = part if mix is None else mix + part
            k0 = k1
        return mix

    def norm_and_logits(c, mix):
        rows = slice(c * rc, (c + 1) * rc)
        h = _layer_norm_rows(DEEPNORM_ALPHA * x_ref[rows, :] + mix, g_ref[...], b_ref[...])
        h_ref[rows, :] = h
        h_hi = h.astype(BF16)
        hbf_ref[rows, :] = h_hi
        h_lo = (h - h_hi.astype(F32)).astype(BF16)
        p_hi = _dot(h_hi, w)
        p_lo = _dot(h_lo, w)
        slab = p_hi + (pltpu.roll(p_hi, LANES - N_EXPERTS, 1) + p_lo)
        logit_chunks.append(slab.T[0:N_EXPERTS])

    _run_chains(list(range(tm // rc)), 1, project, norm_and_logits)

    logits = jnp.concatenate(logit_chunks, axis=1)
    row = lax.broadcasted_iota(I32, (N_EXPERTS, tm), 0)
    mx = jnp.max(logits, axis=0, keepdims=True)
    ex = jnp.exp(logits - mx)
    probs = ex / jnp.sum(ex, axis=0, keepdims=True)
    grp = row // EXPERTS_PER_GROUP

    def top2(vals):
        v1 = jnp.max(vals, axis=0, keepdims=True)
        i1 = jnp.min(jnp.where(vals == v1, row, N_EXPERTS), axis=0, keepdims=True)
        rest = jnp.where(row == i1, -2.0, vals)
        v2 = jnp.max(rest, axis=0, keepdims=True)
        i2 = jnp.min(jnp.where(rest == v2, row, N_EXPERTS), axis=0, keepdims=True)
        return v1, i1, v2, i2

    best_score = None
    best = None
    for gi in range(N_GROUPS):
        v1, _, v2, _ = top2(jnp.where(grp == gi, probs, -1.0))
        score = v1 + v2
        if gi == 0:
            best_score, best = score, jnp.zeros_like(score, dtype=I32)
        else:
            better = score > best_score
            best = jnp.where(better, gi, best)
            best_score = jnp.where(better, score, best_score)
    v1, i1, v2, i2 = top2(jnp.where(grp == best, probs, -1.0))
    tot = v1 + v2
    g1 = v1 / tot
    g2 = v2 / tot

    onehot = jnp.where((row == i1) | (row == i2), 1.0, 0.0)
    pref = _dot(onehot.astype(BF16), tri_ref[...])
    cnt = jnp.broadcast_to(jnp.sum(onehot, axis=1, keepdims=True), cnt_ref.shape)
    cnt_ref[...] = cnt
    grp_rows = jnp.floor((cnt + (SORT_ALIGN - 1)) * (1.0 / SORT_ALIGN)) * SORT_ALIGN
    row_c = lax.broadcasted_iota(I32, cnt.shape, 0)
    start = grp_rows
    step = 1
    while step < N_EXPERTS:
        start = start + jnp.where(row_c >= step, pltpu.roll(start, step, 0), 0.0)
        step *= 2
    start = (start - grp_rows)[:, 0:1]
    s1 = jnp.sum(jnp.where(row == i1, pref + start, 0.0), axis=0, keepdims=True)
    s2 = jnp.sum(jnp.where(row == i2, pref + start, 0.0), axis=0, keepdims=True)

    row8 = lax.broadcasted_iota(I32, (ROUTE_ROWS, tm), 0)
    fields = (i1.astype(F32), i2.astype(F32), g1, g2, s1, s2)
    route = jnp.zeros((ROUTE_ROWS, tm), F32)
    for n, f in enumerate(fields):
        route = jnp.where(row8 == n, f, route)
    route_ref[...] = route
    col_ref[...] = jnp.concatenate([route, jnp.zeros((LANES - ROUTE_ROWS, tm), F32)], axis=0).T

    srow = lax.broadcasted_iota(I32, (xs_ref.shape[0], tm), 0).astype(F32)
    perm = jnp.where((srow == s1) | (srow == s2), 1.0, 0.0).astype(BF16)
    xs_ref[...] = _dot(perm, hbf_ref[...]).astype(BF16)


def _outproj_router(ys, w_all, layer, x2, ln_g, ln_b, rwt, tri):
    t, d = x2.shape
    tm = tri.shape[0]
    n_local = _local_rows(tm)
    nt = t // tm
    row = lambda i: (i, 0)
    full = lambda i: (0, 0)
    n_y = len(ys)
    kern = functools.partial(_outproj_router_kernel, n_y=n_y)
    return pl.pallas_call(
        kern,
        grid=(nt,),
        in_specs=([pl.BlockSpec((tm, y.shape[1]), row) for y in ys]
                  + [_layer_weight_spec(w_all, layer)]
                  + [pl.BlockSpec((tm, d), row), pl.BlockSpec((1, d), full), pl.BlockSpec((1, d), full),
                     pl.BlockSpec(rwt.shape, full), pl.BlockSpec(tri.shape, full)]),
        out_specs=[pl.BlockSpec((tm, d), row), pl.BlockSpec((ROUTE_ROWS, tm), lambda i: (0, i)),
                   pl.BlockSpec((tm, LANES), row), pl.BlockSpec((N_EXPERTS, LANES), row),
                   pl.BlockSpec((n_local, d), row)],
        out_shape=[jax.ShapeDtypeStruct((t, d), F32), jax.ShapeDtypeStruct((ROUTE_ROWS, t), F32),
                   jax.ShapeDtypeStruct((t, LANES), F32), jax.ShapeDtypeStruct((nt * N_EXPERTS, LANES), F32),
                   jax.ShapeDtypeStruct((nt * n_local, d), BF16)],
        scratch_shapes=[pltpu.VMEM((tm, d), BF16)],
        compiler_params=_cparams("parallel"),
        name="outproj_router",
    )(*ys, w_all, x2, ln_g, ln_b, rwt, tri)


def _start_tile_gather(tile, loff_ref, rows_ref, gpos_ref, sorted_hbm, local_ref, sem):
    for e in range(N_EXPERTS):
        n = tile * N_EXPERTS + e

        @pl.when(rows_ref[n] > 0)
        def _(n=n):
            size = pl.multiple_of(rows_ref[n], SORT_ALIGN)
            src = sorted_hbm.at[pl.ds(pl.multiple_of(gpos_ref[n], SORT_ALIGN), size)]
            dst = local_ref.at[pl.ds(pl.multiple_of(loff_ref[n], SORT_ALIGN), size)]
            pltpu.make_async_copy(src, dst, sem).start()


def _wait_rows(n_rows, src_hbm, dst_ref, sem):
    n = pl.multiple_of(n_rows, SORT_ALIGN)
    pltpu.make_async_copy(src_hbm.at[pl.ds(0, n)], dst_ref.at[pl.ds(0, n)], sem).wait()


def _expert_kernel(te_ref, nx_ref, nt_ref, g0_ref, g1_ref, valid_ref, loff_ref, rows_ref, gpos_ref,
                   xs_hbm, wg_hbm, wu_hbm, wd_hbm, o_ref,
                   xbuf, semx, wgs, wus, wds, wgb, wub, wdb, sem, nsw_ref, *, layer, n_local):
    r = pl.program_id(0)
    nt = nt_ref[0]
    cur = te_ref[r]
    tr = xbuf.shape[1]
    xslot = r % 2

    def start_pieces(tile, dst_slot):
        e = te_ref[tile]
        base = tile * tr

        def piece(tau, c):
            n = tau * N_EXPERTS + e
            lo = jnp.maximum(gpos_ref[n], base)
            hi = jnp.minimum(gpos_ref[n] + rows_ref[n], base + tr)

            @pl.when(hi > lo)
            def _():
                size = pl.multiple_of(hi - lo, SORT_ALIGN)
                src = pl.multiple_of(tau * n_local + loff_ref[n] + (lo - gpos_ref[n]), SORT_ALIGN)
                dst = pl.multiple_of(lo - base, SORT_ALIGN)
                pltpu.make_async_copy(xs_hbm.at[pl.ds(src, size)], xbuf.at[dst_slot, pl.ds(dst, size)],
                                      semx.at[dst_slot]).start()

            return c

        lax.fori_loop(g0_ref[tile], g1_ref[tile] + 1, piece, 0)

    @pl.when(r == 0)
    def _():
        xbuf[...] = jnp.zeros_like(xbuf)
        start_pieces(0, 0)

    @pl.when(r + 1 < nt)
    def _():
        start_pieces(r + 1, 1 - xslot)

    def weight_copies(e, slot):
        return [pltpu.make_async_copy(src.at[layer, e], dst.at[slot], sem.at[slot])
                for src, dst in ((wg_hbm, wgs), (wu_hbm, wus), (wd_hbm, wds))]

    @pl.when(r == 0)
    def _():
        nsw_ref[0] = 0
        for cp in weight_copies(cur, 0):
            cp.start()

    @pl.when((r == 0) | (cur != te_ref[jnp.maximum(r - 1, 0)]))
    def _():
        slot = nsw_ref[0] % 2
        nsw_ref[0] = nsw_ref[0] + 1
        for cp in weight_copies(cur, slot):
            cp.wait()

        @pl.when(nx_ref[r] != cur)
        def _():
            for cp in weight_copies(nx_ref[r], 1 - slot):
                cp.start()

        wgb[...] = wgs[slot].astype(BF16)
        wub[...] = wus[slot].astype(BF16)
        wdb[...] = wds[slot].astype(BF16)

    @pl.when(r < nt)
    def _():
        _wait_rows(valid_ref[r], xs_hbm, xbuf.at[xslot], semx.at[xslot])
        rc = EXPERT_ROWS

        def up(c):
            x = xbuf[xslot, c * rc:(c + 1) * rc, :]
            return _dot(x, wgb[...]), _dot(x, wub[...])

        def down(c, gate_up):
            a = _silu(gate_up[0]) * gate_up[1]
            o_ref[c * rc:(c + 1) * rc, :] = _dot(a.astype(BF16), wdb[...]).astype(BF16)

        _run_chains(list(range(tr // rc)), 1, up, down)

    @pl.when(r >= nt)
    def _():
        o_ref[...] = jnp.zeros_like(o_ref)


def _expert_mlps(meta, n_slots, xs_local, wg, wu, wd, layer):
    d = xs_local.shape[1]
    tr = MOE_TR
    dff = wg.shape[3]
    grid_spec = pltpu.PrefetchScalarGridSpec(
        num_scalar_prefetch=9,
        grid=(n_slots // tr,),
        in_specs=[pl.BlockSpec(memory_space=pl.ANY)] * 4,
        out_specs=pl.BlockSpec((tr, d), lambda r, *_: (r, 0)),
        scratch_shapes=[pltpu.VMEM((2, tr, d), BF16), pltpu.SemaphoreType.DMA((2,)),
                        pltpu.VMEM((2, d, dff), F32), pltpu.VMEM((2, d, dff), F32), pltpu.VMEM((2, dff, d), F32),
                        pltpu.VMEM((d, dff), BF16), pltpu.VMEM((d, dff), BF16), pltpu.VMEM((dff, d), BF16),
                        pltpu.SemaphoreType.DMA((2,)), pltpu.SMEM((1,), I32)],
    )
    return pl.pallas_call(
        functools.partial(_expert_kernel, layer=layer, n_local=_local_rows(MOE_TM)),
        grid_spec=grid_spec,
        out_shape=jax.ShapeDtypeStruct((n_slots, d), BF16),
        compiler_params=_cparams("arbitrary"),
        name="expert_mlps",
    )(meta["tile_expert"], meta["next_expert"], meta["n_tiles"], meta["first_group"], meta["last_group"],
      meta["valid"], meta["loff"], meta["rows"], meta["gpos"], xs_local, wg, wu, wd)


def _combine_kernel(loff_ref, rows_ref, gpos_ref, tot_ref, ys_hbm, h_ref, col_ref, p_ref, g_ref, b_ref,
                    pg_ref, pp_ref, o_ref, ybuf, sem):
    i = pl.program_id(0)
    n = pl.num_programs(0)
    tm = h_ref.shape[0]
    n_local = ybuf.shape[1]
    slot = i % 2

    @pl.when(i == 0)
    def _():
        ybuf[...] = jnp.zeros_like(ybuf)
        _start_tile_gather(0, loff_ref, rows_ref, gpos_ref, ys_hbm, ybuf.at[0], sem.at[0])

    @pl.when(i + 1 < n)
    def _():
        _start_tile_gather(i + 1, loff_ref, rows_ref, gpos_ref, ys_hbm, ybuf.at[1 - slot], sem.at[1 - slot])

    _wait_rows(tot_ref[i], ys_hbm, ybuf.at[slot], sem.at[slot])
    rc = COMBINE_ROWS
    scol = lax.broadcasted_iota(I32, (rc, n_local), 1).astype(F32)

    def gather_rows(c):
        rows = slice(c * rc, (c + 1) * rc)
        cols = col_ref[rows, :]
        pick = jnp.zeros((rc, n_local), F32)
        for k in range(2):
            pick = jnp.where(scol == cols[:, ROUTE_SLOT + k:ROUTE_SLOT + k + 1],
                             cols[:, ROUTE_GATE + k:ROUTE_GATE + k + 1], pick)
        ffn = _dot(pick.astype(BF16), ybuf[slot])
        return ffn, _dot(p_ref[rows, :].astype(BF16), pp_ref[...].astype(BF16))

    def finish(c, gathered):
        rows = slice(c * rc, (c + 1) * rc)
        ffn, pe = gathered
        h2 = _layer_norm_rows(DEEPNORM_ALPHA * h_ref[rows, :] + ffn, g_ref[...], b_ref[...])
        gate = _sigmoid(_dot(h2.astype(BF16), pg_ref[...].astype(BF16)))
        o_ref[rows, :] = h2 + gate * pe

    _run_chains(list(range(tm // rc)), 1, gather_rows, finish)


def _combine(meta, ys, h, cols, p3, layer, ln_g, ln_b, ple_gate, ple_proj):
    t, d = h.shape
    tm = MOE_TM
    pdim = p3.shape[2]
    row = lambda i, *_: (i, 0)
    full = lambda i, *_: (0, 0)
    grid_spec = pltpu.PrefetchScalarGridSpec(
        num_scalar_prefetch=4,
        grid=(t // tm,),
        in_specs=[
            pl.BlockSpec(memory_space=pl.ANY),
            pl.BlockSpec((tm, d), row),
            pl.BlockSpec((tm, LANES), row),
            pl.BlockSpec((None, tm, pdim), lambda i, *_: (layer, i, 0)),
            pl.BlockSpec((1, d), full),
            pl.BlockSpec((1, d), full),
            pl.BlockSpec((None,) + ple_gate.shape[1:], lambda i, *_: (layer, 0, 0), pipeline_mode=pl.Buffered(1)),
            pl.BlockSpec((None,) + ple_proj.shape[1:], lambda i, *_: (layer, 0, 0), pipeline_mode=pl.Buffered(1)),
        ],
        out_specs=pl.BlockSpec((tm, d), row),
        scratch_shapes=[pltpu.VMEM((2, _local_rows(tm), d), BF16), pltpu.SemaphoreType.DMA((2,))],
    )
    return pl.pallas_call(
        _combine_kernel,
        grid_spec=grid_spec,
        out_shape=jax.ShapeDtypeStruct((t, d), F32),
        compiler_params=_cparams("arbitrary"),
        name="moe_combine",
    )(meta["loff"], meta["rows"], meta["gpos"], meta["tot"], ys, h, cols, p3, ln_g, ln_b, ple_gate, ple_proj)


def _rotary_tables(seq):
    half = RET_DK // 2
    inv = (np.float32(ROPE_BASE) ** (-np.arange(half, dtype=np.float32) / np.float32(half))).astype(np.float32)
    ang = (np.arange(seq, dtype=np.float32)[:, None] * inv[None, :]).astype(np.float32)
    cos = np.cos(ang.astype(np.float64))
    sin = np.sin(ang.astype(np.float64))
    cos_h = np.concatenate([cos, cos], axis=1)
    sin_h = np.concatenate([-sin, sin], axis=1)
    return (jnp.asarray(np.tile(cos_h, (1, RET_HEADS)), F32), jnp.asarray(np.tile(sin_h, (1, RET_HEADS)), F32))


def _retention_tables():
    c = RET_CHUNK
    h = np.arange(RET_HEADS, dtype=np.float64)
    log_g = np.log1p(-np.exp2(-5.0 - h))
    j = np.arange(c, dtype=np.float64)
    rel = j[:, None] - j[None, :]
    din = np.where(rel >= 0, np.exp(np.maximum(rel, 0.0)[None] * log_g[:, None, None]), 0.0)
    qd = np.exp((j + 1.0)[None] * log_g[:, None])
    kd = np.exp((c - 1.0 - j)[None] * log_g[:, None])
    cd = np.exp(c * log_g)
    qd = np.broadcast_to(qd[:, :, None], (RET_HEADS, c, LANES))
    kd = np.broadcast_to(kd[:, :, None], (RET_HEADS, c, LANES))
    cd = np.broadcast_to(cd[:, None, None], (RET_HEADS, 1, LANES))
    return tuple(jnp.asarray(a, F32) for a in (din, qd, kd, cd))


def _t5_bucket_np(dist):
    max_exact = REL_BUCKETS // 2
    d = np.maximum(dist, 1).astype(np.float32)
    large = max_exact + (np.log(d / np.float32(max_exact)) / np.float32(math.log(REL_MAX_DIST / max_exact))
                         * np.float32(REL_BUCKETS - max_exact)).astype(np.int32)
    large = np.minimum(large, REL_BUCKETS - 1)
    return np.where(dist < max_exact, dist, large)


def _diff_bias_tables(rel_bias, seq):
    c = ATT_RQ
    r = np.arange(c)
    dist0 = r[:, None] - r[None, :]
    far = REL_BUCKETS - 1
    assert np.all(_t5_bucket_np(np.arange(c + 1, max(seq, 2 * c))) == far)
    bidx = np.stack([_t5_bucket_np(np.maximum(dist0, 0)), _t5_bucket_np(dist0 + c)])
    rb = rel_bias.astype(F32).T
    shifted = (rb - rb[:, far:far + 1]) * LOG2E
    bidx = jnp.asarray(bidx, I32)[None]
    tab = jnp.zeros((rb.shape[0], 2, c, c), F32)
    for bucket in range(REL_BUCKETS - 1):
        tab = jnp.where(bidx == bucket, shifted[:, bucket][:, None, None, None], tab)
    causal = jnp.asarray(np.stack([dist0 >= 0, np.ones_like(dist0, bool)]))[None]
    return jnp.where(causal, tab, NEG_BIG)


def _local_rows(tm):
    need = 2 * tm + N_EXPERTS * (SORT_ALIGN - 1)
    return -(-need // LANES) * LANES


def _round_up(a, m):
    return ((a + m - 1) // m) * m


def _route_meta(cnt, t):
    tm, tr = MOE_TM, MOE_TR
    nt = t // tm
    counts = cnt.reshape(nt, N_EXPERTS, LANES)[:, :, 0].astype(I32)
    rows = _round_up(counts, SORT_ALIGN)
    loff = jnp.cumsum(rows, axis=1) - rows
    seg = jnp.sum(rows, axis=0)
    seg_pad = _round_up(seg, tr)
    ends = jnp.cumsum(seg_pad)
    offs = ends - seg_pad
    gpos = offs[None, :] + jnp.cumsum(rows, axis=0) - rows
    n_slots = 2 * t + nt * N_EXPERTS * (SORT_ALIGN - 1)
    n_slots = _round_up(n_slots, tr) + N_EXPERTS * tr
    n_tiles = (ends[-1] // tr).astype(I32)
    tile_start = jnp.arange(n_slots // tr, dtype=I32) * tr
    tile_expert = jnp.sum((tile_start[:, None] >= ends[None, :]).astype(I32), axis=1)
    last = jnp.sum((((n_tiles - 1) * tr) >= ends).astype(I32))
    tile_expert = jnp.minimum(tile_expert, last).astype(I32)
    eid = jnp.arange(N_EXPERTS, dtype=I32)
    later = (eid[None, :] > eid[:, None]) & (seg_pad > 0)[None, :]
    nxt = jnp.min(jnp.where(later, eid[None, :], N_EXPERTS), axis=1)
    nxt = jnp.where(nxt == N_EXPERTS, eid, nxt)
    next_expert = jnp.sum(jnp.where(tile_expert[:, None] == eid[None, :], nxt[None, :], 0), axis=1).astype(I32)
    own = tile_expert[:, None] == eid[None, :]
    pick = lambda tab: jnp.sum(jnp.where(own[:, None, :], tab[None, :, :], 0), axis=-1)
    g_start, g_end = pick(gpos), pick(gpos + rows)
    used = (tile_start < n_tiles * tr)[:, None]
    first_group = jnp.sum((g_end <= tile_start[:, None]).astype(I32), axis=1)
    last_group = jnp.sum(((g_start < tile_start[:, None] + tr) & used).astype(I32), axis=1) - 1
    seg_end = jnp.sum(jnp.where(own, (offs + seg)[None, :], 0), axis=1)
    valid = jnp.clip(seg_end - tile_start, 0, tr)
    meta = {
        "loff": loff.reshape(-1).astype(I32), "rows": rows.reshape(-1).astype(I32),
        "gpos": gpos.reshape(-1).astype(I32), "tot": jnp.sum(rows, axis=1).astype(I32),
        "tile_expert": tile_expert, "next_expert": next_expert, "n_tiles": n_tiles.reshape(1),
        "first_group": first_group.astype(I32), "last_group": last_group.astype(I32), "valid": valid.astype(I32),
    }
    return meta, n_slots


def kernel(x, p, rel_bias, router_w, even_w_in, even_w_out, even_lambda, even_diff_norm, even_ret_norm,
           odd_w_in, odd_b_forget, odd_w_out, ln_mix_g, ln_mix_b, ln_ffn_g, ln_ffn_b,
           moe_w_gate, moe_w_up, moe_w_down, ple_proj, ple_gate):
    b, s, d = x.shape
    t = b * s
    assert d == 1024 and p.shape[0] == DEPTH and even_w_in.shape[2] == 3072
    assert odd_w_in.shape[2] == 3 * d + FOX_HEADS and moe_w_gate.shape[1] == N_EXPERTS
    assert s % RET_CHUNK == 0 and s % min(ATT_TQ, s) == 0 and s % min(FOX_TQ, s) == 0
    assert t % min(PROJ_TM, s) == 0 and t % MOE_TM == 0

    cos_t, sin_t = _rotary_tables(s)
    ret_tabs = _retention_tables()
    bias_tab = _diff_bias_tables(rel_bias, s)
    rw32 = router_w.astype(F32)
    rwt = jnp.zeros((d, LANES), F32).at[:, :N_EXPERTS].set(rw32).at[:, N_EXPERTS:2 * N_EXPERTS].set(rw32)
    tok = np.arange(MOE_TM)
    tri = jnp.asarray(tok[:, None] < tok[None, :], BF16)

    x2 = x.reshape(t, d)
    for i in range(DEPTH):
        j = i // 2
        if i % 2 == 0:
            lam_init = 0.8 - 0.6 * math.exp(-0.3 * i)
            qa, ka, va, qb, kb, vb, gb = _even_inproj(x2, even_w_in, j, cos_t, sin_t, s)
            sh = lambda a: a.reshape(b, s, a.shape[1])
            ya = _diff_attention(sh(qa), sh(ka), sh(va), bias_tab, even_lambda[j].astype(F32),
                                 even_diff_norm[j].reshape(1, -1).astype(F32), lam_init)
            yb = _retention(sh(qb), sh(kb), sh(vb), sh(gb), ret_tabs, even_ret_norm[j].reshape(1, -1).astype(F32))
            ys = [ya.reshape(t, -1), yb.reshape(t, -1)]
            w_out = even_w_out
        else:
            wf = jnp.zeros((d, LANES), BF16).at[:, :FOX_HEADS].set(odd_w_in[j, :, 3 * d:].astype(BF16))
            bfg = jnp.zeros((1, LANES), F32).at[0, :FOX_HEADS].set(odd_b_forget[j].astype(F32))
            q, k, v, cum, cum_t = _odd_inproj(x2, odd_w_in, j, wf, bfg, s)
            y = _fox_attention(q.reshape(b, s, d), k.reshape(b, s, d), v.reshape(b, s, d),
                               cum.reshape(b, s, LANES), cum_t)
            ys = [y.reshape(t, d)]
            w_out = odd_w_out
        h, route, cols, cnt, xs_local = _outproj_router(ys, w_out, j, x2, ln_mix_g[i].reshape(1, d),
                                                        ln_mix_b[i].reshape(1, d), rwt, tri)
        meta, n_slots = _route_meta(cnt, t)
        rows = _expert_mlps(meta, n_slots, xs_local, moe_w_gate, moe_w_up, moe_w_down, i)
        x2 = _combine(meta, rows, h, cols, p.reshape(DEPTH, t, -1), i, ln_ffn_g[i].reshape(1, d),
                      ln_ffn_b[i].reshape(1, d), ple_gate, ple_proj)
    return x2.reshape(b, s, d)
```

```python
import functools
import math

import numpy as np
import jax
import jax.numpy as jnp
from jax import lax
from jax.experimental import pallas as pl
from jax.experimental.pallas import tpu as pltpu

F32 = jnp.float32
BF16 = jnp.bfloat16
I32 = jnp.int32

DIFF_HEADS = 4
DIFF_DK = 64
RET_HEADS = 4
RET_DK = 64
RET_DV = 128
RET_CHUNK = 128
FOX_HEADS = 16
FOX_DH = 64
REL_BUCKETS = 32
REL_MAX_DIST = 128
N_GROUPS = 4
EXPERTS_PER_GROUP = 4
N_EXPERTS = 16
DEPTH = 2
DEEPNORM_ALPHA = (2 * DEPTH) ** 0.25
LN_EPS = 1e-5
ROPE_BASE = 10000.0
NEG_BIG = -1e30
LOG2E = math.log2(math.e)
RET_QK = RET_HEADS * RET_DK
EVEN_WIDTHS = (DIFF_HEADS * 2 * DIFF_DK, DIFF_HEADS * 2 * DIFF_DK, DIFF_HEADS * 2 * DIFF_DK,
               RET_QK, RET_QK, RET_HEADS * RET_DV, RET_HEADS * RET_DV)
EVEN_CUTS = tuple(int(v) for v in np.cumsum(EVEN_WIDTHS))

VMEM_LIMIT_BYTES = 48 * 1024 * 1024
LANES = 128

PROJ_TM = 1024
ATT_TQ = 512
FOX_TQ = 512
ATT_RQ = LANES
DIFF_LOOKAHEAD = 3
FOX_LOOKAHEAD = 4
MOE_TR = 512
MOE_TM = 512
SORT_ALIGN = 16
ROUTE_ROWS = 8
ROUTE_GATE = 2
ROUTE_SLOT = 4
COMBINE_ROWS = 256
EXPERT_ROWS = 256
ROUTER_ROWS = 256


def _cparams(*sem):
    return pltpu.CompilerParams(dimension_semantics=sem, vmem_limit_bytes=VMEM_LIMIT_BYTES)


def _dot(a, b):
    return jnp.dot(a, b, preferred_element_type=F32)


def _dot_nt(a, b):
    return lax.dot_general(a, b, (((1,), (1,)), ((), ())), preferred_element_type=F32)


def _layer_norm_rows(z, g, b):
    mu = jnp.mean(z, axis=-1, keepdims=True)
    zc = z - mu
    var = jnp.mean(zc * zc, axis=-1, keepdims=True)
    return zc * lax.rsqrt(var + LN_EPS) * g + b


def _silu(x):
    return x * (1.0 / (1.0 + jnp.exp(-x)))


def _sigmoid(x):
    return 1.0 / (1.0 + jnp.exp(-x))


def _even_inproj_kernel(x_ref, w_ref, cos_ref, sin_ref,
                        qa_ref, ka_ref, va_ref, qb_ref, kb_ref, vb_ref, gb_ref):
    x = x_ref[...].astype(BF16)

    def mm(c0, c1):
        return _dot(x, w_ref[:, c0:c1].astype(BF16))

    c = EVEN_CUTS
    qa_ref[...] = (mm(0, c[0]) * (DIFF_DK ** -0.5 * LOG2E)).astype(BF16)
    ka_ref[...] = mm(c[0], c[1]).astype(BF16)
    va_ref[...] = mm(c[1], c[2]).astype(BF16)
    qk = mm(c[2], c[4])
    cos = cos_ref[...]
    sin = sin_ref[...]
    lane = lax.broadcasted_iota(I32, cos.shape, 1)
    first_half = (lane % RET_DK) < (RET_DK // 2)

    def rot(t):
        sw = jnp.where(first_half, pltpu.roll(t, t.shape[1] - RET_DK // 2, 1),
                       pltpu.roll(t, RET_DK // 2, 1))
        return t * cos + sw * sin

    qb_ref[...] = rot(qk[:, :RET_QK]).astype(BF16)
    kb_ref[...] = (rot(qk[:, RET_QK:]) * (RET_DK ** -0.5)).astype(BF16)
    vb_ref[...] = mm(c[4], c[5]).astype(BF16)
    gb_ref[...] = mm(c[5], c[6]).astype(BF16)


def _layer_weight_spec(w_all, layer):
    return pl.BlockSpec((None,) + w_all.shape[1:], lambda *_: (layer, 0, 0), pipeline_mode=pl.Buffered(1))


def _even_inproj(x2, w_all, layer, cos_t, sin_t, seq):
    t, d = x2.shape
    tm = min(PROJ_TM, seq)
    nblk_s = seq // tm
    widths = EVEN_WIDTHS
    row = lambda i: (i, 0)
    return pl.pallas_call(
        _even_inproj_kernel,
        grid=(t // tm,),
        in_specs=[
            pl.BlockSpec((tm, d), row),
            _layer_weight_spec(w_all, layer),
            pl.BlockSpec((tm, RET_QK), lambda i: (i % nblk_s, 0)),
            pl.BlockSpec((tm, RET_QK), lambda i: (i % nblk_s, 0)),
        ],
        out_specs=[pl.BlockSpec((tm, w), row) for w in widths],
        out_shape=[jax.ShapeDtypeStruct((t, w), BF16) for w in widths],
        compiler_params=_cparams("parallel"),
        name="even_inproj",
    )(x2, w_all, cos_t, sin_t)


def _run_chains(chains, lookahead, scores, finish):
    pending = [scores(c) for c in chains[:lookahead]]
    for n, chain in enumerate(chains):
        if n + lookahead < len(chains):
            pending.append(scores(chains[n + lookahead]))
        finish(chain, pending.pop(0))


def _diff_attn_kernel(lam_ref, q_ref, k_ref, v_ref, bias_ref, g_ref, o_ref, qm_ref, m_ref, l_ref, acc_ref,
                      *, lam_init, tq):
    seq = q_ref.shape[0]
    tk = tq
    rq = ATT_RQ
    nr = tq // rq
    lane = lax.broadcasted_iota(I32, (rq, LANES), 1)
    for n in range(seq // rq):
        q = q_ref[n * rq:(n + 1) * rq, :]
        zero = jnp.zeros_like(q)
        qm_ref[n, 0:rq, :] = jnp.where(lane < DIFF_DK, q, zero)
        qm_ref[n, rq:2 * rq, :] = jnp.where(lane >= DIFF_DK, q, zero)
    m_ref[...] = jnp.full(m_ref.shape, NEG_BIG, F32)
    l_ref[...] = jnp.zeros(l_ref.shape, F32)
    acc_ref[...] = jnp.zeros(acc_ref.shape, F32)
    lp = lam_ref[...]
    lam = (jnp.exp(jnp.sum(lp[0:1, :] * lp[1:2, :], axis=-1, keepdims=True))
           - jnp.exp(jnp.sum(lp[2:3, :] * lp[3:4, :], axis=-1, keepdims=True)) + lam_init)
    bias2 = [jnp.concatenate([bias_ref[n], bias_ref[n]], axis=0) for n in range(2)]

    chains = [(ii, j, r) for j in range(seq // tq) for ii in range(j, seq // tq) for r in range(nr)]

    def n_keys(ii, j, r):
        return (r + 1) * rq if j == ii else tk

    def scores(chain):
        ii, j, r = chain
        k = k_ref[j * tk:j * tk + n_keys(ii, j, r), :]
        return _dot_nt(qm_ref[ii * nr + r], k)

    def finish(chain, s):
        ii, j, r = chain
        g = ii * nr + r
        nk = n_keys(ii, j, r)
        v = v_ref[j * tk:j * tk + nk, :]
        sc = []
        for kc in range(nk // rq):
            t = s[:, kc * rq:(kc + 1) * rq]
            back = g - (j * nr + kc)
            if back <= 1:
                t = t + bias2[back]
            sc.append(t)
        mx = sc[0]
        for t in sc[1:]:
            mx = jnp.maximum(mx, t)
        m_old = m_ref[g]
        m_new = jnp.maximum(m_old, jnp.max(mx, axis=-1, keepdims=True))
        alpha = jnp.exp2(m_old - m_new)
        ps = [jnp.exp2(t - m_new) for t in sc]
        psum = ps[0]
        for t in ps[1:]:
            psum = psum + t
        l_ref[g] = alpha * l_ref[g] + psum
        p = jnp.concatenate([t.astype(BF16) for t in ps], axis=1)
        acc_ref[g] = alpha * acc_ref[g] + _dot(p, v)
        m_ref[g] = m_new
        if j == ii:
            l_all = jnp.sum(l_ref[g], axis=-1, keepdims=True)
            a = acc_ref[g] / l_all
            o = a[0:rq] - lam * a[rq:2 * rq]
            o = o * lax.rsqrt(jnp.mean(o * o, axis=-1, keepdims=True) + LN_EPS)
            o_ref[g * rq:(g + 1) * rq, :] = (o * g_ref[...] * (1.0 - lam_init)).astype(BF16)

    _run_chains(chains, DIFF_LOOKAHEAD, scores, finish)


def _diff_attention(qa, ka, va, bias_tab, lam_params, diff_g, lam_init):
    b, s, _ = qa.shape
    tq = min(ATT_TQ, s)
    kern = functools.partial(_diff_attn_kernel, lam_init=lam_init, tq=tq)
    seq_blk = lambda bi, h: (bi, 0, h)
    return pl.pallas_call(
        kern,
        grid=(b, DIFF_HEADS),
        in_specs=[
            pl.BlockSpec(lam_params.shape, lambda bi, h: (0, 0)),
            pl.BlockSpec((None, s, LANES), seq_blk),
            pl.BlockSpec((None, s, LANES), seq_blk),
            pl.BlockSpec((None, s, LANES), seq_blk),
            pl.BlockSpec((None, 2, ATT_RQ, ATT_RQ), lambda bi, h: (h, 0, 0, 0)),
            pl.BlockSpec((1, LANES), lambda bi, h: (0, 0)),
        ],
        out_specs=pl.BlockSpec((None, s, LANES), seq_blk),
        out_shape=jax.ShapeDtypeStruct((b, s, DIFF_HEADS * LANES), BF16),
        scratch_shapes=[pltpu.VMEM((s // ATT_RQ, 2 * ATT_RQ, LANES), BF16)]
        + [pltpu.VMEM((s // ATT_RQ, 2 * ATT_RQ, LANES), F32)] * 3,
        compiler_params=_cparams("parallel", "parallel"),
        name="diff_attention",
    )(lam_params, qa, ka, va, bias_tab, diff_g)


def _retention_kernel(q_ref, k_ref, v_ref, gate_ref, din_ref, qd_ref, kd_ref, cd_ref, g_ref, o_ref):
    s = q_ref.shape[0]
    c = RET_CHUNK
    lane = lax.broadcasted_iota(I32, (c, LANES), 1)
    g = g_ref[...]
    states = [jnp.zeros((LANES, RET_DV), F32) for _ in range(2)]
    for n in range(s // c):
        r = slice(n * c, (n + 1) * c)
        q_pair = q_ref[r, :].astype(F32)
        k_pair = k_ref[r, :].astype(F32)
        for par in range(2):
            own = (lane // RET_DK) == par
            cols = slice(par * RET_DV, (par + 1) * RET_DV)
            q = jnp.where(own, q_pair, 0.0)
            k = jnp.where(own, k_pair, 0.0)
            v = v_ref[r, cols]
            scores = _dot_nt(q.astype(BF16), k.astype(BF16)) * din_ref[par]
            inner = _dot(scores.astype(BF16), v)
            cross = _dot((q * qd_ref[par]).astype(BF16), states[par].astype(BF16))
            kv = _dot((k * kd_ref[par]).T.astype(BF16), v)
            states[par] = cd_ref[par] * states[par] + kv
            y = inner + cross
            mu = jnp.mean(y, axis=-1, keepdims=True)
            yc = y - mu
            var = jnp.mean(yc * yc, axis=-1, keepdims=True)
            yn = yc * lax.rsqrt(var + LN_EPS) * g
            gate = gate_ref[r, cols].astype(F32)
            o_ref[r, cols] = (_silu(gate) * yn).astype(BF16)


def _retention(qb, kb, vb, gb, tabs, ret_g):
    b, s, _ = qb.shape
    din, qd, kd, cd = tabs
    pair = lambda bi, hp: (bi, 0, hp)
    tab = lambda bi, hp: (hp, 0, 0)
    return pl.pallas_call(
        _retention_kernel,
        grid=(b, RET_HEADS // 2),
        in_specs=[
            pl.BlockSpec((None, s, LANES), pair),
            pl.BlockSpec((None, s, LANES), pair),
            pl.BlockSpec((None, s, 2 * RET_DV), pair),
            pl.BlockSpec((None, s, 2 * RET_DV), pair),
            pl.BlockSpec((2, RET_CHUNK, RET_CHUNK), tab),
            pl.BlockSpec((2, RET_CHUNK, LANES), tab),
            pl.BlockSpec((2, RET_CHUNK, LANES), tab),
            pl.BlockSpec((2, 1, LANES), tab),
            pl.BlockSpec((1, RET_DV), lambda bi, hp: (0, 0)),
        ],
        out_specs=pl.BlockSpec((None, s, 2 * RET_DV), pair),
        out_shape=jax.ShapeDtypeStruct((b, s, RET_HEADS * RET_DV), BF16),
        compiler_params=_cparams("parallel", "parallel"),
        name="retention",
    )(qb, kb, vb, gb, din, qd, kd, cd, ret_g)


def _odd_inproj_kernel(x_ref, w_ref, wf_ref, bf_ref, q_ref, k_ref, v_ref, cum_ref, cumt_ref, carry_ref, *, nblk_s):
    i = pl.program_id(0)
    x = x_ref[...].astype(BF16)
    d = q_ref.shape[1]
    q_ref[...] = (_dot(x, w_ref[:, 0:d].astype(BF16)) * (FOX_DH ** -0.5 * LOG2E)).astype(BF16)
    k_ref[...] = _dot(x, w_ref[:, d:2 * d].astype(BF16)).astype(BF16)
    v_ref[...] = _dot(x, w_ref[:, 2 * d:3 * d].astype(BF16)).astype(BF16)
    z = _dot(x, wf_ref[...]) + bf_ref[...]
    c = jnp.minimum(z, 0.0) - jnp.log1p(jnp.exp(-jnp.abs(z)))
    tm = c.shape[0]
    row = lax.broadcasted_iota(I32, c.shape, 0)
    step = 1
    while step < tm:
        c = c + jnp.where(row >= step, pltpu.roll(c, step, 0), 0.0)
        step *= 2

    @pl.when(i % nblk_s == 0)
    def _():
        carry_ref[...] = jnp.zeros_like(carry_ref)

    c = c + carry_ref[...]
    cum = c * LOG2E
    cum_ref[...] = cum
    cum_t = cum.T
    for hp in range(cumt_ref.shape[0]):
        cumt_ref[hp] = cum_t[2 * hp:2 * hp + 2, :]
    carry_ref[...] = c[tm - 1:tm, :]


def _odd_inproj(x2, w_all, layer, wf_bf, bfg, seq):
    t, d = x2.shape
    tm = min(PROJ_TM, seq)
    nblk_s = seq // tm
    row = lambda i: (i, 0)
    kern = functools.partial(_odd_inproj_kernel, nblk_s=nblk_s)
    return pl.pallas_call(
        kern,
        grid=(t // tm,),
        in_specs=[
            pl.BlockSpec((tm, d), row),
            _layer_weight_spec(w_all, layer),
            pl.BlockSpec(wf_bf.shape, lambda i: (0, 0)),
            pl.BlockSpec(bfg.shape, lambda i: (0, 0)),
        ],
        out_specs=[pl.BlockSpec((tm, d), row)] * 3 + [
            pl.BlockSpec((tm, LANES), row),
            pl.BlockSpec((None, FOX_HEADS // 2, 2, tm), lambda i: (i // nblk_s, 0, 0, i % nblk_s))],
        out_shape=[jax.ShapeDtypeStruct((t, d), BF16)] * 3 + [
            jax.ShapeDtypeStruct((t, LANES), F32), jax.ShapeDtypeStruct((t // seq, FOX_HEADS // 2, 2, seq), F32)],
        scratch_shapes=[pltpu.VMEM((1, LANES), F32)],
        compiler_params=_cparams("arbitrary"),
        name="odd_inproj",
    )(x2, w_all, wf_bf, bfg)


def _fox_attn_kernel(q_ref, k_ref, v_ref, cq_ref, ck_ref, o_ref, qm_ref, va_ref, cqc_ref, m_ref, acc_ref, *, tq):
    seq = q_ref.shape[0]
    tk = tq
    hp = pl.program_id(1)
    q = q_ref[...]
    v = v_ref[...]
    lane = lax.broadcasted_iota(I32, (seq, LANES), 1)
    cq_all = cq_ref[...]
    for par in range(2):
        own = (lane // FOX_DH) == par
        qm_ref[par] = jnp.where(own, q, jnp.zeros_like(q))
        va_ref[par] = jnp.where(own, v, jnp.ones_like(v))
        cq = jnp.sum(jnp.where(lane == 2 * hp + par, cq_all, 0.0), axis=-1, keepdims=True)
        cqc_ref[par] = jnp.broadcast_to(cq, (seq, LANES))
    m_ref[...] = jnp.full(m_ref.shape, NEG_BIG, F32)
    acc_ref[...] = jnp.zeros(acc_ref.shape, F32)
    rq = ATT_RQ
    upper = (lax.broadcasted_iota(I32, (rq, rq), 1) > lax.broadcasted_iota(I32, (rq, rq), 0))
    lane_q = lax.broadcasted_iota(I32, (rq, LANES), 1)

    def n_pieces(g):
        return -(-(g + 1) * rq // tk)

    n_chunks = seq // rq
    chains = [(g, kp, par) for kp in range(n_pieces(n_chunks - 1)) for g in range(n_chunks)
              if kp < n_pieces(g) for par in range(2)]

    def n_keys(g, kp):
        return min(tk, (g + 1) * rq - kp * tk)

    def scores(chain):
        g, kp, par = chain
        k = k_ref[kp * tk:kp * tk + n_keys(g, kp), :]
        return _dot_nt(qm_ref[par, g * rq:(g + 1) * rq, :], k)

    def finish(chain, s):
        g, kp, par = chain
        rows = slice(g * rq, (g + 1) * rq)
        nk = n_keys(g, kp)
        ck = ck_ref[par:par + 1, kp * tk:kp * tk + nk]
        cq = cqc_ref[par, rows, :]
        sc = []
        for kc in range(nk // rq):
            t = s[:, kc * rq:(kc + 1) * rq] - ck[:, kc * rq:(kc + 1) * rq]
            if kp * (tk // rq) + kc == g:
                t = jnp.where(upper, NEG_BIG, t)
            sc.append(t)
        mx = sc[0]
        for t in sc[1:]:
            mx = jnp.maximum(mx, t)
        m_old = m_ref[par, rows, :]
        m_new = jnp.maximum(m_old, jnp.max(mx, axis=-1, keepdims=True) + cq)
        alpha = jnp.exp2(m_old - m_new)
        shift = m_new - cq
        p = jnp.concatenate([jnp.exp2(t - shift).astype(BF16) for t in sc], axis=1)
        acc_ref[par, rows, :] = alpha * acc_ref[par, rows, :] + _dot(p, va_ref[par, kp * tk:kp * tk + nk, :])
        m_ref[par, rows, :] = m_new
        if kp == n_pieces(g) - 1 and par == 1:
            acc0 = acc_ref[0, rows, :]
            acc1 = acc_ref[1, rows, :]
            out0 = acc0 / acc0[:, FOX_DH:FOX_DH + 1]
            out1 = acc1 / acc1[:, 0:1]
            o_ref[rows, :] = jnp.where(lane_q < FOX_DH, out0, out1).astype(BF16)

    _run_chains(chains, FOX_LOOKAHEAD, scores, finish)


def _fox_attention(q, k, v, cum, cum_t):
    b, s, d = q.shape
    tq = min(FOX_TQ, s)
    npair = d // LANES
    seq_blk = lambda bi, h: (bi, 0, h)
    return pl.pallas_call(
        functools.partial(_fox_attn_kernel, tq=tq),
        grid=(b, npair),
        in_specs=[
            pl.BlockSpec((None, s, LANES), seq_blk),
            pl.BlockSpec((None, s, LANES), seq_blk),
            pl.BlockSpec((None, s, LANES), seq_blk),
            pl.BlockSpec((None, s, LANES), lambda bi, h: (bi, 0, 0)),
            pl.BlockSpec((None, None, 2, s), lambda bi, h: (bi, h, 0, 0)),
        ],
        out_specs=pl.BlockSpec((None, s, LANES), seq_blk),
        out_shape=jax.ShapeDtypeStruct((b, s, d), BF16),
        scratch_shapes=[pltpu.VMEM((2, s, LANES), BF16), pltpu.VMEM((2, s, LANES), BF16),
                        pltpu.VMEM((2, s, LANES), F32), pltpu.VMEM((2, s, LANES), F32),
                        pltpu.VMEM((2, s, LANES), F32)],
        compiler_params=_cparams("parallel", "parallel"),
        name="fox_attention",
    )(q, k, v, cum, cum_t)


def _outproj_router_kernel(*refs, n_y):
    y_refs = refs[:n_y]
    (w_ref, x_ref, g_ref, b_ref, rwt_ref, tri_ref, h_ref, route_ref, col_ref, cnt_ref, xs_ref,
     hbf_ref) = refs[n_y:]
    rw2 = rwt_ref[...]
    rw_hi = rw2.astype(BF16)
    rw_lo = (rw2 - rw_hi.astype(F32)).astype(BF16)
    lane_w = lax.broadcasted_iota(I32, rw2.shape, 1)
    w = jnp.where(lane_w < N_EXPERTS, rw_hi, rw_lo)
    tm = x_ref.shape[0]
    rc = ROUTER_ROWS
    logit_chunks = []

    def project(c):
        rows = slice(c * rc, (c + 1) * rc)
        mix = None
        k0 = 0
        for yr in y_refs:
            k1 = k0 + yr.shape[1]
            part = _dot(yr[rows, :], w_ref[k0:k1, :].astype(BF16))
            mix = part if mix is None else mix + part
            k0 = k1
        return mix

    def norm_and_logits(c, mix):
        rows = slice(c * rc, (c + 1) * rc)
        h = _layer_norm_rows(DEEPNORM_ALPHA * x_ref[rows, :] + mix, g_ref[...], b_ref[...])
        h_ref[rows, :] = h
        h_hi = h.astype(BF16)
        hbf_ref[rows, :] = h_hi
        h_lo = (h - h_hi.astype(F32)).astype(BF16)
        p_hi = _dot(h_hi, w)
        p_lo = _dot(h_lo, w)
        slab = p_hi + (pltpu.roll(p_hi, LANES - N_EXPERTS, 1) + p_lo)
        logit_chunks.append(slab.T[0:N_EXPERTS])

    _run_chains(list(range(tm // rc)), 1, project, norm_and_logits)

    logits = jnp.concatenate(logit_chunks, axis=1)
    row = lax.broadcasted_iota(I32, (N_EXPERTS, tm), 0)
    mx = jnp.max(logits, axis=0, keepdims=True)
    ex = jnp.exp(logits - mx)
    probs = ex / jnp.sum(ex, axis=0, keepdims=True)
    grp = row // EXPERTS_PER_GROUP

    def top2(vals):
        v1 = jnp.max(vals, axis=0, keepdims=True)
        i1 = jnp.min(jnp.where(vals == v1, row, N_EXPERTS), axis=0, keepdims=True)
        rest = jnp.where(row == i1, -2.0, vals)
        v2 = jnp.max(rest, axis=0, keepdims=True)
        i2 = jnp.min(jnp.where(rest == v2, row, N_EXPERTS), axis=0, keepdims=True)
        return v1, i1, v2, i2

    best_score = None
    best = None
    for gi in range(N_GROUPS):
        v1, _, v2, _ = top2(jnp.where(grp == gi, probs, -1.0))
        score = v1 + v2
        if gi == 0:
            best_score, best = score, jnp.zeros_like(score, dtype=I32)
        else:
            better = score > best_score
            best = jnp.where(better, gi, best)
            best_score = jnp.where(better, score, best_score)
    v1, i1, v2, i2 = top2(jnp.where(grp == best, probs, -1.0))
    tot = v1 + v2
    g1 = v1 / tot
    g2 = v2 / tot

    onehot = jnp.where((row == i1) | (row == i2), 1.0, 0.0)
    pref = _dot(onehot.astype(BF16), tri_ref[...])
    cnt = jnp.broadcast_to(jnp.sum(onehot, axis=1, keepdims=True), cnt_ref.shape)
    cnt_ref[...] = cnt
    grp_rows = jnp.floor((cnt + (SORT_ALIGN - 1)) * (1.0 / SORT_ALIGN)) * SORT_ALIGN
    row_c = lax.broadcasted_iota(I32, cnt.shape, 0)
    start = grp_rows
    step = 1
    while step < N_EXPERTS:
        start = start + jnp.where(row_c >= step, pltpu.roll(start, step, 0), 0.0)
        step *= 2
    start = (start - grp_rows)[:, 0:1]
    s1 = jnp.sum(jnp.where(row == i1, pref + start, 0.0), axis=0, keepdims=True)
    s2 = jnp.sum(jnp.where(row == i2, pref + start, 0.0), axis=0, keepdims=True)

    row8 = lax.broadcasted_iota(I32, (ROUTE_ROWS, tm), 0)
    fields = (i1.astype(F32), i2.astype(F32), g1, g2, s1, s2)
    route = jnp.zeros((ROUTE_ROWS, tm), F32)
    for n, f in enumerate(fields):
        route = jnp.where(row8 == n, f, route)
    route_ref[...] = route
    col_ref[...] = jnp.concatenate([route, jnp.zeros((LANES - ROUTE_ROWS, tm), F32)], axis=0).T

    srow = lax.broadcasted_iota(I32, (xs_ref.shape[0], tm), 0).astype(F32)
    perm = jnp.where((srow == s1) | (srow == s2), 1.0, 0.0).astype(BF16)
    xs_ref[...] = _dot(perm, hbf_ref[...]).astype(BF16)


def _outproj_router(ys, w_all, layer, x2, ln_g, ln_b, rwt, tri):
    t, d = x2.shape
    tm = tri.shape[0]
    n_local = _local_rows(tm)
    nt = t // tm
    row = lambda i: (i, 0)
    full = lambda i: (0, 0)
    n_y = len(ys)
    kern = functools.partial(_outproj_router_kernel, n_y=n_y)
    return pl.pallas_call(
        kern,
        grid=(nt,),
        in_specs=([pl.BlockSpec((tm, y.shape[1]), row) for y in ys]
                  + [_layer_weight_spec(w_all, layer)]
                  + [pl.BlockSpec((tm, d), row), pl.BlockSpec((1, d), full), pl.BlockSpec((1, d), full),
                     pl.BlockSpec(rwt.shape, full), pl.BlockSpec(tri.shape, full)]),
        out_specs=[pl.BlockSpec((tm, d), row), pl.BlockSpec((ROUTE_ROWS, tm), lambda i: (0, i)),
                   pl.BlockSpec((tm, LANES), row), pl.BlockSpec((N_EXPERTS, LANES), row),
                   pl.BlockSpec((n_local, d), row)],
        out_shape=[jax.ShapeDtypeStruct((t, d), F32), jax.ShapeDtypeStruct((ROUTE_ROWS, t), F32),
                   jax.ShapeDtypeStruct((t, LANES), F32), jax.ShapeDtypeStruct((nt * N_EXPERTS, LANES), F32),
                   jax.ShapeDtypeStruct((nt * n_local, d), BF16)],
        scratch_shapes=[pltpu.VMEM((tm, d), BF16)],
        compiler_params=_cparams("parallel"),
        name="outproj_router",
    )(*ys, w_all, x2, ln_g, ln_b, rwt, tri)


def _start_tile_gather(tile, loff_ref, rows_ref, gpos_ref, sorted_hbm, local_ref, sem):
    for e in range(N_EXPERTS):
        n = tile * N_EXPERTS + e

        @pl.when(rows_ref[n] > 0)
        def _(n=n):
            size = pl.multiple_of(rows_ref[n], SORT_ALIGN)
            src = sorted_hbm.at[pl.ds(pl.multiple_of(gpos_ref[n], SORT_ALIGN), size)]
            dst = local_ref.at[pl.ds(pl.multiple_of(loff_ref[n], SORT_ALIGN), size)]
            pltpu.make_async_copy(src, dst, sem).start()


def _wait_rows(n_rows, src_hbm, dst_ref, sem):
    n = pl.multiple_of(n_rows, SORT_ALIGN)
    pltpu.make_async_copy(src_hbm.at[pl.ds(0, n)], dst_ref.at[pl.ds(0, n)], sem).wait()


def _expert_kernel(te_ref, nx_ref, nt_ref, g0_ref, g1_ref, valid_ref, loff_ref, rows_ref, gpos_ref,
                   xs_hbm, wg_hbm, wu_hbm, wd_hbm, o_ref,
                   xbuf, semx, wgs, wus, wds, wgb, wub, wdb, sem, nsw_ref, *, layer, n_local):
    r = pl.program_id(0)
    nt = nt_ref[0]
    cur = te_ref[r]
    tr = xbuf.shape[1]
    xslot = r % 2

    def start_pieces(tile, dst_slot):
        e = te_ref[tile]
        base = tile * tr

        def piece(tau, c):
            n = tau * N_EXPERTS + e
            lo = jnp.maximum(gpos_ref[n], base)
            hi = jnp.minimum(gpos_ref[n] + rows_ref[n], base + tr)

            @pl.when(hi > lo)
            def _():
                size = pl.multiple_of(hi - lo, SORT_ALIGN)
                src = pl.multiple_of(tau * n_local + loff_ref[n] + (lo - gpos_ref[n]), SORT_ALIGN)
                dst = pl.multiple_of(lo - base, SORT_ALIGN)
                pltpu.make_async_copy(xs_hbm.at[pl.ds(src, size)], xbuf.at[dst_slot, pl.ds(dst, size)],
                                      semx.at[dst_slot]).start()

            return c

        lax.fori_loop(g0_ref[tile], g1_ref[tile] + 1, piece, 0)

    @pl.when(r == 0)
    def _():
        xbuf[...] = jnp.zeros_like(xbuf)
        start_pieces(0, 0)

    @pl.when(r + 1 < nt)
    def _():
        start_pieces(r + 1, 1 - xslot)

    def weight_copies(e, slot):
        return [pltpu.make_async_copy(src.at[layer, e], dst.at[slot], sem.at[slot])
                for src, dst in ((wg_hbm, wgs), (wu_hbm, wus), (wd_hbm, wds))]

    @pl.when(r == 0)
    def _():
        nsw_ref[0] = 0
        for cp in weight_copies(cur, 0):
            cp.start()

    @pl.when((r == 0) | (cur != te_ref[jnp.maximum(r - 1, 0)]))
    def _():
        slot = nsw_ref[0] % 2
        nsw_ref[0] = nsw_ref[0] + 1
        for cp in weight_copies(cur, slot):
            cp.wait()

        @pl.when(nx_ref[r] != cur)
        def _():
            for cp in weight_copies(nx_ref[r], 1 - slot):
                cp.start()

        wgb[...] = wgs[slot].astype(BF16)
        wub[...] = wus[slot].astype(BF16)
        wdb[...] = wds[slot].astype(BF16)

    @pl.when(r < nt)
    def _():
        _wait_rows(valid_ref[r], xs_hbm, xbuf.at[xslot], semx.at[xslot])
        rc = EXPERT_ROWS

        def up(c):
            x = xbuf[xslot, c * rc:(c + 1) * rc, :]
            return _dot(x, wgb[...]), _dot(x, wub[...])

        def down(c, gate_up):
            a = _silu(gate_up[0]) * gate_up[1]
            o_ref[c * rc:(c + 1) * rc, :] = _dot(a.astype(BF16), wdb[...]).astype(BF16)

        _run_chains(list(range(tr // rc)), 1, up, down)

    @pl.when(r >= nt)
    def _():
        o_ref[...] = jnp.zeros_like(o_ref)


def _expert_mlps(meta, n_slots, xs_local, wg, wu, wd, layer):
    d = xs_local.shape[1]
    tr = MOE_TR
    dff = wg.shape[3]
    grid_spec = pltpu.PrefetchScalarGridSpec(
        num_scalar_prefetch=9,
        grid=(n_slots // tr,),
        in_specs=[pl.BlockSpec(memory_space=pl.ANY)] * 4,
        out_specs=pl.BlockSpec((tr, d), lambda r, *_: (r, 0)),
        scratch_shapes=[pltpu.VMEM((2, tr, d), BF16), pltpu.SemaphoreType.DMA((2,)),
                        pltpu.VMEM((2, d, dff), F32), pltpu.VMEM((2, d, dff), F32), pltpu.VMEM((2, dff, d), F32),
                        pltpu.VMEM((d, dff), BF16), pltpu.VMEM((d, dff), BF16), pltpu.VMEM((dff, d), BF16),
                        pltpu.SemaphoreType.DMA((2,)), pltpu.SMEM((1,), I32)],
    )
    return pl.pallas_call(
        functools.partial(_expert_kernel, layer=layer, n_local=_local_rows(MOE_TM)),
        grid_spec=grid_spec,
        out_shape=jax.ShapeDtypeStruct((n_slots, d), BF16),
        compiler_params=_cparams("arbitrary"),
        name="expert_mlps",
    )(meta["tile_expert"], meta["next_expert"], meta["n_tiles"], meta["first_group"], meta["last_group"],
      meta["valid"], meta["loff"], meta["rows"], meta["gpos"], xs_local, wg, wu, wd)


def _combine_kernel(loff_ref, rows_ref, gpos_ref, tot_ref, ys_hbm, h_ref, col_ref, p_ref, g_ref, b_ref,
                    pg_ref, pp_ref, o_ref, ybuf, sem):
    i = pl.program_id(0)
    n = pl.num_programs(0)
    tm = h_ref.shape[0]
    n_local = ybuf.shape[1]
    slot = i % 2

    @pl.when(i == 0)
    def _():
        ybuf[...] = jnp.zeros_like(ybuf)
        _start_tile_gather(0, loff_ref, rows_ref, gpos_ref, ys_hbm, ybuf.at[0], sem.at[0])

    @pl.when(i + 1 < n)
    def _():
        _start_tile_gather(i + 1, loff_ref, rows_ref, gpos_ref, ys_hbm, ybuf.at[1 - slot], sem.at[1 - slot])

    _wait_rows(tot_ref[i], ys_hbm, ybuf.at[slot], sem.at[slot])
    rc = COMBINE_ROWS
    scol = lax.broadcasted_iota(I32, (rc, n_local), 1).astype(F32)

    def gather_rows(c):
        rows = slice(c * rc, (c + 1) * rc)
        cols = col_ref[rows, :]
        pick = jnp.zeros((rc, n_local), F32)
        for k in range(2):
            pick = jnp.where(scol == cols[:, ROUTE_SLOT + k:ROUTE_SLOT + k + 1],
                             cols[:, ROUTE_GATE + k:ROUTE_GATE + k + 1], pick)
        ffn = _dot(pick.astype(BF16), ybuf[slot])
        return ffn, _dot(p_ref[rows, :].astype(BF16), pp_ref[...].astype(BF16))

    def finish(c, gathered):
        rows = slice(c * rc, (c + 1) * rc)
        ffn, pe = gathered
        h2 = _layer_norm_rows(DEEPNORM_ALPHA * h_ref[rows, :] + ffn, g_ref[...], b_ref[...])
        gate = _sigmoid(_dot(h2.astype(BF16), pg_ref[...].astype(BF16)))
        o_ref[rows, :] = h2 + gate * pe

    _run_chains(list(range(tm // rc)), 1, gather_rows, finish)


def _combine(meta, ys, h, cols, p3, layer, ln_g, ln_b, ple_gate, ple_proj):
    t, d = h.shape
    tm = MOE_TM
    pdim = p3.shape[2]
    row = lambda i, *_: (i, 0)
    full = lambda i, *_: (0, 0)
    grid_spec = pltpu.PrefetchScalarGridSpec(
        num_scalar_prefetch=4,
        grid=(t // tm,),
        in_specs=[
            pl.BlockSpec(memory_space=pl.ANY),
            pl.BlockSpec((tm, d), row),
            pl.BlockSpec((tm, LANES), row),
            pl.BlockSpec((None, tm, pdim), lambda i, *_: (layer, i, 0)),
            pl.BlockSpec((1, d), full),
            pl.BlockSpec((1, d), full),
            pl.BlockSpec((None,) + ple_gate.shape[1:], lambda i, *_: (layer, 0, 0), pipeline_mode=pl.Buffered(1)),
            pl.BlockSpec((None,) + ple_proj.shape[1:], lambda i, *_: (layer, 0, 0), pipeline_mode=pl.Buffered(1)),
        ],
        out_specs=pl.BlockSpec((tm, d), row),
        scratch_shapes=[pltpu.VMEM((2, _local_rows(tm), d), BF16), pltpu.SemaphoreType.DMA((2,))],
    )
    return pl.pallas_call(
        _combine_kernel,
        grid_spec=grid_spec,
        out_shape=jax.ShapeDtypeStruct((t, d), F32),
        compiler_params=_cparams("arbitrary"),
        name="moe_combine",
    )(meta["loff"], meta["rows"], meta["gpos"], meta["tot"], ys, h, cols, p3, ln_g, ln_b, ple_gate, ple_proj)


def _rotary_tables(seq):
    half = RET_DK // 2
    inv = (np.float32(ROPE_BASE) ** (-np.arange(half, dtype=np.float32) / np.float32(half))).astype(np.float32)
    ang = (np.arange(seq, dtype=np.float32)[:, None] * inv[None, :]).astype(np.float32)
    cos = np.cos(ang.astype(np.float64))
    sin = np.sin(ang.astype(np.float64))
    cos_h = np.concatenate([cos, cos], axis=1)
    sin_h = np.concatenate([-sin, sin], axis=1)
    return (jnp.asarray(np.tile(cos_h, (1, RET_HEADS)), F32), jnp.asarray(np.tile(sin_h, (1, RET_HEADS)), F32))


def _retention_tables():
    c = RET_CHUNK
    h = np.arange(RET_HEADS, dtype=np.float64)
    log_g = np.log1p(-np.exp2(-5.0 - h))
    j = np.arange(c, dtype=np.float64)
    rel = j[:, None] - j[None, :]
    din = np.where(rel >= 0, np.exp(np.maximum(rel, 0.0)[None] * log_g[:, None, None]), 0.0)
    qd = np.exp((j + 1.0)[None] * log_g[:, None])
    kd = np.exp((c - 1.0 - j)[None] * log_g[:, None])
    cd = np.exp(c * log_g)
    qd = np.broadcast_to(qd[:, :, None], (RET_HEADS, c, LANES))
    kd = np.broadcast_to(kd[:, :, None], (RET_HEADS, c, LANES))
    cd = np.broadcast_to(cd[:, None, None], (RET_HEADS, 1, LANES))
    return tuple(jnp.asarray(a, F32) for a in (din, qd, kd, cd))


def _t5_bucket_np(dist):
    max_exact = REL_BUCKETS // 2
    d = np.maximum(dist, 1).astype(np.float32)
    large = max_exact + (np.log(d / np.float32(max_exact)) / np.float32(math.log(REL_MAX_DIST / max_exact))
                         * np.float32(REL_BUCKETS - max_exact)).astype(np.int32)
    large = np.minimum(large, REL_BUCKETS - 1)
    return np.where(dist < max_exact, dist, large)


def _diff_bias_tables(rel_bias, seq):
    c = ATT_RQ
    r = np.arange(c)
    dist0 = r[:, None] - r[None, :]
    far = REL_BUCKETS - 1
    assert np.all(_t5_bucket_np(np.arange(c + 1, max(seq, 2 * c))) == far)
    bidx = np.stack([_t5_bucket_np(np.maximum(dist0, 0)), _t5_bucket_np(dist0 + c)])
    rb = rel_bias.astype(F32).T
    shifted = (rb - rb[:, far:far + 1]) * LOG2E
    bidx = jnp.asarray(bidx, I32)[None]
    tab = jnp.zeros((rb.shape[0], 2, c, c), F32)
    for bucket in range(REL_BUCKETS - 1):
        tab = jnp.where(bidx == bucket, shifted[:, bucket][:, None, None, None], tab)
    causal = jnp.asarray(np.stack([dist0 >= 0, np.ones_like(dist0, bool)]))[None]
    return jnp.where(causal, tab, NEG_BIG)


def _local_rows(tm):
    need = 2 * tm + N_EXPERTS * (SORT_ALIGN - 1)
    return -(-need // LANES) * LANES


def _round_up(a, m):
    return ((a + m - 1) // m) * m


def _route_meta(cnt, t):
    tm, tr = MOE_TM, MOE_TR
    nt = t // tm
    counts = cnt.reshape(nt, N_EXPERTS, LANES)[:, :, 0].astype(I32)
    rows = _round_up(counts, SORT_ALIGN)
    loff = jnp.cumsum(rows, axis=1) - rows
    seg = jnp.sum(rows, axis=0)
    seg_pad = _round_up(seg, tr)
    ends = jnp.cumsum(seg_pad)
    offs = ends - seg_pad
    gpos = offs[None, :] + jnp.cumsum(rows, axis=0) - rows
    n_slots = 2 * t + nt * N_EXPERTS * (SORT_ALIGN - 1)
    n_slots = _round_up(n_slots, tr) + N_EXPERTS * tr
    n_tiles = (ends[-1] // tr).astype(I32)
    tile_start = jnp.arange(n_slots // tr, dtype=I32) * tr
    tile_expert = jnp.sum((tile_start[:, None] >= ends[None, :]).astype(I32), axis=1)
    last = jnp.sum((((n_tiles - 1) * tr) >= ends).astype(I32))
    tile_expert = jnp.minimum(tile_expert, last).astype(I32)
    eid = jnp.arange(N_EXPERTS, dtype=I32)
    later = (eid[None, :] > eid[:, None]) & (seg_pad > 0)[None, :]
    nxt = jnp.min(jnp.where(later, eid[None, :], N_EXPERTS), axis=1)
    nxt = jnp.where(nxt == N_EXPERTS, eid, nxt)
    next_expert = jnp.sum(jnp.where(tile_expert[:, None] == eid[None, :], nxt[None, :], 0), axis=1).astype(I32)
    own = tile_expert[:, None] == eid[None, :]
    pick = lambda tab: jnp.sum(jnp.where(own[:, None, :], tab[None, :, :], 0), axis=-1)
    g_start, g_end = pick(gpos), pick(gpos + rows)
    used = (tile_start < n_tiles * tr)[:, None]
    first_group = jnp.sum((g_end <= tile_start[:, None]).astype(I32), axis=1)
    last_group = jnp.sum(((g_start < tile_start[:, None] + tr) & used).astype(I32), axis=1) - 1
    seg_end = jnp.sum(jnp.where(own, (offs + seg)[None, :], 0), axis=1)
    valid = jnp.clip(seg_end - tile_start, 0, tr)
    meta = {
        "loff": loff.reshape(-1).astype(I32), "rows": rows.reshape(-1).astype(I32),
        "gpos": gpos.reshape(-1).astype(I32), "tot": jnp.sum(rows, axis=1).astype(I32),
        "tile_expert": tile_expert, "next_expert": next_expert, "n_tiles": n_tiles.reshape(1),
        "first_group": first_group.astype(I32), "last_group": last_group.astype(I32), "valid": valid.astype(I32),
    }
    return meta, n_slots


def kernel(x, p, rel_bias, router_w, even_w_in, even_w_out, even_lambda, even_diff_norm, even_ret_norm,
           odd_w_in, odd_b_forget, odd_w_out, ln_mix_g, ln_mix_b, ln_ffn_g, ln_ffn_b,
           moe_w_gate, moe_w_up, moe_w_down, ple_proj, ple_gate):
    b, s, d = x.shape
    t = b * s
    assert d == FOX_HEADS * FOX_DH and p.shape[0] == DEPTH and even_w_in.shape[2] == EVEN_CUTS[-1]
    assert odd_w_in.shape[2] == 3 * d + FOX_HEADS and moe_w_gate.shape[1] == N_EXPERTS
    assert s % RET_CHUNK == 0 and s % min(ATT_TQ, s) == 0 and s % min(FOX_TQ, s) == 0
    assert t % min(PROJ_TM, s) == 0 and t % MOE_TM == 0

    cos_t, sin_t = _rotary_tables(s)
    ret_tabs = _retention_tables()
    bias_tab = _diff_bias_tables(rel_bias, s)
    rw32 = router_w.astype(F32)
    rwt = jnp.zeros((d, LANES), F32).at[:, :N_EXPERTS].set(rw32).at[:, N_EXPERTS:2 * N_EXPERTS].set(rw32)
    tok = np.arange(MOE_TM)
    tri = jnp.asarray(tok[:, None] < tok[None, :], BF16)

    x2 = x.reshape(t, d)
    for i in range(DEPTH):
        j = i // 2
        if i % 2 == 0:
            lam_init = 0.8 - 0.6 * math.exp(-0.3 * i)
            qa, ka, va, qb, kb, vb, gb = _even_inproj(x2, even_w_in, j, cos_t, sin_t, s)
            sh = lambda a: a.reshape(b, s, a.shape[1])
            ya = _diff_attention(sh(qa), sh(ka), sh(va), bias_tab, even_lambda[j].astype(F32),
                                 even_diff_norm[j].reshape(1, -1).astype(F32), lam_init)
            yb = _retention(sh(qb), sh(kb), sh(vb), sh(gb), ret_tabs, even_ret_norm[j].reshape(1, -1).astype(F32))
            ys = [ya.reshape(t, -1), yb.reshape(t, -1)]
            w_out = even_w_out
        else:
            wf = jnp.zeros((d, LANES), BF16).at[:, :FOX_HEADS].set(odd_w_in[j, :, 3 * d:].astype(BF16))
            bfg = jnp.zeros((1, LANES), F32).at[0, :FOX_HEADS].set(odd_b_forget[j].astype(F32))
            q, k, v, cum, cum_t = _odd_inproj(x2, odd_w_in, j, wf, bfg, s)
            y = _fox_attention(q.reshape(b, s, d), k.reshape(b, s, d), v.reshape(b, s, d),
                               cum.reshape(b, s, LANES), cum_t)
            ys = [y.reshape(t, d)]
            w_out = odd_w_out
        h, route, cols, cnt, xs_local = _outproj_router(ys, w_out, j, x2, ln_mix_g[i].reshape(1, d),
                                                        ln_mix_b[i].reshape(1, d), rwt, tri)
        meta, n_slots = _route_meta(cnt, t)
        rows = _expert_mlps(meta, n_slots, xs_local, moe_w_gate, moe_w_up, moe_w_down, i)
        x2 = _combine(meta, rows, h, cols, p.reshape(DEPTH, t, -1), i, ln_ffn_g[i].reshape(1, d),
                      ln_ffn_b[i].reshape(1, d), ple_gate, ple_proj)
    return x2.reshape(b, s, d)
```

```python
import functools
import math

import numpy as np
import jax
import jax.numpy as jnp
from jax import lax
from jax.experimental import pallas as pl
from jax.experimental.pallas import tpu as pltpu

F32 = jnp.float32
BF16 = jnp.bfloat16
I32 = jnp.int32

DIFF_HEADS = 4
DIFF_DK = 64
RET_HEADS = 4
RET_DK = 64
RET_DV = 128
RET_CHUNK = 128
FOX_HEADS = 16
FOX_DH = 64
REL_BUCKETS = 32
REL_MAX_DIST = 128
N_GROUPS = 4
EXPERTS_PER_GROUP = 4
N_EXPERTS = 16
DEPTH = 2
DEEPNORM_ALPHA = (2 * DEPTH) ** 0.25
LN_EPS = 1e-5
ROPE_BASE = 10000.0
NEG_BIG = -1e30
LOG2E = math.log2(math.e)
RET_QK = RET_HEADS * RET_DK
EVEN_WIDTHS = (DIFF_HEADS * 2 * DIFF_DK, DIFF_HEADS * 2 * DIFF_DK, DIFF_HEADS * 2 * DIFF_DK,
               RET_QK, RET_QK, RET_HEADS * RET_DV, RET_HEADS * RET_DV)
EVEN_CUTS = tuple(int(v) for v in np.cumsum(EVEN_WIDTHS))

VMEM_LIMIT_BYTES = 48 * 1024 * 1024
LANES = 128

PROJ_TM = 1024
ATT_TQ = 1024
FOX_TQ = 512
ATT_RQ = LANES
DIFF_LOOKAHEAD = 2
FOX_LOOKAHEAD = 4
MOE_TR = 512
MOE_TM = 512
SORT_ALIGN = 16
ROUTE_ROWS = 8
ROUTE_GATE = 2
ROUTE_SLOT = 4
COMBINE_ROWS = 256
EXPERT_ROWS = 256
ROUTER_ROWS = 256


def _cparams(*sem):
    return pltpu.CompilerParams(dimension_semantics=sem, vmem_limit_bytes=VMEM_LIMIT_BYTES)


def _dot(a, b):
    return jnp.dot(a, b, preferred_element_type=F32)


def _dot_nt(a, b):
    return lax.dot_general(a, b, (((1,), (1,)), ((), ())), preferred_element_type=F32)


def _layer_norm_rows(z, g, b):
    mu = jnp.mean(z, axis=-1, keepdims=True)
    zc = z - mu
    var = jnp.mean(zc * zc, axis=-1, keepdims=True)
    return zc * lax.rsqrt(var + LN_EPS) * g + b


def _silu(x):
    return x * (1.0 / (1.0 + jnp.exp(-x)))


def _sigmoid(x):
    return 1.0 / (1.0 + jnp.exp(-x))


def _even_inproj_kernel(x_ref, w_ref, cos_ref, sin_ref,
                        qa_ref, ka_ref, va_ref, qb_ref, kb_ref, vb_ref, gb_ref):
    x = x_ref[...].astype(BF16)

    def mm(c0, c1):
        return _dot(x, w_ref[:, c0:c1].astype(BF16))

    c = EVEN_CUTS
    qa_ref[...] = (mm(0, c[0]) * (DIFF_DK ** -0.5 * LOG2E)).astype(BF16)
    ka_ref[...] = mm(c[0], c[1]).astype(BF16)
    va_ref[...] = mm(c[1], c[2]).astype(BF16)
    qk = mm(c[2], c[4])
    cos = cos_ref[...]
    sin = sin_ref[...]
    lane = lax.broadcasted_iota(I32, cos.shape, 1)
    first_half = (lane % RET_DK) < (RET_DK // 2)

    def rot(t):
        sw = jnp.where(first_half, pltpu.roll(t, t.shape[1] - RET_DK // 2, 1),
                       pltpu.roll(t, RET_DK // 2, 1))
        return t * cos + sw * sin

    qb_ref[...] = rot(qk[:, :RET_QK]).astype(BF16)
    kb_ref[...] = (rot(qk[:, RET_QK:]) * (RET_DK ** -0.5)).astype(BF16)
    vb_ref[...] = mm(c[4], c[5]).astype(BF16)
    gb_ref[...] = mm(c[5], c[6]).astype(BF16)


def _layer_weight_spec(w_all, layer):
    return pl.BlockSpec((None,) + w_all.shape[1:], lambda *_: (layer, 0, 0), pipeline_mode=pl.Buffered(1))


def _even_inproj(x2, w_all, layer, cos_t, sin_t, seq):
    t, d = x2.shape
    tm = min(PROJ_TM, seq)
    nblk_s = seq // tm
    widths = EVEN_WIDTHS
    row = lambda i: (i, 0)
    return pl.pallas_call(
        _even_inproj_kernel,
        grid=(t // tm,),
        in_specs=[
            pl.BlockSpec((tm, d), row),
            _layer_weight_spec(w_all, layer),
            pl.BlockSpec((tm, RET_QK), lambda i: (i % nblk_s, 0)),
            pl.BlockSpec((tm, RET_QK), lambda i: (i % nblk_s, 0)),
        ],
        out_specs=[pl.BlockSpec((tm, w), row) for w in widths],
        out_shape=[jax.ShapeDtypeStruct((t, w), BF16) for w in widths],
        compiler_params=_cparams("parallel"),
        name="even_inproj",
    )(x2, w_all, cos_t, sin_t)


def _run_chains(chains, lookahead, scores, finish):
    pending = [scores(c) for c in chains[:lookahead]]
    for n, chain in enumerate(chains):
        if n + lookahead < len(chains):
            pending.append(scores(chains[n + lookahead]))
        finish(chain, pending.pop(0))


def _diff_attn_kernel(lam_ref, q_ref, k_ref, v_ref, bias_ref, g_ref, o_ref, qm_ref, m_ref, l_ref, acc_ref,
                      *, lam_init, tq):
    seq = q_ref.shape[0]
    tk = tq
    rq = ATT_RQ
    nr = tq // rq
    lane = lax.broadcasted_iota(I32, (rq, LANES), 1)
    for n in range(seq // rq):
        q = q_ref[n * rq:(n + 1) * rq, :]
        zero = jnp.zeros_like(q)
        qm_ref[n, 0:rq, :] = jnp.where(lane < DIFF_DK, q, zero)
        qm_ref[n, rq:2 * rq, :] = jnp.where(lane >= DIFF_DK, q, zero)
    m_ref[...] = jnp.full(m_ref.shape, NEG_BIG, F32)
    l_ref[...] = jnp.zeros(l_ref.shape, F32)
    acc_ref[...] = jnp.zeros(acc_ref.shape, F32)
    lp = lam_ref[...]
    lam = (jnp.exp(jnp.sum(lp[0:1, :] * lp[1:2, :], axis=-1, keepdims=True))
           - jnp.exp(jnp.sum(lp[2:3, :] * lp[3:4, :], axis=-1, keepdims=True)) + lam_init)
    bias2 = [jnp.concatenate([bias_ref[n], bias_ref[n]], axis=0) for n in range(2)]

    chains = [(ii, j, r) for j in range(seq // tq) for ii in range(j, seq // tq) for r in range(nr)]

    def n_keys(ii, j, r):
        return (r + 1) * rq if j == ii else tk

    def scores(chain):
        ii, j, r = chain
        k = k_ref[j * tk:j * tk + n_keys(ii, j, r), :]
        return _dot_nt(qm_ref[ii * nr + r], k)

    def finish(chain, s):
        ii, j, r = chain
        g = ii * nr + r
        nk = n_keys(ii, j, r)
        v = v_ref[j * tk:j * tk + nk, :]
        sc = []
        for kc in range(nk // rq):
            t = s[:, kc * rq:(kc + 1) * rq]
            back = g - (j * nr + kc)
            if back <= 1:
                t = t + bias2[back]
            sc.append(t)
        mx = sc[0]
        for t in sc[1:]:
            mx = jnp.maximum(mx, t)
        m_old = m_ref[g]
        m_new = jnp.maximum(m_old, jnp.max(mx, axis=-1, keepdims=True))
        alpha = jnp.exp2(m_old - m_new)
        ps = [jnp.exp2(t - m_new) for t in sc]
        psum = ps[0]
        for t in ps[1:]:
            psum = psum + t
        l_ref[g] = alpha * l_ref[g] + psum
        p = jnp.concatenate([t.astype(BF16) for t in ps], axis=1)
        acc_ref[g] = alpha * acc_ref[g] + _dot(p, v)
        m_ref[g] = m_new
        if j == ii:
            l_all = jnp.sum(l_ref[g], axis=-1, keepdims=True)
            a = acc_ref[g] / l_all
            o = a[0:rq] - lam * a[rq:2 * rq]
            o = o * lax.rsqrt(jnp.mean(o * o, axis=-1, keepdims=True) + LN_EPS)
            o_ref[g * rq:(g + 1) * rq, :] = (o * g_ref[...] * (1.0 - lam_init)).astype(BF16)

    _run_chains(chains, DIFF_LOOKAHEAD, scores, finish)


def _diff_attention(qa, ka, va, bias_tab, lam_params, diff_g, lam_init):
    b, s, _ = qa.shape
    tq = min(ATT_TQ, s)
    kern = functools.partial(_diff_attn_kernel, lam_init=lam_init, tq=tq)
    seq_blk = lambda bi, h: (bi, 0, h)
    return pl.pallas_call(
        kern,
        grid=(b, DIFF_HEADS),
        in_specs=[
            pl.BlockSpec(lam_params.shape, lambda bi, h: (0, 0)),
            pl.BlockSpec((None, s, LANES), seq_blk),
            pl.BlockSpec((None, s, LANES), seq_blk),
            pl.BlockSpec((None, s, LANES), seq_blk),
            pl.BlockSpec((None, 2, ATT_RQ, ATT_RQ), lambda bi, h: (h, 0, 0, 0)),
            pl.BlockSpec((1, LANES), lambda bi, h: (0, 0)),
        ],
        out_specs=pl.BlockSpec((None, s, LANES), seq_blk),
        out_shape=jax.ShapeDtypeStruct((b, s, DIFF_HEADS * LANES), BF16),
        scratch_shapes=[pltpu.VMEM((s // ATT_RQ, 2 * ATT_RQ, LANES), BF16)]
        + [pltpu.VMEM((s // ATT_RQ, 2 * ATT_RQ, LANES), F32)] * 3,
        compiler_params=_cparams("parallel", "parallel"),
        name="diff_attention",
    )(lam_params, qa, ka, va, bias_tab, diff_g)


def _retention_kernel(q_ref, k_ref, v_ref, gate_ref, din_ref, qd_ref, kd_ref, cd_ref, g_ref, o_ref):
    s = q_ref.shape[0]
    c = RET_CHUNK
    lane = lax.broadcasted_iota(I32, (c, LANES), 1)
    g = g_ref[...]
    states = [jnp.zeros((LANES, RET_DV), F32) for _ in range(2)]
    for n in range(s // c):
        r = slice(n * c, (n + 1) * c)
        q_pair = q_ref[r, :].astype(F32)
        k_pair = k_ref[r, :].astype(F32)
        for par in range(2):
            own = (lane // RET_DK) == par
            cols = slice(par * RET_DV, (par + 1) * RET_DV)
            q = jnp.where(own, q_pair, 0.0)
            k = jnp.where(own, k_pair, 0.0)
            v = v_ref[r, cols]
            scores = _dot_nt(q.astype(BF16), k.astype(BF16)) * din_ref[par]
            inner = _dot(scores.astype(BF16), v)
            cross = _dot((q * qd_ref[par]).astype(BF16), states[par].astype(BF16))
            kv = _dot((k * kd_ref[par]).T.astype(BF16), v)
            states[par] = cd_ref[par] * states[par] + kv
            y = inner + cross
            mu = jnp.mean(y, axis=-1, keepdims=True)
            yc = y - mu
            var = jnp.mean(yc * yc, axis=-1, keepdims=True)
            yn = yc * lax.rsqrt(var + LN_EPS) * g
            gate = gate_ref[r, cols].astype(F32)
            o_ref[r, cols] = (_silu(gate) * yn).astype(BF16)


def _retention(qb, kb, vb, gb, tabs, ret_g):
    b, s, _ = qb.shape
    din, qd, kd, cd = tabs
    pair = lambda bi, hp: (bi, 0, hp)
    tab = lambda bi, hp: (hp, 0, 0)
    return pl.pallas_call(
        _retention_kernel,
        grid=(b, RET_HEADS // 2),
        in_specs=[
            pl.BlockSpec((None, s, LANES), pair),
            pl.BlockSpec((None, s, LANES), pair),
            pl.BlockSpec((None, s, 2 * RET_DV), pair),
            pl.BlockSpec((None, s, 2 * RET_DV), pair),
            pl.BlockSpec((2, RET_CHUNK, RET_CHUNK), tab),
            pl.BlockSpec((2, RET_CHUNK, LANES), tab),
            pl.BlockSpec((2, RET_CHUNK, LANES), tab),
            pl.BlockSpec((2, 1, LANES), tab),
            pl.BlockSpec((1, RET_DV), lambda bi, hp: (0, 0)),
        ],
        out_specs=pl.BlockSpec((None, s, 2 * RET_DV), pair),
        out_shape=jax.ShapeDtypeStruct((b, s, RET_HEADS * RET_DV), BF16),
        compiler_params=_cparams("parallel", "parallel"),
        name="retention",
    )(qb, kb, vb, gb, din, qd, kd, cd, ret_g)


def _odd_inproj_kernel(x_ref, w_ref, wf_ref, bf_ref, q_ref, k_ref, v_ref, cum_ref, cumt_ref, carry_ref, *, nblk_s):
    i = pl.program_id(0)
    x = x_ref[...].astype(BF16)
    d = q_ref.shape[1]
    q_ref[...] = (_dot(x, w_ref[:, 0:d].astype(BF16)) * (FOX_DH ** -0.5 * LOG2E)).astype(BF16)
    k_ref[...] = _dot(x, w_ref[:, d:2 * d].astype(BF16)).astype(BF16)
    v_ref[...] = _dot(x, w_ref[:, 2 * d:3 * d].astype(BF16)).astype(BF16)
    z = _dot(x, wf_ref[...]) + bf_ref[...]
    c = jnp.minimum(z, 0.0) - jnp.log1p(jnp.exp(-jnp.abs(z)))
    tm = c.shape[0]
    row = lax.broadcasted_iota(I32, c.shape, 0)
    step = 1
    while step < tm:
        c = c + jnp.where(row >= step, pltpu.roll(c, step, 0), 0.0)
        step *= 2

    @pl.when(i % nblk_s == 0)
    def _():
        carry_ref[...] = jnp.zeros_like(carry_ref)

    c = c + carry_ref[...]
    cum = c * LOG2E
    cum_ref[...] = cum
    cum_t = cum.T
    for hp in range(cumt_ref.shape[0]):
        cumt_ref[hp] = cum_t[2 * hp:2 * hp + 2, :]
    carry_ref[...] = c[tm - 1:tm, :]


def _odd_inproj(x2, w_all, layer, wf_bf, bfg, seq):
    t, d = x2.shape
    tm = min(PROJ_TM, seq)
    nblk_s = seq // tm
    row = lambda i: (i, 0)
    kern = functools.partial(_odd_inproj_kernel, nblk_s=nblk_s)
    return pl.pallas_call(
        kern,
        grid=(t // tm,),
        in_specs=[
            pl.BlockSpec((tm, d), row),
            _layer_weight_spec(w_all, layer),
            pl.BlockSpec(wf_bf.shape, lambda i: (0, 0)),
            pl.BlockSpec(bfg.shape, lambda i: (0, 0)),
        ],
        out_specs=[pl.BlockSpec((tm, d), row)] * 3 + [
            pl.BlockSpec((tm, LANES), row),
            pl.BlockSpec((None, FOX_HEADS // 2, 2, tm), lambda i: (i // nblk_s, 0, 0, i % nblk_s))],
        out_shape=[jax.ShapeDtypeStruct((t, d), BF16)] * 3 + [
            jax.ShapeDtypeStruct((t, LANES), F32), jax.ShapeDtypeStruct((t // seq, FOX_HEADS // 2, 2, seq), F32)],
        scratch_shapes=[pltpu.VMEM((1, LANES), F32)],
        compiler_params=_cparams("arbitrary"),
        name="odd_inproj",
    )(x2, w_all, wf_bf, bfg)


def _fox_attn_kernel(q_ref, k_ref, v_ref, cq_ref, ck_ref, o_ref, qm_ref, va_ref, cqc_ref, m_ref, acc_ref, *, tq):
    seq = q_ref.shape[0]
    tk = tq
    hp = pl.program_id(1)
    q = q_ref[...]
    v = v_ref[...]
    lane = lax.broadcasted_iota(I32, (seq, LANES), 1)
    cq_all = cq_ref[...]
    for par in range(2):
        own = (lane // FOX_DH) == par
        qm_ref[par] = jnp.where(own, q, jnp.zeros_like(q))
        va_ref[par] = jnp.where(own, v, jnp.ones_like(v))
        cq = jnp.sum(jnp.where(lane == 2 * hp + par, cq_all, 0.0), axis=-1, keepdims=True)
        cqc_ref[par] = jnp.broadcast_to(cq, (seq, LANES))
    m_ref[...] = jnp.full(m_ref.shape, NEG_BIG, F32)
    acc_ref[...] = jnp.zeros(acc_ref.shape, F32)
    rq = ATT_RQ
    upper = (lax.broadcasted_iota(I32, (rq, rq), 1) > lax.broadcasted_iota(I32, (rq, rq), 0))
    lane_q = lax.broadcasted_iota(I32, (rq, LANES), 1)

    def n_pieces(g):
        return -(-(g + 1) * rq // tk)

    n_chunks = seq // rq
    chains = [(g, kp, par) for kp in range(n_pieces(n_chunks - 1)) for g in range(n_chunks)
              if kp < n_pieces(g) for par in range(2)]

    def n_keys(g, kp):
        return min(tk, (g + 1) * rq - kp * tk)

    def scores(chain):
        g, kp, par = chain
        k = k_ref[kp * tk:kp * tk + n_keys(g, kp), :]
        return _dot_nt(qm_ref[par, g * rq:(g + 1) * rq, :], k)

    def finish(chain, s):
        g, kp, par = chain
        rows = slice(g * rq, (g + 1) * rq)
        nk = n_keys(g, kp)
        ck = ck_ref[par:par + 1, kp * tk:kp * tk + nk]
        cq = cqc_ref[par, rows, :]
        sc = []
        for kc in range(nk // rq):
            t = s[:, kc * rq:(kc + 1) * rq] - ck[:, kc * rq:(kc + 1) * rq]
            if kp * (tk // rq) + kc == g:
                t = jnp.where(upper, NEG_BIG, t)
            sc.append(t)
        mx = sc[0]
        for t in sc[1:]:
            mx = jnp.maximum(mx, t)
        m_old = m_ref[par, rows, :]
        m_new = jnp.maximum(m_old, jnp.max(mx, axis=-1, keepdims=True) + cq)
        alpha = jnp.exp2(m_old - m_new)
        shift = m_new - cq
        p = jnp.concatenate([jnp.exp2(t - shift).astype(BF16) for t in sc], axis=1)
        acc_ref[par, rows, :] = alpha * acc_ref[par, rows, :] + _dot(p, va_ref[par, kp * tk:kp * tk + nk, :])
        m_ref[par, rows, :] = m_new
        if kp == n_pieces(g) - 1 and par == 1:
            acc0 = acc_ref[0, rows, :]
            acc1 = acc_ref[1, rows, :]
            out0 = acc0 / acc0[:, FOX_DH:FOX_DH + 1]
            out1 = acc1 / acc1[:, 0:1]
            o_ref[rows, :] = jnp.where(lane_q < FOX_DH, out0, out1).astype(BF16)

    _run_chains(chains, FOX_LOOKAHEAD, scores, finish)


def _fox_attention(q, k, v, cum, cum_t):
    b, s, d = q.shape
    tq = min(FOX_TQ, s)
    npair = d // LANES
    seq_blk = lambda bi, h: (bi, 0, h)
    return pl.pallas_call(
        functools.partial(_fox_attn_kernel, tq=tq),
        grid=(b, npair),
        in_specs=[
            pl.BlockSpec((None, s, LANES), seq_blk),
            pl.BlockSpec((None, s, LANES), seq_blk),
            pl.BlockSpec((None, s, LANES), seq_blk),
            pl.BlockSpec((None, s, LANES), lambda bi, h: (bi, 0, 0)),
            pl.BlockSpec((None, None, 2, s), lambda bi, h: (bi, h, 0, 0)),
        ],
        out_specs=pl.BlockSpec((None, s, LANES), seq_blk),
        out_shape=jax.ShapeDtypeStruct((b, s, d), BF16),
        scratch_shapes=[pltpu.VMEM((2, s, LANES), BF16), pltpu.VMEM((2, s, LANES), BF16),
                        pltpu.VMEM((2, s, LANES), F32), pltpu.VMEM((2, s, LANES), F32),
                        pltpu.VMEM((2, s, LANES), F32)],
        compiler_params=_cparams("parallel", "parallel"),
        name="fox_attention",
    )(q, k, v, cum, cum_t)


def _outproj_router_kernel(*refs, n_y):
    y_refs = refs[:n_y]
    (w_ref, x_ref, g_ref, b_ref, rwt_ref, tri_ref, h_ref, route_ref, col_ref, cnt_ref, xs_ref,
     hbf_ref) = refs[n_y:]
    rw2 = rwt_ref[...]
    rw_hi = rw2.astype(BF16)
    rw_lo = (rw2 - rw_hi.astype(F32)).astype(BF16)
    lane_w = lax.broadcasted_iota(I32, rw2.shape, 1)
    w = jnp.where(lane_w < N_EXPERTS, rw_hi, rw_lo)
    tm = x_ref.shape[0]
    rc = ROUTER_ROWS
    logit_chunks = []

    def project(c):
        rows = slice(c * rc, (c + 1) * rc)
        mix = None
        k0 = 0
        for yr in y_refs:
            k1 = k0 + yr.shape[1]
            part = _dot(yr[rows, :], w_ref[k0:k1, :].astype(BF16))
            mix = part if mix is None else mix + part
            k0 = k1
        return mix

    def norm_and_logits(c, mix):
        rows = slice(c * rc, (c + 1) * rc)
        h = _layer_norm_rows(DEEPNORM_ALPHA * x_ref[rows, :] + mix, g_ref[...], b_ref[...])
        h_ref[rows, :] = h
        h_hi = h.astype(BF16)
        hbf_ref[rows, :] = h_hi
        h_lo = (h - h_hi.astype(F32)).astype(BF16)
        p_hi = _dot(h_hi, w)
        p_lo = _dot(h_lo, w)
        slab = p_hi + (pltpu.roll(p_hi, LANES - N_EXPERTS, 1) + p_lo)
        logit_chunks.append(slab.T[0:N_EXPERTS])

    _run_chains(list(range(tm // rc)), 1, project, norm_and_logits)

    logits = jnp.concatenate(logit_chunks, axis=1)
    row = lax.broadcasted_iota(I32, (N_EXPERTS, tm), 0)
    mx = jnp.max(logits, axis=0, keepdims=True)
    ex = jnp.exp(logits - mx)
    probs = ex / jnp.sum(ex, axis=0, keepdims=True)
    grp = row // EXPERTS_PER_GROUP

    def top2(vals):
        v1 = jnp.max(vals, axis=0, keepdims=True)
        i1 = jnp.min(jnp.where(vals == v1, row, N_EXPERTS), axis=0, keepdims=True)
        rest = jnp.where(row == i1, -2.0, vals)
        v2 = jnp.max(rest, axis=0, keepdims=True)
        i2 = jnp.min(jnp.where(rest == v2, row, N_EXPERTS), axis=0, keepdims=True)
        return v1, i1, v2, i2

    best_score = None
    best = None
    for gi in range(N_GROUPS):
        v1, _, v2, _ = top2(jnp.where(grp == gi, probs, -1.0))
        score = v1 + v2
        if gi == 0:
            best_score, best = score, jnp.zeros_like(score, dtype=I32)
        else:
            better = score > best_score
            best = jnp.where(better, gi, best)
            best_score = jnp.where(better, score, best_score)
    v1, i1, v2, i2 = top2(jnp.where(grp == best, probs, -1.0))
    tot = v1 + v2
    g1 = v1 / tot
    g2 = v2 / tot

    onehot = jnp.where((row == i1) | (row == i2), 1.0, 0.0)
    pref = _dot(onehot.astype(BF16), tri_ref[...])
    cnt = jnp.broadcast_to(jnp.sum(onehot, axis=1, keepdims=True), cnt_ref.shape)
    cnt_ref[...] = cnt
    grp_rows = jnp.floor((cnt + (SORT_ALIGN - 1)) * (1.0 / SORT_ALIGN)) * SORT_ALIGN
    row_c = lax.broadcasted_iota(I32, cnt.shape, 0)
    start = grp_rows
    step = 1
    while step < N_EXPERTS:
        start = start + jnp.where(row_c >= step, pltpu.roll(start, step, 0), 0.0)
        step *= 2
    start = (start - grp_rows)[:, 0:1]
    s1 = jnp.sum(jnp.where(row == i1, pref + start, 0.0), axis=0, keepdims=True)
    s2 = jnp.sum(jnp.where(row == i2, pref + start, 0.0), axis=0, keepdims=True)

    row8 = lax.broadcasted_iota(I32, (ROUTE_ROWS, tm), 0)
    fields = (i1.astype(F32), i2.astype(F32), g1, g2, s1, s2)
    route = jnp.zeros((ROUTE_ROWS, tm), F32)
    for n, f in enumerate(fields):
        route = jnp.where(row8 == n, f, route)
    route_ref[...] = route
    col_ref[...] = jnp.concatenate([route, jnp.zeros((LANES - ROUTE_ROWS, tm), F32)], axis=0).T

    srow = lax.broadcasted_iota(I32, (xs_ref.shape[0], tm), 0).astype(F32)
    perm = jnp.where((srow == s1) | (srow == s2), 1.0, 0.0).astype(BF16)
    xs_ref[...] = _dot(perm, hbf_ref[...]).astype(BF16)


def _outproj_router(ys, w_all, layer, x2, ln_g, ln_b, rwt, tri):
    t, d = x2.shape
    tm = tri.shape[0]
    n_local = _local_rows(tm)
    nt = t // tm
    row = lambda i: (i, 0)
    full = lambda i: (0, 0)
    n_y = len(ys)
    kern = functools.partial(_outproj_router_kernel, n_y=n_y)
    return pl.pallas_call(
        kern,
        grid=(nt,),
        in_specs=([pl.BlockSpec((tm, y.shape[1]), row) for y in ys]
                  + [_layer_weight_spec(w_all, layer)]
                  + [pl.BlockSpec((tm, d), row), pl.BlockSpec((1, d), full), pl.BlockSpec((1, d), full),
                     pl.BlockSpec(rwt.shape, full), pl.BlockSpec(tri.shape, full)]),
        out_specs=[pl.BlockSpec((tm, d), row), pl.BlockSpec((ROUTE_ROWS, tm), lambda i: (0, i)),
                   pl.BlockSpec((tm, LANES), row), pl.BlockSpec((N_EXPERTS, LANES), row),
                   pl.BlockSpec((n_local, d), row)],
        out_shape=[jax.ShapeDtypeStruct((t, d), F32), jax.ShapeDtypeStruct((ROUTE_ROWS, t), F32),
                   jax.ShapeDtypeStruct((t, LANES), F32), jax.ShapeDtypeStruct((nt * N_EXPERTS, LANES), F32),
                   jax.ShapeDtypeStruct((nt * n_local, d), BF16)],
        scratch_shapes=[pltpu.VMEM((tm, d), BF16)],
        compiler_params=_cparams("parallel"),
        name="outproj_router",
    )(*ys, w_all, x2, ln_g, ln_b, rwt, tri)


def _start_tile_gather(tile, loff_ref, rows_ref, gpos_ref, sorted_hbm, local_ref, sem):
    for e in range(N_EXPERTS):
        n = tile * N_EXPERTS + e

        @pl.when(rows_ref[n] > 0)
        def _(n=n):
            size = pl.multiple_of(rows_ref[n], SORT_ALIGN)
            src = sorted_hbm.at[pl.ds(pl.multiple_of(gpos_ref[n], SORT_ALIGN), size)]
            dst = local_ref.at[pl.ds(pl.multiple_of(loff_ref[n], SORT_ALIGN), size)]
            pltpu.make_async_copy(src, dst, sem).start()


def _wait_rows(n_rows, src_hbm, dst_ref, sem):
    n = pl.multiple_of(n_rows, SORT_ALIGN)
    pltpu.make_async_copy(src_hbm.at[pl.ds(0, n)], dst_ref.at[pl.ds(0, n)], sem).wait()


def _expert_kernel(te_ref, nx_ref, nt_ref, g0_ref, g1_ref, valid_ref, loff_ref, rows_ref, gpos_ref,
                   xs_hbm, wg_hbm, wu_hbm, wd_hbm, o_ref,
                   xbuf, semx, wgs, wus, wds, wgb, wub, wdb, sem, nsw_ref, *, layer, n_local):
    r = pl.program_id(0)
    nt = nt_ref[0]
    cur = te_ref[r]
    tr = xbuf.shape[1]
    xslot = r % 2

    def start_pieces(tile, dst_slot):
        e = te_ref[tile]
        base = tile * tr

        def piece(tau, c):
            n = tau * N_EXPERTS + e
            lo = jnp.maximum(gpos_ref[n], base)
            hi = jnp.minimum(gpos_ref[n] + rows_ref[n], base + tr)

            @pl.when(hi > lo)
            def _():
                size = pl.multiple_of(hi - lo, SORT_ALIGN)
                src = pl.multiple_of(tau * n_local + loff_ref[n] + (lo - gpos_ref[n]), SORT_ALIGN)
                dst = pl.multiple_of(lo - base, SORT_ALIGN)
                pltpu.make_async_copy(xs_hbm.at[pl.ds(src, size)], xbuf.at[dst_slot, pl.ds(dst, size)],
                                      semx.at[dst_slot]).start()

            return c

        lax.fori_loop(g0_ref[tile], g1_ref[tile] + 1, piece, 0)

    @pl.when(r == 0)
    def _():
        xbuf[...] = jnp.zeros_like(xbuf)
        start_pieces(0, 0)

    @pl.when(r + 1 < nt)
    def _():
        start_pieces(r + 1, 1 - xslot)

    def weight_copies(e, slot):
        return [pltpu.make_async_copy(src.at[layer, e], dst.at[slot], sem.at[slot])
                for src, dst in ((wg_hbm, wgs), (wu_hbm, wus), (wd_hbm, wds))]

    @pl.when(r == 0)
    def _():
        nsw_ref[0] = 0
        for cp in weight_copies(cur, 0):
            cp.start()

    @pl.when((r == 0) | (cur != te_ref[jnp.maximum(r - 1, 0)]))
    def _():
        slot = nsw_ref[0] % 2
        nsw_ref[0] = nsw_ref[0] + 1
        for cp in weight_copies(cur, slot):
            cp.wait()

        @pl.when(nx_ref[r] != cur)
        def _():
            for cp in weight_copies(nx_ref[r], 1 - slot):
                cp.start()

        wgb[...] = wgs[slot].astype(BF16)
        wub[...] = wus[slot].astype(BF16)
        wdb[...] = wds[slot].astype(BF16)

    @pl.when(r < nt)
    def _():
        _wait_rows(valid_ref[r], xs_hbm, xbuf.at[xslot], semx.at[xslot])
        rc = EXPERT_ROWS

        def up(c):
            x = xbuf[xslot, c * rc:(c + 1) * rc, :]
            return _dot(x, wgb[...]), _dot(x, wub[...])

        def down(c, gate_up):
            a = _silu(gate_up[0]) * gate_up[1]
            o_ref[c * rc:(c + 1) * rc, :] = _dot(a.astype(BF16), wdb[...]).astype(BF16)

        _run_chains(list(range(tr // rc)), 1, up, down)

    @pl.when(r >= nt)
    def _():
        o_ref[...] = jnp.zeros_like(o_ref)


def _expert_mlps(meta, n_slots, xs_local, wg, wu, wd, layer):
    d = xs_local.shape[1]
    tr = MOE_TR
    dff = wg.shape[3]
    grid_spec = pltpu.PrefetchScalarGridSpec(
        num_scalar_prefetch=9,
        grid=(n_slots // tr,),
        in_specs=[pl.BlockSpec(memory_space=pl.ANY)] * 4,
        out_specs=pl.BlockSpec((tr, d), lambda r, *_: (r, 0)),
        scratch_shapes=[pltpu.VMEM((2, tr, d), BF16), pltpu.SemaphoreType.DMA((2,)),
                        pltpu.VMEM((2, d, dff), F32), pltpu.VMEM((2, d, dff), F32), pltpu.VMEM((2, dff, d), F32),
                        pltpu.VMEM((d, dff), BF16), pltpu.VMEM((d, dff), BF16), pltpu.VMEM((dff, d), BF16),
                        pltpu.SemaphoreType.DMA((2,)), pltpu.SMEM((1,), I32)],
    )
    return pl.pallas_call(
        functools.partial(_expert_kernel, layer=layer, n_local=_local_rows(MOE_TM)),
        grid_spec=grid_spec,
        out_shape=jax.ShapeDtypeStruct((n_slots, d), BF16),
        compiler_params=_cparams("arbitrary"),
        name="expert_mlps",
    )(meta["tile_expert"], meta["next_expert"], meta["n_tiles"], meta["first_group"], meta["last_group"],
      meta["valid"], meta["loff"], meta["rows"], meta["gpos"], xs_local, wg, wu, wd)


def _combine_kernel(loff_ref, rows_ref, gpos_ref, tot_ref, ys_hbm, h_ref, col_ref, p_ref, g_ref, b_ref,
                    pg_ref, pp_ref, o_ref, ybuf, sem):
    i = pl.program_id(0)
    n = pl.num_programs(0)
    tm = h_ref.shape[0]
    n_local = ybuf.shape[1]
    slot = i % 2

    @pl.when(i == 0)
    def _():
        ybuf[...] = jnp.zeros_like(ybuf)
        _start_tile_gather(0, loff_ref, rows_ref, gpos_ref, ys_hbm, ybuf.at[0], sem.at[0])

    @pl.when(i + 1 < n)
    def _():
        _start_tile_gather(i + 1, loff_ref, rows_ref, gpos_ref, ys_hbm, ybuf.at[1 - slot], sem.at[1 - slot])

    _wait_rows(tot_ref[i], ys_hbm, ybuf.at[slot], sem.at[slot])
    rc = COMBINE_ROWS
    scol = lax.broadcasted_iota(I32, (rc, n_local), 1).astype(F32)

    def gather_rows(c):
        rows = slice(c * rc, (c + 1) * rc)
        cols = col_ref[rows, :]
        pick = jnp.zeros((rc, n_local), F32)
        for k in range(2):
            pick = jnp.where(scol == cols[:, ROUTE_SLOT + k:ROUTE_SLOT + k + 1],
                             cols[:, ROUTE_GATE + k:ROUTE_GATE + k + 1], pick)
        ffn = _dot(pick.astype(BF16), ybuf[slot])
        return ffn, _dot(p_ref[rows, :].astype(BF16), pp_ref[...].astype(BF16))

    def finish(c, gathered):
        rows = slice(c * rc, (c + 1) * rc)
        ffn, pe = gathered
        h2 = _layer_norm_rows(DEEPNORM_ALPHA * h_ref[rows, :] + ffn, g_ref[...], b_ref[...])
        gate = _sigmoid(_dot(h2.astype(BF16), pg_ref[...].astype(BF16)))
        o_ref[rows, :] = h2 + gate * pe

    _run_chains(list(range(tm // rc)), 1, gather_rows, finish)


def _combine(meta, ys, h, cols, p3, layer, ln_g, ln_b, ple_gate, ple_proj):
    t, d = h.shape
    tm = MOE_TM
    pdim = p3.shape[2]
    row = lambda i, *_: (i, 0)
    full = lambda i, *_: (0, 0)
    grid_spec = pltpu.PrefetchScalarGridSpec(
        num_scalar_prefetch=4,
        grid=(t // tm,),
        in_specs=[
            pl.BlockSpec(memory_space=pl.ANY),
            pl.BlockSpec((tm, d), row),
            pl.BlockSpec((tm, LANES), row),
            pl.BlockSpec((None, tm, pdim), lambda i, *_: (layer, i, 0)),
            pl.BlockSpec((1, d), full),
            pl.BlockSpec((1, d), full),
            pl.BlockSpec((None,) + ple_gate.shape[1:], lambda i, *_: (layer, 0, 0), pipeline_mode=pl.Buffered(1)),
            pl.BlockSpec((None,) + ple_proj.shape[1:], lambda i, *_: (layer, 0, 0), pipeline_mode=pl.Buffered(1)),
        ],
        out_specs=pl.BlockSpec((tm, d), row),
        scratch_shapes=[pltpu.VMEM((2, _local_rows(tm), d), BF16), pltpu.SemaphoreType.DMA((2,))],
    )
    return pl.pallas_call(
        _combine_kernel,
        grid_spec=grid_spec,
        out_shape=jax.ShapeDtypeStruct((t, d), F32),
        compiler_params=_cparams("arbitrary"),
        name="moe_combine",
    )(meta["loff"], meta["rows"], meta["gpos"], meta["tot"], ys, h, cols, p3, ln_g, ln_b, ple_gate, ple_proj)


def _rotary_tables(seq):
    half = RET_DK // 2
    inv = (np.float32(ROPE_BASE) ** (-np.arange(half, dtype=np.float32) / np.float32(half))).astype(np.float32)
    ang = (np.arange(seq, dtype=np.float32)[:, None] * inv[None, :]).astype(np.float32)
    cos = np.cos(ang.astype(np.float64))
    sin = np.sin(ang.astype(np.float64))
    cos_h = np.concatenate([cos, cos], axis=1)
    sin_h = np.concatenate([-sin, sin], axis=1)
    return (jnp.asarray(np.tile(cos_h, (1, RET_HEADS)), F32), jnp.asarray(np.tile(sin_h, (1, RET_HEADS)), F32))


def _retention_tables():
    c = RET_CHUNK
    h = np.arange(RET_HEADS, dtype=np.float64)
    log_g = np.log1p(-np.exp2(-5.0 - h))
    j = np.arange(c, dtype=np.float64)
    rel = j[:, None] - j[None, :]
    din = np.where(rel >= 0, np.exp(np.maximum(rel, 0.0)[None] * log_g[:, None, None]), 0.0)
    qd = np.exp((j + 1.0)[None] * log_g[:, None])
    kd = np.exp((c - 1.0 - j)[None] * log_g[:, None])
    cd = np.exp(c * log_g)
    qd = np.broadcast_to(qd[:, :, None], (RET_HEADS, c, LANES))
    kd = np.broadcast_to(kd[:, :, None], (RET_HEADS, c, LANES))
    cd = np.broadcast_to(cd[:, None, None], (RET_HEADS, 1, LANES))
    return tuple(jnp.asarray(a, F32) for a in (din, qd, kd, cd))


def _t5_bucket_np(dist):
    max_exact = REL_BUCKETS // 2
    d = np.maximum(dist, 1).astype(np.float32)
    large = max_exact + (np.log(d / np.float32(max_exact)) / np.float32(math.log(REL_MAX_DIST / max_exact))
                         * np.float32(REL_BUCKETS - max_exact)).astype(np.int32)
    large = np.minimum(large, REL_BUCKETS - 1)
    return np.where(dist < max_exact, dist, large)


def _diff_bias_tables(rel_bias, seq):
    c = ATT_RQ
    r = np.arange(c)
    dist0 = r[:, None] - r[None, :]
    far = REL_BUCKETS - 1
    assert np.all(_t5_bucket_np(np.arange(c + 1, max(seq, 2 * c))) == far)
    bidx = np.stack([_t5_bucket_np(np.maximum(dist0, 0)), _t5_bucket_np(dist0 + c)])
    rb = rel_bias.astype(F32).T
    shifted = (rb - rb[:, far:far + 1]) * LOG2E
    bidx = jnp.asarray(bidx, I32)[None]
    tab = jnp.zeros((rb.shape[0], 2, c, c), F32)
    for bucket in range(REL_BUCKETS - 1):
        tab = jnp.where(bidx == bucket, shifted[:, bucket][:, None, None, None], tab)
    causal = jnp.asarray(np.stack([dist0 >= 0, np.ones_like(dist0, bool)]))[None]
    return jnp.where(causal, tab, NEG_BIG)


def _local_rows(tm):
    need = 2 * tm + N_EXPERTS * (SORT_ALIGN - 1)
    return -(-need // LANES) * LANES


def _round_up(a, m):
    return ((a + m - 1) // m) * m


def _route_meta(cnt, t):
    tm, tr = MOE_TM, MOE_TR
    nt = t // tm
    counts = cnt.reshape(nt, N_EXPERTS, LANES)[:, :, 0].astype(I32)
    rows = _round_up(counts, SORT_ALIGN)
    loff = jnp.cumsum(rows, axis=1) - rows
    seg = jnp.sum(rows, axis=0)
    seg_pad = _round_up(seg, tr)
    ends = jnp.cumsum(seg_pad)
    offs = ends - seg_pad
    gpos = offs[None, :] + jnp.cumsum(rows, axis=0) - rows
    n_slots = 2 * t + nt * N_EXPERTS * (SORT_ALIGN - 1)
    n_slots = _round_up(n_slots, tr) + N_EXPERTS * tr
    n_tiles = (ends[-1] // tr).astype(I32)
    tile_start = jnp.arange(n_slots // tr, dtype=I32) * tr
    tile_expert = jnp.sum((tile_start[:, None] >= ends[None, :]).astype(I32), axis=1)
    last = jnp.sum((((n_tiles - 1) * tr) >= ends).astype(I32))
    tile_expert = jnp.minimum(tile_expert, last).astype(I32)
    eid = jnp.arange(N_EXPERTS, dtype=I32)
    later = (eid[None, :] > eid[:, None]) & (seg_pad > 0)[None, :]
    nxt = jnp.min(jnp.where(later, eid[None, :], N_EXPERTS), axis=1)
    nxt = jnp.where(nxt == N_EXPERTS, eid, nxt)
    next_expert = jnp.sum(jnp.where(tile_expert[:, None] == eid[None, :], nxt[None, :], 0), axis=1).astype(I32)
    own = tile_expert[:, None] == eid[None, :]
    pick = lambda tab: jnp.sum(jnp.where(own[:, None, :], tab[None, :, :], 0), axis=-1)
    g_start, g_end = pick(gpos), pick(gpos + rows)
    used = (tile_start < n_tiles * tr)[:, None]
    first_group = jnp.sum((g_end <= tile_start[:, None]).astype(I32), axis=1)
    last_group = jnp.sum(((g_start < tile_start[:, None] + tr) & used).astype(I32), axis=1) - 1
    seg_end = jnp.sum(jnp.where(own, (offs + seg)[None, :], 0), axis=1)
    valid = jnp.clip(seg_end - tile_start, 0, tr)
    meta = {
        "loff": loff.reshape(-1).astype(I32), "rows": rows.reshape(-1).astype(I32),
        "gpos": gpos.reshape(-1).astype(I32), "tot": jnp.sum(rows, axis=1).astype(I32),
        "tile_expert": tile_expert, "next_expert": next_expert, "n_tiles": n_tiles.reshape(1),
        "first_group": first_group.astype(I32), "last_group": last_group.astype(I32), "valid": valid.astype(I32),
    }
    return meta, n_slots


def kernel(x, p, rel_bias, router_w, even_w_in, even_w_out, even_lambda, even_diff_norm, even_ret_norm,
           odd_w_in, odd_b_forget, odd_w_out, ln_mix_g, ln_mix_b, ln_ffn_g, ln_ffn_b,
           moe_w_gate, moe_w_up, moe_w_down, ple_proj, ple_gate):
    b, s, d = x.shape
    t = b * s
    assert d == FOX_HEADS * FOX_DH and p.shape[0] == DEPTH and even_w_in.shape[2] == EVEN_CUTS[-1]
    assert odd_w_in.shape[2] == 3 * d + FOX_HEADS and moe_w_gate.shape[1] == N_EXPERTS
    assert s % RET_CHUNK == 0 and s % min(ATT_TQ, s) == 0 and s % min(FOX_TQ, s) == 0
    assert t % min(PROJ_TM, s) == 0 and t % MOE_TM == 0

    cos_t, sin_t = _rotary_tables(s)
    ret_tabs = _retention_tables()
    bias_tab = _diff_bias_tables(rel_bias, s)
    rw32 = router_w.astype(F32)
    rwt = jnp.zeros((d, LANES), F32).at[:, :N_EXPERTS].set(rw32).at[:, N_EXPERTS:2 * N_EXPERTS].set(rw32)
    tok = np.arange(MOE_TM)
    tri = jnp.asarray(tok[:, None] < tok[None, :], BF16)

    x2 = x.reshape(t, d)
    for i in range(DEPTH):
        j = i // 2
        if i % 2 == 0:
            lam_init = 0.8 - 0.6 * math.exp(-0.3 * i)
            qa, ka, va, qb, kb, vb, gb = _even_inproj(x2, even_w_in, j, cos_t, sin_t, s)
            sh = lambda a: a.reshape(b, s, a.shape[1])
            ya = _diff_attention(sh(qa), sh(ka), sh(va), bias_tab, even_lambda[j].astype(F32),
                                 even_diff_norm[j].reshape(1, -1).astype(F32), lam_init)
            yb = _retention(sh(qb), sh(kb), sh(vb), sh(gb), ret_tabs, even_ret_norm[j].reshape(1, -1).astype(F32))
            ys = [ya.reshape(t, -1), yb.reshape(t, -1)]
            w_out = even_w_out
        else:
            wf = jnp.zeros((d, LANES), BF16).at[:, :FOX_HEADS].set(odd_w_in[j, :, 3 * d:].astype(BF16))
            bfg = jnp.zeros((1, LANES), F32).at[0, :FOX_HEADS].set(odd_b_forget[j].astype(F32))
            q, k, v, cum, cum_t = _odd_inproj(x2, odd_w_in, j, wf, bfg, s)
            y = _fox_attention(q.reshape(b, s, d), k.reshape(b, s, d), v.reshape(b, s, d),
                               cum.reshape(b, s, LANES), cum_t)
            ys = [y.reshape(t, d)]
            w_out = odd_w_out
        h, route, cols, cnt, xs_local = _outproj_router(ys, w_out, j, x2, ln_mix_g[i].reshape(1, d),
                                                        ln_mix_b[i].reshape(1, d), rwt, tri)
        meta, n_slots = _route_meta(cnt, t)
        rows = _expert_mlps(meta, n_slots, xs_local, moe_w_gate, moe_w_up, moe_w_down, i)
        x2 = _combine(meta, rows, h, cols, p.reshape(DEPTH, t, -1), i, ln_ffn_g[i].reshape(1, d),
                      ln_ffn_b[i].reshape(1, d), ple_gate, ple_proj)
    return x2.reshape(b, s, d)
```

```python
import functools
import math

import numpy as np
import jax
import jax.numpy as jnp
from jax import lax
from jax.experimental import pallas as pl
from jax.experimental.pallas import tpu as pltpu

F32 = jnp.float32
BF16 = jnp.bfloat16
I32 = jnp.int32

DIFF_HEADS = 4
DIFF_DK = 64
RET_HEADS = 4
RET_DK = 64
RET_DV = 128
RET_CHUNK = 128
FOX_HEADS = 16
FOX_DH = 64
REL_BUCKETS = 32
REL_MAX_DIST = 128
N_GROUPS = 4
EXPERTS_PER_GROUP = 4
N_EXPERTS = 16
DEPTH = 2
DEEPNORM_ALPHA = (2 * DEPTH) ** 0.25
LN_EPS = 1e-5
ROPE_BASE = 10000.0
NEG_BIG = -1e30
LOG2E = math.log2(math.e)
RET_QK = RET_HEADS * RET_DK
EVEN_WIDTHS = (DIFF_HEADS * 2 * DIFF_DK, DIFF_HEADS * 2 * DIFF_DK, DIFF_HEADS * 2 * DIFF_DK,
               RET_QK, RET_QK, RET_HEADS * RET_DV, RET_HEADS * RET_DV)
EVEN_CUTS = tuple(int(v) for v in np.cumsum(EVEN_WIDTHS))

VMEM_LIMIT_BYTES = 48 * 1024 * 1024
LANES = 128

PROJ_TM = 1024
ATT_TQ = 2048
FOX_TQ = 512
ATT_RQ = LANES
DIFF_LOOKAHEAD = 2
FOX_LOOKAHEAD = 4
MOE_TR = 512
MOE_TM = 512
SORT_ALIGN = 16
ROUTE_ROWS = 8
ROUTE_GATE = 2
ROUTE_SLOT = 4
COMBINE_ROWS = 256
EXPERT_ROWS = 256
ROUTER_ROWS = 256


def _cparams(*sem):
    return pltpu.CompilerParams(dimension_semantics=sem, vmem_limit_bytes=VMEM_LIMIT_BYTES)


def _dot(a, b):
    return jnp.dot(a, b, preferred_element_type=F32)


def _dot_nt(a, b):
    return lax.dot_general(a, b, (((1,), (1,)), ((), ())), preferred_element_type=F32)


def _layer_norm_rows(z, g, b):
    mu = jnp.mean(z, axis=-1, keepdims=True)
    zc = z - mu
    var = jnp.mean(zc * zc, axis=-1, keepdims=True)
    return zc * lax.rsqrt(var + LN_EPS) * g + b


def _silu(x):
    return x * (1.0 / (1.0 + jnp.exp(-x)))


def _sigmoid(x):
    return 1.0 / (1.0 + jnp.exp(-x))


def _even_inproj_kernel(x_ref, w_ref, cos_ref, sin_ref,
                        qa_ref, ka_ref, va_ref, qb_ref, kb_ref, vb_ref, gb_ref):
    x = x_ref[...].astype(BF16)

    def mm(c0, c1):
        return _dot(x, w_ref[:, c0:c1].astype(BF16))

    c = EVEN_CUTS
    qa_ref[...] = (mm(0, c[0]) * (DIFF_DK ** -0.5 * LOG2E)).astype(BF16)
    ka_ref[...] = mm(c[0], c[1]).astype(BF16)
    va_ref[...] = mm(c[1], c[2]).astype(BF16)
    qk = mm(c[2], c[4])
    cos = cos_ref[...]
    sin = sin_ref[...]
    lane = lax.broadcasted_iota(I32, cos.shape, 1)
    first_half = (lane % RET_DK) < (RET_DK // 2)

    def rot(t):
        sw = jnp.where(first_half, pltpu.roll(t, t.shape[1] - RET_DK // 2, 1),
                       pltpu.roll(t, RET_DK // 2, 1))
        return t * cos + sw * sin

    qb_ref[...] = rot(qk[:, :RET_QK]).astype(BF16)
    kb_ref[...] = (rot(qk[:, RET_QK:]) * (RET_DK ** -0.5)).astype(BF16)
    vb_ref[...] = mm(c[4], c[5]).astype(BF16)
    gb_ref[...] = mm(c[5], c[6]).astype(BF16)


def _layer_weight_spec(w_all, layer):
    return pl.BlockSpec((None,) + w_all.shape[1:], lambda *_: (layer, 0, 0), pipeline_mode=pl.Buffered(1))


def _even_inproj(x2, w_all, layer, cos_t, sin_t, seq):
    t, d = x2.shape
    tm = min(PROJ_TM, seq)
    nblk_s = seq // tm
    widths = EVEN_WIDTHS
    row = lambda i: (i, 0)
    return pl.pallas_call(
        _even_inproj_kernel,
        grid=(t // tm,),
        in_specs=[
            pl.BlockSpec((tm, d), row),
            _layer_weight_spec(w_all, layer),
            pl.BlockSpec((tm, RET_QK), lambda i: (i % nblk_s, 0)),
            pl.BlockSpec((tm, RET_QK), lambda i: (i % nblk_s, 0)),
        ],
        out_specs=[pl.BlockSpec((tm, w), row) for w in widths],
        out_shape=[jax.ShapeDtypeStruct((t, w), BF16) for w in widths],
        compiler_params=_cparams("parallel"),
        name="even_inproj",
    )(x2, w_all, cos_t, sin_t)


def _run_chains(chains, lookahead, scores, finish):
    pending = [scores(c) for c in chains[:lookahead]]
    for n, chain in enumerate(chains):
        if n + lookahead < len(chains):
            pending.append(scores(chains[n + lookahead]))
        finish(chain, pending.pop(0))


def _diff_attn_kernel(lam_ref, q_ref, k_ref, v_ref, bias_ref, g_ref, o_ref, qm_ref, m_ref, l_ref, acc_ref,
                      *, lam_init, tq):
    seq = q_ref.shape[0]
    tk = tq
    rq = ATT_RQ
    nr = tq // rq
    lane = lax.broadcasted_iota(I32, (rq, LANES), 1)
    for n in range(seq // rq):
        q = q_ref[n * rq:(n + 1) * rq, :]
        zero = jnp.zeros_like(q)
        qm_ref[n, 0:rq, :] = jnp.where(lane < DIFF_DK, q, zero)
        qm_ref[n, rq:2 * rq, :] = jnp.where(lane >= DIFF_DK, q, zero)
    m_ref[...] = jnp.full(m_ref.shape, NEG_BIG, F32)
    l_ref[...] = jnp.zeros(l_ref.shape, F32)
    acc_ref[...] = jnp.zeros(acc_ref.shape, F32)
    lp = lam_ref[...]
    lam = (jnp.exp(jnp.sum(lp[0:1, :] * lp[1:2, :], axis=-1, keepdims=True))
           - jnp.exp(jnp.sum(lp[2:3, :] * lp[3:4, :], axis=-1, keepdims=True)) + lam_init)
    bias2 = [jnp.concatenate([bias_ref[n], bias_ref[n]], axis=0) for n in range(2)]

    chains = [(ii, j, r) for j in range(seq // tq) for ii in range(j, seq // tq) for r in range(nr)]

    def n_keys(ii, j, r):
        return (r + 1) * rq if j == ii else tk

    def scores(chain):
        ii, j, r = chain
        k = k_ref[j * tk:j * tk + n_keys(ii, j, r), :]
        return _dot_nt(qm_ref[ii * nr + r], k)

    def finish(chain, s):
        ii, j, r = chain
        g = ii * nr + r
        nk = n_keys(ii, j, r)
        v = v_ref[j * tk:j * tk + nk, :]
        sc = []
        for kc in range(nk // rq):
            t = s[:, kc * rq:(kc + 1) * rq]
            back = g - (j * nr + kc)
            if back <= 1:
                t = t + bias2[back]
            sc.append(t)
        mx = sc[0]
        for t in sc[1:]:
            mx = jnp.maximum(mx, t)
        m_old = m_ref[g]
        m_new = jnp.maximum(m_old, jnp.max(mx, axis=-1, keepdims=True))
        alpha = jnp.exp2(m_old - m_new)
        ps = [jnp.exp2(t - m_new) for t in sc]
        psum = ps[0]
        for t in ps[1:]:
            psum = psum + t
        l_ref[g] = alpha * l_ref[g] + psum
        p = jnp.concatenate([t.astype(BF16) for t in ps], axis=1)
        acc_ref[g] = alpha * acc_ref[g] + _dot(p, v)
        m_ref[g] = m_new
        if j == ii:
            l_all = jnp.sum(l_ref[g], axis=-1, keepdims=True)
            a = acc_ref[g] / l_all
            o = a[0:rq] - lam * a[rq:2 * rq]
            o = o * lax.rsqrt(jnp.mean(o * o, axis=-1, keepdims=True) + LN_EPS)
            o_ref[g * rq:(g + 1) * rq, :] = (o * g_ref[...] * (1.0 - lam_init)).astype(BF16)

    _run_chains(chains, DIFF_LOOKAHEAD, scores, finish)


def _diff_attention(qa, ka, va, bias_tab, lam_params, diff_g, lam_init):
    b, s, _ = qa.shape
    tq = min(ATT_TQ, s)
    kern = functools.partial(_diff_attn_kernel, lam_init=lam_init, tq=tq)
    seq_blk = lambda bi, h: (bi, 0, h)
    return pl.pallas_call(
        kern,
        grid=(b, DIFF_HEADS),
        in_specs=[
            pl.BlockSpec(lam_params.shape, lambda bi, h: (0, 0)),
            pl.BlockSpec((None, s, LANES), seq_blk),
            pl.BlockSpec((None, s, LANES), seq_blk),
            pl.BlockSpec((None, s, LANES), seq_blk),
            pl.BlockSpec((None, 2, ATT_RQ, ATT_RQ), lambda bi, h: (h, 0, 0, 0)),
            pl.BlockSpec((1, LANES), lambda bi, h: (0, 0)),
        ],
        out_specs=pl.BlockSpec((None, s, LANES), seq_blk),
        out_shape=jax.ShapeDtypeStruct((b, s, DIFF_HEADS * LANES), BF16),
        scratch_shapes=[pltpu.VMEM((s // ATT_RQ, 2 * ATT_RQ, LANES), BF16)]
        + [pltpu.VMEM((s // ATT_RQ, 2 * ATT_RQ, LANES), F32)] * 3,
        compiler_params=_cparams("parallel", "parallel"),
        name="diff_attention",
    )(lam_params, qa, ka, va, bias_tab, diff_g)


def _retention_kernel(q_ref, k_ref, v_ref, gate_ref, din_ref, qd_ref, kd_ref, cd_ref, g_ref, o_ref):
    s = q_ref.shape[0]
    c = RET_CHUNK
    lane = lax.broadcasted_iota(I32, (c, LANES), 1)
    g = g_ref[...]
    states = [jnp.zeros((LANES, RET_DV), F32) for _ in range(2)]
    for n in range(s // c):
        r = slice(n * c, (n + 1) * c)
        q_pair = q_ref[r, :].astype(F32)
        k_pair = k_ref[r, :].astype(F32)
        for par in range(2):
            own = (lane // RET_DK) == par
            cols = slice(par * RET_DV, (par + 1) * RET_DV)
            q = jnp.where(own, q_pair, 0.0)
            k = jnp.where(own, k_pair, 0.0)
            v = v_ref[r, cols]
            scores = _dot_nt(q.astype(BF16), k.astype(BF16)) * din_ref[par]
            inner = _dot(scores.astype(BF16), v)
            cross = _dot((q * qd_ref[par]).astype(BF16), states[par].astype(BF16))
            kv = _dot((k * kd_ref[par]).T.astype(BF16), v)
            states[par] = cd_ref[par] * states[par] + kv
            y = inner + cross
            mu = jnp.mean(y, axis=-1, keepdims=True)
            yc = y - mu
            var = jnp.mean(yc * yc, axis=-1, keepdims=True)
            yn = yc * lax.rsqrt(var + LN_EPS) * g
            gate = gate_ref[r, cols].astype(F32)
            o_ref[r, cols] = (_silu(gate) * yn).astype(BF16)


def _retention(qb, kb, vb, gb, tabs, ret_g):
    b, s, _ = qb.shape
    din, qd, kd, cd = tabs
    pair = lambda bi, hp: (bi, 0, hp)
    tab = lambda bi, hp: (hp, 0, 0)
    return pl.pallas_call(
        _retention_kernel,
        grid=(b, RET_HEADS // 2),
        in_specs=[
            pl.BlockSpec((None, s, LANES), pair),
            pl.BlockSpec((None, s, LANES), pair),
            pl.BlockSpec((None, s, 2 * RET_DV), pair),
            pl.BlockSpec((None, s, 2 * RET_DV), pair),
            pl.BlockSpec((2, RET_CHUNK, RET_CHUNK), tab),
            pl.BlockSpec((2, RET_CHUNK, LANES), tab),
            pl.BlockSpec((2, RET_CHUNK, LANES), tab),
            pl.BlockSpec((2, 1, LANES), tab),
            pl.BlockSpec((1, RET_DV), lambda bi, hp: (0, 0)),
        ],
        out_specs=pl.BlockSpec((None, s, 2 * RET_DV), pair),
        out_shape=jax.ShapeDtypeStruct((b, s, RET_HEADS * RET_DV), BF16),
        compiler_params=_cparams("parallel", "parallel"),
        name="retention",
    )(qb, kb, vb, gb, din, qd, kd, cd, ret_g)


def _odd_inproj_kernel(x_ref, w_ref, wf_ref, bf_ref, q_ref, k_ref, v_ref, cum_ref, cumt_ref, carry_ref, *, nblk_s):
    i = pl.program_id(0)
    x = x_ref[...].astype(BF16)
    d = q_ref.shape[1]
    q_ref[...] = (_dot(x, w_ref[:, 0:d].astype(BF16)) * (FOX_DH ** -0.5 * LOG2E)).astype(BF16)
    k_ref[...] = _dot(x, w_ref[:, d:2 * d].astype(BF16)).astype(BF16)
    v_ref[...] = _dot(x, w_ref[:, 2 * d:3 * d].astype(BF16)).astype(BF16)
    z = _dot(x, wf_ref[...]) + bf_ref[...]
    c = jnp.minimum(z, 0.0) - jnp.log1p(jnp.exp(-jnp.abs(z)))
    tm = c.shape[0]
    row = lax.broadcasted_iota(I32, c.shape, 0)
    step = 1
    while step < tm:
        c = c + jnp.where(row >= step, pltpu.roll(c, step, 0), 0.0)
        step *= 2

    @pl.when(i % nblk_s == 0)
    def _():
        carry_ref[...] = jnp.zeros_like(carry_ref)

    c = c + carry_ref[...]
    cum = c * LOG2E
    cum_ref[...] = cum
    cum_t = cum.T
    for hp in range(cumt_ref.shape[0]):
        cumt_ref[hp] = cum_t[2 * hp:2 * hp + 2, :]
    carry_ref[...] = c[tm - 1:tm, :]


def _odd_inproj(x2, w_all, layer, wf_bf, bfg, seq):
    t, d = x2.shape
    tm = min(PROJ_TM, seq)
    nblk_s = seq // tm
    row = lambda i: (i, 0)
    kern = functools.partial(_odd_inproj_kernel, nblk_s=nblk_s)
    return pl.pallas_call(
        kern,
        grid=(t // tm,),
        in_specs=[
            pl.BlockSpec((tm, d), row),
            _layer_weight_spec(w_all, layer),
            pl.BlockSpec(wf_bf.shape, lambda i: (0, 0)),
            pl.BlockSpec(bfg.shape, lambda i: (0, 0)),
        ],
        out_specs=[pl.BlockSpec((tm, d), row)] * 3 + [
            pl.BlockSpec((tm, LANES), row),
            pl.BlockSpec((None, FOX_HEADS // 2, 2, tm), lambda i: (i // nblk_s, 0, 0, i % nblk_s))],
        out_shape=[jax.ShapeDtypeStruct((t, d), BF16)] * 3 + [
            jax.ShapeDtypeStruct((t, LANES), F32), jax.ShapeDtypeStruct((t // seq, FOX_HEADS // 2, 2, seq), F32)],
        scratch_shapes=[pltpu.VMEM((1, LANES), F32)],
        compiler_params=_cparams("arbitrary"),
        name="odd_inproj",
    )(x2, w_all, wf_bf, bfg)


def _fox_attn_kernel(q_ref, k_ref, v_ref, cq_ref, ck_ref, o_ref, qm_ref, va_ref, cqc_ref, m_ref, acc_ref, *, tq):
    seq = q_ref.shape[0]
    tk = tq
    hp = pl.program_id(1)
    q = q_ref[...]
    v = v_ref[...]
    lane = lax.broadcasted_iota(I32, (seq, LANES), 1)
    cq_all = cq_ref[...]
    for par in range(2):
        own = (lane // FOX_DH) == par
        qm_ref[par] = jnp.where(own, q, jnp.zeros_like(q))
        va_ref[par] = jnp.where(own, v, jnp.ones_like(v))
        cq = jnp.sum(jnp.where(lane == 2 * hp + par, cq_all, 0.0), axis=-1, keepdims=True)
        cqc_ref[par] = jnp.broadcast_to(cq, (seq, LANES))
    m_ref[...] = jnp.full(m_ref.shape, NEG_BIG, F32)
    acc_ref[...] = jnp.zeros(acc_ref.shape, F32)
    rq = ATT_RQ
    upper = (lax.broadcasted_iota(I32, (rq, rq), 1) > lax.broadcasted_iota(I32, (rq, rq), 0))
    lane_q = lax.broadcasted_iota(I32, (rq, LANES), 1)

    def n_pieces(g):
        return -(-(g + 1) * rq // tk)

    n_chunks = seq // rq
    chains = [(g, kp, par) for kp in range(n_pieces(n_chunks - 1)) for g in range(n_chunks)
              if kp < n_pieces(g) for par in range(2)]

    def n_keys(g, kp):
        return min(tk, (g + 1) * rq - kp * tk)

    def scores(chain):
        g, kp, par = chain
        k = k_ref[kp * tk:kp * tk + n_keys(g, kp), :]
        return _dot_nt(qm_ref[par, g * rq:(g + 1) * rq, :], k)

    def finish(chain, s):
        g, kp, par = chain
        rows = slice(g * rq, (g + 1) * rq)
        nk = n_keys(g, kp)
        ck = ck_ref[par:par + 1, kp * tk:kp * tk + nk]
        cq = cqc_ref[par, rows, :]
        sc = []
        for kc in range(nk // rq):
            t = s[:, kc * rq:(kc + 1) * rq] - ck[:, kc * rq:(kc + 1) * rq]
            if kp * (tk // rq) + kc == g:
                t = jnp.where(upper, NEG_BIG, t)
            sc.append(t)
        mx = sc[0]
        for t in sc[1:]:
            mx = jnp.maximum(mx, t)
        m_old = m_ref[par, rows, :]
        m_new = jnp.maximum(m_old, jnp.max(mx, axis=-1, keepdims=True) + cq)
        alpha = jnp.exp2(m_old - m_new)
        shift = m_new - cq
        p = jnp.concatenate([jnp.exp2(t - shift).astype(BF16) for t in sc], axis=1)
        acc_ref[par, rows, :] = alpha * acc_ref[par, rows, :] + _dot(p, va_ref[par, kp * tk:kp * tk + nk, :])
        m_ref[par, rows, :] = m_new
        if kp == n_pieces(g) - 1 and par == 1:
            acc0 = acc_ref[0, rows, :]
            acc1 = acc_ref[1, rows, :]
            out0 = acc0 / acc0[:, FOX_DH:FOX_DH + 1]
            out1 = acc1 / acc1[:, 0:1]
            o_ref[rows, :] = jnp.where(lane_q < FOX_DH, out0, out1).astype(BF16)

    _run_chains(chains, FOX_LOOKAHEAD, scores, finish)


def _fox_attention(q, k, v, cum, cum_t):
    b, s, d = q.shape
    tq = min(FOX_TQ, s)
    npair = d // LANES
    seq_blk = lambda bi, h: (bi, 0, h)
    return pl.pallas_call(
        functools.partial(_fox_attn_kernel, tq=tq),
        grid=(b, npair),
        in_specs=[
            pl.BlockSpec((None, s, LANES), seq_blk),
            pl.BlockSpec((None, s, LANES), seq_blk),
            pl.BlockSpec((None, s, LANES), seq_blk),
            pl.BlockSpec((None, s, LANES), lambda bi, h: (bi, 0, 0)),
            pl.BlockSpec((None, None, 2, s), lambda bi, h: (bi, h, 0, 0)),
        ],
        out_specs=pl.BlockSpec((None, s, LANES), seq_blk),
        out_shape=jax.ShapeDtypeStruct((b, s, d), BF16),
        scratch_shapes=[pltpu.VMEM((2, s, LANES), BF16), pltpu.VMEM((2, s, LANES), BF16),
                        pltpu.VMEM((2, s, LANES), F32), pltpu.VMEM((2, s, LANES), F32),
                        pltpu.VMEM((2, s, LANES), F32)],
        compiler_params=_cparams("parallel", "parallel"),
        name="fox_attention",
    )(q, k, v, cum, cum_t)


def _outproj_router_kernel(*refs, n_y):
    y_refs = refs[:n_y]
    (w_ref, x_ref, g_ref, b_ref, rwt_ref, tri_ref, h_ref, route_ref, col_ref, cnt_ref, xs_ref,
     hbf_ref) = refs[n_y:]
    rw2 = rwt_ref[...]
    rw_hi = rw2.astype(BF16)
    rw_lo = (rw2 - rw_hi.astype(F32)).astype(BF16)
    lane_w = lax.broadcasted_iota(I32, rw2.shape, 1)
    w = jnp.where(lane_w < N_EXPERTS, rw_hi, rw_lo)
    tm = x_ref.shape[0]
    rc = ROUTER_ROWS
    logit_chunks = []

    def project(c):
        rows = slice(c * rc, (c + 1) * rc)
        mix = None
        k0 = 0
        for yr in y_refs:
            k1 = k0 + yr.shape[1]
            part = _dot(yr[rows, :], w_ref[k0:k1, :].astype(BF16))
            mix = part if mix is None else mix + part
            k0 = k1
        return mix

    def norm_and_logits(c, mix):
        rows = slice(c * rc, (c + 1) * rc)
        h = _layer_norm_rows(DEEPNORM_ALPHA * x_ref[rows, :] + mix, g_ref[...], b_ref[...])
        h_ref[rows, :] = h
        h_hi = h.astype(BF16)
        hbf_ref[rows, :] = h_hi
        h_lo = (h - h_hi.astype(F32)).astype(BF16)
        p_hi = _dot(h_hi, w)
        p_lo = _dot(h_lo, w)
        slab = p_hi + (pltpu.roll(p_hi, LANES - N_EXPERTS, 1) + p_lo)
        logit_chunks.append(slab.T[0:N_EXPERTS])

    _run_chains(list(range(tm // rc)), 1, project, norm_and_logits)

    logits = jnp.concatenate(logit_chunks, axis=1)
    row = lax.broadcasted_iota(I32, (N_EXPERTS, tm), 0)
    mx = jnp.max(logits, axis=0, keepdims=True)
    ex = jnp.exp(logits - mx)
    probs = ex / jnp.sum(ex, axis=0, keepdims=True)
    grp = row // EXPERTS_PER_GROUP

    def top2(vals):
        v1 = jnp.max(vals, axis=0, keepdims=True)
        i1 = jnp.min(jnp.where(vals == v1, row, N_EXPERTS), axis=0, keepdims=True)
        rest = jnp.where(row == i1, -2.0, vals)
        v2 = jnp.max(rest, axis=0, keepdims=True)
        i2 = jnp.min(jnp.where(rest == v2, row, N_EXPERTS), axis=0, keepdims=True)
        return v1, i1, v2, i2

    best_score = None
    best = None
    for gi in range(N_GROUPS):
        v1, _, v2, _ = top2(jnp.where(grp == gi, probs, -1.0))
        score = v1 + v2
        if gi == 0:
            best_score, best = score, jnp.zeros_like(score, dtype=I32)
        else:
            better = score > best_score
            best = jnp.where(better, gi, best)
            best_score = jnp.where(better, score, best_score)
    v1, i1, v2, i2 = top2(jnp.where(grp == best, probs, -1.0))
    tot = v1 + v2
    g1 = v1 / tot
    g2 = v2 / tot

    onehot = jnp.where((row == i1) | (row == i2), 1.0, 0.0)
    pref = _dot(onehot.astype(BF16), tri_ref[...])
    cnt = jnp.broadcast_to(jnp.sum(onehot, axis=1, keepdims=True), cnt_ref.shape)
    cnt_ref[...] = cnt
    grp_rows = jnp.floor((cnt + (SORT_ALIGN - 1)) * (1.0 / SORT_ALIGN)) * SORT_ALIGN
    row_c = lax.broadcasted_iota(I32, cnt.shape, 0)
    start = grp_rows
    step = 1
    while step < N_EXPERTS:
        start = start + jnp.where(row_c >= step, pltpu.roll(start, step, 0), 0.0)
        step *= 2
    start = (start - grp_rows)[:, 0:1]
    s1 = jnp.sum(jnp.where(row == i1, pref + start, 0.0), axis=0, keepdims=True)
    s2 = jnp.sum(jnp.where(row == i2, pref + start, 0.0), axis=0, keepdims=True)

    row8 = lax.broadcasted_iota(I32, (ROUTE_ROWS, tm), 0)
    fields = (i1.astype(F32), i2.astype(F32), g1, g2, s1, s2)
    route = jnp.zeros((ROUTE_ROWS, tm), F32)
    for n, f in enumerate(fields):
        route = jnp.where(row8 == n, f, route)
    route_ref[...] = route
    col_ref[...] = jnp.concatenate([route, jnp.zeros((LANES - ROUTE_ROWS, tm), F32)], axis=0).T

    srow = lax.broadcasted_iota(I32, (xs_ref.shape[0], tm), 0).astype(F32)
    perm = jnp.where((srow == s1) | (srow == s2), 1.0, 0.0).astype(BF16)
    xs_ref[...] = _dot(perm, hbf_ref[...]).astype(BF16)


def _outproj_router(ys, w_all, layer, x2, ln_g, ln_b, rwt, tri):
    t, d = x2.shape
    tm = tri.shape[0]
    n_local = _local_rows(tm)
    nt = t // tm
    row = lambda i: (i, 0)
    full = lambda i: (0, 0)
    n_y = len(ys)
    kern = functools.partial(_outproj_router_kernel, n_y=n_y)
    return pl.pallas_call(
        kern,
        grid=(nt,),
        in_specs=([pl.BlockSpec((tm, y.shape[1]), row) for y in ys]
                  + [_layer_weight_spec(w_all, layer)]
                  + [pl.BlockSpec((tm, d), row), pl.BlockSpec((1, d), full), pl.BlockSpec((1, d), full),
                     pl.BlockSpec(rwt.shape, full), pl.BlockSpec(tri.shape, full)]),
        out_specs=[pl.BlockSpec((tm, d), row), pl.BlockSpec((ROUTE_ROWS, tm), lambda i: (0, i)),
                   pl.BlockSpec((tm, LANES), row), pl.BlockSpec((N_EXPERTS, LANES), row),
                   pl.BlockSpec((n_local, d), row)],
        out_shape=[jax.ShapeDtypeStruct((t, d), F32), jax.ShapeDtypeStruct((ROUTE_ROWS, t), F32),
                   jax.ShapeDtypeStruct((t, LANES), F32), jax.ShapeDtypeStruct((nt * N_EXPERTS, LANES), F32),
                   jax.ShapeDtypeStruct((nt * n_local, d), BF16)],
        scratch_shapes=[pltpu.VMEM((tm, d), BF16)],
        compiler_params=_cparams("parallel"),
        name="outproj_router",
    )(*ys, w_all, x2, ln_g, ln_b, rwt, tri)


def _start_tile_gather(tile, loff_ref, rows_ref, gpos_ref, sorted_hbm, local_ref, sem):
    for e in range(N_EXPERTS):
        n = tile * N_EXPERTS + e

        @pl.when(rows_ref[n] > 0)
        def _(n=n):
            size = pl.multiple_of(rows_ref[n], SORT_ALIGN)
            src = sorted_hbm.at[pl.ds(pl.multiple_of(gpos_ref[n], SORT_ALIGN), size)]
            dst = local_ref.at[pl.ds(pl.multiple_of(loff_ref[n], SORT_ALIGN), size)]
            pltpu.make_async_copy(src, dst, sem).start()


def _wait_rows(n_rows, src_hbm, dst_ref, sem):
    n = pl.multiple_of(n_rows, SORT_ALIGN)
    pltpu.make_async_copy(src_hbm.at[pl.ds(0, n)], dst_ref.at[pl.ds(0, n)], sem).wait()


def _expert_kernel(te_ref, nx_ref, nt_ref, g0_ref, g1_ref, valid_ref, loff_ref, rows_ref, gpos_ref,
                   xs_hbm, wg_hbm, wu_hbm, wd_hbm, o_ref,
                   xbuf, semx, wgs, wus, wds, wgb, wub, wdb, sem, nsw_ref, *, layer, n_local):
    r = pl.program_id(0)
    nt = nt_ref[0]
    cur = te_ref[r]
    tr = xbuf.shape[1]
    xslot = r % 2

    def start_pieces(tile, dst_slot):
        e = te_ref[tile]
        base = tile * tr

        def piece(tau, c):
            n = tau * N_EXPERTS + e
            lo = jnp.maximum(gpos_ref[n], base)
            hi = jnp.minimum(gpos_ref[n] + rows_ref[n], base + tr)

            @pl.when(hi > lo)
            def _():
                size = pl.multiple_of(hi - lo, SORT_ALIGN)
                src = pl.multiple_of(tau * n_local + loff_ref[n] + (lo - gpos_ref[n]), SORT_ALIGN)
                dst = pl.multiple_of(lo - base, SORT_ALIGN)
                pltpu.make_async_copy(xs_hbm.at[pl.ds(src, size)], xbuf.at[dst_slot, pl.ds(dst, size)],
                                      semx.at[dst_slot]).start()

            return c

        lax.fori_loop(g0_ref[tile], g1_ref[tile] + 1, piece, 0)

    @pl.when(r == 0)
    def _():
        xbuf[...] = jnp.zeros_like(xbuf)
        start_pieces(0, 0)

    @pl.when(r + 1 < nt)
    def _():
        start_pieces(r + 1, 1 - xslot)

    def weight_copies(e, slot):
        return [pltpu.make_async_copy(src.at[layer, e], dst.at[slot], sem.at[slot])
                for src, dst in ((wg_hbm, wgs), (wu_hbm, wus), (wd_hbm, wds))]

    @pl.when(r == 0)
    def _():
        nsw_ref[0] = 0
        for cp in weight_copies(cur, 0):
            cp.start()

    @pl.when((r == 0) | (cur != te_ref[jnp.maximum(r - 1, 0)]))
    def _():
        slot = nsw_ref[0] % 2
        nsw_ref[0] = nsw_ref[0] + 1
        for cp in weight_copies(cur, slot):
            cp.wait()

        @pl.when(nx_ref[r] != cur)
        def _():
            for cp in weight_copies(nx_ref[r], 1 - slot):
                cp.start()

        wgb[...] = wgs[slot].astype(BF16)
        wub[...] = wus[slot].astype(BF16)
        wdb[...] = wds[slot].astype(BF16)

    @pl.when(r < nt)
    def _():
        _wait_rows(valid_ref[r], xs_hbm, xbuf.at[xslot], semx.at[xslot])
        rc = EXPERT_ROWS

        def up(c):
            x = xbuf[xslot, c * rc:(c + 1) * rc, :]
            return _dot(x, wgb[...]), _dot(x, wub[...])

        def down(c, gate_up):
            a = _silu(gate_up[0]) * gate_up[1]
            o_ref[c * rc:(c + 1) * rc, :] = _dot(a.astype(BF16), wdb[...]).astype(BF16)

        _run_chains(list(range(tr // rc)), 1, up, down)

    @pl.when(r >= nt)
    def _():
        o_ref[...] = jnp.zeros_like(o_ref)


def _expert_mlps(meta, n_slots, xs_local, wg, wu, wd, layer):
    d = xs_local.shape[1]
    tr = MOE_TR
    dff = wg.shape[3]
    grid_spec = pltpu.PrefetchScalarGridSpec(
        num_scalar_prefetch=9,
        grid=(n_slots // tr,),
        in_specs=[pl.BlockSpec(memory_space=pl.ANY)] * 4,
        out_specs=pl.BlockSpec((tr, d), lambda r, *_: (r, 0)),
        scratch_shapes=[pltpu.VMEM((2, tr, d), BF16), pltpu.SemaphoreType.DMA((2,)),
                        pltpu.VMEM((2, d, dff), F32), pltpu.VMEM((2, d, dff), F32), pltpu.VMEM((2, dff, d), F32),
                        pltpu.VMEM((d, dff), BF16), pltpu.VMEM((d, dff), BF16), pltpu.VMEM((dff, d), BF16),
                        pltpu.SemaphoreType.DMA((2,)), pltpu.SMEM((1,), I32)],
    )
    return pl.pallas_call(
        functools.partial(_expert_kernel, layer=layer, n_local=_local_rows(MOE_TM)),
        grid_spec=grid_spec,
        out_shape=jax.ShapeDtypeStruct((n_slots, d), BF16),
        compiler_params=_cparams("arbitrary"),
        name="expert_mlps",
    )(meta["tile_expert"], meta["next_expert"], meta["n_tiles"], meta["first_group"], meta["last_group"],
      meta["valid"], meta["loff"], meta["rows"], meta["gpos"], xs_local, wg, wu, wd)


def _combine_kernel(loff_ref, rows_ref, gpos_ref, tot_ref, ys_hbm, h_ref, col_ref, p_ref, g_ref, b_ref,
                    pg_ref, pp_ref, o_ref, ybuf, sem):
    i = pl.program_id(0)
    n = pl.num_programs(0)
    tm = h_ref.shape[0]
    n_local = ybuf.shape[1]
    slot = i % 2

    @pl.when(i == 0)
    def _():
        ybuf[...] = jnp.zeros_like(ybuf)
        _start_tile_gather(0, loff_ref, rows_ref, gpos_ref, ys_hbm, ybuf.at[0], sem.at[0])

    @pl.when(i + 1 < n)
    def _():
        _start_tile_gather(i + 1, loff_ref, rows_ref, gpos_ref, ys_hbm, ybuf.at[1 - slot], sem.at[1 - slot])

    _wait_rows(tot_ref[i], ys_hbm, ybuf.at[slot], sem.at[slot])
    rc = COMBINE_ROWS
    scol = lax.broadcasted_iota(I32, (rc, n_local), 1).astype(F32)

    def gather_rows(c):
        rows = slice(c * rc, (c + 1) * rc)
        cols = col_ref[rows, :]
        pick = jnp.zeros((rc, n_local), F32)
        for k in range(2):
            pick = jnp.where(scol == cols[:, ROUTE_SLOT + k:ROUTE_SLOT + k + 1],
                             cols[:, ROUTE_GATE + k:ROUTE_GATE + k + 1], pick)
        ffn = _dot(pick.astype(BF16), ybuf[slot])
        return ffn, _dot(p_ref[rows, :].astype(BF16), pp_ref[...].astype(BF16))

    def finish(c, gathered):
        rows = slice(c * rc, (c + 1) * rc)
        ffn, pe = gathered
        h2 = _layer_norm_rows(DEEPNORM_ALPHA * h_ref[rows, :] + ffn, g_ref[...], b_ref[...])
        gate = _sigmoid(_dot(h2.astype(BF16), pg_ref[...].astype(BF16)))
        o_ref[rows, :] = h2 + gate * pe

    _run_chains(list(range(tm // rc)), 1, gather_rows, finish)


def _combine(meta, ys, h, cols, p3, layer, ln_g, ln_b, ple_gate, ple_proj):
    t, d = h.shape
    tm = MOE_TM
    pdim = p3.shape[2]
    row = lambda i, *_: (i, 0)
    full = lambda i, *_: (0, 0)
    grid_spec = pltpu.PrefetchScalarGridSpec(
        num_scalar_prefetch=4,
        grid=(t // tm,),
        in_specs=[
            pl.BlockSpec(memory_space=pl.ANY),
            pl.BlockSpec((tm, d), row),
            pl.BlockSpec((tm, LANES), row),
            pl.BlockSpec((None, tm, pdim), lambda i, *_: (layer, i, 0)),
            pl.BlockSpec((1, d), full),
            pl.BlockSpec((1, d), full),
            pl.BlockSpec((None,) + ple_gate.shape[1:], lambda i, *_: (layer, 0, 0), pipeline_mode=pl.Buffered(1)),
            pl.BlockSpec((None,) + ple_proj.shape[1:], lambda i, *_: (layer, 0, 0), pipeline_mode=pl.Buffered(1)),
        ],
        out_specs=pl.BlockSpec((tm, d), row),
        scratch_shapes=[pltpu.VMEM((2, _local_rows(tm), d), BF16), pltpu.SemaphoreType.DMA((2,))],
    )
    return pl.pallas_call(
        _combine_kernel,
        grid_spec=grid_spec,
        out_shape=jax.ShapeDtypeStruct((t, d), F32),
        compiler_params=_cparams("arbitrary"),
        name="moe_combine",
    )(meta["loff"], meta["rows"], meta["gpos"], meta["tot"], ys, h, cols, p3, ln_g, ln_b, ple_gate, ple_proj)


def _rotary_tables(seq):
    half = RET_DK // 2
    inv = (np.float32(ROPE_BASE) ** (-np.arange(half, dtype=np.float32) / np.float32(half))).astype(np.float32)
    ang = (np.arange(seq, dtype=np.float32)[:, None] * inv[None, :]).astype(np.float32)
    cos = np.cos(ang.astype(np.float64))
    sin = np.sin(ang.astype(np.float64))
    cos_h = np.concatenate([cos, cos], axis=1)
    sin_h = np.concatenate([-sin, sin], axis=1)
    return (jnp.asarray(np.tile(cos_h, (1, RET_HEADS)), F32), jnp.asarray(np.tile(sin_h, (1, RET_HEADS)), F32))


def _retention_tables():
    c = RET_CHUNK
    h = np.arange(RET_HEADS, dtype=np.float64)
    log_g = np.log1p(-np.exp2(-5.0 - h))
    j = np.arange(c, dtype=np.float64)
    rel = j[:, None] - j[None, :]
    din = np.where(rel >= 0, np.exp(np.maximum(rel, 0.0)[None] * log_g[:, None, None]), 0.0)
    qd = np.exp((j + 1.0)[None] * log_g[:, None])
    kd = np.exp((c - 1.0 - j)[None] * log_g[:, None])
    cd = np.exp(c * log_g)
    qd = np.broadcast_to(qd[:, :, None], (RET_HEADS, c, LANES))
    kd = np.broadcast_to(kd[:, :, None], (RET_HEADS, c, LANES))
    cd = np.broadcast_to(cd[:, None, None], (RET_HEADS, 1, LANES))
    return tuple(jnp.asarray(a, F32) for a in (din, qd, kd, cd))


def _t5_bucket_np(dist):
    max_exact = REL_BUCKETS // 2
    d = np.maximum(dist, 1).astype(np.float32)
    large = max_exact + (np.log(d / np.float32(max_exact)) / np.float32(math.log(REL_MAX_DIST / max_exact))
                         * np.float32(REL_BUCKETS - max_exact)).astype(np.int32)
    large = np.minimum(large, REL_BUCKETS - 1)
    return np.where(dist < max_exact, dist, large)


def _diff_bias_tables(rel_bias, seq):
    c = ATT_RQ
    r = np.arange(c)
    dist0 = r[:, None] - r[None, :]
    far = REL_BUCKETS - 1
    assert np.all(_t5_bucket_np(np.arange(c + 1, max(seq, 2 * c))) == far)
    bidx = np.stack([_t5_bucket_np(np.maximum(dist0, 0)), _t5_bucket_np(dist0 + c)])
    rb = rel_bias.astype(F32).T
    shifted = (rb - rb[:, far:far + 1]) * LOG2E
    bidx = jnp.asarray(bidx, I32)[None]
    tab = jnp.zeros((rb.shape[0], 2, c, c), F32)
    for bucket in range(REL_BUCKETS - 1):
        tab = jnp.where(bidx == bucket, shifted[:, bucket][:, None, None, None], tab)
    causal = jnp.asarray(np.stack([dist0 >= 0, np.ones_like(dist0, bool)]))[None]
    return jnp.where(causal, tab, NEG_BIG)


def _local_rows(tm):
    need = 2 * tm + N_EXPERTS * (SORT_ALIGN - 1)
    return -(-need // LANES) * LANES


def _round_up(a, m):
    return ((a + m - 1) // m) * m


def _route_meta(cnt, t):
    tm, tr = MOE_TM, MOE_TR
    nt = t // tm
    counts = cnt.reshape(nt, N_EXPERTS, LANES)[:, :, 0].astype(I32)
    rows = _round_up(counts, SORT_ALIGN)
    loff = jnp.cumsum(rows, axis=1) - rows
    seg = jnp.sum(rows, axis=0)
    seg_pad = _round_up(seg, tr)
    ends = jnp.cumsum(seg_pad)
    offs = ends - seg_pad
    gpos = offs[None, :] + jnp.cumsum(rows, axis=0) - rows
    n_slots = 2 * t + nt * N_EXPERTS * (SORT_ALIGN - 1)
    n_slots = _round_up(n_slots, tr) + N_EXPERTS * tr
    n_tiles = (ends[-1] // tr).astype(I32)
    tile_start = jnp.arange(n_slots // tr, dtype=I32) * tr
    tile_expert = jnp.sum((tile_start[:, None] >= ends[None, :]).astype(I32), axis=1)
    last = jnp.sum((((n_tiles - 1) * tr) >= ends).astype(I32))
    tile_expert = jnp.minimum(tile_expert, last).astype(I32)
    eid = jnp.arange(N_EXPERTS, dtype=I32)
    later = (eid[None, :] > eid[:, None]) & (seg_pad > 0)[None, :]
    nxt = jnp.min(jnp.where(later, eid[None, :], N_EXPERTS), axis=1)
    nxt = jnp.where(nxt == N_EXPERTS, eid, nxt)
    next_expert = jnp.sum(jnp.where(tile_expert[:, None] == eid[None, :], nxt[None, :], 0), axis=1).astype(I32)
    own = tile_expert[:, None] == eid[None, :]
    pick = lambda tab: jnp.sum(jnp.where(own[:, None, :], tab[None, :, :], 0), axis=-1)
    g_start, g_end = pick(gpos), pick(gpos + rows)
    used = (tile_start < n_tiles * tr)[:, None]
    first_group = jnp.sum((g_end <= tile_start[:, None]).astype(I32), axis=1)
    last_group = jnp.sum(((g_start < tile_start[:, None] + tr) & used).astype(I32), axis=1) - 1
    seg_end = jnp.sum(jnp.where(own, (offs + seg)[None, :], 0), axis=1)
    valid = jnp.clip(seg_end - tile_start, 0, tr)
    meta = {
        "loff": loff.reshape(-1).astype(I32), "rows": rows.reshape(-1).astype(I32),
        "gpos": gpos.reshape(-1).astype(I32), "tot": jnp.sum(rows, axis=1).astype(I32),
        "tile_expert": tile_expert, "next_expert": next_expert, "n_tiles": n_tiles.reshape(1),
        "first_group": first_group.astype(I32), "last_group": last_group.astype(I32), "valid": valid.astype(I32),
    }
    return meta, n_slots


def kernel(x, p, rel_bias, router_w, even_w_in, even_w_out, even_lambda, even_diff_norm, even_ret_norm,
           odd_w_in, odd_b_forget, odd_w_out, ln_mix_g, ln_mix_b, ln_ffn_g, ln_ffn_b,
           moe_w_gate, moe_w_up, moe_w_down, ple_proj, ple_gate):
    b, s, d = x.shape
    t = b * s
    assert d == FOX_HEADS * FOX_DH and p.shape[0] == DEPTH and even_w_in.shape[2] == EVEN_CUTS[-1]
    assert odd_w_in.shape[2] == 3 * d + FOX_HEADS and moe_w_gate.shape[1] == N_EXPERTS
    assert s % RET_CHUNK == 0 and s % min(ATT_TQ, s) == 0 and s % min(FOX_TQ, s) == 0
    assert t % min(PROJ_TM, s) == 0 and t % MOE_TM == 0

    cos_t, sin_t = _rotary_tables(s)
    ret_tabs = _retention_tables()
    bias_tab = _diff_bias_tables(rel_bias, s)
    rw32 = router_w.astype(F32)
    rwt = jnp.zeros((d, LANES), F32).at[:, :N_EXPERTS].set(rw32).at[:, N_EXPERTS:2 * N_EXPERTS].set(rw32)
    tok = np.arange(MOE_TM)
    tri = jnp.asarray(tok[:, None] < tok[None, :], BF16)

    x2 = x.reshape(t, d)
    for i in range(DEPTH):
        j = i // 2
        if i % 2 == 0:
            lam_init = 0.8 - 0.6 * math.exp(-0.3 * i)
            qa, ka, va, qb, kb, vb, gb = _even_inproj(x2, even_w_in, j, cos_t, sin_t, s)
            sh = lambda a: a.reshape(b, s, a.shape[1])
            ya = _diff_attention(sh(qa), sh(ka), sh(va), bias_tab, even_lambda[j].astype(F32),
                                 even_diff_norm[j].reshape(1, -1).astype(F32), lam_init)
            yb = _retention(sh(qb), sh(kb), sh(vb), sh(gb), ret_tabs, even_ret_norm[j].reshape(1, -1).astype(F32))
            ys = [ya.reshape(t, -1), yb.reshape(t, -1)]
            w_out = even_w_out
        else:
            wf = jnp.zeros((d, LANES), BF16).at[:, :FOX_HEADS].set(odd_w_in[j, :, 3 * d:].astype(BF16))
            bfg = jnp.zeros((1, LANES), F32).at[0, :FOX_HEADS].set(odd_b_forget[j].astype(F32))
            q, k, v, cum, cum_t = _odd_inproj(x2, odd_w_in, j, wf, bfg, s)
            y = _fox_attention(q.reshape(b, s, d), k.reshape(b, s, d), v.reshape(b, s, d),
                               cum.reshape(b, s, LANES), cum_t)
            ys = [y.reshape(t, d)]
            w_out = odd_w_out
        h, route, cols, cnt, xs_local = _outproj_router(ys, w_out, j, x2, ln_mix_g[i].reshape(1, d),
                                                        ln_mix_b[i].reshape(1, d), rwt, tri)
        meta, n_slots = _route_meta(cnt, t)
        rows = _expert_mlps(meta, n_slots, xs_local, moe_w_gate, moe_w_up, moe_w_down, i)
        x2 = _combine(meta, rows, h, cols, p.reshape(DEPTH, t, -1), i, ln_ffn_g[i].reshape(1, d),
                      ln_ffn_b[i].reshape(1, d), ple_gate, ple_proj)
    return x2.reshape(b, s, d)
```
